```python
import jax, jax.numpy as jnp
from jax import lax
import numpy as np

D_MODEL = 1024
BATCH = 4
SEQ = 4096
DEPTH = 1
DEC_BATCH = 128
DEC_SEQ = 1
PAST_LEN = 8192
PAGE_SIZE = 128

N_HEADS = 16
HEAD_DIM = 64
N_KV_HEADS = 4
GROUP = N_HEADS // N_KV_HEADS
ATTN_WIDTH = N_HEADS * HEAD_DIM
KV_WIDTH = N_KV_HEADS * HEAD_DIM
WINDOW = 128
ATTN_BLOCK = WINDOW
ATTN_SCALE = HEAD_DIM ** -0.5
NEG_BIG = -1e30
SGU_CHUNK = 128
SGU_WIDTH = D_MODEL
SGU_GROUPS = 8
SGU_GROUP_DIM = SGU_WIDTH // SGU_GROUPS
D_FF = 4 * D_MODEL
EPS = 1e-6
IN_SIZES = (SGU_WIDTH, SGU_WIDTH, ATTN_WIDTH, KV_WIDTH, KV_WIDTH, D_MODEL, D_MODEL)
IN_WIDTH = sum(IN_SIZES)
IN_OFFSETS = tuple(int(o) for o in np.cumsum(IN_SIZES)[:-1])

kernel_name = "hybrid_gated_sgu_swa_decoder_step"


def rmsnorm(x, g):
    xf = x.astype(jnp.float32)
    y = xf * lax.rsqrt(jnp.mean(xf * xf, axis=-1, keepdims=True) + EPS)
    return (y * g.astype(jnp.float32)).astype(x.dtype)


def alibi_slopes():
    h = jnp.arange(1, N_HEADS + 1, dtype=jnp.float32)
    return jnp.exp2(-8.0 * h / N_HEADS)


def in_proj(x, ln1_g, w_in):
    xn = rmsnorm(x, ln1_g)
    h = xn @ w_in
    u, v, q, k, va, ga, gb = jnp.split(h, IN_OFFSETS, axis=-1)
    u = jax.nn.gelu(u)
    v = jax.nn.gelu(v)
    lead = x.shape[:-1]
    q = q.reshape(lead + (N_KV_HEADS, GROUP, HEAD_DIM))
    k = k.reshape(lead + (N_KV_HEADS, HEAD_DIM))
    va = va.reshape(lead + (N_KV_HEADS, HEAD_DIM))
    return u, v, q, k, va, ga, gb


def sink_attend(q, k, v, dist, valid, sinks, slopes):
    s = jnp.einsum('...qhgd,...shd->...hgqs', q, k).astype(jnp.float32) * ATTN_SCALE
    s = s - slopes.reshape(N_KV_HEADS, GROUP, 1, 1) * dist.astype(jnp.float32)
    s = jnp.where(valid, s, NEG_BIG)
    sink = jnp.broadcast_to(sinks.astype(jnp.float32).reshape(N_KV_HEADS, GROUP, 1, 1), s.shape[:-1] + (1,))
    p = jax.nn.softmax(jnp.concatenate([s, sink], axis=-1), axis=-1)[..., :-1]
    return jnp.einsum('...hgqs,...shd->...qhgd', p.astype(v.dtype), v)


def swa_prompt(q, k, v, sinks, slopes):
    B, T = q.shape[0], q.shape[1]
    nb = T // ATTN_BLOCK
    qb = q.reshape(B, nb, ATTN_BLOCK, N_KV_HEADS, GROUP, HEAD_DIM)
    kb = k.reshape(B, nb, ATTN_BLOCK, N_KV_HEADS, HEAD_DIM)
    vb = v.reshape(B, nb, ATTN_BLOCK, N_KV_HEADS, HEAD_DIM)
    prev = lambda a: jnp.concatenate([jnp.zeros_like(a[:, :1]), a[:, :-1]], axis=1)
    kband = jnp.concatenate([prev(kb), kb], axis=2)
    vband = jnp.concatenate([prev(vb), vb], axis=2)
    qi = jnp.arange(ATTN_BLOCK)[:, None] + ATTN_BLOCK
    kj = jnp.arange(2 * ATTN_BLOCK)[None, :]
    diff = qi - kj
    blk = jnp.arange(nb)[:, None, None]
    key_ok = (blk * ATTN_BLOCK + kj[None] - ATTN_BLOCK) >= 0
    valid = (((diff >= 0) & (diff <= WINDOW))[None] & key_ok)[:, None, None]
    o = sink_attend(qb, kband, vband, diff, valid, sinks, slopes)
    w = min(WINDOW, T)
    return o.reshape(B, T, ATTN_WIDTH), k[:, T - w:], v[:, T - w:]


def swa_sample(q, k, v, cache_k, cache_v, sinks, slopes):
    T = q.shape[1]
    w_buf = cache_k.shape[1]
    kf = jnp.concatenate([cache_k, k], axis=1)
    vf = jnp.concatenate([cache_v, v], axis=1)
    diff = (jnp.arange(T)[:, None] + w_buf) - jnp.arange(w_buf + T)[None, :]
    valid = (diff >= 0) & (diff <= WINDOW)
    o = sink_attend(q, kf, vf, diff, valid, sinks, slopes)
    return o.reshape(q.shape[0], T, ATTN_WIDTH), kf[:, T:], vf[:, T:]


def causal_mix_matrix(sgu_w):
    mask = jnp.tril(jnp.ones((SGU_CHUNK, SGU_CHUNK), dtype=bool))
    return sgu_w * mask.astype(sgu_w.dtype)


def sgu_prompt(u, v, g, sgu_w, sgu_b):
    B, T, _ = v.shape
    nc = T // SGU_CHUNK
    vn = rmsnorm(v, g).reshape(B, nc, SGU_CHUNK, SGU_GROUPS, SGU_GROUP_DIM)
    mixed = jnp.einsum('gts,bcsgd->bctgd', causal_mix_matrix(sgu_w), vn) + sgu_b.T[None, None, :, :, None]
    return u * mixed.reshape(B, T, SGU_WIDTH)


def sgu_sample(u, v, g, sgu_w, sgu_b):
    Bd, T, _ = v.shape
    vn = rmsnorm(v, g)
    vg = vn.reshape(Bd, T, SGU_GROUPS, SGU_GROUP_DIM)
    w = causal_mix_matrix(sgu_w)[:, :T, :T]
    mixed = jnp.einsum('gts,bsgd->btgd', w, vg) + sgu_b[:, :T].T[None, :, :, None]
    return u * mixed.reshape(Bd, T, SGU_WIDTH), vn


def merge_and_ffn(x, a, b, ga, gb, w_oa, w_ob, w_out, ln2_g, w_up, w_down):
    h = jax.nn.sigmoid(ga) * (a @ w_oa) + jax.nn.sigmoid(gb) * (b @ w_ob)
    x = x + h @ w_out
    xn = rmsnorm(x, ln2_g)
    return x + jnp.square(jax.nn.relu(xn @ w_up)) @ w_down


def setup_inputs(seed: int = 0) -> dict:
    key = jax.random.key(seed)
    ks = jax.random.split(key, 20)
    nrm = lambda k, shape, scale: jax.random.normal(k, shape, jnp.float32) * scale
    gain = lambda k, shape: 1.0 + nrm(k, shape, 0.02)
    cache_shape = (DEPTH, DEC_BATCH, WINDOW, N_KV_HEADS, HEAD_DIM)
    return {
        "x_prompt": nrm(ks[0], (BATCH, SEQ, D_MODEL), 1.0),
        "x_sample": nrm(ks[1], (DEC_BATCH, DEC_SEQ, D_MODEL), 1.0),
        "cache_k_win": nrm(ks[2], cache_shape, 1.0),
        "cache_v_win": nrm(ks[3], cache_shape, 1.0),
        "ln1_g": gain(ks[4], (DEPTH, D_MODEL)),
        "w_in": nrm(ks[5], (DEPTH, D_MODEL, IN_WIDTH), D_MODEL ** -0.5),
        "sgu_norm_g": gain(ks[6], (DEPTH, SGU_WIDTH)),
        "sgu_w": nrm(ks[7], (DEPTH, SGU_GROUPS, SGU_CHUNK, SGU_CHUNK), 0.5 * SGU_CHUNK ** -0.5),
        "sgu_b": 1.0 + nrm(ks[8], (DEPTH, SGU_GROUPS, SGU_CHUNK), 0.1),
        "attn_sinks": nrm(ks[9], (DEPTH, N_HEADS), 1.0),
        "w_oa": nrm(ks[10], (DEPTH, SGU_WIDTH, D_MODEL), SGU_WIDTH ** -0.5),
        "w_ob": nrm(ks[11], (DEPTH, ATTN_WIDTH, D_MODEL), ATTN_WIDTH ** -0.5),
        "w_out": nrm(ks[12], (DEPTH, D_MODEL, D_MODEL), D_MODEL ** -0.5),
        "ln2_g": gain(ks[13], (DEPTH, D_MODEL)),
        "w_up": nrm(ks[14], (DEPTH, D_MODEL, D_FF), D_MODEL ** -0.5),
        "w_down": nrm(ks[15], (DEPTH, D_FF, D_MODEL), D_FF ** -0.5),
        "lnf_g": gain(ks[16], (D_MODEL,)),
    }


def reference(x_prompt, x_sample, cache_k_win, cache_v_win, ln1_g, w_in, sgu_norm_g, sgu_w, sgu_b,
              attn_sinks, w_oa, w_ob, w_out, ln2_g, w_up, w_down, lnf_g):
    slopes = alibi_slopes().astype(jnp.float32)
    xp, xs = x_prompt, x_sample
    kp_l, vp_l, ks_l, vs_l, sv_l = [], [], [], [], []
    for l in range(DEPTH):
        u, v, q, k, va, ga, gb = in_proj(xp, ln1_g[l], w_in[l])
        a = sgu_prompt(u, v, sgu_norm_g[l], sgu_w[l], sgu_b[l])
        b, kw, vw = swa_prompt(q, k, va, attn_sinks[l], slopes)
        xp = merge_and_ffn(xp, a, b, ga, gb, w_oa[l], w_ob[l], w_out[l], ln2_g[l], w_up[l], w_down[l])
        kp_l.append(kw)
        vp_l.append(vw)
        u, v, q, k, va, ga, gb = in_proj(xs, ln1_g[l], w_in[l])
        a, vrows = sgu_sample(u, v, sgu_norm_g[l], sgu_w[l], sgu_b[l])
        b, kw, vw = swa_sample(q, k, va, cache_k_win[l], cache_v_win[l], attn_sinks[l], slopes)
        xs = merge_and_ffn(xs, a, b, ga, gb, w_oa[l], w_ob[l], w_out[l], ln2_g[l], w_up[l], w_down[l])
        ks_l.append(kw)
        vs_l.append(vw)
        sv_l.append(vrows)
    y_prompt = rmsnorm(xp, lnf_g)
    y_sample = rmsnorm(xs, lnf_g)
    new_k_win_prompt = jnp.stack(kp_l)
    new_v_win_prompt = jnp.stack(vp_l)
    new_k_win_sample = jnp.stack(ks_l)
    new_v_win_sample = jnp.stack(vs_l)
    new_sgu_v_sample = jnp.stack(sv_l)
    return (y_prompt, y_sample, new_k_win_prompt, new_v_win_prompt, new_k_win_sample, new_v_win_sample, new_sgu_v_sample)
```

```python
import functools
import math

import jax
import jax.numpy as jnp
from jax import lax
from jax.experimental import pallas as pl
from jax.experimental.pallas import tpu as pltpu

D_MODEL = 1024
N_HEADS = 16
HEAD_DIM = 64
N_KV_HEADS = 4
GROUP = N_HEADS // N_KV_HEADS
KV_WIDTH = N_KV_HEADS * HEAD_DIM
WINDOW = 128
CHUNK = 128
SGU_GROUPS = 8
SGU_GROUP_DIM = D_MODEL // SGU_GROUPS
D_FF = 4 * D_MODEL
FF_SLAB = 1024
EPS = 1e-6
NEG_BIG = -1e30
ATTN_SCALE = HEAD_DIM ** -0.5

OFF_U, OFF_V, OFF_Q, OFF_K, OFF_VA, OFF_GA, OFF_GB = 0, 1024, 2048, 3072, 3328, 3584, 4608
IN_WIDTH = 5632

TOKEN_BLOCK = 512
SAMPLE_BLOCK = 16
VMEM_LIMIT_BYTES = 58 * 1024 * 1024

F32 = jnp.float32
BF16 = jnp.bfloat16


def _rmsnorm(x, g):
    ms = jnp.mean(x * x, axis=-1, keepdims=True)
    return x * lax.rsqrt(ms + EPS) * g


def _gelu_tanh(x):
    c = math.sqrt(2.0 / math.pi)
    return x * (0.5 * (1.0 + jnp.tanh(c * (x + 0.044715 * (x * x * x)))))


def _dot(a, b):
    return jnp.dot(a, b, preferred_element_type=F32)


def _dot_nt(a, b):
    return lax.dot_general(a, b, (((1,), (1,)), ((), ())), preferred_element_type=F32)


def _resident(shape):
    zeros = (0,) * len(shape)
    return pl.BlockSpec(shape, lambda *_: zeros, pipeline_mode=pl.Buffered(1))


def _mix_prompt_kernel(x_ref, ln1_ref, w_in_ref, sgu_g_ref, sgu_w_ref, bexp_ref, sink_ref, bias_ref,
                       w_oa_ref, w_ob_ref, w_out_ref,
                       x1_ref, kwin_ref, vwin_ref,
                       q_scr, kb_scr, vb_scr, vn_scr, u_scr, a_scr, b_scr, wt_scr,
                       *, steps_per_seq):
    step = pl.program_id(0)
    tb = x_ref.shape[0]
    first = (step % steps_per_seq) == 0

    @pl.when(step == 0)
    def _():
        row = lax.broadcasted_iota(jnp.int32, (CHUNK, CHUNK), 0)
        col = lax.broadcasted_iota(jnp.int32, (CHUNK, CHUNK), 1)
        for g in range(SGU_GROUPS):
            wt_scr[g] = jnp.where(row >= col, sgu_w_ref[g], 0.0).astype(BF16)

    @pl.when(first)
    def _():
        kb_scr[0:WINDOW, :] = jnp.zeros((WINDOW, KV_WIDTH), BF16)
        vb_scr[0:WINDOW, :] = jnp.zeros((WINDOW, KV_WIDTH), BF16)

    x = x_ref[...]
    xn = _rmsnorm(x, ln1_ref[...]).astype(BF16)

    u_scr[...] = _gelu_tanh(_dot(xn, w_in_ref[:, OFF_U:OFF_U + D_MODEL]))
    v = _gelu_tanh(_dot(xn, w_in_ref[:, OFF_V:OFF_V + D_MODEL]))
    vn_scr[...] = _rmsnorm(v, sgu_g_ref[...]).astype(BF16)
    q_scr[...] = _dot(xn, w_in_ref[:, OFF_Q:OFF_Q + D_MODEL]) * ATTN_SCALE
    k = _dot(xn, w_in_ref[:, OFF_K:OFF_K + KV_WIDTH])
    va = _dot(xn, w_in_ref[:, OFF_VA:OFF_VA + KV_WIDTH])
    kwin_ref[...] = k[tb - WINDOW:, :]
    vwin_ref[...] = va[tb - WINDOW:, :]
    kb_scr[WINDOW:, :] = k.astype(BF16)
    vb_scr[WINDOW:, :] = va.astype(BF16)

    lane_kvh = lax.broadcasted_iota(jnp.int32, (CHUNK, KV_WIDTH), 1) // HEAD_DIM

    def chunk_body(r, carry):
        row0 = pl.multiple_of(r * CHUNK, CHUNK)
        rows = pl.ds(row0, CHUNK)
        bias_sel = jnp.logical_and(first, r == 0).astype(jnp.int32)

        vn_c = vn_scr[rows, :]
        mixed = jnp.concatenate(
            [_dot(wt_scr[g], vn_c[:, g * SGU_GROUP_DIM:(g + 1) * SGU_GROUP_DIM]) for g in range(SGU_GROUPS)],
            axis=1) + bexp_ref[...]
        a_scr[rows, :] = (u_scr[rows, :] * mixed).astype(BF16)

        kb = kb_scr[pl.ds(row0, 2 * CHUNK), :]
        vb = vb_scr[pl.ds(row0, 2 * CHUNK), :]
        qms = []
        for g in range(GROUP):
            qg = q_scr[rows, g * KV_WIDTH:(g + 1) * KV_WIDTH]
            for kvh in range(N_KV_HEADS):
                qms.append(jnp.where(lane_kvh == kvh, qg, 0.0).astype(BF16))
        s_all = _dot_nt(jnp.concatenate(qms, axis=0), kb)
        ps, invs = [], []
        for h in range(N_HEADS):
            s = s_all[h * CHUNK:(h + 1) * CHUNK, :] + bias_ref[bias_sel, h]
            sink = sink_ref[h]
            m = jnp.maximum(jnp.max(s, axis=1, keepdims=True), sink)
            p = jnp.exp(s - m)
            denom = jnp.sum(p, axis=1, keepdims=True) + jnp.exp(sink - m)
            ps.append(p.astype(BF16))
            invs.append(1.0 / denom)
        o_all = _dot(jnp.concatenate(ps, axis=0), vb)
        for g in range(GROUP):
            acc = jnp.zeros((CHUNK, KV_WIDTH), F32)
            for kvh in range(N_KV_HEADS):
                h = g * N_KV_HEADS + kvh
                o = o_all[h * CHUNK:(h + 1) * CHUNK, :] * invs[h]
                acc = jnp.where(lane_kvh == kvh, o, acc)
            b_scr[rows, g * KV_WIDTH:(g + 1) * KV_WIDTH] = acc.astype(BF16)
        return carry

    lax.fori_loop(0, tb // CHUNK, chunk_body, 0)

    kb_scr[0:WINDOW, :] = kb_scr[tb:tb + WINDOW, :]
    vb_scr[0:WINDOW, :] = vb_scr[tb:tb + WINDOW, :]

    ga = jax.nn.sigmoid(_dot(xn, w_in_ref[:, OFF_GA:OFF_GA + D_MODEL]))
    gb = jax.nn.sigmoid(_dot(xn, w_in_ref[:, OFF_GB:OFF_GB + D_MODEL]))
    hm = ga * _dot(a_scr[...], w_oa_ref[...]) + gb * _dot(b_scr[...], w_ob_ref[...])
    x1_ref[...] = x + _dot(hm.astype(BF16), w_out_ref[...])


def _mix_prompt(x2d, ln1, w_in, sgu_g, sgu_w, bexp, sinks, bias, w_oa, w_ob, w_out, *, batch, seq):
    n = x2d.shape[0]
    tb = TOKEN_BLOCK
    steps_per_seq = seq // tb
    row_block = pl.BlockSpec((tb, D_MODEL), lambda i: (i, 0))
    win_block = pl.BlockSpec((None, WINDOW, KV_WIDTH), lambda i: (i // steps_per_seq, 0, 0))
    return pl.pallas_call(
        functools.partial(_mix_prompt_kernel, steps_per_seq=steps_per_seq),
        grid=(n // tb,),
        in_specs=[
            row_block,
            _resident((1, D_MODEL)),
            _resident((D_MODEL, IN_WIDTH)),
            _resident((1, D_MODEL)),
            _resident((SGU_GROUPS, CHUNK, CHUNK)),
            _resident((CHUNK, D_MODEL)),
            pl.BlockSpec(memory_space=pltpu.SMEM),
            _resident((2, N_HEADS, CHUNK, 2 * CHUNK)),
            _resident((D_MODEL, D_MODEL)),
            _resident((D_MODEL, D_MODEL)),
            _resident((D_MODEL, D_MODEL)),
        ],
        out_specs=[row_block, win_block, win_block],
        out_shape=[
            jax.ShapeDtypeStruct((n, D_MODEL), F32),
            jax.ShapeDtypeStruct((batch, WINDOW, KV_WIDTH), F32),
            jax.ShapeDtypeStruct((batch, WINDOW, KV_WIDTH), F32),
        ],
        scratch_shapes=[
            pltpu.VMEM((tb, D_MODEL), F32),
            pltpu.VMEM((WINDOW + tb, KV_WIDTH), BF16),
            pltpu.VMEM((WINDOW + tb, KV_WIDTH), BF16),
            pltpu.VMEM((tb, D_MODEL), BF16),
            pltpu.VMEM((tb, D_MODEL), F32),
            pltpu.VMEM((tb, D_MODEL), BF16),
            pltpu.VMEM((tb, D_MODEL), BF16),
            pltpu.VMEM((SGU_GROUPS, CHUNK, CHUNK), BF16),
        ],
        compiler_params=pltpu.CompilerParams(
            dimension_semantics=("arbitrary",), vmem_limit_bytes=VMEM_LIMIT_BYTES),
        name="mix_prompt",
    )(x2d, ln1, w_in, sgu_g, sgu_w, bexp, sinks, bias, w_oa, w_ob, w_out)


def _ffn_kernel(x_ref, ln2_ref, w_up_ref, w_down_ref, lnf_ref, y_ref):
    x = x_ref[...]
    xn = _rmsnorm(x, ln2_ref[...]).astype(BF16)
    acc = x
    for j in range(D_FF // FF_SLAB):
        h = _dot(xn, w_up_ref[:, j * FF_SLAB:(j + 1) * FF_SLAB])
        h = jnp.square(jnp.maximum(h, 0.0)).astype(BF16)
        acc = acc + _dot(h, w_down_ref[j * FF_SLAB:(j + 1) * FF_SLAB, :])
    y_ref[...] = _rmsnorm(acc, lnf_ref[...])


def _ffn(x2d, ln2, w_up, w_down, lnf, *, block, name):
    n = x2d.shape[0]
    row_block = pl.BlockSpec((block, D_MODEL), lambda i: (i, 0))
    return pl.pallas_call(
        _ffn_kernel,
        grid=(n // block,),
        in_specs=[row_block, _resident((1, D_MODEL)), _resident((D_MODEL, D_FF)),
                  _resident((D_FF, D_MODEL)), _resident((1, D_MODEL))],
        out_specs=row_block,
        out_shape=jax.ShapeDtypeStruct((n, D_MODEL), F32),
        compiler_params=pltpu.CompilerParams(
            dimension_semantics=("arbitrary",), vmem_limit_bytes=VMEM_LIMIT_BYTES),
        name=name,
    )(x2d, ln2, w_up, w_down, lnf)


def _sample_proj_kernel(x_ref, ln1_ref, w_in_ref, sgu_g_ref, wdiag_ref, b0_ref, sel_ref,
                        qsel_ref, knew_ref, vnew_ref, vn_ref, a_ref, ga_ref, gb_ref):
    nb = x_ref.shape[0]
    xn = _rmsnorm(x_ref[...], ln1_ref[...]).astype(BF16)
    u = _gelu_tanh(_dot(xn, w_in_ref[:, OFF_U:OFF_U + D_MODEL]))
    v = _gelu_tanh(_dot(xn, w_in_ref[:, OFF_V:OFF_V + D_MODEL]))
    vn = _rmsnorm(v, sgu_g_ref[...])
    vn_ref[...] = vn
    a_ref[...] = (u * (vn * wdiag_ref[...] + b0_ref[...])).astype(BF16)
    knew_ref[...] = _dot(xn, w_in_ref[:, OFF_K:OFF_K + KV_WIDTH])
    vnew_ref[...] = _dot(xn, w_in_ref[:, OFF_VA:OFF_VA + KV_WIDTH])
    ga_ref[...] = jax.nn.sigmoid(_dot(xn, w_in_ref[:, OFF_GA:OFF_GA + D_MODEL]))
    gb_ref[...] = jax.nn.sigmoid(_dot(xn, w_in_ref[:, OFF_GB:OFF_GB + D_MODEL]))
    q = _dot(xn, w_in_ref[:, OFF_Q:OFF_Q + D_MODEL]) * ATTN_SCALE
    qstack = jnp.concatenate([q[:, g * KV_WIDTH:(g + 1) * KV_WIDTH] for g in range(GROUP)], axis=0).astype(BF16)
    qrep = _dot(sel_ref[...], qstack)
    row_kvh = lax.broadcasted_iota(jnp.int32, (nb * N_HEADS, KV_WIDTH), 0) % N_KV_HEADS
    lane_kvh = lax.broadcasted_iota(jnp.int32, (nb * N_HEADS, KV_WIDTH), 1) // HEAD_DIM
    qsel_ref[...] = jnp.where(row_kvh == lane_kvh, qrep, 0.0).astype(BF16)


def _sample_attn_kernel(qsel_ref, knew_ref, vnew_ref, ck_ref, cv_ref, bias_ref, sink_ref,
                        o_ref, nk_ref, nv_ref):
    bs = ck_ref.shape[0]
    row_kvh = lax.broadcasted_iota(jnp.int32, (N_HEADS, KV_WIDTH), 0) % N_KV_HEADS
    lane_kvh = lax.broadcasted_iota(jnp.int32, (N_HEADS, KV_WIDTH), 1) // HEAD_DIM
    own = row_kvh == lane_kvh
    bias = bias_ref[...]
    sink = sink_ref[...]

    def body(i, carry):
        kc = ck_ref[i]
        vc = cv_ref[i]
        kn = knew_ref[pl.ds(i, 1), :]
        vw = vnew_ref[pl.ds(i, 1), :]
        qs = qsel_ref[pl.ds(pl.multiple_of(i * N_HEADS, N_HEADS), N_HEADS), :]
        s = _dot_nt(qs, kc.astype(BF16)) + bias
        s_new = jnp.sum(qs.astype(F32) * kn, axis=1, keepdims=True)
        m = jnp.maximum(jnp.maximum(jnp.max(s, axis=1, keepdims=True), s_new), sink)
        p = jnp.exp(s - m)
        p_new = jnp.exp(s_new - m)
        denom = jnp.sum(p, axis=1, keepdims=True) + p_new + jnp.exp(sink - m)
        o = (_dot(p.astype(BF16), vc.astype(BF16)) + p_new * vw) / denom
        o_ref[pl.ds(pl.multiple_of(i * N_HEADS, N_HEADS), N_HEADS), :] = jnp.where(own, o, 0.0).astype(BF16)
        nk_ref[i, 0:WINDOW - 1, :] = ck_ref[i, 1:WINDOW, :]
        nk_ref[i, WINDOW - 1:WINDOW, :] = kn
        nv_ref[i, 0:WINDOW - 1, :] = cv_ref[i, 1:WINDOW, :]
        nv_ref[i, WINDOW - 1:WINDOW, :] = vw
        return carry

    lax.fori_loop(0, bs, body, 0)


def _sample_merge_kernel(o_ref, selt_ref, a_ref, ga_ref, gb_ref, x_ref, w_oa_ref, w_ob_ref, w_out_ref, x1_ref):
    nb = x_ref.shape[0]
    bst = _dot(selt_ref[...], o_ref[...]).astype(BF16)
    ob = _dot(bst[0:nb, :], w_ob_ref[0:KV_WIDTH, :])
    for g in range(1, GROUP):
        ob = ob + _dot(bst[g * nb:(g + 1) * nb, :], w_ob_ref[g * KV_WIDTH:(g + 1) * KV_WIDTH, :])
    hm = ga_ref[...] * _dot(a_ref[...], w_oa_ref[...]) + gb_ref[...] * ob
    x1_ref[...] = x_ref[...] + _dot(hm.astype(BF16), w_out_ref[...])


def _whole(shape):
    zeros = (0,) * len(shape)
    return pl.BlockSpec(shape, lambda *_: zeros)


def _sample_path(xs2d, cache_k, cache_v, ln1, w_in, sgu_g, wdiag, b0, sel, selt, bias_s, sink_col,
                 w_oa, w_ob, w_out):
    nb = xs2d.shape[0]
    params = pltpu.CompilerParams(dimension_semantics=("arbitrary",), vmem_limit_bytes=VMEM_LIMIT_BYTES)
    qsel, knew, vnew, vn, a, ga, gb = pl.pallas_call(
        _sample_proj_kernel,
        grid=(1,),
        in_specs=[_whole((nb, D_MODEL)), _whole((1, D_MODEL)), _resident((D_MODEL, IN_WIDTH)),
                  _whole((1, D_MODEL)), _whole((1, D_MODEL)), _whole((1, D_MODEL)),
                  _whole((nb * N_HEADS, GROUP * nb))],
        out_specs=[_whole((nb * N_HEADS, KV_WIDTH)), _whole((nb, KV_WIDTH)), _whole((nb, KV_WIDTH)),
                   _whole((nb, D_MODEL)), _whole((nb, D_MODEL)), _whole((nb, D_MODEL)), _whole((nb, D_MODEL))],
        out_shape=[
            jax.ShapeDtypeStruct((nb * N_HEADS, KV_WIDTH), BF16),
            jax.ShapeDtypeStruct((nb, KV_WIDTH), F32),
            jax.ShapeDtypeStruct((nb, KV_WIDTH), F32),
            jax.ShapeDtypeStruct((nb, D_MODEL), F32),
            jax.ShapeDtypeStruct((nb, D_MODEL), BF16),
            jax.ShapeDtypeStruct((nb, D_MODEL), F32),
            jax.ShapeDtypeStruct((nb, D_MODEL), F32),
        ],
        compiler_params=params,
        name="sample_proj",
    )(xs2d, ln1, w_in, sgu_g, wdiag, b0, sel)

    bs = SAMPLE_BLOCK
    cache_block = pl.BlockSpec((bs, WINDOW, KV_WIDTH), lambda i: (i, 0, 0))
    o, nk, nv = pl.pallas_call(
        _sample_attn_kernel,
        grid=(nb // bs,),
        in_specs=[pl.BlockSpec((bs * N_HEADS, KV_WIDTH), lambda i: (i, 0)),
                  pl.BlockSpec((bs, KV_WIDTH), lambda i: (i, 0)),
                  pl.BlockSpec((bs, KV_WIDTH), lambda i: (i, 0)),
                  cache_block, cache_block,
                  _whole((N_HEADS, WINDOW)), _whole((N_HEADS, 1))],
        out_specs=[pl.BlockSpec((bs * N_HEADS, KV_WIDTH), lambda i: (i, 0)), cache_block, cache_block],
        out_shape=[
            jax.ShapeDtypeStruct((nb * N_HEADS, KV_WIDTH), BF16),
            jax.ShapeDtypeStruct((nb, WINDOW, KV_WIDTH), F32),
            jax.ShapeDtypeStruct((nb, WINDOW, KV_WIDTH), F32),
        ],
        compiler_params=params,
        name="sample_attn",
    )(qsel, knew, vnew, cache_k, cache_v, bias_s, sink_col)

    x1 = pl.pallas_call(
        _sample_merge_kernel,
        grid=(1,),
        in_specs=[_whole((nb * N_HEADS, KV_WIDTH)), _whole((GROUP * nb, nb * N_HEADS)),
                  _whole((nb, D_MODEL)), _whole((nb, D_MODEL)), _whole((nb, D_MODEL)), _whole((nb, D_MODEL)),
                  _resident((D_MODEL, D_MODEL)), _resident((D_MODEL, D_MODEL)), _resident((D_MODEL, D_MODEL))],
        out_specs=_whole((nb, D_MODEL)),
        out_shape=jax.ShapeDtypeStruct((nb, D_MODEL), F32),
        compiler_params=params,
        name="sample_merge",
    )(o, selt, a, ga, gb, xs2d, w_oa, w_ob, w_out)
    return x1, nk, nv, vn


def _head_perm(v):
    return v.reshape(N_KV_HEADS, GROUP).T.reshape(N_HEADS)


def _alibi_slopes():
    h = jnp.arange(1, N_HEADS + 1, dtype=F32)
    return jnp.exp2(-8.0 * h / N_HEADS)


def _selection_matrix(nb):
    r = jnp.arange(nb * N_HEADS)
    c = jnp.arange(GROUP * nb)
    same_sample = (r[:, None] // N_HEADS) == (c[None, :] % nb)
    same_member = ((r[:, None] % N_HEADS) // N_KV_HEADS) == (c[None, :] // nb)
    return (same_sample & same_member).astype(BF16)


def kernel(x_prompt, x_sample, cache_k_win, cache_v_win, ln1_g, w_in, sgu_norm_g, sgu_w, sgu_b, attn_sinks,
           w_oa, w_ob, w_out, ln2_g, w_up, w_down, lnf_g):
    batch, seq, _ = x_prompt.shape
    dec_batch, dec_seq, _ = x_sample.shape
    depth = w_in.shape[0]
    assert depth == 1 and dec_seq == 1
    assert seq % TOKEN_BLOCK == 0 and TOKEN_BLOCK % CHUNK == 0 and dec_batch % SAMPLE_BLOCK == 0

    wi = w_in[0]
    wq = wi[:, OFF_Q:OFF_K].reshape(D_MODEL, N_KV_HEADS, GROUP, HEAD_DIM).transpose(0, 2, 1, 3).reshape(D_MODEL, D_MODEL)
    w_in_b = jnp.concatenate([wi[:, :OFF_Q], wq, wi[:, OFF_K:]], axis=1).astype(BF16)
    w_ob_b = w_ob[0].reshape(N_KV_HEADS, GROUP, HEAD_DIM, D_MODEL).transpose(1, 0, 2, 3).reshape(D_MODEL, D_MODEL).astype(BF16)
    w_oa_b = w_oa[0].astype(BF16)
    w_out_b = w_out[0].astype(BF16)
    w_up_b = w_up[0].astype(BF16)
    w_down_b = w_down[0].astype(BF16)
    ln1 = ln1_g[0].reshape(1, D_MODEL)
    ln2 = ln2_g[0].reshape(1, D_MODEL)
    lnf = lnf_g.reshape(1, D_MODEL)
    sgu_g = sgu_norm_g[0].reshape(1, D_MODEL)
    sinks_p = _head_perm(attn_sinks[0].astype(F32))
    slopes_p = _head_perm(_alibi_slopes())

    qi = jnp.arange(CHUNK)[:, None] + CHUNK
    kj = jnp.arange(2 * CHUNK)[None, :]
    diff = qi - kj
    band = (diff >= 0) & (diff <= WINDOW)
    alibi = -slopes_p[:, None, None] * diff.astype(F32)[None]
    bias = jnp.stack([jnp.where(band[None], alibi, NEG_BIG),
                      jnp.where((band & (kj >= CHUNK))[None], alibi, NEG_BIG)])

    bexp = jnp.repeat(sgu_b[0].T, SGU_GROUP_DIM, axis=1)

    x1, kwin, vwin = _mix_prompt(x_prompt.reshape(batch * seq, D_MODEL), ln1, w_in_b, sgu_g, sgu_w[0], bexp,
                                 sinks_p, bias, w_oa_b, w_ob_b, w_out_b, batch=batch, seq=seq)
    y_prompt = _ffn(x1, ln2, w_up_b, w_down_b, lnf, block=TOKEN_BLOCK, name="ffn_prompt")

    wdiag = jnp.repeat(sgu_w[0][:, 0, 0], SGU_GROUP_DIM).reshape(1, D_MODEL)
    b0 = jnp.repeat(sgu_b[0][:, 0], SGU_GROUP_DIM).reshape(1, D_MODEL)
    sel = _selection_matrix(dec_batch)
    bias_s = -slopes_p[:, None] * (WINDOW - jnp.arange(WINDOW, dtype=F32))[None, :]
    xs1, nk, nv, vn = _sample_path(
        x_sample.reshape(dec_batch, D_MODEL),
        cache_k_win[0].reshape(dec_batch, WINDOW, KV_WIDTH), cache_v_win[0].reshape(dec_batch, WINDOW, KV_WIDTH),
        ln1, w_in_b, sgu_g, wdiag, b0, sel, sel.T, bias_s, sinks_p.reshape(N_HEADS, 1), w_oa_b, w_ob_b, w_out_b)
    y_sample = _ffn(xs1, ln2, w_up_b, w_down_b, lnf, block=dec_batch, name="ffn_sample")

    win_shape = (depth, batch, WINDOW, N_KV_HEADS, HEAD_DIM)
    cache_shape = (depth, dec_batch, WINDOW, N_KV_HEADS, HEAD_DIM)
    return (y_prompt.reshape(batch, seq, D_MODEL),
            y_sample.reshape(dec_batch, dec_seq, D_MODEL),
            kwin.reshape(win_shape), vwin.reshape(win_shape),
            nk.reshape(cache_shape), nv.reshape(cache_shape),
            vn.reshape(depth, dec_batch, dec_seq, D_MODEL))
```

```python
import functools
import math

import numpy as np
import jax
import jax.numpy as jnp
from jax import lax
from jax.experimental import pallas as pl
from jax.experimental.pallas import tpu as pltpu

D_MODEL = 1024
N_HEADS = 16
HEAD_DIM = 64
N_KV_HEADS = 4
GROUP = N_HEADS // N_KV_HEADS
KV_WIDTH = N_KV_HEADS * HEAD_DIM
WINDOW = 128
CHUNK = 128
SGU_GROUPS = 8
SGU_GROUP_DIM = D_MODEL // SGU_GROUPS
D_FF = 4 * D_MODEL
FF_SLAB = 1024
EPS = 1e-6
NEG_BIG = -1e30
ATTN_SCALE = HEAD_DIM ** -0.5
LOG2E = math.log2(math.e)

OFF_Q, OFF_K = 2048, 3072
R_K, R_VA, R_GA, R_GB, R_END = 0, 256, 512, 1536, 2560

TOKEN_BLOCK = 512
SAMPLE_BLOCK = 16
SAMPLE_UNROLL = 4
VMEM_LIMIT_BYTES = 58 * 1024 * 1024

F32 = jnp.float32
BF16 = jnp.bfloat16


def _rmsnorm(x, g):
    ms = jnp.mean(x * x, axis=-1, keepdims=True)
    return x * lax.rsqrt(ms + EPS) * g


def _gelu_tanh(x):
    c = math.sqrt(2.0 / math.pi)
    return x * (0.5 * (1.0 + jnp.tanh(c * (x + 0.044715 * (x * x * x)))))


def _dot(a, b):
    return jnp.dot(a, b, preferred_element_type=F32)


def _dot_nt(a, b):
    return lax.dot_general(a, b, (((1,), (1,)), ((), ())), preferred_element_type=F32)


def _resident(shape):
    zeros = (0,) * len(shape)
    return pl.BlockSpec(shape, lambda *_: zeros, pipeline_mode=pl.Buffered(1))


def _mix_prompt_kernel(x_ref, ln1_ref, w_uv_ref, w_q_ref, w_rest_ref, sgu_g_ref, sgu_w_ref, bexp_ref,
                       sink_ref, bias_ref, w_oa_ref, w_ob_ref, w_out_ref,
                       x1_ref, kwin_ref, vwin_ref,
                       qs_scr, kt_scr, vm_scr, kprev_scr, vprev_scr, vn_scr, u_scr, gate_scr, a_scr, b_scr, wt_scr,
                       *, steps_per_seq):
    step = pl.program_id(0)
    tb = x_ref.shape[0]
    nblk = tb // CHUNK
    first = (step % steps_per_seq) == 0
    rd = step % 2
    wr = 1 - rd

    @pl.when(step == 0)
    def _():
        row = lax.broadcasted_iota(jnp.int32, (CHUNK, CHUNK), 0)
        col = lax.broadcasted_iota(jnp.int32, (CHUNK, CHUNK), 1)
        for g in range(SGU_GROUPS):
            wt_scr[g] = jnp.where(row >= col, sgu_w_ref[g], 0.0).astype(BF16)
        kt_scr[...] = jnp.zeros(kt_scr.shape, BF16)
        vm_scr[...] = jnp.zeros(vm_scr.shape, BF16)
        kprev_scr[...] = jnp.zeros(kprev_scr.shape, BF16)
        vprev_scr[...] = jnp.zeros(vprev_scr.shape, BF16)

    @pl.when(first)
    def _():
        kprev_scr[rd] = jnp.zeros(kprev_scr.shape[1:], BF16)
        vprev_scr[rd] = jnp.zeros(vprev_scr.shape[1:], BF16)

    x = x_ref[...]
    xn = _rmsnorm(x, ln1_ref[...]).astype(BF16)

    q = (_dot(xn, w_q_ref[...]) * (ATTN_SCALE * LOG2E)).astype(BF16)
    for c in range(nblk):
        for g in range(GROUP):
            qs_scr[c, g * CHUNK:(g + 1) * CHUNK, :] = q[c * CHUNK:(c + 1) * CHUNK, g * KV_WIDTH:(g + 1) * KV_WIDTH]

    k = _dot(xn, w_rest_ref[:, R_K:R_VA])
    va = _dot(xn, w_rest_ref[:, R_VA:R_GA])
    kwin_ref[...] = k[tb - WINDOW:, :]
    vwin_ref[...] = va[tb - WINDOW:, :]
    kt = k.T.astype(BF16)
    vab = va.astype(BF16)
    for kvh in range(N_KV_HEADS):
        own = slice(kvh * HEAD_DIM, (kvh + 1) * HEAD_DIM)
        for c in range(nblk):
            kt_scr[kvh, c, own, :] = kt[own, c * CHUNK:(c + 1) * CHUNK]
        vm_scr[kvh, :, own] = vab[:, own]
        kprev_scr[wr, kvh, own, :] = kt[own, tb - WINDOW:]
        vprev_scr[wr, kvh, :, own] = vab[tb - WINDOW:, own]

    first_i = first.astype(jnp.int32)

    def attn_chunk(c):
        rows = slice(c * CHUNK, (c + 1) * CHUNK)
        bias_sel = first_i if c == 0 else 0
        qs = qs_scr[c]
        acc = None
        for kvh in range(N_KV_HEADS):
            if c == 0:
                k_prev, v_band = kprev_scr[rd, kvh], jnp.concatenate([vprev_scr[rd, kvh], vm_scr[kvh, 0:CHUNK, :]], axis=0)
            else:
                k_prev, v_band = kt_scr[kvh, c - 1], vm_scr[kvh, (c - 1) * CHUNK:(c + 1) * CHUNK, :]
            ktb = jnp.concatenate([k_prev, kt_scr[kvh, c]], axis=1)
            s_kvh = _dot(qs, ktb)
            ps = []
            for g in range(GROUP):
                h = g * N_KV_HEADS + kvh
                s = s_kvh[g * CHUNK:(g + 1) * CHUNK, :] + bias_ref[bias_sel, h]
                sink = sink_ref[h]
                m = jnp.max(s, axis=1, keepdims=True)
                p = jnp.exp2(s - m)
                denom = jnp.sum(p, axis=1, keepdims=True) + jnp.exp2(sink - m)
                ps.append((p * (1.0 / denom)).astype(BF16))
            o = _dot(jnp.concatenate(ps, axis=0), v_band)
            acc = o if acc is None else acc + o
        for g in range(GROUP):
            b_scr[rows, g * KV_WIDTH:(g + 1) * KV_WIDTH] = acc[g * CHUNK:(g + 1) * CHUNK, :].astype(BF16)

    def sgu_chunk(c):
        rows = slice(c * CHUNK, (c + 1) * CHUNK)
        vn_c = vn_scr[rows, :]
        mixed = jnp.concatenate(
            [_dot(wt_scr[g], vn_c[:, g * SGU_GROUP_DIM:(g + 1) * SGU_GROUP_DIM]) for g in range(SGU_GROUPS)],
            axis=1) + bexp_ref[...]
        a_scr[rows, :] = (u_scr[rows, :] * mixed).astype(BF16)

    def proj_v():
        v = _gelu_tanh(_dot(xn, w_uv_ref[:, D_MODEL:2 * D_MODEL]))
        vn_scr[...] = _rmsnorm(v, sgu_g_ref[...]).astype(BF16)

    def proj_u():
        u_scr[...] = _gelu_tanh(_dot(xn, w_uv_ref[:, 0:D_MODEL]))

    def proj_ga():
        gate_scr[0] = jax.nn.sigmoid(_dot(xn, w_rest_ref[:, R_GA:R_GB]))

    def proj_gb():
        gate_scr[1] = jax.nn.sigmoid(_dot(xn, w_rest_ref[:, R_GB:R_END]))

    fillers = [proj_v, proj_u, proj_ga, proj_gb]
    for c in range(nblk):
        attn_chunk(c)
        if c < len(fillers):
            fillers[c]()
    for f in fillers[nblk:]:
        f()
    for c in range(nblk):
        sgu_chunk(c)

    hm = gate_scr[0] * _dot(a_scr[...], w_oa_ref[...]) + gate_scr[1] * _dot(b_scr[...], w_ob_ref[...])
    x1_ref[...] = x + _dot(hm.astype(BF16), w_out_ref[...])


def _mix_prompt(x2d, ln1, w_uv, w_q, w_rest, sgu_g, sgu_w, bexp, sinks, bias, w_oa, w_ob, w_out, *, batch, seq):
    n = x2d.shape[0]
    tb = TOKEN_BLOCK
    nblk = tb // CHUNK
    steps_per_seq = seq // tb
    row_block = pl.BlockSpec((tb, D_MODEL), lambda i: (i, 0))
    win_block = pl.BlockSpec((None, WINDOW, KV_WIDTH), lambda i: (i // steps_per_seq, 0, 0))
    return pl.pallas_call(
        functools.partial(_mix_prompt_kernel, steps_per_seq=steps_per_seq),
        grid=(n // tb,),
        in_specs=[
            row_block,
            _resident((1, D_MODEL)),
            _resident((D_MODEL, 2 * D_MODEL)),
            _resident((D_MODEL, D_MODEL)),
            _resident((D_MODEL, R_END)),
            _resident((1, D_MODEL)),
            _resident((SGU_GROUPS, CHUNK, CHUNK)),
            _resident((CHUNK, D_MODEL)),
            pl.BlockSpec(memory_space=pltpu.SMEM),
            _resident((2, N_HEADS, CHUNK, 2 * CHUNK)),
            _resident((D_MODEL, D_MODEL)),
            _resident((D_MODEL, D_MODEL)),
            _resident((D_MODEL, D_MODEL)),
        ],
        out_specs=[row_block, win_block, win_block],
        out_shape=[
            jax.ShapeDtypeStruct((n, D_MODEL), F32),
            jax.ShapeDtypeStruct((batch, WINDOW, KV_WIDTH), F32),
            jax.ShapeDtypeStruct((batch, WINDOW, KV_WIDTH), F32),
        ],
        scratch_shapes=[
            pltpu.VMEM((nblk, GROUP * CHUNK, KV_WIDTH), BF16),
            pltpu.VMEM((N_KV_HEADS, nblk, KV_WIDTH, CHUNK), BF16),
            pltpu.VMEM((N_KV_HEADS, tb, KV_WIDTH), BF16),
            pltpu.VMEM((2, N_KV_HEADS, KV_WIDTH, CHUNK), BF16),
            pltpu.VMEM((2, N_KV_HEADS, WINDOW, KV_WIDTH), BF16),
            pltpu.VMEM((tb, D_MODEL), BF16),
            pltpu.VMEM((tb, D_MODEL), F32),
            pltpu.VMEM((2, tb, D_MODEL), F32),
            pltpu.VMEM((tb, D_MODEL), BF16),
            pltpu.VMEM((tb, D_MODEL), BF16),
            pltpu.VMEM((SGU_GROUPS, CHUNK, CHUNK), BF16),
        ],
        compiler_params=pltpu.CompilerParams(
            dimension_semantics=("arbitrary",), vmem_limit_bytes=VMEM_LIMIT_BYTES),
        name="mix_prompt",
    )(x2d, ln1, w_uv, w_q, w_rest, sgu_g, sgu_w, bexp, sinks, bias, w_oa, w_ob, w_out)


def _ffn_kernel(x_ref, ln2_ref, w_up_ref, w_down_ref, lnf_ref, y_ref):
    x = x_ref[...]
    xn = _rmsnorm(x, ln2_ref[...]).astype(BF16)
    acc = x
    for j in range(D_FF // FF_SLAB):
        h = _dot(xn, w_up_ref[:, j * FF_SLAB:(j + 1) * FF_SLAB])
        h = jnp.square(jnp.maximum(h, 0.0)).astype(BF16)
        acc = acc + _dot(h, w_down_ref[j * FF_SLAB:(j + 1) * FF_SLAB, :])
    y_ref[...] = _rmsnorm(acc, lnf_ref[...])


def _ffn(x2d, ln2, w_up, w_down, lnf, *, block, name):
    n = x2d.shape[0]
    row_block = pl.BlockSpec((block, D_MODEL), lambda i: (i, 0))
    return pl.pallas_call(
        _ffn_kernel,
        grid=(n // block,),
        in_specs=[row_block, _resident((1, D_MODEL)), _resident((D_MODEL, D_FF)),
                  _resident((D_FF, D_MODEL)), _resident((1, D_MODEL))],
        out_specs=row_block,
        out_shape=jax.ShapeDtypeStruct((n, D_MODEL), F32),
        compiler_params=pltpu.CompilerParams(
            dimension_semantics=("arbitrary",), vmem_limit_bytes=VMEM_LIMIT_BYTES),
        name=name,
    )(x2d, ln2, w_up, w_down, lnf)


def _sample_proj_kernel(x_ref, ln1_ref, w_uv_ref, w_q_ref, w_rest_ref, sgu_g_ref, wdiag_ref, b0_ref, sel_ref,
                        qsel_ref, knew_ref, vnew_ref, vn_ref, a_ref, ga_ref, gb_ref):
    nb = x_ref.shape[0]
    xn = _rmsnorm(x_ref[...], ln1_ref[...]).astype(BF16)
    u = _gelu_tanh(_dot(xn, w_uv_ref[:, 0:D_MODEL]))
    v = _gelu_tanh(_dot(xn, w_uv_ref[:, D_MODEL:2 * D_MODEL]))
    vn = _rmsnorm(v, sgu_g_ref[...])
    vn_ref[...] = vn
    a_ref[...] = (u * (vn * wdiag_ref[...] + b0_ref[...])).astype(BF16)
    knew_ref[...] = _dot(xn, w_rest_ref[:, R_K:R_VA])
    vnew_ref[...] = _dot(xn, w_rest_ref[:, R_VA:R_GA])
    ga_ref[...] = jax.nn.sigmoid(_dot(xn, w_rest_ref[:, R_GA:R_GB]))
    gb_ref[...] = jax.nn.sigmoid(_dot(xn, w_rest_ref[:, R_GB:R_END]))
    q = _dot(xn, w_q_ref[...]) * ATTN_SCALE
    qstack = jnp.concatenate([q[:, g * KV_WIDTH:(g + 1) * KV_WIDTH] for g in range(GROUP)], axis=0).astype(BF16)
    qrep = _dot(sel_ref[...], qstack)
    row_kvh = lax.broadcasted_iota(jnp.int32, (nb * N_HEADS, KV_WIDTH), 0) % N_KV_HEADS
    lane_kvh = lax.broadcasted_iota(jnp.int32, (nb * N_HEADS, KV_WIDTH), 1) // HEAD_DIM
    qsel_ref[...] = jnp.where(row_kvh == lane_kvh, qrep, 0.0).astype(BF16)


def _sample_attn_kernel(qsel_ref, knew_ref, vnew_ref, ck_ref, cv_ref, bias_ref, sink_ref,
                        o_ref, nk_ref, nv_ref):
    bs = ck_ref.shape[0]
    row_kvh = lax.broadcasted_iota(jnp.int32, (N_HEADS, KV_WIDTH), 0) % N_KV_HEADS
    lane_kvh = lax.broadcasted_iota(jnp.int32, (N_HEADS, KV_WIDTH), 1) // HEAD_DIM
    own = row_kvh == lane_kvh
    bias = bias_ref[...]
    sink = sink_ref[...]

    def body(i, carry):
        kc = ck_ref[i]
        vc = cv_ref[i]
        kn = knew_ref[pl.ds(i, 1), :]
        vw = vnew_ref[pl.ds(i, 1), :]
        qs = qsel_ref[pl.ds(pl.multiple_of(i * N_HEADS, N_HEADS), N_HEADS), :]
        s = _dot_nt(qs, kc.astype(BF16)) + bias
        s_new = jnp.sum(qs.astype(F32) * kn, axis=1, keepdims=True)
        m = jnp.maximum(jnp.maximum(jnp.max(s, axis=1, keepdims=True), s_new), sink)
        p = jnp.exp(s - m)
        p_new = jnp.exp(s_new - m)
        denom = jnp.sum(p, axis=1, keepdims=True) + p_new + jnp.exp(sink - m)
        o = (_dot(p.astype(BF16), vc.astype(BF16)) + p_new * vw) / denom
        o_ref[pl.ds(pl.multiple_of(i * N_HEADS, N_HEADS), N_HEADS), :] = jnp.where(own, o, 0.0).astype(BF16)
        nk_ref[i, 0:WINDOW - 1, :] = ck_ref[i, 1:WINDOW, :]
        nk_ref[i, WINDOW - 1:WINDOW, :] = kn
        nv_ref[i, 0:WINDOW - 1, :] = cv_ref[i, 1:WINDOW, :]
        nv_ref[i, WINDOW - 1:WINDOW, :] = vw
        return carry

    lax.fori_loop(0, bs, body, 0, unroll=SAMPLE_UNROLL)


def _sample_merge_kernel(o_ref, selt_ref, a_ref, ga_ref, gb_ref, x_ref, w_oa_ref, w_ob_ref, w_out_ref, x1_ref):
    nb = x_ref.shape[0]
    bst = _dot(selt_ref[...], o_ref[...]).astype(BF16)
    ob = _dot(bst[0:nb, :], w_ob_ref[0:KV_WIDTH, :])
    for g in range(1, GROUP):
        ob = ob + _dot(bst[g * nb:(g + 1) * nb, :], w_ob_ref[g * KV_WIDTH:(g + 1) * KV_WIDTH, :])
    hm = ga_ref[...] * _dot(a_ref[...], w_oa_ref[...]) + gb_ref[...] * ob
    x1_ref[...] = x_ref[...] + _dot(hm.astype(BF16), w_out_ref[...])


def _whole(shape):
    zeros = (0,) * len(shape)
    return pl.BlockSpec(shape, lambda *_: zeros)


def _sample_path(xs2d, cache_k, cache_v, ln1, w_uv, w_q, w_rest, sgu_g, wdiag, b0, sel, selt, bias_s, sink_col,
                 w_oa, w_ob, w_out):
    nb = xs2d.shape[0]
    params = pltpu.CompilerParams(dimension_semantics=("arbitrary",), vmem_limit_bytes=VMEM_LIMIT_BYTES)
    qsel, knew, vnew, vn, a, ga, gb = pl.pallas_call(
        _sample_proj_kernel,
        grid=(1,),
        in_specs=[_whole((nb, D_MODEL)), _whole((1, D_MODEL)),
                  _resident((D_MODEL, 2 * D_MODEL)), _resident((D_MODEL, D_MODEL)), _resident((D_MODEL, R_END)),
                  _whole((1, D_MODEL)), _whole((1, D_MODEL)), _whole((1, D_MODEL)),
                  _whole((nb * N_HEADS, GROUP * nb))],
        out_specs=[_whole((nb * N_HEADS, KV_WIDTH)), _whole((nb, KV_WIDTH)), _whole((nb, KV_WIDTH)),
                   _whole((nb, D_MODEL)), _whole((nb, D_MODEL)), _whole((nb, D_MODEL)), _whole((nb, D_MODEL))],
        out_shape=[
            jax.ShapeDtypeStruct((nb * N_HEADS, KV_WIDTH), BF16),
            jax.ShapeDtypeStruct((nb, KV_WIDTH), F32),
            jax.ShapeDtypeStruct((nb, KV_WIDTH), F32),
            jax.ShapeDtypeStruct((nb, D_MODEL), F32),
            jax.ShapeDtypeStruct((nb, D_MODEL), BF16),
            jax.ShapeDtypeStruct((nb, D_MODEL), F32),
            jax.ShapeDtypeStruct((nb, D_MODEL), F32),
        ],
        compiler_params=params,
        name="sample_proj",
    )(xs2d, ln1, w_uv, w_q, w_rest, sgu_g, wdiag, b0, sel)

    bs = SAMPLE_BLOCK
    cache_block = pl.BlockSpec((bs, WINDOW, KV_WIDTH), lambda i: (i, 0, 0))
    o, nk, nv = pl.pallas_call(
        _sample_attn_kernel,
        grid=(nb // bs,),
        in_specs=[pl.BlockSpec((bs * N_HEADS, KV_WIDTH), lambda i: (i, 0)),
                  pl.BlockSpec((bs, KV_WIDTH), lambda i: (i, 0)),
                  pl.BlockSpec((bs, KV_WIDTH), lambda i: (i, 0)),
                  cache_block, cache_block,
                  _whole((N_HEADS, WINDOW)), _whole((N_HEADS, 1))],
        out_specs=[pl.BlockSpec((bs * N_HEADS, KV_WIDTH), lambda i: (i, 0)), cache_block, cache_block],
        out_shape=[
            jax.ShapeDtypeStruct((nb * N_HEADS, KV_WIDTH), BF16),
            jax.ShapeDtypeStruct((nb, WINDOW, KV_WIDTH), F32),
            jax.ShapeDtypeStruct((nb, WINDOW, KV_WIDTH), F32),
        ],
        compiler_params=params,
        name="sample_attn",
    )(qsel, knew, vnew, cache_k, cache_v, bias_s, sink_col)

    x1 = pl.pallas_call(
        _sample_merge_kernel,
        grid=(1,),
        in_specs=[_whole((nb * N_HEADS, KV_WIDTH)), _whole((GROUP * nb, nb * N_HEADS)),
                  _whole((nb, D_MODEL)), _whole((nb, D_MODEL)), _whole((nb, D_MODEL)), _whole((nb, D_MODEL)),
                  _resident((D_MODEL, D_MODEL)), _resident((D_MODEL, D_MODEL)), _resident((D_MODEL, D_MODEL))],
        out_specs=_whole((nb, D_MODEL)),
        out_shape=jax.ShapeDtypeStruct((nb, D_MODEL), F32),
        compiler_params=params,
        name="sample_merge",
    )(o, selt, a, ga, gb, xs2d, w_oa, w_ob, w_out)
    return x1, nk, nv, vn


def _head_perm(v):
    return v.reshape(N_KV_HEADS, GROUP).T.reshape(N_HEADS)


def _alibi_slopes():
    h = jnp.arange(1, N_HEADS + 1, dtype=F32)
    return jnp.exp2(-8.0 * h / N_HEADS)


def _selection_matrix(nb):
    r = np.arange(nb * N_HEADS)
    c = np.arange(GROUP * nb)
    same_sample = (r[:, None] // N_HEADS) == (c[None, :] % nb)
    same_member = ((r[:, None] % N_HEADS) // N_KV_HEADS) == (c[None, :] // nb)
    return (same_sample & same_member).astype(np.float32)


def kernel(x_prompt, x_sample, cache_k_win, cache_v_win, ln1_g, w_in, sgu_norm_g, sgu_w, sgu_b, attn_sinks,
           w_oa, w_ob, w_out, ln2_g, w_up, w_down, lnf_g):
    batch, seq, _ = x_prompt.shape
    dec_batch, dec_seq, _ = x_sample.shape
    depth = w_in.shape[0]
    assert depth == 1 and dec_seq == 1
    assert seq % TOKEN_BLOCK == 0 and TOKEN_BLOCK % CHUNK == 0 and dec_batch % SAMPLE_BLOCK == 0

    wi = w_in[0]
    w_uv_b = wi[:, :OFF_Q].astype(BF16)
    w_q_b = wi[:, OFF_Q:OFF_K].reshape(D_MODEL, N_KV_HEADS, GROUP, HEAD_DIM).transpose(0, 2, 1, 3).reshape(
        D_MODEL, D_MODEL).astype(BF16)
    w_rest_b = wi[:, OFF_K:].astype(BF16)
    w_ob_b = w_ob[0].reshape(N_KV_HEADS, GROUP, HEAD_DIM, D_MODEL).transpose(1, 0, 2, 3).reshape(D_MODEL, D_MODEL).astype(BF16)
    w_oa_b = w_oa[0].astype(BF16)
    w_out_b = w_out[0].astype(BF16)
    w_up_b = w_up[0].astype(BF16)
    w_down_b = w_down[0].astype(BF16)
    ln1 = ln1_g[0].reshape(1, D_MODEL)
    ln2 = ln2_g[0].reshape(1, D_MODEL)
    lnf = lnf_g.reshape(1, D_MODEL)
    sgu_g = sgu_norm_g[0].reshape(1, D_MODEL)
    sinks_p = _head_perm(attn_sinks[0].astype(F32))
    slopes_p = _head_perm(_alibi_slopes())

    qi = np.arange(CHUNK)[:, None] + CHUNK
    kj = np.arange(2 * CHUNK)[None, :]
    diff = qi - kj
    band = (diff >= 0) & (diff <= WINDOW)
    alibi = -(slopes_p * LOG2E)[:, None, None] * jnp.asarray(diff, F32)[None]
    bias = jnp.stack([jnp.where(band[None], alibi, NEG_BIG),
                      jnp.where((band & (kj >= CHUNK))[None], alibi, NEG_BIG)])

    bexp = jnp.repeat(sgu_b[0].T, SGU_GROUP_DIM, axis=1)

    x1, kwin, vwin = _mix_prompt(x_prompt.reshape(batch * seq, D_MODEL), ln1, w_uv_b, w_q_b, w_rest_b, sgu_g,
                                 sgu_w[0], bexp, sinks_p * LOG2E, bias, w_oa_b, w_ob_b, w_out_b, batch=batch, seq=seq)
    y_prompt = _ffn(x1, ln2, w_up_b, w_down_b, lnf, block=TOKEN_BLOCK, name="ffn_prompt")

    wdiag = jnp.repeat(sgu_w[0][:, 0, 0], SGU_GROUP_DIM).reshape(1, D_MODEL)
    b0 = jnp.repeat(sgu_b[0][:, 0], SGU_GROUP_DIM).reshape(1, D_MODEL)
    sel_np = _selection_matrix(dec_batch)
    sel = jnp.asarray(sel_np, BF16)
    selt = jnp.asarray(sel_np.T, BF16)
    bias_s = -slopes_p[:, None] * jnp.asarray(WINDOW - np.arange(WINDOW), F32)[None, :]
    xs1, nk, nv, vn = _sample_path(
        x_sample.reshape(dec_batch, D_MODEL),
        cache_k_win[0].reshape(dec_batch, WINDOW, KV_WIDTH), cache_v_win[0].reshape(dec_batch, WINDOW, KV_WIDTH),
        ln1, w_uv_b, w_q_b, w_rest_b, sgu_g, wdiag, b0, sel, selt, bias_s, sinks_p.reshape(N_HEADS, 1),
        w_oa_b, w_ob_b, w_out_b)
    y_sample = _ffn(xs1, ln2, w_up_b, w_down_b, lnf, block=dec_batch, name="ffn_sample")

    win_shape = (depth, batch, WINDOW, N_KV_HEADS, HEAD_DIM)
    cache_shape = (depth, dec_batch, WINDOW, N_KV_HEADS, HEAD_DIM)
    return (y_prompt.reshape(batch, seq, D_MODEL),
            y_sample.reshape(dec_batch, dec_seq, D_MODEL),
            kwin.reshape(win_shape), vwin.reshape(win_shape),
            nk.reshape(cache_shape), nv.reshape(cache_shape),
            vn.reshape(depth, dec_batch, dec_seq, D_MODEL))
```

```python
import functools
import math

import numpy as np
import jax
import jax.numpy as jnp
from jax import lax
from jax.experimental import pallas as pl
from jax.experimental.pallas import tpu as pltpu

D_MODEL = 1024
N_HEADS = 16
HEAD_DIM = 64
N_KV_HEADS = 4
GROUP = N_HEADS // N_KV_HEADS
KV_WIDTH = N_KV_HEADS * HEAD_DIM
WINDOW = 128
CHUNK = 128
SGU_GROUPS = 8
SGU_GROUP_DIM = D_MODEL // SGU_GROUPS
D_FF = 4 * D_MODEL
FF_SLAB = 1024
EPS = 1e-6
NEG_BIG = -1e30
ATTN_SCALE = HEAD_DIM ** -0.5
LOG2E = math.log2(math.e)

OFF_Q, OFF_K = 2048, 3072
R_K, R_VA, R_GA, R_GB, R_END = 0, 256, 512, 1536, 2560

TOKEN_BLOCK = 512
SAMPLE_BLOCK = 16
SAMPLE_UNROLL = 4
VMEM_LIMIT_BYTES = 58 * 1024 * 1024

F32 = jnp.float32
BF16 = jnp.bfloat16


def _rmsnorm(x, g):
    ms = jnp.mean(x * x, axis=-1, keepdims=True)
    return x * lax.rsqrt(ms + EPS) * g


def _gelu_tanh(x):
    c = math.sqrt(2.0 / math.pi)
    return x * (0.5 * (1.0 + jnp.tanh(c * (x + 0.044715 * (x * x * x)))))


def _dot(a, b):
    return jnp.dot(a, b, preferred_element_type=F32)


def _dot_nt(a, b):
    return lax.dot_general(a, b, (((1,), (1,)), ((), ())), preferred_element_type=F32)


def _resident(shape):
    zeros = (0,) * len(shape)
    return pl.BlockSpec(shape, lambda *_: zeros, pipeline_mode=pl.Buffered(1))


def _mix_prompt_kernel(x_ref, ln1_ref, w_uv_ref, w_q_ref, w_rest_ref, sgu_g_ref, sgu_w_ref, bexp_ref,
                       sink_ref, bias_ref, w_oa_ref, w_ob_ref, w_out_ref,
                       x1_ref, kwin_ref, vwin_ref,
                       qs_scr, kt_scr, vm_scr, kprev_scr, vprev_scr, vn_scr, u_scr, gate_scr, a_scr, b_scr, wt_scr,
                       *, steps_per_seq):
    step = pl.program_id(0)
    tb = x_ref.shape[0]
    nblk = tb // CHUNK
    first = (step % steps_per_seq) == 0
    rd = step % 2
    wr = 1 - rd

    @pl.when(step == 0)
    def _():
        row = lax.broadcasted_iota(jnp.int32, (CHUNK, CHUNK), 0)
        col = lax.broadcasted_iota(jnp.int32, (CHUNK, CHUNK), 1)
        for g in range(SGU_GROUPS):
            wt_scr[g] = jnp.where(row >= col, sgu_w_ref[g], 0.0).astype(BF16)
        kt_scr[...] = jnp.zeros(kt_scr.shape, BF16)
        vm_scr[...] = jnp.zeros(vm_scr.shape, BF16)
        kprev_scr[...] = jnp.zeros(kprev_scr.shape, BF16)
        vprev_scr[...] = jnp.zeros(vprev_scr.shape, BF16)

    @pl.when(first)
    def _():
        kprev_scr[rd] = jnp.zeros(kprev_scr.shape[1:], BF16)
        vprev_scr[rd] = jnp.zeros(vprev_scr.shape[1:], BF16)

    x = x_ref[...]
    xn = _rmsnorm(x, ln1_ref[...]).astype(BF16)

    q = (_dot(xn, w_q_ref[...]) * (ATTN_SCALE * LOG2E)).astype(BF16)
    for c in range(nblk):
        for g in range(GROUP):
            qs_scr[c, g * CHUNK:(g + 1) * CHUNK, :] = q[c * CHUNK:(c + 1) * CHUNK, g * KV_WIDTH:(g + 1) * KV_WIDTH]

    k = _dot(xn, w_rest_ref[:, R_K:R_VA])
    va = _dot(xn, w_rest_ref[:, R_VA:R_GA])
    kwin_ref[...] = k[tb - WINDOW:, :]
    vwin_ref[...] = va[tb - WINDOW:, :]
    kt = k.T.astype(BF16)
    vab = va.astype(BF16)
    for kvh in range(N_KV_HEADS):
        own = slice(kvh * HEAD_DIM, (kvh + 1) * HEAD_DIM)
        for c in range(nblk):
            kt_scr[kvh, c, own, :] = kt[own, c * CHUNK:(c + 1) * CHUNK]
        vm_scr[kvh, :, own] = vab[:, own]
        kprev_scr[wr, kvh, own, :] = kt[own, tb - WINDOW:]
        vprev_scr[wr, kvh, :, own] = vab[tb - WINDOW:, own]

    first_i = first.astype(jnp.int32)

    def attn_scores(c):
        qs = qs_scr[c]
        out = []
        for kvh in range(N_KV_HEADS):
            k_prev = kprev_scr[rd, kvh] if c == 0 else kt_scr[kvh, c - 1]
            out.append(_dot(qs, jnp.concatenate([k_prev, kt_scr[kvh, c]], axis=1)))
        return out

    def attn_softmax(c, scores):
        bias_sel = first_i if c == 0 else 0
        out = []
        for kvh in range(N_KV_HEADS):
            ps = []
            for g in range(GROUP):
                h = g * N_KV_HEADS + kvh
                s = scores[kvh][g * CHUNK:(g + 1) * CHUNK, :] + bias_ref[bias_sel, h]
                sink = sink_ref[h]
                m = jnp.max(s, axis=1, keepdims=True)
                p = jnp.exp2(s - m)
                denom = jnp.sum(p, axis=1, keepdims=True) + jnp.exp2(sink - m)
                ps.append((p * (1.0 / denom)).astype(BF16))
            out.append(jnp.concatenate(ps, axis=0))
        return out

    def attn_values(c, probs):
        rows = slice(c * CHUNK, (c + 1) * CHUNK)
        acc = None
        for kvh in range(N_KV_HEADS):
            if c == 0:
                v_band = jnp.concatenate([vprev_scr[rd, kvh], vm_scr[kvh, 0:CHUNK, :]], axis=0)
            else:
                v_band = vm_scr[kvh, (c - 1) * CHUNK:(c + 1) * CHUNK, :]
            o = _dot(probs[kvh], v_band)
            acc = o if acc is None else acc + o
        for g in range(GROUP):
            b_scr[rows, g * KV_WIDTH:(g + 1) * KV_WIDTH] = acc[g * CHUNK:(g + 1) * CHUNK, :].astype(BF16)

    def sgu_chunk(c):
        rows = slice(c * CHUNK, (c + 1) * CHUNK)
        vn_c = vn_scr[rows, :]
        mixed = jnp.concatenate(
            [_dot(wt_scr[g], vn_c[:, g * SGU_GROUP_DIM:(g + 1) * SGU_GROUP_DIM]) for g in range(SGU_GROUPS)],
            axis=1) + bexp_ref[...]
        a_scr[rows, :] = (u_scr[rows, :] * mixed).astype(BF16)

    def tail_v(h):
        vn_scr[...] = _rmsnorm(_gelu_tanh(h), sgu_g_ref[...]).astype(BF16)

    def tail_u(h):
        u_scr[...] = _gelu_tanh(h)

    def tail_ga(h):
        gate_scr[0] = jax.nn.sigmoid(h)

    def tail_gb(h):
        gate_scr[1] = jax.nn.sigmoid(h)

    fillers = [
        (lambda: _dot(xn, w_uv_ref[:, D_MODEL:2 * D_MODEL]), tail_v),
        (lambda: _dot(xn, w_uv_ref[:, 0:D_MODEL]), tail_u),
        (lambda: _dot(xn, w_rest_ref[:, R_GA:R_GB]), tail_ga),
        (lambda: _dot(xn, w_rest_ref[:, R_GB:R_END]), tail_gb),
    ]
    for c in range(nblk):
        scores = attn_scores(c)
        proj = fillers[c][0]() if c < len(fillers) else None
        probs = attn_softmax(c, scores)
        attn_values(c, probs)
        if proj is not None:
            fillers[c][1](proj)
    for matmul, tail in fillers[nblk:]:
        tail(matmul())
    for c in range(nblk):
        sgu_chunk(c)

    hm = gate_scr[0] * _dot(a_scr[...], w_oa_ref[...]) + gate_scr[1] * _dot(b_scr[...], w_ob_ref[...])
    x1_ref[...] = x + _dot(hm.astype(BF16), w_out_ref[...])


def _mix_prompt(x2d, ln1, w_uv, w_q, w_rest, sgu_g, sgu_w, bexp, sinks, bias, w_oa, w_ob, w_out, *, batch, seq):
    n = x2d.shape[0]
    tb = TOKEN_BLOCK
    nblk = tb // CHUNK
    steps_per_seq = seq // tb
    row_block = pl.BlockSpec((tb, D_MODEL), lambda i: (i, 0))
    win_block = pl.BlockSpec((None, WINDOW, KV_WIDTH), lambda i: (i // steps_per_seq, 0, 0))
    return pl.pallas_call(
        functools.partial(_mix_prompt_kernel, steps_per_seq=steps_per_seq),
        grid=(n // tb,),
        in_specs=[
            row_block,
            _resident((1, D_MODEL)),
            _resident((D_MODEL, 2 * D_MODEL)),
            _resident((D_MODEL, D_MODEL)),
            _resident((D_MODEL, R_END)),
            _resident((1, D_MODEL)),
            _resident((SGU_GROUPS, CHUNK, CHUNK)),
            _resident((CHUNK, D_MODEL)),
            pl.BlockSpec(memory_space=pltpu.SMEM),
            _resident((2, N_HEADS, CHUNK, 2 * CHUNK)),
            _resident((D_MODEL, D_MODEL)),
            _resident((D_MODEL, D_MODEL)),
            _resident((D_MODEL, D_MODEL)),
        ],
        out_specs=[row_block, win_block, win_block],
        out_shape=[
            jax.ShapeDtypeStruct((n, D_MODEL), F32),
            jax.ShapeDtypeStruct((batch, WINDOW, KV_WIDTH), F32),
            jax.ShapeDtypeStruct((batch, WINDOW, KV_WIDTH), F32),
        ],
        scratch_shapes=[
            pltpu.VMEM((nblk, GROUP * CHUNK, KV_WIDTH), BF16),
            pltpu.VMEM((N_KV_HEADS, nblk, KV_WIDTH, CHUNK), BF16),
            pltpu.VMEM((N_KV_HEADS, tb, KV_WIDTH), BF16),
            pltpu.VMEM((2, N_KV_HEADS, KV_WIDTH, CHUNK), BF16),
            pltpu.VMEM((2, N_KV_HEADS, WINDOW, KV_WIDTH), BF16),
            pltpu.VMEM((tb, D_MODEL), BF16),
            pltpu.VMEM((tb, D_MODEL), F32),
            pltpu.VMEM((2, tb, D_MODEL), F32),
            pltpu.VMEM((tb, D_MODEL), BF16),
            pltpu.VMEM((tb, D_MODEL), BF16),
            pltpu.VMEM((SGU_GROUPS, CHUNK, CHUNK), BF16),
        ],
        compiler_params=pltpu.CompilerParams(
            dimension_semantics=("arbitrary",), vmem_limit_bytes=VMEM_LIMIT_BYTES),
        name="mix_prompt",
    )(x2d, ln1, w_uv, w_q, w_rest, sgu_g, sgu_w, bexp, sinks, bias, w_oa, w_ob, w_out)


def _ffn_kernel(x_ref, ln2_ref, w_up_ref, w_down_ref, lnf_ref, y_ref):
    x = x_ref[...]
    xn = _rmsnorm(x, ln2_ref[...]).astype(BF16)
    acc = x
    for j in range(D_FF // FF_SLAB):
        h = _dot(xn, w_up_ref[:, j * FF_SLAB:(j + 1) * FF_SLAB])
        h = jnp.square(jnp.maximum(h, 0.0)).astype(BF16)
        acc = acc + _dot(h, w_down_ref[j * FF_SLAB:(j + 1) * FF_SLAB, :])
    y_ref[...] = _rmsnorm(acc, lnf_ref[...])


def _ffn(x2d, ln2, w_up, w_down, lnf, *, block, name):
    n = x2d.shape[0]
    row_block = pl.BlockSpec((block, D_MODEL), lambda i: (i, 0))
    return pl.pallas_call(
        _ffn_kernel,
        grid=(n // block,),
        in_specs=[row_block, _resident((1, D_MODEL)), _resident((D_MODEL, D_FF)),
                  _resident((D_FF, D_MODEL)), _resident((1, D_MODEL))],
        out_specs=row_block,
        out_shape=jax.ShapeDtypeStruct((n, D_MODEL), F32),
        compiler_params=pltpu.CompilerParams(
            dimension_semantics=("arbitrary",), vmem_limit_bytes=VMEM_LIMIT_BYTES),
        name=name,
    )(x2d, ln2, w_up, w_down, lnf)


def _sample_proj_kernel(x_ref, ln1_ref, w_uv_ref, w_q_ref, w_rest_ref, sgu_g_ref, wdiag_ref, b0_ref, sel_ref,
                        qsel_ref, knew_ref, vnew_ref, vn_ref, a_ref, ga_ref, gb_ref):
    nb = x_ref.shape[0]
    xn = _rmsnorm(x_ref[...], ln1_ref[...]).astype(BF16)
    u = _gelu_tanh(_dot(xn, w_uv_ref[:, 0:D_MODEL]))
    v = _gelu_tanh(_dot(xn, w_uv_ref[:, D_MODEL:2 * D_MODEL]))
    vn = _rmsnorm(v, sgu_g_ref[...])
    vn_ref[...] = vn
    a_ref[...] = (u * (vn * wdiag_ref[...] + b0_ref[...])).astype(BF16)
    knew_ref[...] = _dot(xn, w_rest_ref[:, R_K:R_VA])
    vnew_ref[...] = _dot(xn, w_rest_ref[:, R_VA:R_GA])
    ga_ref[...] = jax.nn.sigmoid(_dot(xn, w_rest_ref[:, R_GA:R_GB]))
    gb_ref[...] = jax.nn.sigmoid(_dot(xn, w_rest_ref[:, R_GB:R_END]))
    q = _dot(xn, w_q_ref[...]) * ATTN_SCALE
    qstack = jnp.concatenate([q[:, g * KV_WIDTH:(g + 1) * KV_WIDTH] for g in range(GROUP)], axis=0).astype(BF16)
    qrep = _dot(sel_ref[...], qstack)
    row_kvh = lax.broadcasted_iota(jnp.int32, (nb * N_HEADS, KV_WIDTH), 0) % N_KV_HEADS
    lane_kvh = lax.broadcasted_iota(jnp.int32, (nb * N_HEADS, KV_WIDTH), 1) // HEAD_DIM
    qsel_ref[...] = jnp.where(row_kvh == lane_kvh, qrep, 0.0).astype(BF16)


def _sample_attn_kernel(qsel_ref, knew_ref, vnew_ref, ck_ref, cv_ref, bias_ref, sink_ref,
                        o_ref, nk_ref, nv_ref):
    bs = ck_ref.shape[0]
    row_kvh = lax.broadcasted_iota(jnp.int32, (N_HEADS, KV_WIDTH), 0) % N_KV_HEADS
    lane_kvh = lax.broadcasted_iota(jnp.int32, (N_HEADS, KV_WIDTH), 1) // HEAD_DIM
    own = row_kvh == lane_kvh
    bias = bias_ref[...]
    sink = sink_ref[...]

    qss = [qsel_ref[i * N_HEADS:(i + 1) * N_HEADS, :] for i in range(bs)]
    kns = [knew_ref[i:i + 1, :] for i in range(bs)]
    vws = [vnew_ref[i:i + 1, :] for i in range(bs)]
    scores = [_dot_nt(qss[i], ck_ref[i].astype(BF16)) + bias for i in range(bs)]
    probs = []
    for i in range(bs):
        s = scores[i]
        s_new = jnp.sum(qss[i].astype(F32) * kns[i], axis=1, keepdims=True)
        m = jnp.maximum(jnp.maximum(jnp.max(s, axis=1, keepdims=True), s_new), sink)
        p = jnp.exp(s - m)
        p_new = jnp.exp(s_new - m)
        denom = jnp.sum(p, axis=1, keepdims=True) + p_new + jnp.exp(sink - m)
        probs.append((p.astype(BF16), p_new, denom))
    for i in range(bs):
        p, p_new, denom = probs[i]
        o = (_dot(p, cv_ref[i].astype(BF16)) + p_new * vws[i]) / denom
        o_ref[i * N_HEADS:(i + 1) * N_HEADS, :] = jnp.where(own, o, 0.0).astype(BF16)
    for i in range(bs):
        nk_ref[i, 0:WINDOW - 1, :] = ck_ref[i, 1:WINDOW, :]
        nk_ref[i, WINDOW - 1:WINDOW, :] = kns[i]
        nv_ref[i, 0:WINDOW - 1, :] = cv_ref[i, 1:WINDOW, :]
        nv_ref[i, WINDOW - 1:WINDOW, :] = vws[i]


def _sample_merge_kernel(o_ref, selt_ref, a_ref, ga_ref, gb_ref, x_ref, w_oa_ref, w_ob_ref, w_out_ref, x1_ref):
    nb = x_ref.shape[0]
    bst = _dot(selt_ref[...], o_ref[...]).astype(BF16)
    ob = _dot(bst[0:nb, :], w_ob_ref[0:KV_WIDTH, :])
    for g in range(1, GROUP):
        ob = ob + _dot(bst[g * nb:(g + 1) * nb, :], w_ob_ref[g * KV_WIDTH:(g + 1) * KV_WIDTH, :])
    hm = ga_ref[...] * _dot(a_ref[...], w_oa_ref[...]) + gb_ref[...] * ob
    x1_ref[...] = x_ref[...] + _dot(hm.astype(BF16), w_out_ref[...])


def _whole(shape):
    zeros = (0,) * len(shape)
    return pl.BlockSpec(shape, lambda *_: zeros)


def _sample_path(xs2d, cache_k, cache_v, ln1, w_uv, w_q, w_rest, sgu_g, wdiag, b0, sel, selt, bias_s, sink_col,
                 w_oa, w_ob, w_out):
    nb = xs2d.shape[0]
    params = pltpu.CompilerParams(dimension_semantics=("arbitrary",), vmem_limit_bytes=VMEM_LIMIT_BYTES)
    qsel, knew, vnew, vn, a, ga, gb = pl.pallas_call(
        _sample_proj_kernel,
        grid=(1,),
        in_specs=[_whole((nb, D_MODEL)), _whole((1, D_MODEL)),
                  _resident((D_MODEL, 2 * D_MODEL)), _resident((D_MODEL, D_MODEL)), _resident((D_MODEL, R_END)),
                  _whole((1, D_MODEL)), _whole((1, D_MODEL)), _whole((1, D_MODEL)),
                  _whole((nb * N_HEADS, GROUP * nb))],
        out_specs=[_whole((nb * N_HEADS, KV_WIDTH)), _whole((nb, KV_WIDTH)), _whole((nb, KV_WIDTH)),
                   _whole((nb, D_MODEL)), _whole((nb, D_MODEL)), _whole((nb, D_MODEL)), _whole((nb, D_MODEL))],
        out_shape=[
            jax.ShapeDtypeStruct((nb * N_HEADS, KV_WIDTH), BF16),
            jax.ShapeDtypeStruct((nb, KV_WIDTH), F32),
            jax.ShapeDtypeStruct((nb, KV_WIDTH), F32),
            jax.ShapeDtypeStruct((nb, D_MODEL), F32),
            jax.ShapeDtypeStruct((nb, D_MODEL), BF16),
            jax.ShapeDtypeStruct((nb, D_MODEL), F32),
            jax.ShapeDtypeStruct((nb, D_MODEL), F32),
        ],
        compiler_params=params,
        name="sample_proj",
    )(xs2d, ln1, w_uv, w_q, w_rest, sgu_g, wdiag, b0, sel)

    bs = SAMPLE_BLOCK
    cache_block = pl.BlockSpec((bs, WINDOW, KV_WIDTH), lambda i: (i, 0, 0))
    o, nk, nv = pl.pallas_call(
        _sample_attn_kernel,
        grid=(nb // bs,),
        in_specs=[pl.BlockSpec((bs * N_HEADS, KV_WIDTH), lambda i: (i, 0)),
                  pl.BlockSpec((bs, KV_WIDTH), lambda i: (i, 0)),
                  pl.BlockSpec((bs, KV_WIDTH), lambda i: (i, 0)),
                  cache_block, cache_block,
                  _whole((N_HEADS, WINDOW)), _whole((N_HEADS, 1))],
        out_specs=[pl.BlockSpec((bs * N_HEADS, KV_WIDTH), lambda i: (i, 0)), cache_block, cache_block],
        out_shape=[
            jax.ShapeDtypeStruct((nb * N_HEADS, KV_WIDTH), BF16),
            jax.ShapeDtypeStruct((nb, WINDOW, KV_WIDTH), F32),
            jax.ShapeDtypeStruct((nb, WINDOW, KV_WIDTH), F32),
        ],
        compiler_params=params,
        name="sample_attn",
    )(qsel, knew, vnew, cache_k, cache_v, bias_s, sink_col)

    x1 = pl.pallas_call(
        _sample_merge_kernel,
        grid=(1,),
        in_specs=[_whole((nb * N_HEADS, KV_WIDTH)), _whole((GROUP * nb, nb * N_HEADS)),
                  _whole((nb, D_MODEL)), _whole((nb, D_MODEL)), _whole((nb, D_MODEL)), _whole((nb, D_MODEL)),
                  _resident((D_MODEL, D_MODEL)), _resident((D_MODEL, D_MODEL)), _resident((D_MODEL, D_MODEL))],
        out_specs=_whole((nb, D_MODEL)),
        out_shape=jax.ShapeDtypeStruct((nb, D_MODEL), F32),
        compiler_params=params,
        name="sample_merge",
    )(o, selt, a, ga, gb, xs2d, w_oa, w_ob, w_out)
    return x1, nk, nv, vn


def _head_perm(v):
    return v.reshape(N_KV_HEADS, GROUP).T.reshape(N_HEADS)


def _alibi_slopes():
    h = jnp.arange(1, N_HEADS + 1, dtype=F32)
    return jnp.exp2(-8.0 * h / N_HEADS)


def _selection_matrix(nb):
    r = np.arange(nb * N_HEADS)
    c = np.arange(GROUP * nb)
    same_sample = (r[:, None] // N_HEADS) == (c[None, :] % nb)
    same_member = ((r[:, None] % N_HEADS) // N_KV_HEADS) == (c[None, :] // nb)
    return (same_sample & same_member).astype(np.float32)


def kernel(x_prompt, x_sample, cache_k_win, cache_v_win, ln1_g, w_in, sgu_norm_g, sgu_w, sgu_b, attn_sinks,
           w_oa, w_ob, w_out, ln2_g, w_up, w_down, lnf_g):
    batch, seq, _ = x_prompt.shape
    dec_batch, dec_seq, _ = x_sample.shape
    depth = w_in.shape[0]
    assert depth == 1 and dec_seq == 1
    assert seq % TOKEN_BLOCK == 0 and TOKEN_BLOCK % CHUNK == 0 and dec_batch % SAMPLE_BLOCK == 0

    wi = w_in[0]
    w_uv_b = wi[:, :OFF_Q].astype(BF16)
    w_q_b = wi[:, OFF_Q:OFF_K].reshape(D_MODEL, N_KV_HEADS, GROUP, HEAD_DIM).transpose(0, 2, 1, 3).reshape(
        D_MODEL, D_MODEL).astype(BF16)
    w_rest_b = wi[:, OFF_K:].astype(BF16)
    w_ob_b = w_ob[0].reshape(N_KV_HEADS, GROUP, HEAD_DIM, D_MODEL).transpose(1, 0, 2, 3).reshape(D_MODEL, D_MODEL).astype(BF16)
    w_oa_b = w_oa[0].astype(BF16)
    w_out_b = w_out[0].astype(BF16)
    w_up_b = w_up[0].astype(BF16)
    w_down_b = w_down[0].astype(BF16)
    ln1 = ln1_g[0].reshape(1, D_MODEL)
    ln2 = ln2_g[0].reshape(1, D_MODEL)
    lnf = lnf_g.reshape(1, D_MODEL)
    sgu_g = sgu_norm_g[0].reshape(1, D_MODEL)
    sinks_p = _head_perm(attn_sinks[0].astype(F32))
    slopes_p = _head_perm(_alibi_slopes())

    qi = np.arange(CHUNK)[:, None] + CHUNK
    kj = np.arange(2 * CHUNK)[None, :]
    diff = qi - kj
    band = (diff >= 0) & (diff <= WINDOW)
    alibi = -(slopes_p * LOG2E)[:, None, None] * jnp.asarray(diff, F32)[None]
    bias = jnp.stack([jnp.where(band[None], alibi, NEG_BIG),
                      jnp.where((band & (kj >= CHUNK))[None], alibi, NEG_BIG)])

    bexp = jnp.repeat(sgu_b[0].T, SGU_GROUP_DIM, axis=1)

    x1, kwin, vwin = _mix_prompt(x_prompt.reshape(batch * seq, D_MODEL), ln1, w_uv_b, w_q_b, w_rest_b, sgu_g,
                                 sgu_w[0], bexp, sinks_p * LOG2E, bias, w_oa_b, w_ob_b, w_out_b, batch=batch, seq=seq)
    y_prompt = _ffn(x1, ln2, w_up_b, w_down_b, lnf, block=TOKEN_BLOCK, name="ffn_prompt")

    wdiag = jnp.repeat(sgu_w[0][:, 0, 0], SGU_GROUP_DIM).reshape(1, D_MODEL)
    b0 = jnp.repeat(sgu_b[0][:, 0], SGU_GROUP_DIM).reshape(1, D_MODEL)
    sel_np = _selection_matrix(dec_batch)
    sel = jnp.asarray(sel_np, BF16)
    selt = jnp.asarray(sel_np.T, BF16)
    bias_s = -slopes_p[:, None] * jnp.asarray(WINDOW - np.arange(WINDOW), F32)[None, :]
    xs1, nk, nv, vn = _sample_path(
        x_sample.reshape(dec_batch, D_MODEL),
        cache_k_win[0].reshape(dec_batch, WINDOW, KV_WIDTH), cache_v_win[0].reshape(dec_batch, WINDOW, KV_WIDTH),
        ln1, w_uv_b, w_q_b, w_rest_b, sgu_g, wdiag, b0, sel, selt, bias_s, sinks_p.reshape(N_HEADS, 1),
        w_oa_b, w_ob_b, w_out_b)
    y_sample = _ffn(xs1, ln2, w_up_b, w_down_b, lnf, block=dec_batch, name="ffn_sample")

    win_shape = (depth, batch, WINDOW, N_KV_HEADS, HEAD_DIM)
    cache_shape = (depth, dec_batch, WINDOW, N_KV_HEADS, HEAD_DIM)
    return (y_prompt.reshape(batch, seq, D_MODEL),
            y_sample.reshape(dec_batch, dec_seq, D_MODEL),
            kwin.reshape(win_shape), vwin.reshape(win_shape),
            nk.reshape(cache_shape), nv.reshape(cache_shape),
            vn.reshape(depth, dec_batch, dec_seq, D_MODEL))
```

```python
import functools
import math

import numpy as np
import jax
import jax.numpy as jnp
from jax import lax
from jax.experimental import pallas as pl
from jax.experimental.pallas import tpu as pltpu

D_MODEL = 1024
N_HEADS = 16
HEAD_DIM = 64
N_KV_HEADS = 4
GROUP = N_HEADS // N_KV_HEADS
KV_WIDTH = N_KV_HEADS * HEAD_DIM
WINDOW = 128
CHUNK = 128
SGU_GROUPS = 8
SGU_GROUP_DIM = D_MODEL // SGU_GROUPS
D_FF = 4 * D_MODEL
FF_SLAB = 1024
EPS = 1e-6
NEG_BIG = -1e30
ATTN_SCALE = HEAD_DIM ** -0.5
LOG2E = math.log2(math.e)

IN_WIDTH = 5632
OFF_Q, OFF_K = 2048, 3072
C_U, C_V, C_K, C_VA, C_GA, C_GB, C_END = 0, 1024, 2048, 2304, 2560, 3584, 4608
W_BLOCK = 512
ROW_BLOCK = 256

TOKEN_BLOCK = 512
SAMPLE_BLOCK = 16
VMEM_LIMIT_BYTES = 58 * 1024 * 1024

F32 = jnp.float32
BF16 = jnp.bfloat16


def _rmsnorm(x, g):
    ms = jnp.mean(x * x, axis=-1, keepdims=True)
    return x * lax.rsqrt(ms + EPS) * g


def _gelu_tanh(x):
    c = math.sqrt(2.0 / math.pi)
    return x * (0.5 * (1.0 + jnp.tanh(c * (x + 0.044715 * (x * x * x)))))


def _dot(a, b):
    return jnp.dot(a, b, preferred_element_type=F32)


def _dot_nt(a, b):
    return lax.dot_general(a, b, (((1,), (1,)), ((), ())), preferred_element_type=F32)


def _resident(shape):
    zeros = (0,) * len(shape)
    return pl.BlockSpec(shape, lambda *_: zeros, pipeline_mode=pl.Buffered(1))


def _whole(shape):
    zeros = (0,) * len(shape)
    return pl.BlockSpec(shape, lambda *_: zeros)


def _params():
    return pltpu.CompilerParams(dimension_semantics=("arbitrary",), vmem_limit_bytes=VMEM_LIMIT_BYTES)


def _mix_prompt_kernel(x_ref, ln1_ref, w_in_ref, w_q_ref, sgu_g_ref, sgu_w_ref, bexp_ref,
                       sink_ref, bias_ref, w_oa_ref, w_ob_ref, w_out_ref,
                       x1_ref, kwin_ref, vwin_ref,
                       qs_scr, kt_scr, vm_scr, kprev_scr, vprev_scr, vn_scr, u_scr, gate_scr, a_scr, b_scr, wt_scr,
                       *, steps_per_seq):
    step = pl.program_id(0)
    tb = x_ref.shape[0]
    nblk = tb // CHUNK
    first = (step % steps_per_seq) == 0
    rd = step % 2
    wr = 1 - rd

    @pl.when(step == 0)
    def _():
        row = lax.broadcasted_iota(jnp.int32, (CHUNK, CHUNK), 0)
        col = lax.broadcasted_iota(jnp.int32, (CHUNK, CHUNK), 1)
        for g in range(SGU_GROUPS):
            wt_scr[g] = jnp.where(row >= col, sgu_w_ref[g], 0.0).astype(BF16)
        kt_scr[...] = jnp.zeros(kt_scr.shape, BF16)
        vm_scr[...] = jnp.zeros(vm_scr.shape, BF16)
        kprev_scr[...] = jnp.zeros(kprev_scr.shape, BF16)
        vprev_scr[...] = jnp.zeros(vprev_scr.shape, BF16)

    @pl.when(first)
    def _():
        kprev_scr[rd] = jnp.zeros(kprev_scr.shape[1:], BF16)
        vprev_scr[rd] = jnp.zeros(vprev_scr.shape[1:], BF16)

    x = x_ref[...]
    xn = _rmsnorm(x, ln1_ref[...]).astype(BF16)

    q = (_dot(xn, w_q_ref[...]) * (ATTN_SCALE * LOG2E)).astype(BF16)
    for c in range(nblk):
        for g in range(GROUP):
            qs_scr[c, g * CHUNK:(g + 1) * CHUNK, :] = q[c * CHUNK:(c + 1) * CHUNK, g * KV_WIDTH:(g + 1) * KV_WIDTH]

    k = _dot(xn, w_in_ref[:, C_K:C_VA])
    va = _dot(xn, w_in_ref[:, C_VA:C_GA])
    kwin_ref[...] = k[tb - WINDOW:, :]
    vwin_ref[...] = va[tb - WINDOW:, :]
    kt = k.T.astype(BF16)
    vab = va.astype(BF16)
    for kvh in range(N_KV_HEADS):
        own = slice(kvh * HEAD_DIM, (kvh + 1) * HEAD_DIM)
        for c in range(nblk):
            kt_scr[kvh, c, own, :] = kt[own, c * CHUNK:(c + 1) * CHUNK]
        vm_scr[kvh, :, own] = vab[:, own]
        kprev_scr[wr, kvh, own, :] = kt[own, tb - WINDOW:]
        vprev_scr[wr, kvh, :, own] = vab[tb - WINDOW:, own]

    first_i = first.astype(jnp.int32)

    def attn_scores(c):
        qs = qs_scr[c]
        out = []
        for kvh in range(N_KV_HEADS):
            k_prev = kprev_scr[rd, kvh] if c == 0 else kt_scr[kvh, c - 1]
            out.append(_dot(qs, jnp.concatenate([k_prev, kt_scr[kvh, c]], axis=1)))
        return out

    def attn_softmax(c, scores):
        bias_sel = first_i if c == 0 else 0
        out = []
        for kvh in range(N_KV_HEADS):
            ps = []
            for g in range(GROUP):
                h = g * N_KV_HEADS + kvh
                s = scores[kvh][g * CHUNK:(g + 1) * CHUNK, :] + bias_ref[bias_sel, h]
                sink = sink_ref[h]
                m = jnp.max(s, axis=1, keepdims=True)
                p = jnp.exp2(s - m)
                denom = jnp.sum(p, axis=1, keepdims=True) + jnp.exp2(sink - m)
                ps.append((p * (1.0 / denom)).astype(BF16))
            out.append(jnp.concatenate(ps, axis=0))
        return out

    def attn_values(c, probs):
        rows = slice(c * CHUNK, (c + 1) * CHUNK)
        acc = None
        for kvh in range(N_KV_HEADS):
            if c == 0:
                v_band = jnp.concatenate([vprev_scr[rd, kvh], vm_scr[kvh, 0:CHUNK, :]], axis=0)
            else:
                v_band = vm_scr[kvh, (c - 1) * CHUNK:(c + 1) * CHUNK, :]
            o = _dot(probs[kvh], v_band)
            acc = o if acc is None else acc + o
        for g in range(GROUP):
            b_scr[rows, g * KV_WIDTH:(g + 1) * KV_WIDTH] = acc[g * CHUNK:(g + 1) * CHUNK, :].astype(BF16)

    def sgu_chunk(c):
        rows = slice(c * CHUNK, (c + 1) * CHUNK)
        vn_c = vn_scr[rows, :]
        mixed = jnp.concatenate(
            [_dot(wt_scr[g], vn_c[:, g * SGU_GROUP_DIM:(g + 1) * SGU_GROUP_DIM]) for g in range(SGU_GROUPS)],
            axis=1) + bexp_ref[...]
        a_scr[rows, :] = (u_scr[rows, :] * mixed).astype(BF16)

    def tail_v(h):
        vn_scr[...] = _rmsnorm(_gelu_tanh(h), sgu_g_ref[...]).astype(BF16)

    def tail_u(h):
        u_scr[...] = _gelu_tanh(h)

    def tail_ga(h):
        gate_scr[0] = jax.nn.sigmoid(h)

    def tail_gb(h):
        gate_scr[1] = jax.nn.sigmoid(h)

    fillers = [
        (lambda: _dot(xn, w_in_ref[:, C_V:C_K]), tail_v),
        (lambda: _dot(xn, w_in_ref[:, C_U:C_V]), tail_u),
        (lambda: _dot(xn, w_in_ref[:, C_GA:C_GB]), tail_ga),
        (lambda: _dot(xn, w_in_ref[:, C_GB:C_END]), tail_gb),
    ]
    for c in range(nblk):
        scores = attn_scores(c)
        proj = fillers[c][0]() if c < len(fillers) else None
        probs = attn_softmax(c, scores)
        attn_values(c, probs)
        if proj is not None:
            fillers[c][1](proj)
    for matmul, tail in fillers[nblk:]:
        tail(matmul())
    for c in range(nblk):
        sgu_chunk(c)

    hm = gate_scr[0] * _dot(a_scr[...], w_oa_ref[...]) + gate_scr[1] * _dot(b_scr[...], w_ob_ref[...])
    x1_ref[...] = x + _dot(hm.astype(BF16), w_out_ref[...])


def _mix_prompt(x2d, ln1, w_in, w_q, sgu_g, sgu_w, bexp, sinks, bias, w_oa, w_ob, w_out, *, batch, seq):
    n = x2d.shape[0]
    tb = TOKEN_BLOCK
    nblk = tb // CHUNK
    steps_per_seq = seq // tb
    row_block = pl.BlockSpec((tb, D_MODEL), lambda i: (i, 0))
    win_block = pl.BlockSpec((None, WINDOW, KV_WIDTH), lambda i: (i // steps_per_seq, 0, 0))
    return pl.pallas_call(
        functools.partial(_mix_prompt_kernel, steps_per_seq=steps_per_seq),
        grid=(n // tb,),
        in_specs=[
            row_block,
            _resident((1, D_MODEL)),
            _resident((D_MODEL, C_END)),
            _resident((D_MODEL, D_MODEL)),
            _resident((1, D_MODEL)),
            _resident((SGU_GROUPS, CHUNK, CHUNK)),
            _resident((CHUNK, D_MODEL)),
            pl.BlockSpec(memory_space=pltpu.SMEM),
            _resident((2, N_HEADS, CHUNK, 2 * CHUNK)),
            _resident((D_MODEL, D_MODEL)),
            _resident((D_MODEL, D_MODEL)),
            _resident((D_MODEL, D_MODEL)),
        ],
        out_specs=[row_block, win_block, win_block],
        out_shape=[
            jax.ShapeDtypeStruct((n, D_MODEL), F32),
            jax.ShapeDtypeStruct((batch, WINDOW, KV_WIDTH), F32),
            jax.ShapeDtypeStruct((batch, WINDOW, KV_WIDTH), F32),
        ],
        scratch_shapes=[
            pltpu.VMEM((nblk, GROUP * CHUNK, KV_WIDTH), BF16),
            pltpu.VMEM((N_KV_HEADS, nblk, KV_WIDTH, CHUNK), BF16),
            pltpu.VMEM((N_KV_HEADS, tb, KV_WIDTH), BF16),
            pltpu.VMEM((2, N_KV_HEADS, KV_WIDTH, CHUNK), BF16),
            pltpu.VMEM((2, N_KV_HEADS, WINDOW, KV_WIDTH), BF16),
            pltpu.VMEM((tb, D_MODEL), BF16),
            pltpu.VMEM((tb, D_MODEL), F32),
            pltpu.VMEM((2, tb, D_MODEL), F32),
            pltpu.VMEM((tb, D_MODEL), BF16),
            pltpu.VMEM((tb, D_MODEL), BF16),
            pltpu.VMEM((SGU_GROUPS, CHUNK, CHUNK), BF16),
        ],
        compiler_params=_params(),
        name="mix_prompt",
    )(x2d, ln1, w_in, w_q, sgu_g, sgu_w, bexp, sinks, bias, w_oa, w_ob, w_out)


def _ffn_kernel(x_ref, ln2_ref, w_up_ref, w_down_ref, lnf_ref, y_ref):
    x = x_ref[...]
    xn = _rmsnorm(x, ln2_ref[...]).astype(BF16)
    acc = x
    for j in range(D_FF // FF_SLAB):
        h = _dot(xn, w_up_ref[:, j * FF_SLAB:(j + 1) * FF_SLAB])
        h = jnp.square(jnp.maximum(h, 0.0)).astype(BF16)
        acc = acc + _dot(h, w_down_ref[j * FF_SLAB:(j + 1) * FF_SLAB, :])
    y_ref[...] = _rmsnorm(acc, lnf_ref[...])


def _ffn_prompt(x2d, ln2, w_up, w_down, lnf):
    n = x2d.shape[0]
    row_block = pl.BlockSpec((TOKEN_BLOCK, D_MODEL), lambda i: (i, 0))
    return pl.pallas_call(
        _ffn_kernel,
        grid=(n // TOKEN_BLOCK,),
        in_specs=[row_block, _resident((1, D_MODEL)), _resident((D_MODEL, D_FF)),
                  _resident((D_FF, D_MODEL)), _resident((1, D_MODEL))],
        out_specs=row_block,
        out_shape=jax.ShapeDtypeStruct((n, D_MODEL), F32),
        compiler_params=_params(),
        name="ffn_prompt",
    )(x2d, ln2, w_up, w_down, lnf)


def _sample_proj_kernel(x_ref, ln1_ref, w_blk_ref, w_q_ref, sgu_g_ref, wdiag_ref, b0_ref, sel_ref,
                        w_bf_ref, qsel_ref, knew_ref, vnew_ref, vn_ref, a_ref, ga_ref, gb_ref,
                        xn_scr, h_scr):
    j = pl.program_id(0)
    nb = x_ref.shape[0]

    @pl.when(j == 0)
    def _():
        xn_scr[...] = _rmsnorm(x_ref[...], ln1_ref[...]).astype(BF16)

    wb = w_blk_ref[...].astype(BF16)
    w_bf_ref[...] = wb
    h_scr[j] = _dot(xn_scr[...], wb)

    @pl.when(j == pl.num_programs(0) - 1)
    def _():
        def cols(lo, hi):
            blocks = [h_scr[b] for b in range(lo // W_BLOCK, hi // W_BLOCK)]
            return blocks[0] if len(blocks) == 1 else jnp.concatenate(blocks, axis=1)

        u = _gelu_tanh(cols(C_U, C_V))
        v = _gelu_tanh(cols(C_V, C_K))
        vn = _rmsnorm(v, sgu_g_ref[...])
        vn_ref[...] = vn
        a = (u * (vn * wdiag_ref[...] + b0_ref[...])).astype(BF16)
        for r in range(D_MODEL // ROW_BLOCK):
            a_ref[r] = a[:, r * ROW_BLOCK:(r + 1) * ROW_BLOCK]
        kv = cols(C_K, C_GA)
        knew_ref[...] = kv[:, 0:KV_WIDTH]
        vnew_ref[...] = kv[:, KV_WIDTH:2 * KV_WIDTH]
        ga_ref[...] = jax.nn.sigmoid(cols(C_GA, C_GB))
        gb_ref[...] = jax.nn.sigmoid(cols(C_GB, C_END))
        q = _dot(xn_scr[...], w_q_ref[...]) * ATTN_SCALE
        qstack = jnp.concatenate([q[:, g * KV_WIDTH:(g + 1) * KV_WIDTH] for g in range(GROUP)], axis=0).astype(BF16)
        qrep = _dot(sel_ref[...], qstack)
        row_kvh = lax.broadcasted_iota(jnp.int32, (nb * N_HEADS, KV_WIDTH), 0) % N_KV_HEADS
        lane_kvh = lax.broadcasted_iota(jnp.int32, (nb * N_HEADS, KV_WIDTH), 1) // HEAD_DIM
        qsel_ref[...] = jnp.where(row_kvh == lane_kvh, qrep, 0.0).astype(BF16)


def _sample_proj(xs2d, ln1, w_in_f32, w_q, sgu_g, wdiag, b0, sel):
    nb = xs2d.shape[0]
    n_blocks = C_END // W_BLOCK
    q_blocks = (OFF_K - OFF_Q) // W_BLOCK
    first_after_q = OFF_Q // W_BLOCK
    w_block = pl.BlockSpec((D_MODEL, W_BLOCK), lambda j: (0, jnp.where(j >= first_after_q, j + q_blocks, j)))
    return pl.pallas_call(
        _sample_proj_kernel,
        grid=(n_blocks,),
        in_specs=[_whole((nb, D_MODEL)), _whole((1, D_MODEL)), w_block, _resident((D_MODEL, D_MODEL)),
                  _whole((1, D_MODEL)), _whole((1, D_MODEL)), _whole((1, D_MODEL)),
                  _resident((nb * N_HEADS, GROUP * nb))],
        out_specs=[pl.BlockSpec((D_MODEL, W_BLOCK), lambda j: (0, j)),
                   _whole((nb * N_HEADS, KV_WIDTH)), _whole((nb, KV_WIDTH)), _whole((nb, KV_WIDTH)),
                   _whole((nb, D_MODEL)), _whole((D_MODEL // ROW_BLOCK, nb, ROW_BLOCK)),
                   _whole((nb, D_MODEL)), _whole((nb, D_MODEL))],
        out_shape=[
            jax.ShapeDtypeStruct((D_MODEL, C_END), BF16),
            jax.ShapeDtypeStruct((nb * N_HEADS, KV_WIDTH), BF16),
            jax.ShapeDtypeStruct((nb, KV_WIDTH), F32),
            jax.ShapeDtypeStruct((nb, KV_WIDTH), F32),
            jax.ShapeDtypeStruct((nb, D_MODEL), F32),
            jax.ShapeDtypeStruct((D_MODEL // ROW_BLOCK, nb, ROW_BLOCK), BF16),
            jax.ShapeDtypeStruct((nb, D_MODEL), F32),
            jax.ShapeDtypeStruct((nb, D_MODEL), F32),
        ],
        scratch_shapes=[pltpu.VMEM((nb, D_MODEL), BF16), pltpu.VMEM((n_blocks, nb, W_BLOCK), F32)],
        compiler_params=_params(),
        name="sample_proj",
    )(xs2d, ln1, w_in_f32, w_q, sgu_g, wdiag, b0, sel)


def _sample_attn_kernel(qsel_ref, knew_ref, vnew_ref, ck_ref, cv_ref, bias_ref, sink_ref,
                        o_ref, nk_ref, nv_ref):
    bs = ck_ref.shape[0]
    row_kvh = lax.broadcasted_iota(jnp.int32, (N_HEADS, KV_WIDTH), 0) % N_KV_HEADS
    lane_kvh = lax.broadcasted_iota(jnp.int32, (N_HEADS, KV_WIDTH), 1) // HEAD_DIM
    own = row_kvh == lane_kvh
    bias = bias_ref[...]
    sink = sink_ref[...]

    qss = [qsel_ref[i * N_HEADS:(i + 1) * N_HEADS, :] for i in range(bs)]
    kns = [knew_ref[i:i + 1, :] for i in range(bs)]
    vws = [vnew_ref[i:i + 1, :] for i in range(bs)]
    scores = [_dot_nt(qss[i], ck_ref[i].astype(BF16)) + bias for i in range(bs)]
    probs = []
    for i in range(bs):
        s = scores[i]
        s_new = jnp.sum(qss[i].astype(F32) * kns[i], axis=1, keepdims=True)
        m = jnp.maximum(jnp.maximum(jnp.max(s, axis=1, keepdims=True), s_new), sink)
        p = jnp.exp(s - m)
        p_new = jnp.exp(s_new - m)
        denom = jnp.sum(p, axis=1, keepdims=True) + p_new + jnp.exp(sink - m)
        probs.append((p.astype(BF16), p_new, denom))
    for i in range(bs):
        p, p_new, denom = probs[i]
        o = (_dot(p, cv_ref[i].astype(BF16)) + p_new * vws[i]) / denom
        o_ref[i * N_HEADS:(i + 1) * N_HEADS, :] = jnp.where(own, o, 0.0).astype(BF16)
    for i in range(bs):
        nk_ref[i, 0:WINDOW - 1, :] = ck_ref[i, 1:WINDOW, :]
        nk_ref[i, WINDOW - 1:WINDOW, :] = kns[i]
        nv_ref[i, 0:WINDOW - 1, :] = cv_ref[i, 1:WINDOW, :]
        nv_ref[i, WINDOW - 1:WINDOW, :] = vws[i]


def _sample_attn(qsel, knew, vnew, cache_k, cache_v, bias_s, sink_col):
    nb = knew.shape[0]
    bs = SAMPLE_BLOCK
    cache_block = pl.BlockSpec((bs, WINDOW, KV_WIDTH), lambda i: (i, 0, 0))
    return pl.pallas_call(
        _sample_attn_kernel,
        grid=(nb // bs,),
        in_specs=[pl.BlockSpec((bs * N_HEADS, KV_WIDTH), lambda i: (i, 0)),
                  pl.BlockSpec((bs, KV_WIDTH), lambda i: (i, 0)),
                  pl.BlockSpec((bs, KV_WIDTH), lambda i: (i, 0)),
                  cache_block, cache_block,
                  _whole((N_HEADS, WINDOW)), _whole((N_HEADS, 1))],
        out_specs=[pl.BlockSpec((bs * N_HEADS, KV_WIDTH), lambda i: (i, 0)), cache_block, cache_block],
        out_shape=[
            jax.ShapeDtypeStruct((nb * N_HEADS, KV_WIDTH), BF16),
            jax.ShapeDtypeStruct((nb, WINDOW, KV_WIDTH), F32),
            jax.ShapeDtypeStruct((nb, WINDOW, KV_WIDTH), F32),
        ],
        compiler_params=_params(),
        name="sample_attn",
    )(qsel, knew, vnew, cache_k, cache_v, bias_s, sink_col)


def _sample_merge_kernel(o_ref, selt_ref, a_ref, ga_ref, gb_ref, x_ref, w_oa_blk, w_ob_blk, w_out_blk,
                         w_oa_bf, w_ob_bf, w_out_bf, x1_ref, acc_scr):
    r = pl.program_id(0)
    nb = x_ref.shape[0]

    w_oa_b = w_oa_blk[...].astype(BF16)
    w_oa_bf[pl.ds(pl.multiple_of(r * ROW_BLOCK, ROW_BLOCK), ROW_BLOCK), :] = w_oa_b
    w_out_bf[pl.ds(pl.multiple_of(r * ROW_BLOCK, ROW_BLOCK), ROW_BLOCK), :] = w_out_blk[...].astype(BF16)
    for g in range(GROUP):
        dst = pl.multiple_of(g * KV_WIDTH + r * HEAD_DIM, HEAD_DIM)
        w_ob_bf[pl.ds(dst, HEAD_DIM), :] = w_ob_blk[g * HEAD_DIM:(g + 1) * HEAD_DIM, :].astype(BF16)

    part = _dot(a_ref[r], w_oa_b)

    @pl.when(r == 0)
    def _():
        acc_scr[...] = part

    @pl.when(r > 0)
    def _():
        acc_scr[...] += part

    @pl.when(r == pl.num_programs(0) - 1)
    def _():
        bst = _dot(selt_ref[...], o_ref[...]).astype(BF16)
        ob = _dot(bst[0:nb, :], w_ob_bf[0:KV_WIDTH, :])
        for g in range(1, GROUP):
            ob = ob + _dot(bst[g * nb:(g + 1) * nb, :], w_ob_bf[g * KV_WIDTH:(g + 1) * KV_WIDTH, :])
        hm = ga_ref[...] * acc_scr[...] + gb_ref[...] * ob
        x1_ref[...] = x_ref[...] + _dot(hm.astype(BF16), w_out_bf[...])


def _sample_merge(o, selt, a, ga, gb, xs2d, w_oa_f32, w_ob_f32, w_out_f32):
    nb = xs2d.shape[0]
    row_block = pl.BlockSpec((ROW_BLOCK, D_MODEL), lambda r: (r, 0))
    w_shape = jax.ShapeDtypeStruct((D_MODEL, D_MODEL), BF16)
    return pl.pallas_call(
        _sample_merge_kernel,
        grid=(D_MODEL // ROW_BLOCK,),
        in_specs=[_whole((nb * N_HEADS, KV_WIDTH)), _whole((GROUP * nb, nb * N_HEADS)),
                  _whole((D_MODEL // ROW_BLOCK, nb, ROW_BLOCK)),
                  _whole((nb, D_MODEL)), _whole((nb, D_MODEL)), _whole((nb, D_MODEL)),
                  row_block, row_block, row_block],
        out_specs=[_whole((D_MODEL, D_MODEL)), _whole((D_MODEL, D_MODEL)), _whole((D_MODEL, D_MODEL)),
                   _whole((nb, D_MODEL))],
        out_shape=[w_shape, w_shape, w_shape, jax.ShapeDtypeStruct((nb, D_MODEL), F32)],
        scratch_shapes=[pltpu.VMEM((nb, D_MODEL), F32)],
        compiler_params=_params(),
        name="sample_merge",
    )(o, selt, a, ga, gb, xs2d, w_oa_f32, w_ob_f32, w_out_f32)


def _ffn_sample_kernel(x_ref, ln2_ref, w_up_blk, w_down_blk, lnf_ref, w_up_bf, w_down_bf, y_ref, xn_scr, acc_scr):
    j = pl.program_id(0)

    @pl.when(j == 0)
    def _():
        x = x_ref[...]
        xn_scr[...] = _rmsnorm(x, ln2_ref[...]).astype(BF16)
        acc_scr[...] = x

    wu = w_up_blk[...].astype(BF16)
    wd = w_down_blk[...].astype(BF16)
    w_up_bf[...] = wu
    w_down_bf[...] = wd
    h = jnp.square(jnp.maximum(_dot(xn_scr[...], wu), 0.0)).astype(BF16)
    acc_scr[...] += _dot(h, wd)

    @pl.when(j == pl.num_programs(0) - 1)
    def _():
        y_ref[...] = _rmsnorm(acc_scr[...], lnf_ref[...])


def _ffn_sample(x2d, ln2, w_up_f32, w_down_f32, lnf):
    nb = x2d.shape[0]
    up_block = pl.BlockSpec((D_MODEL, FF_SLAB), lambda j: (0, j))
    down_block = pl.BlockSpec((FF_SLAB, D_MODEL), lambda j: (j, 0))
    return pl.pallas_call(
        _ffn_sample_kernel,
        grid=(D_FF // FF_SLAB,),
        in_specs=[_whole((nb, D_MODEL)), _whole((1, D_MODEL)), up_block, down_block, _whole((1, D_MODEL))],
        out_specs=[up_block, down_block, _whole((nb, D_MODEL))],
        out_shape=[jax.ShapeDtypeStruct((D_MODEL, D_FF), BF16), jax.ShapeDtypeStruct((D_FF, D_MODEL), BF16),
                   jax.ShapeDtypeStruct((nb, D_MODEL), F32)],
        scratch_shapes=[pltpu.VMEM((nb, D_MODEL), BF16), pltpu.VMEM((nb, D_MODEL), F32)],
        compiler_params=_params(),
        name="ffn_sample",
    )(x2d, ln2, w_up_f32, w_down_f32, lnf)


def _head_perm(v):
    return v.reshape(N_KV_HEADS, GROUP).T.reshape(N_HEADS)


def _alibi_slopes():
    h = jnp.arange(1, N_HEADS + 1, dtype=F32)
    return jnp.exp2(-8.0 * h / N_HEADS)


def _selection_matrix(nb):
    r = np.arange(nb * N_HEADS)
    c = np.arange(GROUP * nb)
    same_sample = (r[:, None] // N_HEADS) == (c[None, :] % nb)
    same_member = ((r[:, None] % N_HEADS) // N_KV_HEADS) == (c[None, :] // nb)
    return (same_sample & same_member).astype(np.float32)


def kernel(x_prompt, x_sample, cache_k_win, cache_v_win, ln1_g, w_in, sgu_norm_g, sgu_w, sgu_b, attn_sinks,
           w_oa, w_ob, w_out, ln2_g, w_up, w_down, lnf_g):
    batch, seq, _ = x_prompt.shape
    dec_batch, dec_seq, _ = x_sample.shape
    depth = w_in.shape[0]
    assert depth == 1 and dec_seq == 1
    assert seq % TOKEN_BLOCK == 0 and TOKEN_BLOCK % CHUNK == 0 and dec_batch % SAMPLE_BLOCK == 0
    assert w_in.shape[-1] == IN_WIDTH

    wi = w_in[0]
    w_q_b = wi[:, OFF_Q:OFF_K].reshape(D_MODEL, N_KV_HEADS, GROUP, HEAD_DIM).transpose(0, 2, 1, 3).reshape(
        D_MODEL, D_MODEL).astype(BF16)
    ln1 = ln1_g[0].reshape(1, D_MODEL)
    ln2 = ln2_g[0].reshape(1, D_MODEL)
    lnf = lnf_g.reshape(1, D_MODEL)
    sgu_g = sgu_norm_g[0].reshape(1, D_MODEL)
    sinks_p = _head_perm(attn_sinks[0].astype(F32))
    slopes_p = _head_perm(_alibi_slopes())

    xs2d = x_sample.reshape(dec_batch, D_MODEL)
    wdiag = jnp.repeat(sgu_w[0][:, 0, 0], SGU_GROUP_DIM).reshape(1, D_MODEL)
    b0 = jnp.repeat(sgu_b[0][:, 0], SGU_GROUP_DIM).reshape(1, D_MODEL)
    sel_np = _selection_matrix(dec_batch)
    sel = jnp.asarray(sel_np, BF16)
    selt = jnp.asarray(sel_np.T, BF16)
    bias_s = -slopes_p[:, None] * jnp.asarray(WINDOW - np.arange(WINDOW), F32)[None, :]

    w_in_b, qsel, knew, vnew, vn, a_s, ga_s, gb_s = _sample_proj(xs2d, ln1, wi, w_q_b, sgu_g, wdiag, b0, sel)
    o_s, nk, nv = _sample_attn(qsel, knew, vnew,
                               cache_k_win[0].reshape(dec_batch, WINDOW, KV_WIDTH),
                               cache_v_win[0].reshape(dec_batch, WINDOW, KV_WIDTH),
                               bias_s, sinks_p.reshape(N_HEADS, 1))
    w_oa_b, w_ob_b, w_out_b, xs1 = _sample_merge(o_s, selt, a_s, ga_s, gb_s, xs2d, w_oa[0], w_ob[0], w_out[0])
    w_up_b, w_down_b, y_sample = _ffn_sample(xs1, ln2, w_up[0], w_down[0], lnf)

    qi = np.arange(CHUNK)[:, None] + CHUNK
    kj = np.arange(2 * CHUNK)[None, :]
    diff = qi - kj
    band = (diff >= 0) & (diff <= WINDOW)
    alibi = -(slopes_p * LOG2E)[:, None, None] * jnp.asarray(diff, F32)[None]
    bias = jnp.stack([jnp.where(band[None], alibi, NEG_BIG),
                      jnp.where((band & (kj >= CHUNK))[None], alibi, NEG_BIG)])
    bexp = jnp.repeat(sgu_b[0].T, SGU_GROUP_DIM, axis=1)

    x1, kwin, vwin = _mix_prompt(x_prompt.reshape(batch * seq, D_MODEL), ln1, w_in_b, w_q_b, sgu_g,
                                 sgu_w[0], bexp, sinks_p * LOG2E, bias, w_oa_b, w_ob_b, w_out_b, batch=batch, seq=seq)
    y_prompt = _ffn_prompt(x1, ln2, w_up_b, w_down_b, lnf)

    win_shape = (depth, batch, WINDOW, N_KV_HEADS, HEAD_DIM)
    cache_shape = (depth, dec_batch, WINDOW, N_KV_HEADS, HEAD_DIM)
    return (y_prompt.reshape(batch, seq, D_MODEL),
            y_sample.reshape(dec_batch, dec_seq, D_MODEL),
            kwin.reshape(win_shape), vwin.reshape(win_shape),
            nk.reshape(cache_shape), nv.reshape(cache_shape),
            vn.reshape(depth, dec_batch, dec_seq, D_MODEL))
```

```python
import functools
import math

import numpy as np
import jax
import jax.numpy as jnp
from jax import lax
from jax.experimental import pallas as pl
from jax.experimental.pallas import tpu as pltpu

D_MODEL = 1024
N_HEADS = 16
HEAD_DIM = 64
N_KV_HEADS = 4
GROUP = N_HEADS // N_KV_HEADS
KV_WIDTH = N_KV_HEADS * HEAD_DIM
WINDOW = 128
CHUNK = 128
SGU_GROUPS = 8
SGU_GROUP_DIM = D_MODEL // SGU_GROUPS
D_FF = 4 * D_MODEL
FF_SLAB = 1024
EPS = 1e-6
NEG_BIG = -1e30
ATTN_SCALE = HEAD_DIM ** -0.5
LOG2E = math.log2(math.e)

IN_WIDTH = 5632
OFF_Q, OFF_K = 2048, 3072
C_U, C_V, C_K, C_VA, C_GA, C_GB, C_END = 0, 1024, 2048, 2304, 2560, 3584, 4608
W_BLOCK = 512

TOKEN_BLOCK = 512
SAMPLE_BLOCK = 16
VMEM_LIMIT_BYTES = 58 * 1024 * 1024

F32 = jnp.float32
BF16 = jnp.bfloat16


def _rmsnorm(x, g):
    ms = jnp.mean(x * x, axis=-1, keepdims=True)
    return x * lax.rsqrt(ms + EPS) * g


def _gelu_tanh(x):
    c = math.sqrt(2.0 / math.pi)
    return x * (0.5 * (1.0 + jnp.tanh(c * (x + 0.044715 * (x * x * x)))))


def _dot(a, b):
    return jnp.dot(a, b, preferred_element_type=F32)


def _dot_nt(a, b):
    return lax.dot_general(a, b, (((1,), (1,)), ((), ())), preferred_element_type=F32)


def _resident(shape):
    zeros = (0,) * len(shape)
    return pl.BlockSpec(shape, lambda *_: zeros, pipeline_mode=pl.Buffered(1))


def _whole(shape):
    zeros = (0,) * len(shape)
    return pl.BlockSpec(shape, lambda *_: zeros)


def _params():
    return pltpu.CompilerParams(dimension_semantics=("arbitrary",), vmem_limit_bytes=VMEM_LIMIT_BYTES)


def _mix_prompt_kernel(x_ref, ln1_ref, w_in_ref, w_q_ref, sgu_g_ref, sgu_w_ref, bexp_ref,
                       sink_ref, bias_ref, w_oa_ref, w_ob_ref, w_out_ref, w_up_blk_ref, w_down_blk_ref,
                       x1_ref, kwin_ref, vwin_ref, w_up_bf_ref, w_down_bf_ref,
                       qs_scr, kt_scr, vm_scr, kprev_scr, vprev_scr, vn_scr, u_scr, gate_scr, a_scr, b_scr, wt_scr,
                       *, steps_per_seq):
    step = pl.program_id(0)
    tb = x_ref.shape[0]
    nblk = tb // CHUNK
    first = (step % steps_per_seq) == 0
    rd = step % 2
    wr = 1 - rd

    @pl.when(step == 0)
    def _():
        row = lax.broadcasted_iota(jnp.int32, (CHUNK, CHUNK), 0)
        col = lax.broadcasted_iota(jnp.int32, (CHUNK, CHUNK), 1)
        for g in range(SGU_GROUPS):
            wt_scr[g] = jnp.where(row >= col, sgu_w_ref[g], 0.0).astype(BF16)
        kt_scr[...] = jnp.zeros(kt_scr.shape, BF16)
        vm_scr[...] = jnp.zeros(vm_scr.shape, BF16)
        kprev_scr[...] = jnp.zeros(kprev_scr.shape, BF16)
        vprev_scr[...] = jnp.zeros(vprev_scr.shape, BF16)

    @pl.when(first)
    def _():
        kprev_scr[rd] = jnp.zeros(kprev_scr.shape[1:], BF16)
        vprev_scr[rd] = jnp.zeros(vprev_scr.shape[1:], BF16)

    w_up_bf_ref[...] = w_up_blk_ref[...].astype(BF16)
    w_down_bf_ref[...] = w_down_blk_ref[...].astype(BF16)

    x = x_ref[...]
    xn = _rmsnorm(x, ln1_ref[...]).astype(BF16)

    q = (_dot(xn, w_q_ref[...]) * (ATTN_SCALE * LOG2E)).astype(BF16)
    for c in range(nblk):
        for g in range(GROUP):
            qs_scr[c, g * CHUNK:(g + 1) * CHUNK, :] = q[c * CHUNK:(c + 1) * CHUNK, g * KV_WIDTH:(g + 1) * KV_WIDTH]

    k = _dot(xn, w_in_ref[:, C_K:C_VA])
    va = _dot(xn, w_in_ref[:, C_VA:C_GA])
    kwin_ref[...] = k[tb - WINDOW:, :]
    vwin_ref[...] = va[tb - WINDOW:, :]
    kt = k.T.astype(BF16)
    vab = va.astype(BF16)
    for kvh in range(N_KV_HEADS):
        own = slice(kvh * HEAD_DIM, (kvh + 1) * HEAD_DIM)
        for c in range(nblk):
            kt_scr[kvh, c, own, :] = kt[own, c * CHUNK:(c + 1) * CHUNK]
        vm_scr[kvh, :, own] = vab[:, own]
        kprev_scr[wr, kvh, own, :] = kt[own, tb - WINDOW:]
        vprev_scr[wr, kvh, :, own] = vab[tb - WINDOW:, own]

    first_i = first.astype(jnp.int32)

    def attn_scores(c):
        qs = qs_scr[c]
        out = []
        for kvh in range(N_KV_HEADS):
            k_prev = kprev_scr[rd, kvh] if c == 0 else kt_scr[kvh, c - 1]
            out.append(_dot(qs, jnp.concatenate([k_prev, kt_scr[kvh, c]], axis=1)))
        return out

    def attn_softmax(c, scores):
        bias_sel = first_i if c == 0 else 0
        out = []
        for kvh in range(N_KV_HEADS):
            ps = []
            for g in range(GROUP):
                h = g * N_KV_HEADS + kvh
                s = scores[kvh][g * CHUNK:(g + 1) * CHUNK, :] + bias_ref[bias_sel, h]
                sink = sink_ref[h]
                m = jnp.max(s, axis=1, keepdims=True)
                p = jnp.exp2(s - m)
                denom = jnp.sum(p, axis=1, keepdims=True) + jnp.exp2(sink - m)
                ps.append((p * (1.0 / denom)).astype(BF16))
            out.append(jnp.concatenate(ps, axis=0))
        return out

    def attn_values(c, probs):
        rows = slice(c * CHUNK, (c + 1) * CHUNK)
        acc = None
        for kvh in range(N_KV_HEADS):
            if c == 0:
                v_band = jnp.concatenate([vprev_scr[rd, kvh], vm_scr[kvh, 0:CHUNK, :]], axis=0)
            else:
                v_band = vm_scr[kvh, (c - 1) * CHUNK:(c + 1) * CHUNK, :]
            o = _dot(probs[kvh], v_band)
            acc = o if acc is None else acc + o
        for g in range(GROUP):
            b_scr[rows, g * KV_WIDTH:(g + 1) * KV_WIDTH] = acc[g * CHUNK:(g + 1) * CHUNK, :].astype(BF16)

    def sgu_chunk(c):
        rows = slice(c * CHUNK, (c + 1) * CHUNK)
        vn_c = vn_scr[rows, :]
        mixed = jnp.concatenate(
            [_dot(wt_scr[g], vn_c[:, g * SGU_GROUP_DIM:(g + 1) * SGU_GROUP_DIM]) for g in range(SGU_GROUPS)],
            axis=1) + bexp_ref[...]
        a_scr[rows, :] = (u_scr[rows, :] * mixed).astype(BF16)

    def tail_v(h):
        vn_scr[...] = _rmsnorm(_gelu_tanh(h), sgu_g_ref[...]).astype(BF16)

    def tail_u(h):
        u_scr[...] = _gelu_tanh(h)

    def tail_ga(h):
        gate_scr[0] = jax.nn.sigmoid(h)

    def tail_gb(h):
        gate_scr[1] = jax.nn.sigmoid(h)

    fillers = [
        (lambda: _dot(xn, w_in_ref[:, C_V:C_K]), tail_v),
        (lambda: _dot(xn, w_in_ref[:, C_U:C_V]), tail_u),
        (lambda: _dot(xn, w_in_ref[:, C_GA:C_GB]), tail_ga),
        (lambda: _dot(xn, w_in_ref[:, C_GB:C_END]), tail_gb),
    ]
    for c in range(nblk):
        scores = attn_scores(c)
        proj = fillers[c][0]() if c < len(fillers) else None
        probs = attn_softmax(c, scores)
        attn_values(c, probs)
        if proj is not None:
            fillers[c][1](proj)
    for matmul, tail in fillers[nblk:]:
        tail(matmul())
    for c in range(nblk):
        sgu_chunk(c)

    hm = gate_scr[0] * _dot(a_scr[...], w_oa_ref[...]) + gate_scr[1] * _dot(b_scr[...], w_ob_ref[...])
    x1_ref[...] = x + _dot(hm.astype(BF16), w_out_ref[...])


def _mix_prompt(x2d, ln1, w_in, w_q, sgu_g, sgu_w, bexp, sinks, bias, w_oa, w_ob, w_out, w_up_f32, w_down_f32,
                *, batch, seq):
    n = x2d.shape[0]
    tb = TOKEN_BLOCK
    nblk = tb // CHUNK
    steps = n // tb
    steps_per_seq = seq // tb
    row_block = pl.BlockSpec((tb, D_MODEL), lambda i: (i, 0))
    win_block = pl.BlockSpec((None, WINDOW, KV_WIDTH), lambda i: (i // steps_per_seq, 0, 0))
    up_block = pl.BlockSpec((D_MODEL // steps, D_FF), lambda i: (i, 0))
    down_block = pl.BlockSpec((D_FF // steps, D_MODEL), lambda i: (i, 0))
    return pl.pallas_call(
        functools.partial(_mix_prompt_kernel, steps_per_seq=steps_per_seq),
        grid=(n // tb,),
        in_specs=[
            row_block,
            _resident((1, D_MODEL)),
            _resident((D_MODEL, C_END)),
            _resident((D_MODEL, D_MODEL)),
            _resident((1, D_MODEL)),
            _resident((SGU_GROUPS, CHUNK, CHUNK)),
            _resident((CHUNK, D_MODEL)),
            pl.BlockSpec(memory_space=pltpu.SMEM),
            _resident((2, N_HEADS, CHUNK, 2 * CHUNK)),
            _resident((D_MODEL, D_MODEL)),
            _resident((D_MODEL, D_MODEL)),
            _resident((D_MODEL, D_MODEL)),
            up_block,
            down_block,
        ],
        out_specs=[row_block, win_block, win_block, up_block, down_block],
        out_shape=[
            jax.ShapeDtypeStruct((n, D_MODEL), F32),
            jax.ShapeDtypeStruct((batch, WINDOW, KV_WIDTH), F32),
            jax.ShapeDtypeStruct((batch, WINDOW, KV_WIDTH), F32),
            jax.ShapeDtypeStruct((D_MODEL, D_FF), BF16),
            jax.ShapeDtypeStruct((D_FF, D_MODEL), BF16),
        ],
        scratch_shapes=[
            pltpu.VMEM((nblk, GROUP * CHUNK, KV_WIDTH), BF16),
            pltpu.VMEM((N_KV_HEADS, nblk, KV_WIDTH, CHUNK), BF16),
            pltpu.VMEM((N_KV_HEADS, tb, KV_WIDTH), BF16),
            pltpu.VMEM((2, N_KV_HEADS, KV_WIDTH, CHUNK), BF16),
            pltpu.VMEM((2, N_KV_HEADS, WINDOW, KV_WIDTH), BF16),
            pltpu.VMEM((tb, D_MODEL), BF16),
            pltpu.VMEM((tb, D_MODEL), F32),
            pltpu.VMEM((2, tb, D_MODEL), F32),
            pltpu.VMEM((tb, D_MODEL), BF16),
            pltpu.VMEM((tb, D_MODEL), BF16),
            pltpu.VMEM((SGU_GROUPS, CHUNK, CHUNK), BF16),
        ],
        compiler_params=_params(),
        name="mix_prompt",
    )(x2d, ln1, w_in, w_q, sgu_g, sgu_w, bexp, sinks, bias, w_oa, w_ob, w_out, w_up_f32, w_down_f32)


def _ffn_rows(x, ln2_ref, w_up_ref, w_down_ref, lnf_ref):
    xn = _rmsnorm(x, ln2_ref[...]).astype(BF16)
    acc = x
    for j in range(D_FF // FF_SLAB):
        h = _dot(xn, w_up_ref[:, j * FF_SLAB:(j + 1) * FF_SLAB])
        h = jnp.square(jnp.maximum(h, 0.0)).astype(BF16)
        acc = acc + _dot(h, w_down_ref[j * FF_SLAB:(j + 1) * FF_SLAB, :])
    return _rmsnorm(acc, lnf_ref[...])


def _ffn_kernel(x_ref, xs_ref, ln2_ref, w_up_ref, w_down_ref, lnf_ref, y_ref, ys_ref):
    i = pl.program_id(0)
    last = pl.num_programs(0) - 1

    @pl.when(i < last)
    def _():
        y_ref[...] = _ffn_rows(x_ref[...], ln2_ref, w_up_ref, w_down_ref, lnf_ref)

    @pl.when(i == last)
    def _():
        ys_ref[...] = _ffn_rows(xs_ref[...], ln2_ref, w_up_ref, w_down_ref, lnf_ref)


def _ffn(x2d, xs2d, ln2, w_up, w_down, lnf):
    n = x2d.shape[0]
    nb = xs2d.shape[0]
    n_prompt_steps = n // TOKEN_BLOCK
    row_block = pl.BlockSpec((TOKEN_BLOCK, D_MODEL), lambda i: (jnp.minimum(i, n_prompt_steps - 1), 0))
    return pl.pallas_call(
        _ffn_kernel,
        grid=(n_prompt_steps + 1,),
        in_specs=[row_block, _resident((nb, D_MODEL)), _resident((1, D_MODEL)), _resident((D_MODEL, D_FF)),
                  _resident((D_FF, D_MODEL)), _resident((1, D_MODEL))],
        out_specs=[row_block, _whole((nb, D_MODEL))],
        out_shape=[jax.ShapeDtypeStruct((n, D_MODEL), F32), jax.ShapeDtypeStruct((nb, D_MODEL), F32)],
        compiler_params=_params(),
        name="ffn",
    )(x2d, xs2d, ln2, w_up, w_down, lnf)


def _sample_proj_kernel(x_ref, ln1_ref, w_blk_ref, w_q_ref, sgu_g_ref, wdiag_ref, b0_ref, sel_ref,
                        w_bf_ref, qsel_ref, knew_ref, vnew_ref, vn_ref, a_ref, ga_ref, gb_ref,
                        xn_scr, h_scr):
    j = pl.program_id(0)
    nb = x_ref.shape[0]

    @pl.when(j == 0)
    def _():
        xn_scr[...] = _rmsnorm(x_ref[...], ln1_ref[...]).astype(BF16)

    wb = w_blk_ref[...].astype(BF16)
    w_bf_ref[...] = wb
    h_scr[j] = _dot(xn_scr[...], wb)

    @pl.when(j == pl.num_programs(0) - 1)
    def _():
        def cols(lo, hi):
            blocks = [h_scr[b] for b in range(lo // W_BLOCK, hi // W_BLOCK)]
            return blocks[0] if len(blocks) == 1 else jnp.concatenate(blocks, axis=1)

        u = _gelu_tanh(cols(C_U, C_V))
        v = _gelu_tanh(cols(C_V, C_K))
        vn = _rmsnorm(v, sgu_g_ref[...])
        vn_ref[...] = vn
        a_ref[...] = (u * (vn * wdiag_ref[...] + b0_ref[...])).astype(BF16)
        kv = cols(C_K, C_GA)
        knew_ref[...] = kv[:, 0:KV_WIDTH]
        vnew_ref[...] = kv[:, KV_WIDTH:2 * KV_WIDTH]
        ga_ref[...] = jax.nn.sigmoid(cols(C_GA, C_GB))
        gb_ref[...] = jax.nn.sigmoid(cols(C_GB, C_END))
        q = _dot(xn_scr[...], w_q_ref[...]) * ATTN_SCALE
        qstack = jnp.concatenate([q[:, g * KV_WIDTH:(g + 1) * KV_WIDTH] for g in range(GROUP)], axis=0).astype(BF16)
        qrep = _dot(sel_ref[...], qstack)
        row_kvh = lax.broadcasted_iota(jnp.int32, (nb * N_HEADS, KV_WIDTH), 0) % N_KV_HEADS
        lane_kvh = lax.broadcasted_iota(jnp.int32, (nb * N_HEADS, KV_WIDTH), 1) // HEAD_DIM
        qsel_ref[...] = jnp.where(row_kvh == lane_kvh, qrep, 0.0).astype(BF16)


def _sample_proj(xs2d, ln1, w_in_f32, w_q, sgu_g, wdiag, b0, sel):
    nb = xs2d.shape[0]
    n_blocks = C_END // W_BLOCK
    q_blocks = (OFF_K - OFF_Q) // W_BLOCK
    first_after_q = OFF_Q // W_BLOCK
    w_block = pl.BlockSpec((D_MODEL, W_BLOCK), lambda j: (0, jnp.where(j >= first_after_q, j + q_blocks, j)))
    return pl.pallas_call(
        _sample_proj_kernel,
        grid=(n_blocks,),
        in_specs=[_whole((nb, D_MODEL)), _whole((1, D_MODEL)), w_block, _resident((D_MODEL, D_MODEL)),
                  _whole((1, D_MODEL)), _whole((1, D_MODEL)), _whole((1, D_MODEL)),
                  _resident((nb * N_HEADS, GROUP * nb))],
        out_specs=[pl.BlockSpec((D_MODEL, W_BLOCK), lambda j: (0, j)),
                   _whole((nb * N_HEADS, KV_WIDTH)), _whole((nb, KV_WIDTH)), _whole((nb, KV_WIDTH)),
                   _whole((nb, D_MODEL)), _whole((nb, D_MODEL)),
                   _whole((nb, D_MODEL)), _whole((nb, D_MODEL))],
        out_shape=[
            jax.ShapeDtypeStruct((D_MODEL, C_END), BF16),
            jax.ShapeDtypeStruct((nb * N_HEADS, KV_WIDTH), BF16),
            jax.ShapeDtypeStruct((nb, KV_WIDTH), F32),
            jax.ShapeDtypeStruct((nb, KV_WIDTH), F32),
            jax.ShapeDtypeStruct((nb, D_MODEL), F32),
            jax.ShapeDtypeStruct((nb, D_MODEL), BF16),
            jax.ShapeDtypeStruct((nb, D_MODEL), F32),
            jax.ShapeDtypeStruct((nb, D_MODEL), F32),
        ],
        scratch_shapes=[pltpu.VMEM((nb, D_MODEL), BF16), pltpu.VMEM((n_blocks, nb, W_BLOCK), F32)],
        compiler_params=_params(),
        name="sample_proj",
    )(xs2d, ln1, w_in_f32, w_q, sgu_g, wdiag, b0, sel)


def _sample_attn_kernel(qsel_ref, knew_ref, vnew_ref, ck_ref, cv_ref, bias_ref, sink_ref,
                        o_ref, nk_ref, nv_ref):
    bs = ck_ref.shape[0]
    row_kvh = lax.broadcasted_iota(jnp.int32, (N_HEADS, KV_WIDTH), 0) % N_KV_HEADS
    lane_kvh = lax.broadcasted_iota(jnp.int32, (N_HEADS, KV_WIDTH), 1) // HEAD_DIM
    own = row_kvh == lane_kvh
    bias = bias_ref[...]
    sink = sink_ref[...]

    qss = [qsel_ref[i * N_HEADS:(i + 1) * N_HEADS, :] for i in range(bs)]
    kns = [knew_ref[i:i + 1, :] for i in range(bs)]
    vws = [vnew_ref[i:i + 1, :] for i in range(bs)]
    scores = [_dot_nt(qss[i], ck_ref[i].astype(BF16)) + bias for i in range(bs)]
    probs = []
    for i in range(bs):
        s = scores[i]
        s_new = jnp.sum(qss[i].astype(F32) * kns[i], axis=1, keepdims=True)
        m = jnp.maximum(jnp.maximum(jnp.max(s, axis=1, keepdims=True), s_new), sink)
        p = jnp.exp(s - m)
        p_new = jnp.exp(s_new - m)
        denom = jnp.sum(p, axis=1, keepdims=True) + p_new + jnp.exp(sink - m)
        probs.append((p.astype(BF16), p_new, denom))
    for i in range(bs):
        p, p_new, denom = probs[i]
        o = (_dot(p, cv_ref[i].astype(BF16)) + p_new * vws[i]) / denom
        o_ref[i * N_HEADS:(i + 1) * N_HEADS, :] = jnp.where(own, o, 0.0).astype(BF16)
    for i in range(bs):
        nk_ref[i, 0:WINDOW - 1, :] = ck_ref[i, 1:WINDOW, :]
        nk_ref[i, WINDOW - 1:WINDOW, :] = kns[i]
        nv_ref[i, 0:WINDOW - 1, :] = cv_ref[i, 1:WINDOW, :]
        nv_ref[i, WINDOW - 1:WINDOW, :] = vws[i]


def _sample_attn(qsel, knew, vnew, cache_k, cache_v, bias_s, sink_col):
    nb = knew.shape[0]
    bs = SAMPLE_BLOCK
    cache_block = pl.BlockSpec((bs, WINDOW, KV_WIDTH), lambda i: (i, 0, 0))
    return pl.pallas_call(
        _sample_attn_kernel,
        grid=(nb // bs,),
        in_specs=[pl.BlockSpec((bs * N_HEADS, KV_WIDTH), lambda i: (i, 0)),
                  pl.BlockSpec((bs, KV_WIDTH), lambda i: (i, 0)),
                  pl.BlockSpec((bs, KV_WIDTH), lambda i: (i, 0)),
                  cache_block, cache_block,
                  _whole((N_HEADS, WINDOW)), _whole((N_HEADS, 1))],
        out_specs=[pl.BlockSpec((bs * N_HEADS, KV_WIDTH), lambda i: (i, 0)), cache_block, cache_block],
        out_shape=[
            jax.ShapeDtypeStruct((nb * N_HEADS, KV_WIDTH), BF16),
            jax.ShapeDtypeStruct((nb, WINDOW, KV_WIDTH), F32),
            jax.ShapeDtypeStruct((nb, WINDOW, KV_WIDTH), F32),
        ],
        compiler_params=_params(),
        name="sample_attn",
    )(qsel, knew, vnew, cache_k, cache_v, bias_s, sink_col)


def _sample_merge_kernel(o_ref, selt_ref, a_ref, ga_ref, gb_ref, x_ref, w_oa_ref, w_ob_ref, w_out_ref, x1_ref):
    nb = x_ref.shape[0]
    bst = _dot(selt_ref[...], o_ref[...]).astype(BF16)
    ob = _dot(bst[0:nb, :], w_ob_ref[0:KV_WIDTH, :])
    for g in range(1, GROUP):
        ob = ob + _dot(bst[g * nb:(g + 1) * nb, :], w_ob_ref[g * KV_WIDTH:(g + 1) * KV_WIDTH, :])
    hm = ga_ref[...] * _dot(a_ref[...], w_oa_ref[...]) + gb_ref[...] * ob
    x1_ref[...] = x_ref[...] + _dot(hm.astype(BF16), w_out_ref[...])


def _sample_merge(o, selt, a, ga, gb, xs2d, w_oa, w_ob, w_out):
    nb = xs2d.shape[0]
    return pl.pallas_call(
        _sample_merge_kernel,
        grid=(1,),
        in_specs=[_whole((nb * N_HEADS, KV_WIDTH)), _whole((GROUP * nb, nb * N_HEADS)),
                  _whole((nb, D_MODEL)), _whole((nb, D_MODEL)), _whole((nb, D_MODEL)), _whole((nb, D_MODEL)),
                  _resident((D_MODEL, D_MODEL)), _resident((D_MODEL, D_MODEL)), _resident((D_MODEL, D_MODEL))],
        out_specs=_whole((nb, D_MODEL)),
        out_shape=jax.ShapeDtypeStruct((nb, D_MODEL), F32),
        compiler_params=_params(),
        name="sample_merge",
    )(o, selt, a, ga, gb, xs2d, w_oa, w_ob, w_out)


def _head_perm(v):
    return v.reshape(N_KV_HEADS, GROUP).T.reshape(N_HEADS)


def _alibi_slopes():
    h = jnp.arange(1, N_HEADS + 1, dtype=F32)
    return jnp.exp2(-8.0 * h / N_HEADS)


def _selection_matrix(nb):
    r = np.arange(nb * N_HEADS)
    c = np.arange(GROUP * nb)
    same_sample = (r[:, None] // N_HEADS) == (c[None, :] % nb)
    same_member = ((r[:, None] % N_HEADS) // N_KV_HEADS) == (c[None, :] // nb)
    return (same_sample & same_member).astype(np.float32)


def kernel(x_prompt, x_sample, cache_k_win, cache_v_win, ln1_g, w_in, sgu_norm_g, sgu_w, sgu_b, attn_sinks,
           w_oa, w_ob, w_out, ln2_g, w_up, w_down, lnf_g):
    batch, seq, _ = x_prompt.shape
    dec_batch, dec_seq, _ = x_sample.shape
    depth = w_in.shape[0]
    assert depth == 1 and dec_seq == 1
    assert seq % TOKEN_BLOCK == 0 and TOKEN_BLOCK % CHUNK == 0 and dec_batch % SAMPLE_BLOCK == 0
    assert w_in.shape[-1] == IN_WIDTH

    wi = w_in[0]
    w_q_b = wi[:, OFF_Q:OFF_K].reshape(D_MODEL, N_KV_HEADS, GROUP, HEAD_DIM).transpose(0, 2, 1, 3).reshape(
        D_MODEL, D_MODEL).astype(BF16)
    w_ob_b = w_ob[0].reshape(N_KV_HEADS, GROUP, HEAD_DIM, D_MODEL).transpose(1, 0, 2, 3).reshape(
        D_MODEL, D_MODEL).astype(BF16)
    w_oa_b = w_oa[0].astype(BF16)
    w_out_b = w_out[0].astype(BF16)
    ln1 = ln1_g[0].reshape(1, D_MODEL)
    ln2 = ln2_g[0].reshape(1, D_MODEL)
    lnf = lnf_g.reshape(1, D_MODEL)
    sgu_g = sgu_norm_g[0].reshape(1, D_MODEL)
    sinks_p = _head_perm(attn_sinks[0].astype(F32))
    slopes_p = _head_perm(_alibi_slopes())

    xs2d = x_sample.reshape(dec_batch, D_MODEL)
    wdiag = jnp.repeat(sgu_w[0][:, 0, 0], SGU_GROUP_DIM).reshape(1, D_MODEL)
    b0 = jnp.repeat(sgu_b[0][:, 0], SGU_GROUP_DIM).reshape(1, D_MODEL)
    sel_np = _selection_matrix(dec_batch)
    sel = jnp.asarray(sel_np, BF16)
    selt = jnp.asarray(sel_np.T, BF16)
    bias_s = -slopes_p[:, None] * jnp.asarray(WINDOW - np.arange(WINDOW), F32)[None, :]

    w_in_b, qsel, knew, vnew, vn, a_s, ga_s, gb_s = _sample_proj(xs2d, ln1, wi, w_q_b, sgu_g, wdiag, b0, sel)
    o_s, nk, nv = _sample_attn(qsel, knew, vnew,
                               cache_k_win[0].reshape(dec_batch, WINDOW, KV_WIDTH),
                               cache_v_win[0].reshape(dec_batch, WINDOW, KV_WIDTH),
                               bias_s, sinks_p.reshape(N_HEADS, 1))
    xs1 = _sample_merge(o_s, selt, a_s, ga_s, gb_s, xs2d, w_oa_b, w_ob_b, w_out_b)

    qi = np.arange(CHUNK)[:, None] + CHUNK
    kj = np.arange(2 * CHUNK)[None, :]
    diff = qi - kj
    band = (diff >= 0) & (diff <= WINDOW)
    alibi = -(slopes_p * LOG2E)[:, None, None] * jnp.asarray(diff, F32)[None]
    bias = jnp.stack([jnp.where(band[None], alibi, NEG_BIG),
                      jnp.where((band & (kj >= CHUNK))[None], alibi, NEG_BIG)])
    bexp = jnp.repeat(sgu_b[0].T, SGU_GROUP_DIM, axis=1)

    x1, kwin, vwin, w_up_b, w_down_b = _mix_prompt(
        x_prompt.reshape(batch * seq, D_MODEL), ln1, w_in_b, w_q_b, sgu_g, sgu_w[0], bexp, sinks_p * LOG2E, bias,
        w_oa_b, w_ob_b, w_out_b, w_up[0], w_down[0], batch=batch, seq=seq)
    y_prompt, y_sample = _ffn(x1, xs1, ln2, w_up_b, w_down_b, lnf)

    win_shape = (depth, batch, WINDOW, N_KV_HEADS, HEAD_DIM)
    cache_shape = (depth, dec_batch, WINDOW, N_KV_HEADS, HEAD_DIM)
    return (y_prompt.reshape(batch, seq, D_MODEL),
            y_sample.reshape(dec_batch, dec_seq, D_MODEL),
            kwin.reshape(win_shape), vwin.reshape(win_shape),
            nk.reshape(cache_shape), nv.reshape(cache_shape),
            vn.reshape(depth, dec_batch, dec_seq, D_MODEL))
```

```python
import functools
import math

import numpy as np
import jax
import jax.numpy as jnp
from jax import lax
from jax.experimental import pallas as pl
from jax.experimental.pallas import tpu as pltpu

D_MODEL = 1024
N_HEADS = 16
HEAD_DIM = 64
N_KV_HEADS = 4
GROUP = N_HEADS // N_KV_HEADS
KV_WIDTH = N_KV_HEADS * HEAD_DIM
WINDOW = 128
CHUNK = 128
SGU_GROUPS = 8
SGU_GROUP_DIM = D_MODEL // SGU_GROUPS
D_FF = 4 * D_MODEL
FF_SLAB = 1024
EPS = 1e-6
NEG_BIG = -1e30
ATTN_SCALE = HEAD_DIM ** -0.5
LOG2E = math.log2(math.e)

IN_WIDTH = 5632
OFF_Q, OFF_K = 2048, 3072
C_U, C_V, C_K, C_VA, C_GA, C_GB, C_END = 0, 1024, 2048, 2304, 2560, 3584, 4608
W_BLOCK = 512

TOKEN_BLOCK = 512
SAMPLE_BLOCK = 16
VMEM_LIMIT_BYTES = 58 * 1024 * 1024

F32 = jnp.float32
BF16 = jnp.bfloat16


def _rmsnorm(x, g):
    ms = jnp.mean(x * x, axis=-1, keepdims=True)
    return x * lax.rsqrt(ms + EPS) * g


def _gelu_tanh(x):
    c = math.sqrt(2.0 / math.pi)
    return x * (0.5 * (1.0 + jnp.tanh(c * (x + 0.044715 * (x * x * x)))))


def _dot(a, b):
    return jnp.dot(a, b, preferred_element_type=F32)


def _dot_nt(a, b):
    return lax.dot_general(a, b, (((1,), (1,)), ((), ())), preferred_element_type=F32)


def _resident(shape):
    zeros = (0,) * len(shape)
    return pl.BlockSpec(shape, lambda *_: zeros, pipeline_mode=pl.Buffered(1))


def _whole(shape):
    zeros = (0,) * len(shape)
    return pl.BlockSpec(shape, lambda *_: zeros)


def _params():
    return pltpu.CompilerParams(dimension_semantics=("arbitrary",), vmem_limit_bytes=VMEM_LIMIT_BYTES)


def _mix_prompt_kernel(x_ref, ln1_ref, w_in_ref, w_q_ref, sgu_g_ref, sgu_w_ref, bexp_ref,
                       sink_ref, bias_ref, w_oa_ref, w_ob_ref, w_out_ref, w_up_blk_ref, w_down_blk_ref,
                       x1_ref, kwin_ref, vwin_ref, w_up_bf_ref, w_down_bf_ref,
                       qs_scr, kt_scr, vm_scr, kprev_scr, vprev_scr, vn_scr, u_scr, gate_scr, a_scr, b_scr, wt_scr,
                       *, steps_per_seq):
    step = pl.program_id(0)
    tb = x_ref.shape[0]
    nblk = tb // CHUNK
    first = (step % steps_per_seq) == 0
    rd = step % 2
    wr = 1 - rd

    @pl.when(step == 0)
    def _():
        row = lax.broadcasted_iota(jnp.int32, (CHUNK, CHUNK), 0)
        col = lax.broadcasted_iota(jnp.int32, (CHUNK, CHUNK), 1)
        for g in range(SGU_GROUPS):
            wt_scr[g] = jnp.where(row >= col, sgu_w_ref[g], 0.0).astype(BF16)
        kt_scr[...] = jnp.zeros(kt_scr.shape, BF16)
        vm_scr[...] = jnp.zeros(vm_scr.shape, BF16)
        kprev_scr[...] = jnp.zeros(kprev_scr.shape, BF16)
        vprev_scr[...] = jnp.zeros(vprev_scr.shape, BF16)

    @pl.when(first)
    def _():
        kprev_scr[rd] = jnp.zeros(kprev_scr.shape[1:], BF16)
        vprev_scr[rd] = jnp.zeros(vprev_scr.shape[1:], BF16)

    w_up_bf_ref[...] = w_up_blk_ref[...].astype(BF16)
    w_down_bf_ref[...] = w_down_blk_ref[...].astype(BF16)

    x = x_ref[...]
    xn = _rmsnorm(x, ln1_ref[...]).astype(BF16)

    q = (_dot(xn, w_q_ref[...]) * (ATTN_SCALE * LOG2E)).astype(BF16)
    for c in range(nblk):
        for g in range(GROUP):
            qs_scr[c, g * CHUNK:(g + 1) * CHUNK, :] = q[c * CHUNK:(c + 1) * CHUNK, g * KV_WIDTH:(g + 1) * KV_WIDTH]

    k = _dot(xn, w_in_ref[:, C_K:C_VA])
    va = _dot(xn, w_in_ref[:, C_VA:C_GA])
    kt_f32 = k.T
    kwin_ref[...] = kt_f32[:, tb - WINDOW:]
    vwin_ref[...] = va[tb - WINDOW:, :].T
    kt = kt_f32.astype(BF16)
    vab = va.astype(BF16)
    for kvh in range(N_KV_HEADS):
        own = slice(kvh * HEAD_DIM, (kvh + 1) * HEAD_DIM)
        for c in range(nblk):
            kt_scr[kvh, c, own, :] = kt[own, c * CHUNK:(c + 1) * CHUNK]
        vm_scr[kvh, :, own] = vab[:, own]
        kprev_scr[wr, kvh, own, :] = kt[own, tb - WINDOW:]
        vprev_scr[wr, kvh, :, own] = vab[tb - WINDOW:, own]

    first_i = first.astype(jnp.int32)

    def attn_scores(c):
        qs = qs_scr[c]
        out = []
        for kvh in range(N_KV_HEADS):
            k_prev = kprev_scr[rd, kvh] if c == 0 else kt_scr[kvh, c - 1]
            out.append(_dot(qs, jnp.concatenate([k_prev, kt_scr[kvh, c]], axis=1)))
        return out

    def attn_softmax(c, scores):
        bias_sel = first_i if c == 0 else 0
        out = []
        for kvh in range(N_KV_HEADS):
            ps = []
            for g in range(GROUP):
                h = g * N_KV_HEADS + kvh
                s = scores[kvh][g * CHUNK:(g + 1) * CHUNK, :] + bias_ref[bias_sel, h]
                sink = sink_ref[h]
                m = jnp.max(s, axis=1, keepdims=True)
                p = jnp.exp2(s - m)
                denom = jnp.sum(p, axis=1, keepdims=True) + jnp.exp2(sink - m)
                ps.append((p * (1.0 / denom)).astype(BF16))
            out.append(jnp.concatenate(ps, axis=0))
        return out

    def attn_values(c, probs):
        rows = slice(c * CHUNK, (c + 1) * CHUNK)
        acc = None
        for kvh in range(N_KV_HEADS):
            if c == 0:
                v_band = jnp.concatenate([vprev_scr[rd, kvh], vm_scr[kvh, 0:CHUNK, :]], axis=0)
            else:
                v_band = vm_scr[kvh, (c - 1) * CHUNK:(c + 1) * CHUNK, :]
            o = _dot(probs[kvh], v_band)
            acc = o if acc is None else acc + o
        for g in range(GROUP):
            b_scr[rows, g * KV_WIDTH:(g + 1) * KV_WIDTH] = acc[g * CHUNK:(g + 1) * CHUNK, :].astype(BF16)

    def sgu_chunk(c):
        rows = slice(c * CHUNK, (c + 1) * CHUNK)
        vn_c = vn_scr[rows, :]
        mixed = jnp.concatenate(
            [_dot(wt_scr[g], vn_c[:, g * SGU_GROUP_DIM:(g + 1) * SGU_GROUP_DIM]) for g in range(SGU_GROUPS)],
            axis=1) + bexp_ref[...]
        a_scr[rows, :] = (u_scr[rows, :] * mixed).astype(BF16)

    def tail_v(h):
        vn_scr[...] = _rmsnorm(_gelu_tanh(h), sgu_g_ref[...]).astype(BF16)

    def tail_u(h):
        u_scr[...] = _gelu_tanh(h)

    def tail_ga(h):
        gate_scr[0] = jax.nn.sigmoid(h)

    def tail_gb(h):
        gate_scr[1] = jax.nn.sigmoid(h)

    fillers = [
        (lambda: _dot(xn, w_in_ref[:, C_V:C_K]), tail_v),
        (lambda: _dot(xn, w_in_ref[:, C_U:C_V]), tail_u),
        (lambda: _dot(xn, w_in_ref[:, C_GA:C_GB]), tail_ga),
        (lambda: _dot(xn, w_in_ref[:, C_GB:C_END]), tail_gb),
    ]
    for c in range(nblk):
        scores = attn_scores(c)
        proj = fillers[c][0]() if c < len(fillers) else None
        probs = attn_softmax(c, scores)
        attn_values(c, probs)
        if proj is not None:
            fillers[c][1](proj)
    for matmul, tail in fillers[nblk:]:
        tail(matmul())
    for c in range(nblk):
        sgu_chunk(c)

    hm = gate_scr[0] * _dot(a_scr[...], w_oa_ref[...]) + gate_scr[1] * _dot(b_scr[...], w_ob_ref[...])
    x1_ref[...] = x + _dot(hm.astype(BF16), w_out_ref[...])


def _mix_prompt(x2d, ln1, w_in, w_q, sgu_g, sgu_w, bexp, sinks, bias, w_oa, w_ob, w_out, w_up_f32, w_down_f32,
                *, batch, seq):
    n = x2d.shape[0]
    tb = TOKEN_BLOCK
    nblk = tb // CHUNK
    steps = n // tb
    steps_per_seq = seq // tb
    row_block = pl.BlockSpec((tb, D_MODEL), lambda i: (i, 0))
    win_block = pl.BlockSpec((None, KV_WIDTH, WINDOW), lambda i: (i // steps_per_seq, 0, 0))
    up_block = pl.BlockSpec((D_MODEL // steps, D_FF), lambda i: (i, 0))
    down_block = pl.BlockSpec((D_FF // steps, D_MODEL), lambda i: (i, 0))
    return pl.pallas_call(
        functools.partial(_mix_prompt_kernel, steps_per_seq=steps_per_seq),
        grid=(n // tb,),
        in_specs=[
            row_block,
            _resident((1, D_MODEL)),
            _resident((D_MODEL, C_END)),
            _resident((D_MODEL, D_MODEL)),
            _resident((1, D_MODEL)),
            _resident((SGU_GROUPS, CHUNK, CHUNK)),
            _resident((CHUNK, D_MODEL)),
            pl.BlockSpec(memory_space=pltpu.SMEM),
            _resident((2, N_HEADS, CHUNK, 2 * CHUNK)),
            _resident((D_MODEL, D_MODEL)),
            _resident((D_MODEL, D_MODEL)),
            _resident((D_MODEL, D_MODEL)),
            up_block,
            down_block,
        ],
        out_specs=[row_block, win_block, win_block, up_block, down_block],
        out_shape=[
            jax.ShapeDtypeStruct((n, D_MODEL), F32),
            jax.ShapeDtypeStruct((batch, KV_WIDTH, WINDOW), F32),
            jax.ShapeDtypeStruct((batch, KV_WIDTH, WINDOW), F32),
            jax.ShapeDtypeStruct((D_MODEL, D_FF), BF16),
            jax.ShapeDtypeStruct((D_FF, D_MODEL), BF16),
        ],
        scratch_shapes=[
            pltpu.VMEM((nblk, GROUP * CHUNK, KV_WIDTH), BF16),
            pltpu.VMEM((N_KV_HEADS, nblk, KV_WIDTH, CHUNK), BF16),
            pltpu.VMEM((N_KV_HEADS, tb, KV_WIDTH), BF16),
            pltpu.VMEM((2, N_KV_HEADS, KV_WIDTH, CHUNK), BF16),
            pltpu.VMEM((2, N_KV_HEADS, WINDOW, KV_WIDTH), BF16),
            pltpu.VMEM((tb, D_MODEL), BF16),
            pltpu.VMEM((tb, D_MODEL), F32),
            pltpu.VMEM((2, tb, D_MODEL), F32),
            pltpu.VMEM((tb, D_MODEL), BF16),
            pltpu.VMEM((tb, D_MODEL), BF16),
            pltpu.VMEM((SGU_GROUPS, CHUNK, CHUNK), BF16),
        ],
        compiler_params=_params(),
        name="mix_prompt",
    )(x2d, ln1, w_in, w_q, sgu_g, sgu_w, bexp, sinks, bias, w_oa, w_ob, w_out, w_up_f32, w_down_f32)


def _ffn_rows(x, ln2_ref, w_up_ref, w_down_ref, lnf_ref):
    xn = _rmsnorm(x, ln2_ref[...]).astype(BF16)
    acc = x
    for j in range(D_FF // FF_SLAB):
        h = _dot(xn, w_up_ref[:, j * FF_SLAB:(j + 1) * FF_SLAB])
        h = jnp.square(jnp.maximum(h, 0.0)).astype(BF16)
        acc = acc + _dot(h, w_down_ref[j * FF_SLAB:(j + 1) * FF_SLAB, :])
    return _rmsnorm(acc, lnf_ref[...])


def _ffn_kernel(x_ref, xs_ref, ln2_ref, w_up_ref, w_down_ref, lnf_ref, y_ref, ys_ref):
    i = pl.program_id(0)
    last = pl.num_programs(0) - 1

    @pl.when(i < last)
    def _():
        y_ref[...] = _ffn_rows(x_ref[...], ln2_ref, w_up_ref, w_down_ref, lnf_ref)

    @pl.when(i == last)
    def _():
        ys_ref[...] = _ffn_rows(xs_ref[...], ln2_ref, w_up_ref, w_down_ref, lnf_ref)


def _ffn(x2d, xs2d, ln2, w_up, w_down, lnf):
    n = x2d.shape[0]
    nb = xs2d.shape[0]
    n_prompt_steps = n // TOKEN_BLOCK
    row_block = pl.BlockSpec((TOKEN_BLOCK, D_MODEL), lambda i: (jnp.minimum(i, n_prompt_steps - 1), 0))
    return pl.pallas_call(
        _ffn_kernel,
        grid=(n_prompt_steps + 1,),
        in_specs=[row_block, _resident((nb, D_MODEL)), _resident((1, D_MODEL)), _resident((D_MODEL, D_FF)),
                  _resident((D_FF, D_MODEL)), _resident((1, D_MODEL))],
        out_specs=[row_block, _whole((nb, D_MODEL))],
        out_shape=[jax.ShapeDtypeStruct((n, D_MODEL), F32), jax.ShapeDtypeStruct((nb, D_MODEL), F32)],
        compiler_params=_params(),
        name="ffn",
    )(x2d, xs2d, ln2, w_up, w_down, lnf)


def _sample_proj_kernel(x_ref, ln1_ref, w_blk_ref, w_q_ref, sgu_g_ref, wdiag_ref, b0_ref, sel_ref,
                        w_bf_ref, qsel_ref, knew_ref, vnew_ref, vn_ref, a_ref, ga_ref, gb_ref,
                        xn_scr, h_scr):
    j = pl.program_id(0)
    nb = x_ref.shape[0]

    @pl.when(j == 0)
    def _():
        xn_scr[...] = _rmsnorm(x_ref[...], ln1_ref[...]).astype(BF16)

    wb = w_blk_ref[...].astype(BF16)
    w_bf_ref[...] = wb
    h_scr[j] = _dot(xn_scr[...], wb)

    @pl.when(j == pl.num_programs(0) - 1)
    def _():
        def cols(lo, hi):
            blocks = [h_scr[b] for b in range(lo // W_BLOCK, hi // W_BLOCK)]
            return blocks[0] if len(blocks) == 1 else jnp.concatenate(blocks, axis=1)

        u = _gelu_tanh(cols(C_U, C_V))
        v = _gelu_tanh(cols(C_V, C_K))
        vn = _rmsnorm(v, sgu_g_ref[...])
        vn_ref[...] = vn
        a_ref[...] = (u * (vn * wdiag_ref[...] + b0_ref[...])).astype(BF16)
        kv = cols(C_K, C_GA)
        knew_ref[...] = kv[:, 0:KV_WIDTH]
        vnew_ref[...] = kv[:, KV_WIDTH:2 * KV_WIDTH]
        ga_ref[...] = jax.nn.sigmoid(cols(C_GA, C_GB))
        gb_ref[...] = jax.nn.sigmoid(cols(C_GB, C_END))
        q = _dot(xn_scr[...], w_q_ref[...]) * ATTN_SCALE
        qstack = jnp.concatenate([q[:, g * KV_WIDTH:(g + 1) * KV_WIDTH] for g in range(GROUP)], axis=0).astype(BF16)
        qrep = _dot(sel_ref[...], qstack)
        row_kvh = lax.broadcasted_iota(jnp.int32, (nb * N_HEADS, KV_WIDTH), 0) % N_KV_HEADS
        lane_kvh = lax.broadcasted_iota(jnp.int32, (nb * N_HEADS, KV_WIDTH), 1) // HEAD_DIM
        qsel_ref[...] = jnp.where(row_kvh == lane_kvh, qrep, 0.0).astype(BF16)


def _sample_proj(xs2d, ln1, w_in_f32, w_q, sgu_g, wdiag, b0, sel):
    nb = xs2d.shape[0]
    n_blocks = C_END // W_BLOCK
    q_blocks = (OFF_K - OFF_Q) // W_BLOCK
    first_after_q = OFF_Q // W_BLOCK
    w_block = pl.BlockSpec((D_MODEL, W_BLOCK), lambda j: (0, jnp.where(j >= first_after_q, j + q_blocks, j)))
    return pl.pallas_call(
        _sample_proj_kernel,
        grid=(n_blocks,),
        in_specs=[_whole((nb, D_MODEL)), _whole((1, D_MODEL)), w_block, _resident((D_MODEL, D_MODEL)),
                  _whole((1, D_MODEL)), _whole((1, D_MODEL)), _whole((1, D_MODEL)),
                  _resident((nb * N_HEADS, GROUP * nb))],
        out_specs=[pl.BlockSpec((D_MODEL, W_BLOCK), lambda j: (0, j)),
                   _whole((nb * N_HEADS, KV_WIDTH)), _whole((nb, KV_WIDTH)), _whole((nb, KV_WIDTH)),
                   _whole((nb, D_MODEL)), _whole((nb, D_MODEL)),
                   _whole((nb, D_MODEL)), _whole((nb, D_MODEL))],
        out_shape=[
            jax.ShapeDtypeStruct((D_MODEL, C_END), BF16),
            jax.ShapeDtypeStruct((nb * N_HEADS, KV_WIDTH), BF16),
            jax.ShapeDtypeStruct((nb, KV_WIDTH), F32),
            jax.ShapeDtypeStruct((nb, KV_WIDTH), F32),
            jax.ShapeDtypeStruct((nb, D_MODEL), F32),
            jax.ShapeDtypeStruct((nb, D_MODEL), BF16),
            jax.ShapeDtypeStruct((nb, D_MODEL), F32),
            jax.ShapeDtypeStruct((nb, D_MODEL), F32),
        ],
        scratch_shapes=[pltpu.VMEM((nb, D_MODEL), BF16), pltpu.VMEM((n_blocks, nb, W_BLOCK), F32)],
        compiler_params=_params(),
        name="sample_proj",
    )(xs2d, ln1, w_in_f32, w_q, sgu_g, wdiag, b0, sel)


def _sample_attn_kernel(qsel_ref, knew_ref, vnew_ref, ck_ref, cv_ref, bias_ref, sink_ref,
                        o_ref, nk_ref, nv_ref):
    bs = ck_ref.shape[0]
    row_kvh = lax.broadcasted_iota(jnp.int32, (N_HEADS, KV_WIDTH), 0) % N_KV_HEADS
    lane_kvh = lax.broadcasted_iota(jnp.int32, (N_HEADS, KV_WIDTH), 1) // HEAD_DIM
    own = row_kvh == lane_kvh
    bias = bias_ref[...]
    sink = sink_ref[...]

    qss = [qsel_ref[i * N_HEADS:(i + 1) * N_HEADS, :] for i in range(bs)]
    kns = [knew_ref[i:i + 1, :] for i in range(bs)]
    vws = [vnew_ref[i:i + 1, :] for i in range(bs)]
    scores = [_dot(qss[i], ck_ref[i].astype(BF16)) + bias for i in range(bs)]
    probs = []
    for i in range(bs):
        s = scores[i]
        s_new = jnp.sum(qss[i].astype(F32) * kns[i], axis=1, keepdims=True)
        m = jnp.maximum(jnp.maximum(jnp.max(s, axis=1, keepdims=True), s_new), sink)
        p = jnp.exp(s - m)
        p_new = jnp.exp(s_new - m)
        denom = jnp.sum(p, axis=1, keepdims=True) + p_new + jnp.exp(sink - m)
        probs.append((p.astype(BF16), p_new, denom))
    for i in range(bs):
        p, p_new, denom = probs[i]
        o = (_dot_nt(p, cv_ref[i].astype(BF16)) + p_new * vws[i]) / denom
        o_ref[i * N_HEADS:(i + 1) * N_HEADS, :] = jnp.where(own, o, 0.0).astype(BF16)

    pad = jnp.zeros((WINDOW - bs, KV_WIDTH), F32)
    kn_t = jnp.concatenate([knew_ref[...], pad], axis=0).T
    vw_t = jnp.concatenate([vnew_ref[...], pad], axis=0).T
    last_lane = lax.broadcasted_iota(jnp.int32, (KV_WIDTH, WINDOW), 1) == WINDOW - 1
    for i in range(bs):
        nk_ref[i] = jnp.where(last_lane, kn_t[:, i:i + 1], pltpu.roll(ck_ref[i], WINDOW - 1, 1))
        nv_ref[i] = jnp.where(last_lane, vw_t[:, i:i + 1], pltpu.roll(cv_ref[i], WINDOW - 1, 1))


def _sample_attn(qsel, knew, vnew, cache_k, cache_v, bias_s, sink_col):
    nb = knew.shape[0]
    bs = SAMPLE_BLOCK
    cache_block = pl.BlockSpec((bs, KV_WIDTH, WINDOW), lambda i: (i, 0, 0))
    return pl.pallas_call(
        _sample_attn_kernel,
        grid=(nb // bs,),
        in_specs=[pl.BlockSpec((bs * N_HEADS, KV_WIDTH), lambda i: (i, 0)),
                  pl.BlockSpec((bs, KV_WIDTH), lambda i: (i, 0)),
                  pl.BlockSpec((bs, KV_WIDTH), lambda i: (i, 0)),
                  cache_block, cache_block,
                  _whole((N_HEADS, WINDOW)), _whole((N_HEADS, 1))],
        out_specs=[pl.BlockSpec((bs * N_HEADS, KV_WIDTH), lambda i: (i, 0)), cache_block, cache_block],
        out_shape=[
            jax.ShapeDtypeStruct((nb * N_HEADS, KV_WIDTH), BF16),
            jax.ShapeDtypeStruct((nb, KV_WIDTH, WINDOW), F32),
            jax.ShapeDtypeStruct((nb, KV_WIDTH, WINDOW), F32),
        ],
        compiler_params=_params(),
        name="sample_attn",
    )(qsel, knew, vnew, cache_k, cache_v, bias_s, sink_col)


def _sample_merge_kernel(o_ref, selt_ref, a_ref, ga_ref, gb_ref, x_ref, w_oa_ref, w_ob_ref, w_out_ref, x1_ref):
    nb = x_ref.shape[0]
    bst = _dot(selt_ref[...], o_ref[...]).astype(BF16)
    ob = _dot(bst[0:nb, :], w_ob_ref[0:KV_WIDTH, :])
    for g in range(1, GROUP):
        ob = ob + _dot(bst[g * nb:(g + 1) * nb, :], w_ob_ref[g * KV_WIDTH:(g + 1) * KV_WIDTH, :])
    hm = ga_ref[...] * _dot(a_ref[...], w_oa_ref[...]) + gb_ref[...] * ob
    x1_ref[...] = x_ref[...] + _dot(hm.astype(BF16), w_out_ref[...])


def _sample_merge(o, selt, a, ga, gb, xs2d, w_oa, w_ob, w_out):
    nb = xs2d.shape[0]
    return pl.pallas_call(
        _sample_merge_kernel,
        grid=(1,),
        in_specs=[_whole((nb * N_HEADS, KV_WIDTH)), _whole((GROUP * nb, nb * N_HEADS)),
                  _whole((nb, D_MODEL)), _whole((nb, D_MODEL)), _whole((nb, D_MODEL)), _whole((nb, D_MODEL)),
                  _resident((D_MODEL, D_MODEL)), _resident((D_MODEL, D_MODEL)), _resident((D_MODEL, D_MODEL))],
        out_specs=_whole((nb, D_MODEL)),
        out_shape=jax.ShapeDtypeStruct((nb, D_MODEL), F32),
        compiler_params=_params(),
        name="sample_merge",
    )(o, selt, a, ga, gb, xs2d, w_oa, w_ob, w_out)


def _head_perm(v):
    return v.reshape(N_KV_HEADS, GROUP).T.reshape(N_HEADS)


def _alibi_slopes():
    h = jnp.arange(1, N_HEADS + 1, dtype=F32)
    return jnp.exp2(-8.0 * h / N_HEADS)


def _selection_matrix(nb):
    r = np.arange(nb * N_HEADS)
    c = np.arange(GROUP * nb)
    same_sample = (r[:, None] // N_HEADS) == (c[None, :] % nb)
    same_member = ((r[:, None] % N_HEADS) // N_KV_HEADS) == (c[None, :] // nb)
    return (same_sample & same_member).astype(np.float32)


def kernel(x_prompt, x_sample, cache_k_win, cache_v_win, ln1_g, w_in, sgu_norm_g, sgu_w, sgu_b, attn_sinks,
           w_oa, w_ob, w_out, ln2_g, w_up, w_down, lnf_g):
    batch, seq, _ = x_prompt.shape
    dec_batch, dec_seq, _ = x_sample.shape
    depth = w_in.shape[0]
    assert depth == 1 and dec_seq == 1
    assert seq % TOKEN_BLOCK == 0 and TOKEN_BLOCK % CHUNK == 0 and dec_batch % SAMPLE_BLOCK == 0
    assert w_in.shape[-1] == IN_WIDTH

    wi = w_in[0]
    w_q_b = wi[:, OFF_Q:OFF_K].reshape(D_MODEL, N_KV_HEADS, GROUP, HEAD_DIM).transpose(0, 2, 1, 3).reshape(
        D_MODEL, D_MODEL).astype(BF16)
    w_ob_b = w_ob[0].reshape(N_KV_HEADS, GROUP, HEAD_DIM, D_MODEL).transpose(1, 0, 2, 3).reshape(
        D_MODEL, D_MODEL).astype(BF16)
    w_oa_b = w_oa[0].astype(BF16)
    w_out_b = w_out[0].astype(BF16)
    ln1 = ln1_g[0].reshape(1, D_MODEL)
    ln2 = ln2_g[0].reshape(1, D_MODEL)
    lnf = lnf_g.reshape(1, D_MODEL)
    sgu_g = sgu_norm_g[0].reshape(1, D_MODEL)
    sinks_p = _head_perm(attn_sinks[0].astype(F32))
    slopes_p = _head_perm(_alibi_slopes())

    xs2d = x_sample.reshape(dec_batch, D_MODEL)
    wdiag = jnp.repeat(sgu_w[0][:, 0, 0], SGU_GROUP_DIM).reshape(1, D_MODEL)
    b0 = jnp.repeat(sgu_b[0][:, 0], SGU_GROUP_DIM).reshape(1, D_MODEL)
    sel_np = _selection_matrix(dec_batch)
    sel = jnp.asarray(sel_np, BF16)
    selt = jnp.asarray(sel_np.T, BF16)
    bias_s = -slopes_p[:, None] * jnp.asarray(WINDOW - np.arange(WINDOW), F32)[None, :]

    w_in_b, qsel, knew, vnew, vn, a_s, ga_s, gb_s = _sample_proj(xs2d, ln1, wi, w_q_b, sgu_g, wdiag, b0, sel)
    def to_feature_major(c):
        return c[0].transpose(0, 2, 3, 1).reshape(dec_batch, KV_WIDTH, WINDOW)

    def from_feature_major(c):
        return c.reshape(c.shape[0], N_KV_HEADS, HEAD_DIM, WINDOW).transpose(0, 3, 1, 2)[None]

    o_s, nk, nv = _sample_attn(qsel, knew, vnew, to_feature_major(cache_k_win), to_feature_major(cache_v_win),
                               bias_s, sinks_p.reshape(N_HEADS, 1))
    xs1 = _sample_merge(o_s, selt, a_s, ga_s, gb_s, xs2d, w_oa_b, w_ob_b, w_out_b)

    qi = np.arange(CHUNK)[:, None] + CHUNK
    kj = np.arange(2 * CHUNK)[None, :]
    diff = qi - kj
    band = (diff >= 0) & (diff <= WINDOW)
    alibi = -(slopes_p * LOG2E)[:, None, None] * jnp.asarray(diff, F32)[None]
    valid = np.stack([band, band & (kj >= CHUNK)])[:, None]
    bias = jnp.where(valid, alibi[None], NEG_BIG)
    bexp = jnp.repeat(sgu_b[0].T, SGU_GROUP_DIM, axis=1)

    x1, kwin, vwin, w_up_b, w_down_b = _mix_prompt(
        x_prompt.reshape(batch * seq, D_MODEL), ln1, w_in_b, w_q_b, sgu_g, sgu_w[0], bexp, sinks_p * LOG2E, bias,
        w_oa_b, w_ob_b, w_out_b, w_up[0], w_down[0], batch=batch, seq=seq)
    y_prompt, y_sample = _ffn(x1, xs1, ln2, w_up_b, w_down_b, lnf)

    return (y_prompt.reshape(batch, seq, D_MODEL),
            y_sample.reshape(dec_batch, dec_seq, D_MODEL),
            from_feature_major(kwin), from_feature_major(vwin),
            from_feature_major(nk), from_feature_major(nv),
            vn.reshape(depth, dec_batch, dec_seq, D_MODEL))
```

```python
import functools
import math

import numpy as np
import jax
import jax.numpy as jnp
from jax import lax
from jax.experimental import pallas as pl
from jax.experimental.pallas import tpu as pltpu

D_MODEL = 1024
N_HEADS = 16
HEAD_DIM = 64
N_KV_HEADS = 4
GROUP = N_HEADS // N_KV_HEADS
KV_WIDTH = N_KV_HEADS * HEAD_DIM
WINDOW = 128
CHUNK = 128
SGU_GROUPS = 8
SGU_GROUP_DIM = D_MODEL // SGU_GROUPS
D_FF = 4 * D_MODEL
FF_SLAB = 1024
EPS = 1e-6
NEG_BIG = -1e30
ATTN_SCALE = HEAD_DIM ** -0.5
LOG2E = math.log2(math.e)

IN_WIDTH = 5632
OFF_Q, OFF_K = 2048, 3072
C_U, C_V, C_K, C_VA, C_GA, C_GB, C_END = 0, 1024, 2048, 2304, 2560, 3584, 4608
W_BLOCK = 512

TOKEN_BLOCK = 512
FFN_BLOCK = 1024
SAMPLE_BLOCK = 16
VMEM_LIMIT_BYTES = 58 * 1024 * 1024

F32 = jnp.float32
BF16 = jnp.bfloat16


def _rmsnorm(x, g):
    ms = jnp.mean(x * x, axis=-1, keepdims=True)
    return x * lax.rsqrt(ms + EPS) * g


def _gelu_tanh(x):
    c = math.sqrt(2.0 / math.pi)
    return x * (0.5 * (1.0 + jnp.tanh(c * (x + 0.044715 * (x * x * x)))))


def _dot(a, b):
    return jnp.dot(a, b, preferred_element_type=F32)


def _dot_nt(a, b):
    return lax.dot_general(a, b, (((1,), (1,)), ((), ())), preferred_element_type=F32)


def _resident(shape):
    zeros = (0,) * len(shape)
    return pl.BlockSpec(shape, lambda *_: zeros, pipeline_mode=pl.Buffered(1))


def _whole(shape):
    zeros = (0,) * len(shape)
    return pl.BlockSpec(shape, lambda *_: zeros)


def _params():
    return pltpu.CompilerParams(dimension_semantics=("arbitrary",), vmem_limit_bytes=VMEM_LIMIT_BYTES)


def _mix_prompt_kernel(x_ref, ln1_ref, w_in_ref, w_q_ref, sgu_g_ref, sgu_w_ref, bexp_ref,
                       sink_ref, bias_ref, w_oa_ref, w_ob_ref, w_out_ref, w_up_blk_ref, w_down_blk_ref,
                       x1_ref, kwin_ref, vwin_ref, w_up_bf_ref, w_down_bf_ref,
                       qs_scr, kt_scr, vm_scr, kprev_scr, vprev_scr, vn_scr, u_scr, gate_scr, a_scr, b_scr, wt_scr,
                       *, steps_per_seq):
    step = pl.program_id(0)
    tb = x_ref.shape[0]
    nblk = tb // CHUNK
    first = (step % steps_per_seq) == 0
    rd = step % 2
    wr = 1 - rd

    @pl.when(step == 0)
    def _():
        row = lax.broadcasted_iota(jnp.int32, (CHUNK, CHUNK), 0)
        col = lax.broadcasted_iota(jnp.int32, (CHUNK, CHUNK), 1)
        for g in range(SGU_GROUPS):
            wt_scr[g] = jnp.where(row >= col, sgu_w_ref[g], 0.0).astype(BF16)
        kt_scr[...] = jnp.zeros(kt_scr.shape, BF16)
        vm_scr[...] = jnp.zeros(vm_scr.shape, BF16)
        kprev_scr[...] = jnp.zeros(kprev_scr.shape, BF16)
        vprev_scr[...] = jnp.zeros(vprev_scr.shape, BF16)

    @pl.when(first)
    def _():
        kprev_scr[rd] = jnp.zeros(kprev_scr.shape[1:], BF16)
        vprev_scr[rd] = jnp.zeros(vprev_scr.shape[1:], BF16)

    w_up_bf_ref[...] = w_up_blk_ref[...].astype(BF16)
    w_down_bf_ref[...] = w_down_blk_ref[...].astype(BF16)

    x = x_ref[...]
    xn = _rmsnorm(x, ln1_ref[...]).astype(BF16)

    q = _dot(xn, w_q_ref[...])
    k = _dot(xn, w_in_ref[:, C_K:C_VA])
    va = _dot(xn, w_in_ref[:, C_VA:C_GA])
    h_v = _dot(xn, w_in_ref[:, C_V:C_K])

    q = (q * (ATTN_SCALE * LOG2E)).astype(BF16)
    for c in range(nblk):
        for g in range(GROUP):
            qs_scr[c, g * CHUNK:(g + 1) * CHUNK, :] = q[c * CHUNK:(c + 1) * CHUNK, g * KV_WIDTH:(g + 1) * KV_WIDTH]

    kt_f32 = k.T
    kwin_ref[...] = kt_f32[:, tb - WINDOW:]
    vwin_ref[...] = va[tb - WINDOW:, :].T
    kt = kt_f32.astype(BF16)
    vab = va.astype(BF16)
    for kvh in range(N_KV_HEADS):
        own = slice(kvh * HEAD_DIM, (kvh + 1) * HEAD_DIM)
        for c in range(nblk):
            kt_scr[kvh, c, own, :] = kt[own, c * CHUNK:(c + 1) * CHUNK]
        vm_scr[kvh, :, own] = vab[:, own]
        kprev_scr[wr, kvh, own, :] = kt[own, tb - WINDOW:]
        vprev_scr[wr, kvh, :, own] = vab[tb - WINDOW:, own]

    first_i = first.astype(jnp.int32)

    def attn_scores(c):
        qs = qs_scr[c]
        out = []
        for kvh in range(N_KV_HEADS):
            k_prev = kprev_scr[rd, kvh] if c == 0 else kt_scr[kvh, c - 1]
            out.append(_dot(qs, jnp.concatenate([k_prev, kt_scr[kvh, c]], axis=1)))
        return out

    def attn_softmax(c, scores):
        bias_sel = first_i if c == 0 else 0
        out = []
        for kvh in range(N_KV_HEADS):
            ps = []
            for g in range(GROUP):
                h = g * N_KV_HEADS + kvh
                s = scores[kvh][g * CHUNK:(g + 1) * CHUNK, :] + bias_ref[bias_sel, h]
                sink = sink_ref[h]
                m = jnp.max(s, axis=1, keepdims=True)
                p = jnp.exp2(s - m)
                denom = jnp.sum(p, axis=1, keepdims=True) + jnp.exp2(sink - m)
                ps.append((p * (1.0 / denom)).astype(BF16))
            out.append(jnp.concatenate(ps, axis=0))
        return out

    def attn_values(c, probs):
        rows = slice(c * CHUNK, (c + 1) * CHUNK)
        acc = None
        for kvh in range(N_KV_HEADS):
            if c == 0:
                v_band = jnp.concatenate([vprev_scr[rd, kvh], vm_scr[kvh, 0:CHUNK, :]], axis=0)
            else:
                v_band = vm_scr[kvh, (c - 1) * CHUNK:(c + 1) * CHUNK, :]
            o = _dot(probs[kvh], v_band)
            acc = o if acc is None else acc + o
        for g in range(GROUP):
            b_scr[rows, g * KV_WIDTH:(g + 1) * KV_WIDTH] = acc[g * CHUNK:(g + 1) * CHUNK, :].astype(BF16)

    def sgu_chunk(c):
        rows = slice(c * CHUNK, (c + 1) * CHUNK)
        vn_c = vn_scr[rows, :]
        mixed = jnp.concatenate(
            [_dot(wt_scr[g], vn_c[:, g * SGU_GROUP_DIM:(g + 1) * SGU_GROUP_DIM]) for g in range(SGU_GROUPS)],
            axis=1) + bexp_ref[...]
        a_scr[rows, :] = (u_scr[rows, :] * mixed).astype(BF16)

    def tail_v(h):
        vn_scr[...] = _rmsnorm(_gelu_tanh(h), sgu_g_ref[...]).astype(BF16)

    def tail_u(h):
        u_scr[...] = _gelu_tanh(h)

    def tail_ga(h):
        gate_scr[0] = jax.nn.sigmoid(h)

    def tail_gb(h):
        gate_scr[1] = jax.nn.sigmoid(h)

    fillers = [
        (lambda: h_v, tail_v),
        (lambda: _dot(xn, w_in_ref[:, C_U:C_V]), tail_u),
        (lambda: _dot(xn, w_in_ref[:, C_GA:C_GB]), tail_ga),
        (lambda: _dot(xn, w_in_ref[:, C_GB:C_END]), tail_gb),
    ]
    for c in range(nblk):
        scores = attn_scores(c)
        proj = fillers[c][0]() if c < len(fillers) else None
        probs = attn_softmax(c, scores)
        attn_values(c, probs)
        if proj is not None:
            fillers[c][1](proj)
    for matmul, tail in fillers[nblk:]:
        tail(matmul())
    for c in range(nblk):
        sgu_chunk(c)

    hm = gate_scr[0] * _dot(a_scr[...], w_oa_ref[...]) + gate_scr[1] * _dot(b_scr[...], w_ob_ref[...])
    x1_ref[...] = x + _dot(hm.astype(BF16), w_out_ref[...])


def _mix_prompt(x2d, ln1, w_in, w_q, sgu_g, sgu_w, bexp, sinks, bias, w_oa, w_ob, w_out, w_up_f32, w_down_f32,
                *, batch, seq):
    n = x2d.shape[0]
    tb = TOKEN_BLOCK
    nblk = tb // CHUNK
    steps = n // tb
    steps_per_seq = seq // tb
    row_block = pl.BlockSpec((tb, D_MODEL), lambda i: (i, 0))
    win_block = pl.BlockSpec((None, KV_WIDTH, WINDOW), lambda i: (i // steps_per_seq, 0, 0))
    up_block = pl.BlockSpec((D_MODEL // steps, D_FF), lambda i: (i, 0))
    down_block = pl.BlockSpec((D_FF // steps, D_MODEL), lambda i: (i, 0))
    return pl.pallas_call(
        functools.partial(_mix_prompt_kernel, steps_per_seq=steps_per_seq),
        grid=(n // tb,),
        in_specs=[
            row_block,
            _resident((1, D_MODEL)),
            _resident((D_MODEL, C_END)),
            _resident((D_MODEL, D_MODEL)),
            _resident((1, D_MODEL)),
            _resident((SGU_GROUPS, CHUNK, CHUNK)),
            _resident((CHUNK, D_MODEL)),
            pl.BlockSpec(memory_space=pltpu.SMEM),
            _resident((2, N_HEADS, CHUNK, 2 * CHUNK)),
            _resident((D_MODEL, D_MODEL)),
            _resident((D_MODEL, D_MODEL)),
            _resident((D_MODEL, D_MODEL)),
            up_block,
            down_block,
        ],
        out_specs=[row_block, win_block, win_block, up_block, down_block],
        out_shape=[
            jax.ShapeDtypeStruct((n, D_MODEL), F32),
            jax.ShapeDtypeStruct((batch, KV_WIDTH, WINDOW), F32),
            jax.ShapeDtypeStruct((batch, KV_WIDTH, WINDOW), F32),
            jax.ShapeDtypeStruct((D_MODEL, D_FF), BF16),
            jax.ShapeDtypeStruct((D_FF, D_MODEL), BF16),
        ],
        scratch_shapes=[
            pltpu.VMEM((nblk, GROUP * CHUNK, KV_WIDTH), BF16),
            pltpu.VMEM((N_KV_HEADS, nblk, KV_WIDTH, CHUNK), BF16),
            pltpu.VMEM((N_KV_HEADS, tb, KV_WIDTH), BF16),
            pltpu.VMEM((2, N_KV_HEADS, KV_WIDTH, CHUNK), BF16),
            pltpu.VMEM((2, N_KV_HEADS, WINDOW, KV_WIDTH), BF16),
            pltpu.VMEM((tb, D_MODEL), BF16),
            pltpu.VMEM((tb, D_MODEL), F32),
            pltpu.VMEM((2, tb, D_MODEL), F32),
            pltpu.VMEM((tb, D_MODEL), BF16),
            pltpu.VMEM((tb, D_MODEL), BF16),
            pltpu.VMEM((SGU_GROUPS, CHUNK, CHUNK), BF16),
        ],
        compiler_params=_params(),
        name="mix_prompt",
    )(x2d, ln1, w_in, w_q, sgu_g, sgu_w, bexp, sinks, bias, w_oa, w_ob, w_out, w_up_f32, w_down_f32)


def _ffn_rows(x, ln2_ref, w_up_ref, w_down_ref, lnf_ref):
    xn = _rmsnorm(x, ln2_ref[...]).astype(BF16)
    n_slabs = D_FF // FF_SLAB

    def up(j):
        return _dot(xn, w_up_ref[:, j * FF_SLAB:(j + 1) * FF_SLAB])

    acc = x
    h_next = up(0)
    for j in range(n_slabs):
        h = h_next
        if j + 1 < n_slabs:
            h_next = up(j + 1)
        h = jnp.square(jnp.maximum(h, 0.0)).astype(BF16)
        acc = acc + _dot(h, w_down_ref[j * FF_SLAB:(j + 1) * FF_SLAB, :])
    return _rmsnorm(acc, lnf_ref[...])


def _ffn_kernel(x_ref, xs_ref, ln2_ref, w_up_ref, w_down_ref, lnf_ref, y_ref, ys_ref):
    i = pl.program_id(0)
    last = pl.num_programs(0) - 1

    @pl.when(i < last)
    def _():
        y_ref[...] = _ffn_rows(x_ref[...], ln2_ref, w_up_ref, w_down_ref, lnf_ref)

    @pl.when(i == last)
    def _():
        ys_ref[...] = _ffn_rows(xs_ref[...], ln2_ref, w_up_ref, w_down_ref, lnf_ref)


def _ffn(x2d, xs2d, ln2, w_up, w_down, lnf):
    n = x2d.shape[0]
    nb = xs2d.shape[0]
    n_prompt_steps = n // FFN_BLOCK
    row_block = pl.BlockSpec((FFN_BLOCK, D_MODEL), lambda i: (jnp.minimum(i, n_prompt_steps - 1), 0))
    return pl.pallas_call(
        _ffn_kernel,
        grid=(n_prompt_steps + 1,),
        in_specs=[row_block, _resident((nb, D_MODEL)), _resident((1, D_MODEL)), _resident((D_MODEL, D_FF)),
                  _resident((D_FF, D_MODEL)), _resident((1, D_MODEL))],
        out_specs=[row_block, _whole((nb, D_MODEL))],
        out_shape=[jax.ShapeDtypeStruct((n, D_MODEL), F32), jax.ShapeDtypeStruct((nb, D_MODEL), F32)],
        compiler_params=_params(),
        name="ffn",
    )(x2d, xs2d, ln2, w_up, w_down, lnf)


def _sample_proj_kernel(x_ref, ln1_ref, w_blk_ref, w_q_ref, sgu_g_ref, wdiag_ref, b0_ref, sel_ref,
                        w_bf_ref, qsel_ref, knew_ref, vnew_ref, vn_ref, a_ref, ga_ref, gb_ref,
                        xn_scr, h_scr):
    j = pl.program_id(0)
    nb = x_ref.shape[0]

    @pl.when(j == 0)
    def _():
        xn_scr[...] = _rmsnorm(x_ref[...], ln1_ref[...]).astype(BF16)

    wb = w_blk_ref[...].astype(BF16)
    w_bf_ref[...] = wb
    h_scr[j] = _dot(xn_scr[...], wb)

    @pl.when(j == pl.num_programs(0) - 1)
    def _():
        def cols(lo, hi):
            blocks = [h_scr[b] for b in range(lo // W_BLOCK, hi // W_BLOCK)]
            return blocks[0] if len(blocks) == 1 else jnp.concatenate(blocks, axis=1)

        u = _gelu_tanh(cols(C_U, C_V))
        v = _gelu_tanh(cols(C_V, C_K))
        vn = _rmsnorm(v, sgu_g_ref[...])
        vn_ref[...] = vn
        a_ref[...] = (u * (vn * wdiag_ref[...] + b0_ref[...])).astype(BF16)
        kv = cols(C_K, C_GA)
        knew_ref[...] = kv[:, 0:KV_WIDTH]
        vnew_ref[...] = kv[:, KV_WIDTH:2 * KV_WIDTH]
        ga_ref[...] = jax.nn.sigmoid(cols(C_GA, C_GB))
        gb_ref[...] = jax.nn.sigmoid(cols(C_GB, C_END))
        q = _dot(xn_scr[...], w_q_ref[...]) * ATTN_SCALE
        qstack = jnp.concatenate([q[:, g * KV_WIDTH:(g + 1) * KV_WIDTH] for g in range(GROUP)], axis=0).astype(BF16)
        qrep = _dot(sel_ref[...], qstack)
        row_kvh = lax.broadcasted_iota(jnp.int32, (nb * N_HEADS, KV_WIDTH), 0) % N_KV_HEADS
        lane_kvh = lax.broadcasted_iota(jnp.int32, (nb * N_HEADS, KV_WIDTH), 1) // HEAD_DIM
        qsel_ref[...] = jnp.where(row_kvh == lane_kvh, qrep, 0.0).astype(BF16)


def _sample_proj(xs2d, ln1, w_in_f32, w_q, sgu_g, wdiag, b0, sel):
    nb = xs2d.shape[0]
    n_blocks = C_END // W_BLOCK
    q_blocks = (OFF_K - OFF_Q) // W_BLOCK
    first_after_q = OFF_Q // W_BLOCK
    w_block = pl.BlockSpec((D_MODEL, W_BLOCK), lambda j: (0, jnp.where(j >= first_after_q, j + q_blocks, j)))
    return pl.pallas_call(
        _sample_proj_kernel,
        grid=(n_blocks,),
        in_specs=[_whole((nb, D_MODEL)), _whole((1, D_MODEL)), w_block, _resident((D_MODEL, D_MODEL)),
                  _whole((1, D_MODEL)), _whole((1, D_MODEL)), _whole((1, D_MODEL)),
                  _resident((nb * N_HEADS, GROUP * nb))],
        out_specs=[pl.BlockSpec((D_MODEL, W_BLOCK), lambda j: (0, j)),
                   _whole((nb * N_HEADS, KV_WIDTH)), _whole((nb, KV_WIDTH)), _whole((nb, KV_WIDTH)),
                   _whole((nb, D_MODEL)), _whole((nb, D_MODEL)),
                   _whole((nb, D_MODEL)), _whole((nb, D_MODEL))],
        out_shape=[
            jax.ShapeDtypeStruct((D_MODEL, C_END), BF16),
            jax.ShapeDtypeStruct((nb * N_HEADS, KV_WIDTH), BF16),
            jax.ShapeDtypeStruct((nb, KV_WIDTH), F32),
            jax.ShapeDtypeStruct((nb, KV_WIDTH), F32),
            jax.ShapeDtypeStruct((nb, D_MODEL), F32),
            jax.ShapeDtypeStruct((nb, D_MODEL), BF16),
            jax.ShapeDtypeStruct((nb, D_MODEL), F32),
            jax.ShapeDtypeStruct((nb, D_MODEL), F32),
        ],
        scratch_shapes=[pltpu.VMEM((nb, D_MODEL), BF16), pltpu.VMEM((n_blocks, nb, W_BLOCK), F32)],
        compiler_params=_params(),
        name="sample_proj",
    )(xs2d, ln1, w_in_f32, w_q, sgu_g, wdiag, b0, sel)


def _sample_attn_kernel(qsel_ref, knew_ref, vnew_ref, ck_ref, cv_ref, bias_ref, sink_ref,
                        o_ref, nk_ref, nv_ref):
    bs = ck_ref.shape[0]
    row_kvh = lax.broadcasted_iota(jnp.int32, (N_HEADS, KV_WIDTH), 0) % N_KV_HEADS
    lane_kvh = lax.broadcasted_iota(jnp.int32, (N_HEADS, KV_WIDTH), 1) // HEAD_DIM
    own = row_kvh == lane_kvh
    bias = bias_ref[...]
    sink = sink_ref[...]

    qss = [qsel_ref[i * N_HEADS:(i + 1) * N_HEADS, :] for i in range(bs)]
    kns = [knew_ref[i:i + 1, :] for i in range(bs)]
    vws = [vnew_ref[i:i + 1, :] for i in range(bs)]
    scores = [_dot(qss[i], ck_ref[i].astype(BF16)) + bias for i in range(bs)]
    probs = []
    for i in range(bs):
        s = scores[i]
        s_new = jnp.sum(qss[i].astype(F32) * kns[i], axis=1, keepdims=True)
        m = jnp.maximum(jnp.maximum(jnp.max(s, axis=1, keepdims=True), s_new), sink)
        p = jnp.exp(s - m)
        p_new = jnp.exp(s_new - m)
        denom = jnp.sum(p, axis=1, keepdims=True) + p_new + jnp.exp(sink - m)
        probs.append((p.astype(BF16), p_new, denom))
    for i in range(bs):
        p, p_new, denom = probs[i]
        o = (_dot_nt(p, cv_ref[i].astype(BF16)) + p_new * vws[i]) / denom
        o_ref[i * N_HEADS:(i + 1) * N_HEADS, :] = jnp.where(own, o, 0.0).astype(BF16)

    pad = jnp.zeros((WINDOW - bs, KV_WIDTH), F32)
    kn_t = jnp.concatenate([knew_ref[...], pad], axis=0).T
    vw_t = jnp.concatenate([vnew_ref[...], pad], axis=0).T
    last_lane = lax.broadcasted_iota(jnp.int32, (KV_WIDTH, WINDOW), 1) == WINDOW - 1
    for i in range(bs):
        nk_ref[i] = jnp.where(last_lane, kn_t[:, i:i + 1], pltpu.roll(ck_ref[i], WINDOW - 1, 1))
        nv_ref[i] = jnp.where(last_lane, vw_t[:, i:i + 1], pltpu.roll(cv_ref[i], WINDOW - 1, 1))


def _sample_attn(qsel, knew, vnew, cache_k, cache_v, bias_s, sink_col):
    nb = knew.shape[0]
    bs = SAMPLE_BLOCK
    cache_block = pl.BlockSpec((bs, KV_WIDTH, WINDOW), lambda i: (i, 0, 0))
    return pl.pallas_call(
        _sample_attn_kernel,
        grid=(nb // bs,),
        in_specs=[pl.BlockSpec((bs * N_HEADS, KV_WIDTH), lambda i: (i, 0)),
                  pl.BlockSpec((bs, KV_WIDTH), lambda i: (i, 0)),
                  pl.BlockSpec((bs, KV_WIDTH), lambda i: (i, 0)),
                  cache_block, cache_block,
                  _whole((N_HEADS, WINDOW)), _whole((N_HEADS, 1))],
        out_specs=[pl.BlockSpec((bs * N_HEADS, KV_WIDTH), lambda i: (i, 0)), cache_block, cache_block],
        out_shape=[
            jax.ShapeDtypeStruct((nb * N_HEADS, KV_WIDTH), BF16),
            jax.ShapeDtypeStruct((nb, KV_WIDTH, WINDOW), F32),
            jax.ShapeDtypeStruct((nb, KV_WIDTH, WINDOW), F32),
        ],
        compiler_params=_params(),
        name="sample_attn",
    )(qsel, knew, vnew, cache_k, cache_v, bias_s, sink_col)


def _sample_merge_kernel(o_ref, selt_ref, a_ref, ga_ref, gb_ref, x_ref, w_oa_ref, w_ob_ref, w_out_ref, x1_ref):
    nb = x_ref.shape[0]
    bst = _dot(selt_ref[...], o_ref[...]).astype(BF16)
    ob = _dot(bst[0:nb, :], w_ob_ref[0:KV_WIDTH, :])
    for g in range(1, GROUP):
        ob = ob + _dot(bst[g * nb:(g + 1) * nb, :], w_ob_ref[g * KV_WIDTH:(g + 1) * KV_WIDTH, :])
    hm = ga_ref[...] * _dot(a_ref[...], w_oa_ref[...]) + gb_ref[...] * ob
    x1_ref[...] = x_ref[...] + _dot(hm.astype(BF16), w_out_ref[...])


def _sample_merge(o, selt, a, ga, gb, xs2d, w_oa, w_ob, w_out):
    nb = xs2d.shape[0]
    return pl.pallas_call(
        _sample_merge_kernel,
        grid=(1,),
        in_specs=[_whole((nb * N_HEADS, KV_WIDTH)), _whole((GROUP * nb, nb * N_HEADS)),
                  _whole((nb, D_MODEL)), _whole((nb, D_MODEL)), _whole((nb, D_MODEL)), _whole((nb, D_MODEL)),
                  _resident((D_MODEL, D_MODEL)), _resident((D_MODEL, D_MODEL)), _resident((D_MODEL, D_MODEL))],
        out_specs=_whole((nb, D_MODEL)),
        out_shape=jax.ShapeDtypeStruct((nb, D_MODEL), F32),
        compiler_params=_params(),
        name="sample_merge",
    )(o, selt, a, ga, gb, xs2d, w_oa, w_ob, w_out)


def _head_perm(v):
    return v.reshape(N_KV_HEADS, GROUP).T.reshape(N_HEADS)


def _alibi_slopes():
    h = jnp.arange(1, N_HEADS + 1, dtype=F32)
    return jnp.exp2(-8.0 * h / N_HEADS)


def _selection_matrix(nb):
    r = np.arange(nb * N_HEADS)
    c = np.arange(GROUP * nb)
    same_sample = (r[:, None] // N_HEADS) == (c[None, :] % nb)
    same_member = ((r[:, None] % N_HEADS) // N_KV_HEADS) == (c[None, :] // nb)
    return (same_sample & same_member).astype(np.float32)


def kernel(x_prompt, x_sample, cache_k_win, cache_v_win, ln1_g, w_in, sgu_norm_g, sgu_w, sgu_b, attn_sinks,
           w_oa, w_ob, w_out, ln2_g, w_up, w_down, lnf_g):
    batch, seq, _ = x_prompt.shape
    dec_batch, dec_seq, _ = x_sample.shape
    depth = w_in.shape[0]
    assert depth == 1 and dec_seq == 1
    assert seq % TOKEN_BLOCK == 0 and TOKEN_BLOCK % CHUNK == 0 and dec_batch % SAMPLE_BLOCK == 0
    assert (batch * seq) % FFN_BLOCK == 0
    assert w_in.shape[-1] == IN_WIDTH

    wi = w_in[0]
    w_q_b = wi[:, OFF_Q:OFF_K].reshape(D_MODEL, N_KV_HEADS, GROUP, HEAD_DIM).transpose(0, 2, 1, 3).reshape(
        D_MODEL, D_MODEL).astype(BF16)
    w_ob_b = w_ob[0].reshape(N_KV_HEADS, GROUP, HEAD_DIM, D_MODEL).transpose(1, 0, 2, 3).reshape(
        D_MODEL, D_MODEL).astype(BF16)
    w_oa_b = w_oa[0].astype(BF16)
    w_out_b = w_out[0].astype(BF16)
    ln1 = ln1_g[0].reshape(1, D_MODEL)
    ln2 = ln2_g[0].reshape(1, D_MODEL)
    lnf = lnf_g.reshape(1, D_MODEL)
    sgu_g = sgu_norm_g[0].reshape(1, D_MODEL)
    sinks_p = _head_perm(attn_sinks[0].astype(F32))
    slopes_p = _head_perm(_alibi_slopes())

    xs2d = x_sample.reshape(dec_batch, D_MODEL)
    wdiag = jnp.repeat(sgu_w[0][:, 0, 0], SGU_GROUP_DIM).reshape(1, D_MODEL)
    b0 = jnp.repeat(sgu_b[0][:, 0], SGU_GROUP_DIM).reshape(1, D_MODEL)
    sel_np = _selection_matrix(dec_batch)
    sel = jnp.asarray(sel_np, BF16)
    selt = jnp.asarray(sel_np.T, BF16)
    bias_s = -slopes_p[:, None] * jnp.asarray(WINDOW - np.arange(WINDOW), F32)[None, :]

    w_in_b, qsel, knew, vnew, vn, a_s, ga_s, gb_s = _sample_proj(xs2d, ln1, wi, w_q_b, sgu_g, wdiag, b0, sel)
    def to_feature_major(c):
        return c[0].transpose(0, 2, 3, 1).reshape(dec_batch, KV_WIDTH, WINDOW)

    def from_feature_major(c):
        return c.reshape(c.shape[0], N_KV_HEADS, HEAD_DIM, WINDOW).transpose(0, 3, 1, 2)[None]

    o_s, nk, nv = _sample_attn(qsel, knew, vnew, to_feature_major(cache_k_win), to_feature_major(cache_v_win),
                               bias_s, sinks_p.reshape(N_HEADS, 1))
    xs1 = _sample_merge(o_s, selt, a_s, ga_s, gb_s, xs2d, w_oa_b, w_ob_b, w_out_b)

    qi = np.arange(CHUNK)[:, None] + CHUNK
    kj = np.arange(2 * CHUNK)[None, :]
    diff = qi - kj
    band = (diff >= 0) & (diff <= WINDOW)
    alibi = -(slopes_p * LOG2E)[:, None, None] * jnp.asarray(diff, F32)[None]
    valid = np.stack([band, band & (kj >= CHUNK)])[:, None]
    bias = jnp.where(valid, alibi[None], NEG_BIG)
    bexp = jnp.repeat(sgu_b[0].T, SGU_GROUP_DIM, axis=1)

    x1, kwin, vwin, w_up_b, w_down_b = _mix_prompt(
        x_prompt.reshape(batch * seq, D_MODEL), ln1, w_in_b, w_q_b, sgu_g, sgu_w[0], bexp, sinks_p * LOG2E, bias,
        w_oa_b, w_ob_b, w_out_b, w_up[0], w_down[0], batch=batch, seq=seq)
    y_prompt, y_sample = _ffn(x1, xs1, ln2, w_up_b, w_down_b, lnf)

    return (y_prompt.reshape(batch, seq, D_MODEL),
            y_sample.reshape(dec_batch, dec_seq, D_MODEL),
            from_feature_major(kwin), from_feature_major(vwin),
            from_feature_major(nk), from_feature_major(nv),
            vn.reshape(depth, dec_batch, dec_seq, D_MODEL))
```

```python
import functools
import math

import numpy as np
import jax
import jax.numpy as jnp
from jax import lax
from jax.experimental import pallas as pl
from jax.experimental.pallas import tpu as pltpu

D_MODEL = 1024
N_HEADS = 16
HEAD_DIM = 64
N_KV_HEADS = 4
GROUP = N_HEADS // N_KV_HEADS
KV_WIDTH = N_KV_HEADS * HEAD_DIM
WINDOW = 128
CHUNK = 128
SGU_GROUPS = 8
SGU_GROUP_DIM = D_MODEL // SGU_GROUPS
D_FF = 4 * D_MODEL
FF_SLAB = 1024
EPS = 1e-6
NEG_BIG = -1e30
ATTN_SCALE = HEAD_DIM ** -0.5
LOG2E = math.log2(math.e)

IN_WIDTH = 5632
OFF_Q, OFF_K = 2048, 3072
C_U, C_V, C_K, C_VA, C_GA, C_GB, C_END = 0, 1024, 2048, 2304, 2560, 3584, 4608
W_BLOCK = 512

TOKEN_BLOCK = 512
FFN_BLOCK = 1024
SUBLANES = 8
VMEM_LIMIT_BYTES = 58 * 1024 * 1024

F32 = jnp.float32
BF16 = jnp.bfloat16


def _rmsnorm(x, g):
    ms = jnp.mean(x * x, axis=-1, keepdims=True)
    return x * lax.rsqrt(ms + EPS) * g


def _gelu_tanh(x):
    c = math.sqrt(2.0 / math.pi)
    return x * (0.5 * (1.0 + jnp.tanh(c * (x + 0.044715 * (x * x * x)))))


def _dot(a, b):
    return jnp.dot(a, b, preferred_element_type=F32)


def _dot_nt(a, b):
    return lax.dot_general(a, b, (((1,), (1,)), ((), ())), preferred_element_type=F32)


def _resident(shape):
    zeros = (0,) * len(shape)
    return pl.BlockSpec(shape, lambda *_: zeros, pipeline_mode=pl.Buffered(1))


def _whole(shape):
    zeros = (0,) * len(shape)
    return pl.BlockSpec(shape, lambda *_: zeros)


def _params():
    return pltpu.CompilerParams(dimension_semantics=("arbitrary",), vmem_limit_bytes=VMEM_LIMIT_BYTES)


def _mix_prompt_kernel(x_ref, ln1_ref, w_in_ref, w_q_ref, sgu_g_ref, sgu_w_ref, bexp_ref,
                       sink_ref, slope_ref, w_oa_ref, w_ob_ref, w_out_ref, w_up_blk_ref, w_down_blk_ref,
                       s_qsel_ref, s_knew_ref, s_vnew_ref, s_ck_ref, s_cv_ref, s_bias_ref, s_sink_ref,
                       x1_ref, kwin_ref, vwin_ref, w_up_bf_ref, w_down_bf_ref, s_o_ref, s_nk_ref, s_nv_ref,
                       qs_scr, kt_scr, vm_scr, kprev_scr, vprev_scr, vn_scr, u_scr, gate_scr, a_scr, b_scr, wt_scr,
                       bias_ref,
                       *, steps_per_seq):
    step = pl.program_id(0)
    tb = x_ref.shape[0]
    nblk = tb // CHUNK
    first = (step % steps_per_seq) == 0
    rd = step % 2
    wr = 1 - rd

    @pl.when(step == 0)
    def _():
        row = lax.broadcasted_iota(jnp.int32, (CHUNK, CHUNK), 0)
        col = lax.broadcasted_iota(jnp.int32, (CHUNK, CHUNK), 1)
        for g in range(SGU_GROUPS):
            wt_scr[g] = jnp.where(row >= col, sgu_w_ref[g], 0.0).astype(BF16)
        dist = (lax.broadcasted_iota(jnp.int32, (CHUNK, 2 * CHUNK), 0) + CHUNK
                - lax.broadcasted_iota(jnp.int32, (CHUNK, 2 * CHUNK), 1))
        in_band = jnp.logical_and(dist >= 0, dist <= WINDOW)
        in_band_first = jnp.logical_and(in_band, lax.broadcasted_iota(jnp.int32, (CHUNK, 2 * CHUNK), 1) >= CHUNK)
        dist_f = dist.astype(F32)
        for h in range(N_HEADS):
            alibi = -slope_ref[h] * dist_f
            bias_ref[0, h] = jnp.where(in_band, alibi, NEG_BIG)
            bias_ref[1, h] = jnp.where(in_band_first, alibi, NEG_BIG)
        kt_scr[...] = jnp.zeros(kt_scr.shape, BF16)
        vm_scr[...] = jnp.zeros(vm_scr.shape, BF16)
        kprev_scr[...] = jnp.zeros(kprev_scr.shape, BF16)
        vprev_scr[...] = jnp.zeros(vprev_scr.shape, BF16)

    @pl.when(first)
    def _():
        kprev_scr[rd] = jnp.zeros(kprev_scr.shape[1:], BF16)
        vprev_scr[rd] = jnp.zeros(vprev_scr.shape[1:], BF16)

    w_up_bf_ref[...] = w_up_blk_ref[...].astype(BF16)
    w_down_bf_ref[...] = w_down_blk_ref[...].astype(BF16)

    x = x_ref[...]
    xn = _rmsnorm(x, ln1_ref[...]).astype(BF16)

    q = _dot(xn, w_q_ref[...])
    k = _dot(xn, w_in_ref[:, C_K:C_VA])
    va = _dot(xn, w_in_ref[:, C_VA:C_GA])
    h_v = _dot(xn, w_in_ref[:, C_V:C_K])

    _sample_attn_kernel(s_qsel_ref, s_knew_ref, s_vnew_ref, s_ck_ref, s_cv_ref, s_bias_ref, s_sink_ref,
                        s_o_ref, s_nk_ref, s_nv_ref)

    q = (q * (ATTN_SCALE * LOG2E)).astype(BF16)
    for c in range(nblk):
        for g in range(GROUP):
            qs_scr[c, g * CHUNK:(g + 1) * CHUNK, :] = q[c * CHUNK:(c + 1) * CHUNK, g * KV_WIDTH:(g + 1) * KV_WIDTH]

    kt_f32 = k.T
    kwin_ref[...] = kt_f32[:, tb - WINDOW:]
    vwin_ref[...] = va[tb - WINDOW:, :].T
    kt = kt_f32.astype(BF16)
    vab = va.astype(BF16)
    for kvh in range(N_KV_HEADS):
        own = slice(kvh * HEAD_DIM, (kvh + 1) * HEAD_DIM)
        for c in range(nblk):
            kt_scr[kvh, c, own, :] = kt[own, c * CHUNK:(c + 1) * CHUNK]
        vm_scr[kvh, :, own] = vab[:, own]
        kprev_scr[wr, kvh, own, :] = kt[own, tb - WINDOW:]
        vprev_scr[wr, kvh, :, own] = vab[tb - WINDOW:, own]

    first_i = first.astype(jnp.int32)

    def attn_scores(c):
        qs = qs_scr[c]
        out = []
        for kvh in range(N_KV_HEADS):
            k_prev = kprev_scr[rd, kvh] if c == 0 else kt_scr[kvh, c - 1]
            out.append(_dot(qs, jnp.concatenate([k_prev, kt_scr[kvh, c]], axis=1)))
        return out

    def attn_softmax(c, scores):
        bias_sel = first_i if c == 0 else 0
        out = []
        for kvh in range(N_KV_HEADS):
            ps = []
            for g in range(GROUP):
                h = g * N_KV_HEADS + kvh
                s = scores[kvh][g * CHUNK:(g + 1) * CHUNK, :] + bias_ref[bias_sel, h]
                sink = sink_ref[h]
                m = jnp.max(s, axis=1, keepdims=True)
                p = jnp.exp2(s - m)
                denom = jnp.sum(p, axis=1, keepdims=True) + jnp.exp2(sink - m)
                ps.append((p * (1.0 / denom)).astype(BF16))
            out.append(jnp.concatenate(ps, axis=0))
        return out

    def attn_values(c, probs):
        rows = slice(c * CHUNK, (c + 1) * CHUNK)
        acc = None
        for kvh in range(N_KV_HEADS):
            if c == 0:
                v_band = jnp.concatenate([vprev_scr[rd, kvh], vm_scr[kvh, 0:CHUNK, :]], axis=0)
            else:
                v_band = vm_scr[kvh, (c - 1) * CHUNK:(c + 1) * CHUNK, :]
            o = _dot(probs[kvh], v_band)
            acc = o if acc is None else acc + o
        for g in range(GROUP):
            b_scr[rows, g * KV_WIDTH:(g + 1) * KV_WIDTH] = acc[g * CHUNK:(g + 1) * CHUNK, :].astype(BF16)

    def sgu_chunk(c):
        rows = slice(c * CHUNK, (c + 1) * CHUNK)
        vn_c = vn_scr[rows, :]
        mixed = jnp.concatenate(
            [_dot(wt_scr[g], vn_c[:, g * SGU_GROUP_DIM:(g + 1) * SGU_GROUP_DIM]) for g in range(SGU_GROUPS)],
            axis=1) + bexp_ref[...]
        a_scr[rows, :] = (u_scr[rows, :] * mixed).astype(BF16)

    def tail_v(h):
        vn_scr[...] = _rmsnorm(_gelu_tanh(h), sgu_g_ref[...]).astype(BF16)

    def tail_u(h):
        u_scr[...] = _gelu_tanh(h)

    def tail_ga(h):
        gate_scr[0] = jax.nn.sigmoid(h)

    def tail_gb(h):
        gate_scr[1] = jax.nn.sigmoid(h)

    fillers = [
        (lambda: h_v, tail_v),
        (lambda: _dot(xn, w_in_ref[:, C_U:C_V]), tail_u),
        (lambda: _dot(xn, w_in_ref[:, C_GA:C_GB]), tail_ga),
        (lambda: _dot(xn, w_in_ref[:, C_GB:C_END]), tail_gb),
    ]
    for c in range(nblk):
        scores = attn_scores(c)
        proj = fillers[c][0]() if c < len(fillers) else None
        probs = attn_softmax(c, scores)
        attn_values(c, probs)
        if proj is not None:
            fillers[c][1](proj)
    for matmul, tail in fillers[nblk:]:
        tail(matmul())
    for c in range(nblk):
        sgu_chunk(c)

    hm = gate_scr[0] * _dot(a_scr[...], w_oa_ref[...]) + gate_scr[1] * _dot(b_scr[...], w_ob_ref[...])
    x1_ref[...] = x + _dot(hm.astype(BF16), w_out_ref[...])


def _mix_prompt(x2d, ln1, w_in, w_q, sgu_g, sgu_w, bexp, sinks, slopes, w_oa, w_ob, w_out, w_up_f32, w_down_f32,
                s_qsel, s_knew, s_vnew, s_cache_k, s_cache_v, s_bias, s_sink, *, batch, seq):
    n = x2d.shape[0]
    tb = TOKEN_BLOCK
    nblk = tb // CHUNK
    steps = n // tb
    steps_per_seq = seq // tb
    nb = s_cache_k.shape[0]
    per_step = nb // steps
    assert per_step * steps == nb and s_knew.shape == (steps, SUBLANES, KV_WIDTH) and per_step <= SUBLANES
    s_head_block = pl.BlockSpec((None, per_step * N_HEADS, KV_WIDTH), lambda i: (i, 0, 0))
    s_new_block = pl.BlockSpec((None, SUBLANES, KV_WIDTH), lambda i: (i, 0, 0))
    s_cache_block = pl.BlockSpec((per_step, KV_WIDTH, WINDOW), lambda i: (i, 0, 0))
    row_block = pl.BlockSpec((tb, D_MODEL), lambda i: (i, 0))
    win_block = pl.BlockSpec((None, KV_WIDTH, WINDOW), lambda i: (i // steps_per_seq, 0, 0))
    up_block = pl.BlockSpec((D_MODEL // steps, D_FF), lambda i: (i, 0))
    down_block = pl.BlockSpec((D_FF // steps, D_MODEL), lambda i: (i, 0))
    return pl.pallas_call(
        functools.partial(_mix_prompt_kernel, steps_per_seq=steps_per_seq),
        grid=(n // tb,),
        in_specs=[
            row_block,
            _resident((1, D_MODEL)),
            _resident((D_MODEL, C_END)),
            _resident((D_MODEL, D_MODEL)),
            _resident((1, D_MODEL)),
            _resident((SGU_GROUPS, CHUNK, CHUNK)),
            _resident((CHUNK, D_MODEL)),
            pl.BlockSpec(memory_space=pltpu.SMEM),
            pl.BlockSpec(memory_space=pltpu.SMEM),
            _resident((D_MODEL, D_MODEL)),
            _resident((D_MODEL, D_MODEL)),
            _resident((D_MODEL, D_MODEL)),
            up_block,
            down_block,
            s_head_block, s_new_block, s_new_block, s_cache_block, s_cache_block,
            _resident((N_HEADS, WINDOW)), _resident((N_HEADS, 1)),
        ],
        out_specs=[row_block, win_block, win_block, up_block, down_block,
                   s_head_block, s_cache_block, s_cache_block],
        out_shape=[
            jax.ShapeDtypeStruct((n, D_MODEL), F32),
            jax.ShapeDtypeStruct((batch, KV_WIDTH, WINDOW), F32),
            jax.ShapeDtypeStruct((batch, KV_WIDTH, WINDOW), F32),
            jax.ShapeDtypeStruct((D_MODEL, D_FF), BF16),
            jax.ShapeDtypeStruct((D_FF, D_MODEL), BF16),
            jax.ShapeDtypeStruct((steps, per_step * N_HEADS, KV_WIDTH), BF16),
            jax.ShapeDtypeStruct((nb, KV_WIDTH, WINDOW), F32),
            jax.ShapeDtypeStruct((nb, KV_WIDTH, WINDOW), F32),
        ],
        scratch_shapes=[
            pltpu.VMEM((nblk, GROUP * CHUNK, KV_WIDTH), BF16),
            pltpu.VMEM((N_KV_HEADS, nblk, KV_WIDTH, CHUNK), BF16),
            pltpu.VMEM((N_KV_HEADS, tb, KV_WIDTH), BF16),
            pltpu.VMEM((2, N_KV_HEADS, KV_WIDTH, CHUNK), BF16),
            pltpu.VMEM((2, N_KV_HEADS, WINDOW, KV_WIDTH), BF16),
            pltpu.VMEM((tb, D_MODEL), BF16),
            pltpu.VMEM((tb, D_MODEL), F32),
            pltpu.VMEM((2, tb, D_MODEL), F32),
            pltpu.VMEM((tb, D_MODEL), BF16),
            pltpu.VMEM((tb, D_MODEL), BF16),
            pltpu.VMEM((SGU_GROUPS, CHUNK, CHUNK), BF16),
            pltpu.VMEM((2, N_HEADS, CHUNK, 2 * CHUNK), F32),
        ],
        compiler_params=_params(),
        name="mix_prompt",
    )(x2d, ln1, w_in, w_q, sgu_g, sgu_w, bexp, sinks, slopes, w_oa, w_ob, w_out, w_up_f32, w_down_f32,
      s_qsel, s_knew, s_vnew, s_cache_k, s_cache_v, s_bias, s_sink)


def _ffn_rows(x, ln2_ref, w_up_ref, w_down_ref, lnf_ref):
    xn = _rmsnorm(x, ln2_ref[...]).astype(BF16)
    n_slabs = D_FF // FF_SLAB

    def up(j):
        return _dot(xn, w_up_ref[:, j * FF_SLAB:(j + 1) * FF_SLAB])

    acc = x
    h_next = up(0)
    for j in range(n_slabs):
        h = h_next
        if j + 1 < n_slabs:
            h_next = up(j + 1)
        h = jnp.square(jnp.maximum(h, 0.0)).astype(BF16)
        acc = acc + _dot(h, w_down_ref[j * FF_SLAB:(j + 1) * FF_SLAB, :])
    return _rmsnorm(acc, lnf_ref[...])


def _ffn_kernel(x_ref, xs_ref, ln2_ref, w_up_ref, w_down_ref, lnf_ref, y_ref, ys_ref):
    i = pl.program_id(0)
    last = pl.num_programs(0) - 1

    @pl.when(i < last)
    def _():
        y_ref[...] = _ffn_rows(x_ref[...], ln2_ref, w_up_ref, w_down_ref, lnf_ref)

    @pl.when(i == last)
    def _():
        ys_ref[...] = _ffn_rows(xs_ref[...], ln2_ref, w_up_ref, w_down_ref, lnf_ref)


def _ffn(x2d, xs2d, ln2, w_up, w_down, lnf):
    n = x2d.shape[0]
    nb = xs2d.shape[0]
    n_prompt_steps = n // FFN_BLOCK
    row_block = pl.BlockSpec((FFN_BLOCK, D_MODEL), lambda i: (jnp.minimum(i, n_prompt_steps - 1), 0))
    return pl.pallas_call(
        _ffn_kernel,
        grid=(n_prompt_steps + 1,),
        in_specs=[row_block, _resident((nb, D_MODEL)), _resident((1, D_MODEL)), _resident((D_MODEL, D_FF)),
                  _resident((D_FF, D_MODEL)), _resident((1, D_MODEL))],
        out_specs=[row_block, _whole((nb, D_MODEL))],
        out_shape=[jax.ShapeDtypeStruct((n, D_MODEL), F32), jax.ShapeDtypeStruct((nb, D_MODEL), F32)],
        compiler_params=_params(),
        name="ffn",
    )(x2d, xs2d, ln2, w_up, w_down, lnf)


def _sample_proj_kernel(x_ref, ln1_ref, w_blk_ref, w_q_ref, sgu_g_ref, wdiag_ref, b0_ref, sel_ref,
                        w_bf_ref, qsel_ref, knew_ref, vnew_ref, vn_ref, a_ref, ga_ref, gb_ref,
                        xn_scr, h_scr):
    j = pl.program_id(0)
    nb = x_ref.shape[0]

    @pl.when(j == 0)
    def _():
        xn_scr[...] = _rmsnorm(x_ref[...], ln1_ref[...]).astype(BF16)

    wb = w_blk_ref[...].astype(BF16)
    w_bf_ref[...] = wb
    h_scr[j] = _dot(xn_scr[...], wb)

    @pl.when(j == pl.num_programs(0) - 1)
    def _():
        def cols(lo, hi):
            blocks = [h_scr[b] for b in range(lo // W_BLOCK, hi // W_BLOCK)]
            return blocks[0] if len(blocks) == 1 else jnp.concatenate(blocks, axis=1)

        u = _gelu_tanh(cols(C_U, C_V))
        v = _gelu_tanh(cols(C_V, C_K))
        vn = _rmsnorm(v, sgu_g_ref[...])
        vn_ref[...] = vn
        a_ref[...] = (u * (vn * wdiag_ref[...] + b0_ref[...])).astype(BF16)
        kv = cols(C_K, C_GA)
        knew_ref[...] = kv[:, 0:KV_WIDTH]
        vnew_ref[...] = kv[:, KV_WIDTH:2 * KV_WIDTH]
        ga_ref[...] = jax.nn.sigmoid(cols(C_GA, C_GB))
        gb_ref[...] = jax.nn.sigmoid(cols(C_GB, C_END))
        q = _dot(xn_scr[...], w_q_ref[...]) * ATTN_SCALE
        qstack = jnp.concatenate([q[:, g * KV_WIDTH:(g + 1) * KV_WIDTH] for g in range(GROUP)], axis=0).astype(BF16)
        qrep = _dot(sel_ref[...], qstack)
        row_kvh = lax.broadcasted_iota(jnp.int32, (nb * N_HEADS, KV_WIDTH), 0) % N_KV_HEADS
        lane_kvh = lax.broadcasted_iota(jnp.int32, (nb * N_HEADS, KV_WIDTH), 1) // HEAD_DIM
        qsel_ref[...] = jnp.where(row_kvh == lane_kvh, qrep, 0.0).astype(BF16)


def _sample_proj(xs2d, ln1, w_in_f32, w_q, sgu_g, wdiag, b0, sel):
    nb = xs2d.shape[0]
    n_blocks = C_END // W_BLOCK
    q_blocks = (OFF_K - OFF_Q) // W_BLOCK
    first_after_q = OFF_Q // W_BLOCK
    w_block = pl.BlockSpec((D_MODEL, W_BLOCK), lambda j: (0, jnp.where(j >= first_after_q, j + q_blocks, j)))
    return pl.pallas_call(
        _sample_proj_kernel,
        grid=(n_blocks,),
        in_specs=[_whole((nb, D_MODEL)), _whole((1, D_MODEL)), w_block, _resident((D_MODEL, D_MODEL)),
                  _whole((1, D_MODEL)), _whole((1, D_MODEL)), _whole((1, D_MODEL)),
                  _resident((nb * N_HEADS, GROUP * nb))],
        out_specs=[pl.BlockSpec((D_MODEL, W_BLOCK), lambda j: (0, j)),
                   _whole((nb * N_HEADS, KV_WIDTH)), _whole((nb, KV_WIDTH)), _whole((nb, KV_WIDTH)),
                   _whole((nb, D_MODEL)), _whole((nb, D_MODEL)),
                   _whole((nb, D_MODEL)), _whole((nb, D_MODEL))],
        out_shape=[
            jax.ShapeDtypeStruct((D_MODEL, C_END), BF16),
            jax.ShapeDtypeStruct((nb * N_HEADS, KV_WIDTH), BF16),
            jax.ShapeDtypeStruct((nb, KV_WIDTH), F32),
            jax.ShapeDtypeStruct((nb, KV_WIDTH), F32),
            jax.ShapeDtypeStruct((nb, D_MODEL), F32),
            jax.ShapeDtypeStruct((nb, D_MODEL), BF16),
            jax.ShapeDtypeStruct((nb, D_MODEL), F32),
            jax.ShapeDtypeStruct((nb, D_MODEL), F32),
        ],
        scratch_shapes=[pltpu.VMEM((nb, D_MODEL), BF16), pltpu.VMEM((n_blocks, nb, W_BLOCK), F32)],
        compiler_params=_params(),
        name="sample_proj",
    )(xs2d, ln1, w_in_f32, w_q, sgu_g, wdiag, b0, sel)


def _sample_attn_kernel(qsel_ref, knew_ref, vnew_ref, ck_ref, cv_ref, bias_ref, sink_ref,
                        o_ref, nk_ref, nv_ref):
    bs = ck_ref.shape[0]
    row_kvh = lax.broadcasted_iota(jnp.int32, (N_HEADS, KV_WIDTH), 0) % N_KV_HEADS
    lane_kvh = lax.broadcasted_iota(jnp.int32, (N_HEADS, KV_WIDTH), 1) // HEAD_DIM
    own = row_kvh == lane_kvh
    bias = bias_ref[...]
    sink = sink_ref[...]

    qss = [qsel_ref[i * N_HEADS:(i + 1) * N_HEADS, :] for i in range(bs)]
    kns = [knew_ref[i:i + 1, :] for i in range(bs)]
    vws = [vnew_ref[i:i + 1, :] for i in range(bs)]
    scores = [_dot(qss[i], ck_ref[i].astype(BF16)) + bias for i in range(bs)]
    probs = []
    for i in range(bs):
        s = scores[i]
        s_new = jnp.sum(qss[i].astype(F32) * kns[i], axis=1, keepdims=True)
        m = jnp.maximum(jnp.maximum(jnp.max(s, axis=1, keepdims=True), s_new), sink)
        p = jnp.exp(s - m)
        p_new = jnp.exp(s_new - m)
        denom = jnp.sum(p, axis=1, keepdims=True) + p_new + jnp.exp(sink - m)
        probs.append((p.astype(BF16), p_new, denom))
    for i in range(bs):
        p, p_new, denom = probs[i]
        o = (_dot_nt(p, cv_ref[i].astype(BF16)) + p_new * vws[i]) / denom
        o_ref[i * N_HEADS:(i + 1) * N_HEADS, :] = jnp.where(own, o, 0.0).astype(BF16)

    kn_t = knew_ref[...].T
    vw_t = vnew_ref[...].T
    last_lane = lax.broadcasted_iota(jnp.int32, (KV_WIDTH, WINDOW), 1) == WINDOW - 1
    for i in range(bs):
        nk_ref[i] = jnp.where(last_lane, kn_t[:, i:i + 1], pltpu.roll(ck_ref[i], WINDOW - 1, 1))
        nv_ref[i] = jnp.where(last_lane, vw_t[:, i:i + 1], pltpu.roll(cv_ref[i], WINDOW - 1, 1))


def _sample_merge_kernel(o_ref, selt_ref, a_ref, ga_ref, gb_ref, x_ref, w_oa_ref, w_ob_ref, w_out_ref, x1_ref):
    nb = x_ref.shape[0]
    bst = _dot(selt_ref[...], o_ref[...]).astype(BF16)
    ob = _dot(bst[0:nb, :], w_ob_ref[0:KV_WIDTH, :])
    for g in range(1, GROUP):
        ob = ob + _dot(bst[g * nb:(g + 1) * nb, :], w_ob_ref[g * KV_WIDTH:(g + 1) * KV_WIDTH, :])
    hm = ga_ref[...] * _dot(a_ref[...], w_oa_ref[...]) + gb_ref[...] * ob
    x1_ref[...] = x_ref[...] + _dot(hm.astype(BF16), w_out_ref[...])


def _sample_merge(o, selt, a, ga, gb, xs2d, w_oa, w_ob, w_out):
    nb = xs2d.shape[0]
    return pl.pallas_call(
        _sample_merge_kernel,
        grid=(1,),
        in_specs=[_whole((nb * N_HEADS, KV_WIDTH)), _whole((GROUP * nb, nb * N_HEADS)),
                  _whole((nb, D_MODEL)), _whole((nb, D_MODEL)), _whole((nb, D_MODEL)), _whole((nb, D_MODEL)),
                  _resident((D_MODEL, D_MODEL)), _resident((D_MODEL, D_MODEL)), _resident((D_MODEL, D_MODEL))],
        out_specs=_whole((nb, D_MODEL)),
        out_shape=jax.ShapeDtypeStruct((nb, D_MODEL), F32),
        compiler_params=_params(),
        name="sample_merge",
    )(o, selt, a, ga, gb, xs2d, w_oa, w_ob, w_out)


def _head_perm(v):
    return v.reshape(N_KV_HEADS, GROUP).T.reshape(N_HEADS)


def _alibi_slopes():
    h = jnp.arange(1, N_HEADS + 1, dtype=F32)
    return jnp.exp2(-8.0 * h / N_HEADS)


def _selection_matrix(nb):
    r = np.arange(nb * N_HEADS)
    c = np.arange(GROUP * nb)
    same_sample = (r[:, None] // N_HEADS) == (c[None, :] % nb)
    same_member = ((r[:, None] % N_HEADS) // N_KV_HEADS) == (c[None, :] // nb)
    return (same_sample & same_member).astype(np.float32)


def kernel(x_prompt, x_sample, cache_k_win, cache_v_win, ln1_g, w_in, sgu_norm_g, sgu_w, sgu_b, attn_sinks,
           w_oa, w_ob, w_out, ln2_g, w_up, w_down, lnf_g):
    batch, seq, _ = x_prompt.shape
    dec_batch, dec_seq, _ = x_sample.shape
    depth = w_in.shape[0]
    assert depth == 1 and dec_seq == 1
    assert seq % TOKEN_BLOCK == 0 and TOKEN_BLOCK % CHUNK == 0
    assert (batch * seq) % FFN_BLOCK == 0
    assert w_in.shape[-1] == IN_WIDTH

    wi = w_in[0]
    w_q_b = wi[:, OFF_Q:OFF_K].reshape(D_MODEL, N_KV_HEADS, GROUP, HEAD_DIM).transpose(0, 2, 1, 3).reshape(
        D_MODEL, D_MODEL).astype(BF16)
    w_ob_b = w_ob[0].reshape(N_KV_HEADS, GROUP, HEAD_DIM, D_MODEL).transpose(1, 0, 2, 3).reshape(
        D_MODEL, D_MODEL).astype(BF16)
    w_oa_b = w_oa[0].astype(BF16)
    w_out_b = w_out[0].astype(BF16)
    ln1 = ln1_g[0].reshape(1, D_MODEL)
    ln2 = ln2_g[0].reshape(1, D_MODEL)
    lnf = lnf_g.reshape(1, D_MODEL)
    sgu_g = sgu_norm_g[0].reshape(1, D_MODEL)
    sinks_p = _head_perm(attn_sinks[0].astype(F32))
    slopes_p = _head_perm(_alibi_slopes())

    xs2d = x_sample.reshape(dec_batch, D_MODEL)
    wdiag = jnp.repeat(sgu_w[0][:, 0, 0], SGU_GROUP_DIM).reshape(1, D_MODEL)
    b0 = jnp.repeat(sgu_b[0][:, 0], SGU_GROUP_DIM).reshape(1, D_MODEL)
    sel_np = _selection_matrix(dec_batch)
    sel = jnp.asarray(sel_np, BF16)
    selt = jnp.asarray(sel_np.T, BF16)
    bias_s = -slopes_p[:, None] * jnp.asarray(WINDOW - np.arange(WINDOW), F32)[None, :]

    w_in_b, qsel, knew, vnew, vn, a_s, ga_s, gb_s = _sample_proj(xs2d, ln1, wi, w_q_b, sgu_g, wdiag, b0, sel)
    def to_feature_major(c):
        return c[0].transpose(0, 2, 3, 1).reshape(dec_batch, KV_WIDTH, WINDOW)

    def from_feature_major(c):
        return c.reshape(c.shape[0], N_KV_HEADS, HEAD_DIM, WINDOW).transpose(0, 3, 1, 2)[None]

    steps = (batch * seq) // TOKEN_BLOCK
    per_step = dec_batch // steps

    def per_step_rows(a):
        return jnp.pad(a.reshape(steps, per_step, KV_WIDTH), ((0, 0), (0, SUBLANES - per_step), (0, 0)))

    bexp = jnp.repeat(sgu_b[0].T, SGU_GROUP_DIM, axis=1)
    x1, kwin, vwin, w_up_b, w_down_b, o_s, nk, nv = _mix_prompt(
        x_prompt.reshape(batch * seq, D_MODEL), ln1, w_in_b, w_q_b, sgu_g, sgu_w[0], bexp,
        sinks_p * LOG2E, slopes_p * LOG2E, w_oa_b, w_ob_b, w_out_b, w_up[0], w_down[0],
        qsel.reshape(steps, per_step * N_HEADS, KV_WIDTH), per_step_rows(knew), per_step_rows(vnew),
        to_feature_major(cache_k_win), to_feature_major(cache_v_win), bias_s, sinks_p.reshape(N_HEADS, 1),
        batch=batch, seq=seq)

    xs1 = _sample_merge(o_s.reshape(dec_batch * N_HEADS, KV_WIDTH), selt, a_s, ga_s, gb_s, xs2d,
                        w_oa_b, w_ob_b, w_out_b)
    y_prompt, y_sample = _ffn(x1, xs1, ln2, w_up_b, w_down_b, lnf)

    return (y_prompt.reshape(batch, seq, D_MODEL),
            y_sample.reshape(dec_batch, dec_seq, D_MODEL),
            from_feature_major(kwin), from_feature_major(vwin),
            from_feature_major(nk), from_feature_major(nv),
            vn.reshape(depth, dec_batch, dec_seq, D_MODEL))
```

```python
import functools
import math

import numpy as np
import jax
import jax.numpy as jnp
from jax import lax
from jax.experimental import pallas as pl
from jax.experimental.pallas import tpu as pltpu

D_MODEL = 1024
N_HEADS = 16
HEAD_DIM = 64
N_KV_HEADS = 4
GROUP = N_HEADS // N_KV_HEADS
KV_WIDTH = N_KV_HEADS * HEAD_DIM
WINDOW = 128
CHUNK = 128
SGU_GROUPS = 8
SGU_GROUP_DIM = D_MODEL // SGU_GROUPS
D_FF = 4 * D_MODEL
FF_SLAB = 1024
EPS = 1e-6
NEG_BIG = -1e30
ATTN_SCALE = HEAD_DIM ** -0.5
LOG2E = math.log2(math.e)

OFF_U, OFF_V, OFF_Q, OFF_K, OFF_VA, OFF_GA, OFF_GB, IN_WIDTH = 0, 1024, 2048, 3072, 3328, 3584, 4608, 5632
W_HALF = IN_WIDTH // 2

TOKEN_BLOCK = 512
FFN_BLOCK = 1024
SUBLANES = 8
VMEM_LIMIT_BYTES = 58 * 1024 * 1024

F32 = jnp.float32
BF16 = jnp.bfloat16


def _rmsnorm(x, g):
    ms = jnp.mean(x * x, axis=-1, keepdims=True)
    return x * lax.rsqrt(ms + EPS) * g


def _gelu_tanh(x):
    c = math.sqrt(2.0 / math.pi)
    return x * (0.5 * (1.0 + jnp.tanh(c * (x + 0.044715 * (x * x * x)))))


def _dot(a, b):
    return jnp.dot(a, b, preferred_element_type=F32)


def _dot_nt(a, b):
    return lax.dot_general(a, b, (((1,), (1,)), ((), ())), preferred_element_type=F32)


def _resident(shape):
    zeros = (0,) * len(shape)
    return pl.BlockSpec(shape, lambda *_: zeros, pipeline_mode=pl.Buffered(1))


def _whole(shape):
    zeros = (0,) * len(shape)
    return pl.BlockSpec(shape, lambda *_: zeros)


def _params():
    return pltpu.CompilerParams(dimension_semantics=("arbitrary",), vmem_limit_bytes=VMEM_LIMIT_BYTES)


def _mix_prompt_kernel(x_ref, ln1_ref, w_in_ref, w_q_ref, sgu_g_ref, sgu_w_ref, bexp_ref,
                       sink_ref, slope_ref, w_oa_ref, w_ob_ref, w_out_ref, w_up_blk_ref, w_down_blk_ref,
                       s_qsel_ref, s_knew_ref, s_vnew_ref, s_ck_ref, s_cv_ref, s_bias_ref, s_sink_ref,
                       x1_ref, kwin_ref, vwin_ref, w_up_bf_ref, w_down_bf_ref, s_o_ref, s_nk_ref, s_nv_ref,
                       qs_scr, kt_scr, vm_scr, kprev_scr, vprev_scr, vn_scr, u_scr, gate_scr, a_scr, b_scr, wt_scr,
                       bias_ref,
                       *, steps_per_seq):
    step = pl.program_id(0)
    tb = x_ref.shape[0]
    nblk = tb // CHUNK
    first = (step % steps_per_seq) == 0
    rd = step % 2
    wr = 1 - rd

    @pl.when(step == 0)
    def _():
        row = lax.broadcasted_iota(jnp.int32, (CHUNK, CHUNK), 0)
        col = lax.broadcasted_iota(jnp.int32, (CHUNK, CHUNK), 1)
        for g in range(SGU_GROUPS):
            wt_scr[g] = jnp.where(row >= col, sgu_w_ref[g], 0.0).astype(BF16)
        dist = (lax.broadcasted_iota(jnp.int32, (CHUNK, 2 * CHUNK), 0) + CHUNK
                - lax.broadcasted_iota(jnp.int32, (CHUNK, 2 * CHUNK), 1))
        in_band = jnp.logical_and(dist >= 0, dist <= WINDOW)
        in_band_first = jnp.logical_and(in_band, lax.broadcasted_iota(jnp.int32, (CHUNK, 2 * CHUNK), 1) >= CHUNK)
        dist_f = dist.astype(F32)
        for h in range(N_HEADS):
            alibi = -slope_ref[h] * dist_f
            bias_ref[0, h] = jnp.where(in_band, alibi, NEG_BIG)
            bias_ref[1, h] = jnp.where(in_band_first, alibi, NEG_BIG)
        kt_scr[...] = jnp.zeros(kt_scr.shape, BF16)
        vm_scr[...] = jnp.zeros(vm_scr.shape, BF16)
        kprev_scr[...] = jnp.zeros(kprev_scr.shape, BF16)
        vprev_scr[...] = jnp.zeros(vprev_scr.shape, BF16)

    @pl.when(first)
    def _():
        kprev_scr[rd] = jnp.zeros(kprev_scr.shape[1:], BF16)
        vprev_scr[rd] = jnp.zeros(vprev_scr.shape[1:], BF16)

    w_up_bf_ref[...] = w_up_blk_ref[...].astype(BF16)
    w_down_bf_ref[...] = w_down_blk_ref[...].astype(BF16)

    x = x_ref[...]
    xn = _rmsnorm(x, ln1_ref[...]).astype(BF16)

    q = _dot(xn, w_q_ref[...])
    k = _dot(xn, w_in_ref[:, OFF_K:OFF_VA])
    va = _dot(xn, w_in_ref[:, OFF_VA:OFF_GA])
    h_v = _dot(xn, w_in_ref[:, OFF_V:OFF_Q])

    _sample_attn_kernel(s_qsel_ref, s_knew_ref, s_vnew_ref, s_ck_ref, s_cv_ref, s_bias_ref, s_sink_ref,
                        s_o_ref, s_nk_ref, s_nv_ref)

    q = (q * (ATTN_SCALE * LOG2E)).astype(BF16)
    for c in range(nblk):
        for g in range(GROUP):
            qs_scr[c, g * CHUNK:(g + 1) * CHUNK, :] = q[c * CHUNK:(c + 1) * CHUNK, g * KV_WIDTH:(g + 1) * KV_WIDTH]

    kt_f32 = k.T
    kwin_ref[...] = kt_f32[:, tb - WINDOW:]
    vwin_ref[...] = va[tb - WINDOW:, :].T
    kt = kt_f32.astype(BF16)
    vab = va.astype(BF16)
    for kvh in range(N_KV_HEADS):
        own = slice(kvh * HEAD_DIM, (kvh + 1) * HEAD_DIM)
        for c in range(nblk):
            kt_scr[kvh, c, own, :] = kt[own, c * CHUNK:(c + 1) * CHUNK]
        vm_scr[kvh, :, own] = vab[:, own]
        kprev_scr[wr, kvh, own, :] = kt[own, tb - WINDOW:]
        vprev_scr[wr, kvh, :, own] = vab[tb - WINDOW:, own]

    first_i = first.astype(jnp.int32)

    def attn_scores(c):
        qs = qs_scr[c]
        out = []
        for kvh in range(N_KV_HEADS):
            k_prev = kprev_scr[rd, kvh] if c == 0 else kt_scr[kvh, c - 1]
            out.append(_dot(qs, jnp.concatenate([k_prev, kt_scr[kvh, c]], axis=1)))
        return out

    def attn_softmax(c, scores):
        bias_sel = first_i if c == 0 else 0
        out = []
        for kvh in range(N_KV_HEADS):
            ps = []
            for g in range(GROUP):
                h = g * N_KV_HEADS + kvh
                s = scores[kvh][g * CHUNK:(g + 1) * CHUNK, :] + bias_ref[bias_sel, h]
                sink = sink_ref[h]
                m = jnp.max(s, axis=1, keepdims=True)
                p = jnp.exp2(s - m)
                denom = jnp.sum(p, axis=1, keepdims=True) + jnp.exp2(sink - m)
                ps.append((p * (1.0 / denom)).astype(BF16))
            out.append(jnp.concatenate(ps, axis=0))
        return out

    def attn_values(c, probs):
        rows = slice(c * CHUNK, (c + 1) * CHUNK)
        acc = None
        for kvh in range(N_KV_HEADS):
            if c == 0:
                v_band = jnp.concatenate([vprev_scr[rd, kvh], vm_scr[kvh, 0:CHUNK, :]], axis=0)
            else:
                v_band = vm_scr[kvh, (c - 1) * CHUNK:(c + 1) * CHUNK, :]
            o = _dot(probs[kvh], v_band)
            acc = o if acc is None else acc + o
        for g in range(GROUP):
            b_scr[rows, g * KV_WIDTH:(g + 1) * KV_WIDTH] = acc[g * CHUNK:(g + 1) * CHUNK, :].astype(BF16)

    def sgu_chunk(c):
        rows = slice(c * CHUNK, (c + 1) * CHUNK)
        vn_c = vn_scr[rows, :]
        mixed = jnp.concatenate(
            [_dot(wt_scr[g], vn_c[:, g * SGU_GROUP_DIM:(g + 1) * SGU_GROUP_DIM]) for g in range(SGU_GROUPS)],
            axis=1) + bexp_ref[...]
        a_scr[rows, :] = (u_scr[rows, :] * mixed).astype(BF16)

    def tail_v(h):
        vn_scr[...] = _rmsnorm(_gelu_tanh(h), sgu_g_ref[...]).astype(BF16)

    def tail_u(h):
        u_scr[...] = _gelu_tanh(h)

    def tail_ga(h):
        gate_scr[0] = jax.nn.sigmoid(h)

    def tail_gb(h):
        gate_scr[1] = jax.nn.sigmoid(h)

    fillers = [
        (lambda: h_v, tail_v),
        (lambda: _dot(xn, w_in_ref[:, OFF_U:OFF_V]), tail_u),
        (lambda: _dot(xn, w_in_ref[:, OFF_GA:OFF_GB]), tail_ga),
        (lambda: _dot(xn, w_in_ref[:, OFF_GB:IN_WIDTH]), tail_gb),
    ]
    for c in range(nblk):
        scores = attn_scores(c)
        proj = fillers[c][0]() if c < len(fillers) else None
        probs = attn_softmax(c, scores)
        attn_values(c, probs)
        if proj is not None:
            fillers[c][1](proj)
    for matmul, tail in fillers[nblk:]:
        tail(matmul())
    for c in range(nblk):
        sgu_chunk(c)

    hm = gate_scr[0] * _dot(a_scr[...], w_oa_ref[...]) + gate_scr[1] * _dot(b_scr[...], w_ob_ref[...])
    x1_ref[...] = x + _dot(hm.astype(BF16), w_out_ref[...])


def _mix_prompt(x2d, ln1, w_in, w_q, sgu_g, sgu_w, bexp, sinks, slopes, w_oa, w_ob, w_out, w_up_f32, w_down_f32,
                s_qsel, s_knew, s_vnew, s_cache_k, s_cache_v, s_bias, s_sink, *, batch, seq):
    n = x2d.shape[0]
    tb = TOKEN_BLOCK
    nblk = tb // CHUNK
    steps = n // tb
    steps_per_seq = seq // tb
    nb = s_cache_k.shape[0]
    per_step = nb // steps
    assert per_step * steps == nb and s_knew.shape == (steps, SUBLANES, KV_WIDTH) and per_step <= SUBLANES
    s_head_block = pl.BlockSpec((None, per_step * N_HEADS, KV_WIDTH), lambda i: (i, 0, 0))
    s_new_block = pl.BlockSpec((None, SUBLANES, KV_WIDTH), lambda i: (i, 0, 0))
    s_cache_block = pl.BlockSpec((per_step, KV_WIDTH, WINDOW), lambda i: (i, 0, 0))
    row_block = pl.BlockSpec((tb, D_MODEL), lambda i: (i, 0))
    win_block = pl.BlockSpec((None, KV_WIDTH, WINDOW), lambda i: (i // steps_per_seq, 0, 0))
    up_block = pl.BlockSpec((D_MODEL // steps, D_FF), lambda i: (i, 0))
    down_block = pl.BlockSpec((D_FF // steps, D_MODEL), lambda i: (i, 0))
    return pl.pallas_call(
        functools.partial(_mix_prompt_kernel, steps_per_seq=steps_per_seq),
        grid=(n // tb,),
        in_specs=[
            row_block,
            _resident((1, D_MODEL)),
            _resident((D_MODEL, IN_WIDTH)),
            _resident((D_MODEL, D_MODEL)),
            _resident((1, D_MODEL)),
            _resident((SGU_GROUPS, CHUNK, CHUNK)),
            _resident((CHUNK, D_MODEL)),
            pl.BlockSpec(memory_space=pltpu.SMEM),
            pl.BlockSpec(memory_space=pltpu.SMEM),
            _resident((D_MODEL, D_MODEL)),
            _resident((D_MODEL, D_MODEL)),
            _resident((D_MODEL, D_MODEL)),
            up_block,
            down_block,
            s_head_block, s_new_block, s_new_block, s_cache_block, s_cache_block,
            _resident((N_HEADS, WINDOW)), _resident((N_HEADS, 1)),
        ],
        out_specs=[row_block, win_block, win_block, up_block, down_block,
                   s_head_block, s_cache_block, s_cache_block],
        out_shape=[
            jax.ShapeDtypeStruct((n, D_MODEL), F32),
            jax.ShapeDtypeStruct((batch, KV_WIDTH, WINDOW), F32),
            jax.ShapeDtypeStruct((batch, KV_WIDTH, WINDOW), F32),
            jax.ShapeDtypeStruct((D_MODEL, D_FF), BF16),
            jax.ShapeDtypeStruct((D_FF, D_MODEL), BF16),
            jax.ShapeDtypeStruct((steps, per_step * N_HEADS, KV_WIDTH), BF16),
            jax.ShapeDtypeStruct((nb, KV_WIDTH, WINDOW), F32),
            jax.ShapeDtypeStruct((nb, KV_WIDTH, WINDOW), F32),
        ],
        scratch_shapes=[
            pltpu.VMEM((nblk, GROUP * CHUNK, KV_WIDTH), BF16),
            pltpu.VMEM((N_KV_HEADS, nblk, KV_WIDTH, CHUNK), BF16),
            pltpu.VMEM((N_KV_HEADS, tb, KV_WIDTH), BF16),
            pltpu.VMEM((2, N_KV_HEADS, KV_WIDTH, CHUNK), BF16),
            pltpu.VMEM((2, N_KV_HEADS, WINDOW, KV_WIDTH), BF16),
            pltpu.VMEM((tb, D_MODEL), BF16),
            pltpu.VMEM((tb, D_MODEL), F32),
            pltpu.VMEM((2, tb, D_MODEL), F32),
            pltpu.VMEM((tb, D_MODEL), BF16),
            pltpu.VMEM((tb, D_MODEL), BF16),
            pltpu.VMEM((SGU_GROUPS, CHUNK, CHUNK), BF16),
            pltpu.VMEM((2, N_HEADS, CHUNK, 2 * CHUNK), F32),
        ],
        compiler_params=_params(),
        name="mix_prompt",
    )(x2d, ln1, w_in, w_q, sgu_g, sgu_w, bexp, sinks, slopes, w_oa, w_ob, w_out, w_up_f32, w_down_f32,
      s_qsel, s_knew, s_vnew, s_cache_k, s_cache_v, s_bias, s_sink)


def _ffn_rows(x, ln2_ref, w_up_ref, w_down_ref, lnf_ref):
    xn = _rmsnorm(x, ln2_ref[...]).astype(BF16)
    n_slabs = D_FF // FF_SLAB

    def up(j):
        return _dot(xn, w_up_ref[:, j * FF_SLAB:(j + 1) * FF_SLAB])

    acc = x
    h_next = up(0)
    for j in range(n_slabs):
        h = h_next
        if j + 1 < n_slabs:
            h_next = up(j + 1)
        h = jnp.square(jnp.maximum(h, 0.0)).astype(BF16)
        acc = acc + _dot(h, w_down_ref[j * FF_SLAB:(j + 1) * FF_SLAB, :])
    return _rmsnorm(acc, lnf_ref[...])


def _ffn_kernel(x_ref, ln2_ref, w_up_ref, w_down_ref, lnf_ref,
                s_o_ref, s_selt_ref, s_a_ref, s_ga_ref, s_gb_ref, s_x_ref, w_oa_ref, w_ob_ref, w_out_ref,
                y_ref, ys_ref):
    i = pl.program_id(0)
    last = pl.num_programs(0) - 1

    @pl.when(i < last)
    def _():
        y_ref[...] = _ffn_rows(x_ref[...], ln2_ref, w_up_ref, w_down_ref, lnf_ref)

    @pl.when(i == last)
    def _():
        nb = s_x_ref.shape[0]
        bst = _dot(s_selt_ref[...], s_o_ref[...]).astype(BF16)
        ob = _dot(bst[0:nb, :], w_ob_ref[0:KV_WIDTH, :])
        for g in range(1, GROUP):
            ob = ob + _dot(bst[g * nb:(g + 1) * nb, :], w_ob_ref[g * KV_WIDTH:(g + 1) * KV_WIDTH, :])
        hm = s_ga_ref[...] * _dot(s_a_ref[...], w_oa_ref[...]) + s_gb_ref[...] * ob
        xs1 = s_x_ref[...] + _dot(hm.astype(BF16), w_out_ref[...])
        ys_ref[...] = _ffn_rows(xs1, ln2_ref, w_up_ref, w_down_ref, lnf_ref)


def _ffn(x2d, ln2, w_up, w_down, lnf, s_o, s_selt, s_a, s_ga, s_gb, xs2d, w_oa, w_ob, w_out):
    n = x2d.shape[0]
    nb = xs2d.shape[0]
    n_prompt_steps = n // FFN_BLOCK
    row_block = pl.BlockSpec((FFN_BLOCK, D_MODEL), lambda i: (jnp.minimum(i, n_prompt_steps - 1), 0))
    return pl.pallas_call(
        _ffn_kernel,
        grid=(n_prompt_steps + 1,),
        in_specs=[row_block, _resident((1, D_MODEL)), _resident((D_MODEL, D_FF)),
                  _resident((D_FF, D_MODEL)), _resident((1, D_MODEL)),
                  _resident((nb * N_HEADS, KV_WIDTH)), _resident((GROUP * nb, nb * N_HEADS)),
                  _resident((nb, D_MODEL)), _resident((nb, D_MODEL)), _resident((nb, D_MODEL)),
                  _resident((nb, D_MODEL)),
                  _resident((D_MODEL, D_MODEL)), _resident((D_MODEL, D_MODEL)), _resident((D_MODEL, D_MODEL))],
        out_specs=[row_block, _whole((nb, D_MODEL))],
        out_shape=[jax.ShapeDtypeStruct((n, D_MODEL), F32), jax.ShapeDtypeStruct((nb, D_MODEL), F32)],
        compiler_params=_params(),
        name="ffn",
    )(x2d, ln2, w_up, w_down, lnf, s_o, s_selt, s_a, s_ga, s_gb, xs2d, w_oa, w_ob, w_out)


def _sample_proj_kernel(x_ref, ln1_ref, w_blk_ref, sgu_g_ref, wdiag_ref, b0_ref, sel_ref,
                        w_bf_ref, w_q_bf_ref, qsel_ref, knew_ref, vnew_ref, vn_ref, a_ref, ga_ref, gb_ref,
                        xn_scr, h_scr):
    j = pl.program_id(0)
    nb = x_ref.shape[0]

    @pl.when(j == 0)
    def _():
        xn_scr[...] = _rmsnorm(x_ref[...], ln1_ref[...]).astype(BF16)

    wb = w_blk_ref[...].astype(BF16)
    w_bf_ref[...] = wb
    h_scr[j] = _dot(xn_scr[...], wb)

    def reorder_q(half):
        for head in range(N_HEADS):
            src = OFF_Q + head * HEAD_DIM - half * W_HALF
            if 0 <= src < W_HALF:
                kvh, g = divmod(head, GROUP)
                dst = g * KV_WIDTH + kvh * HEAD_DIM
                w_q_bf_ref[:, dst:dst + HEAD_DIM] = wb[:, src:src + HEAD_DIM]

    @pl.when(j == 0)
    def _():
        reorder_q(0)

    @pl.when(j == 1)
    def _():
        reorder_q(1)

        def cols(lo, hi):
            half = lo // W_HALF
            assert half == (hi - 1) // W_HALF
            return h_scr[half, :, lo - half * W_HALF:hi - half * W_HALF]

        u = _gelu_tanh(cols(OFF_U, OFF_V))
        v = _gelu_tanh(cols(OFF_V, OFF_Q))
        vn = _rmsnorm(v, sgu_g_ref[...])
        vn_ref[...] = vn
        a_ref[...] = (u * (vn * wdiag_ref[...] + b0_ref[...])).astype(BF16)
        knew_ref[...] = cols(OFF_K, OFF_VA)
        vnew_ref[...] = cols(OFF_VA, OFF_GA)
        ga_ref[...] = jax.nn.sigmoid(cols(OFF_GA, OFF_GB))
        gb_ref[...] = jax.nn.sigmoid(cols(OFF_GB, IN_WIDTH))
        q = _dot(xn_scr[...], w_q_bf_ref[...]) * ATTN_SCALE
        qstack = jnp.concatenate([q[:, g * KV_WIDTH:(g + 1) * KV_WIDTH] for g in range(GROUP)], axis=0).astype(BF16)
        qrep = _dot(sel_ref[...], qstack)
        row_kvh = lax.broadcasted_iota(jnp.int32, (nb * N_HEADS, KV_WIDTH), 0) % N_KV_HEADS
        lane_kvh = lax.broadcasted_iota(jnp.int32, (nb * N_HEADS, KV_WIDTH), 1) // HEAD_DIM
        qsel_ref[...] = jnp.where(row_kvh == lane_kvh, qrep, 0.0).astype(BF16)


def _sample_proj(xs2d, ln1, w_in_f32, sgu_g, wdiag, b0, sel):
    nb = xs2d.shape[0]
    n_blocks = IN_WIDTH // W_HALF
    w_block = pl.BlockSpec((D_MODEL, W_HALF), lambda j: (0, j))
    return pl.pallas_call(
        _sample_proj_kernel,
        grid=(n_blocks,),
        in_specs=[_whole((nb, D_MODEL)), _whole((1, D_MODEL)), w_block,
                  _whole((1, D_MODEL)), _whole((1, D_MODEL)), _whole((1, D_MODEL)),
                  _resident((nb * N_HEADS, GROUP * nb))],
        out_specs=[w_block, _whole((D_MODEL, D_MODEL)),
                   _whole((nb * N_HEADS, KV_WIDTH)), _whole((nb, KV_WIDTH)), _whole((nb, KV_WIDTH)),
                   _whole((nb, D_MODEL)), _whole((nb, D_MODEL)),
                   _whole((nb, D_MODEL)), _whole((nb, D_MODEL))],
        out_shape=[
            jax.ShapeDtypeStruct((D_MODEL, IN_WIDTH), BF16),
            jax.ShapeDtypeStruct((D_MODEL, D_MODEL), BF16),
            jax.ShapeDtypeStruct((nb * N_HEADS, KV_WIDTH), BF16),
            jax.ShapeDtypeStruct((nb, KV_WIDTH), F32),
            jax.ShapeDtypeStruct((nb, KV_WIDTH), F32),
            jax.ShapeDtypeStruct((nb, D_MODEL), F32),
            jax.ShapeDtypeStruct((nb, D_MODEL), BF16),
            jax.ShapeDtypeStruct((nb, D_MODEL), F32),
            jax.ShapeDtypeStruct((nb, D_MODEL), F32),
        ],
        scratch_shapes=[pltpu.VMEM((nb, D_MODEL), BF16), pltpu.VMEM((n_blocks, nb, W_HALF), F32)],
        compiler_params=_params(),
        name="sample_proj",
    )(xs2d, ln1, w_in_f32, sgu_g, wdiag, b0, sel)


def _sample_attn_kernel(qsel_ref, knew_ref, vnew_ref, ck_ref, cv_ref, bias_ref, sink_ref,
                        o_ref, nk_ref, nv_ref):
    bs = ck_ref.shape[0]
    row_kvh = lax.broadcasted_iota(jnp.int32, (N_HEADS, KV_WIDTH), 0) % N_KV_HEADS
    lane_kvh = lax.broadcasted_iota(jnp.int32, (N_HEADS, KV_WIDTH), 1) // HEAD_DIM
    own = row_kvh == lane_kvh
    bias = bias_ref[...]
    sink = sink_ref[...]

    qss = [qsel_ref[i * N_HEADS:(i + 1) * N_HEADS, :] for i in range(bs)]
    kns = [knew_ref[i:i + 1, :] for i in range(bs)]
    vws = [vnew_ref[i:i + 1, :] for i in range(bs)]
    scores = [_dot(qss[i], ck_ref[i].astype(BF16)) + bias for i in range(bs)]
    probs = []
    for i in range(bs):
        s = scores[i]
        s_new = jnp.sum(qss[i].astype(F32) * kns[i], axis=1, keepdims=True)
        m = jnp.maximum(jnp.maximum(jnp.max(s, axis=1, keepdims=True), s_new), sink)
        p = jnp.exp(s - m)
        p_new = jnp.exp(s_new - m)
        denom = jnp.sum(p, axis=1, keepdims=True) + p_new + jnp.exp(sink - m)
        probs.append((p.astype(BF16), p_new, denom))
    for i in range(bs):
        p, p_new, denom = probs[i]
        o = (_dot_nt(p, cv_ref[i].astype(BF16)) + p_new * vws[i]) / denom
        o_ref[i * N_HEADS:(i + 1) * N_HEADS, :] = jnp.where(own, o, 0.0).astype(BF16)

    kn_t = knew_ref[...].T
    vw_t = vnew_ref[...].T
    last_lane = lax.broadcasted_iota(jnp.int32, (KV_WIDTH, WINDOW), 1) == WINDOW - 1
    for i in range(bs):
        nk_ref[i] = jnp.where(last_lane, kn_t[:, i:i + 1], pltpu.roll(ck_ref[i], WINDOW - 1, 1))
        nv_ref[i] = jnp.where(last_lane, vw_t[:, i:i + 1], pltpu.roll(cv_ref[i], WINDOW - 1, 1))


def _head_perm(v):
    return v.reshape(N_KV_HEADS, GROUP).T.reshape(N_HEADS)


def _alibi_slopes():
    h = np.arange(1, N_HEADS + 1, dtype=np.float32)
    return np.exp2(-8.0 * h / N_HEADS).astype(np.float32)


def _selection_matrix(nb):
    r = np.arange(nb * N_HEADS)
    c = np.arange(GROUP * nb)
    same_sample = (r[:, None] // N_HEADS) == (c[None, :] % nb)
    same_member = ((r[:, None] % N_HEADS) // N_KV_HEADS) == (c[None, :] // nb)
    return (same_sample & same_member).astype(np.float32)


def kernel(x_prompt, x_sample, cache_k_win, cache_v_win, ln1_g, w_in, sgu_norm_g, sgu_w, sgu_b, attn_sinks,
           w_oa, w_ob, w_out, ln2_g, w_up, w_down, lnf_g):
    batch, seq, _ = x_prompt.shape
    dec_batch, dec_seq, _ = x_sample.shape
    depth = w_in.shape[0]
    assert depth == 1 and dec_seq == 1
    assert seq % TOKEN_BLOCK == 0 and TOKEN_BLOCK % CHUNK == 0
    assert (batch * seq) % FFN_BLOCK == 0
    assert w_in.shape[-1] == IN_WIDTH

    wi = w_in[0]
    w_ob_b = w_ob[0].reshape(N_KV_HEADS, GROUP, HEAD_DIM, D_MODEL).transpose(1, 0, 2, 3).reshape(
        D_MODEL, D_MODEL).astype(BF16)
    w_oa_b = w_oa[0].astype(BF16)
    w_out_b = w_out[0].astype(BF16)
    ln1 = ln1_g[0].reshape(1, D_MODEL)
    ln2 = ln2_g[0].reshape(1, D_MODEL)
    lnf = lnf_g.reshape(1, D_MODEL)
    sgu_g = sgu_norm_g[0].reshape(1, D_MODEL)
    sinks_p = _head_perm(attn_sinks[0].astype(F32))
    slopes_p = _head_perm(_alibi_slopes())

    xs2d = x_sample.reshape(dec_batch, D_MODEL)
    wdiag = jnp.repeat(sgu_w[0][:, 0, 0], SGU_GROUP_DIM).reshape(1, D_MODEL)
    b0 = jnp.repeat(sgu_b[0][:, 0], SGU_GROUP_DIM).reshape(1, D_MODEL)
    sel_np = _selection_matrix(dec_batch)
    sel = jnp.asarray(sel_np, BF16)
    selt = jnp.asarray(sel_np.T, BF16)
    bias_s = -slopes_p[:, None] * jnp.asarray(WINDOW - np.arange(WINDOW), F32)[None, :]

    w_in_b, w_q_b, qsel, knew, vnew, vn, a_s, ga_s, gb_s = _sample_proj(xs2d, ln1, wi, sgu_g, wdiag, b0, sel)
    def to_feature_major(c):
        return c[0].transpose(0, 2, 3, 1).reshape(dec_batch, KV_WIDTH, WINDOW)

    def from_feature_major(c):
        return c.reshape(c.shape[0], N_KV_HEADS, HEAD_DIM, WINDOW).transpose(0, 3, 1, 2)[None]

    steps = (batch * seq) // TOKEN_BLOCK
    per_step = dec_batch // steps

    def per_step_rows(a):
        return jnp.pad(a.reshape(steps, per_step, KV_WIDTH), ((0, 0), (0, SUBLANES - per_step), (0, 0)))

    bexp = jnp.repeat(sgu_b[0].T, SGU_GROUP_DIM, axis=1)
    x1, kwin, vwin, w_up_b, w_down_b, o_s, nk, nv = _mix_prompt(
        x_prompt.reshape(batch * seq, D_MODEL), ln1, w_in_b, w_q_b, sgu_g, sgu_w[0], bexp,
        sinks_p * LOG2E, slopes_p * LOG2E, w_oa_b, w_ob_b, w_out_b, w_up[0], w_down[0],
        qsel.reshape(steps, per_step * N_HEADS, KV_WIDTH), per_step_rows(knew), per_step_rows(vnew),
        to_feature_major(cache_k_win), to_feature_major(cache_v_win), bias_s, sinks_p.reshape(N_HEADS, 1),
        batch=batch, seq=seq)

    y_prompt, y_sample = _ffn(x1, ln2, w_up_b, w_down_b, lnf,
                              o_s.reshape(dec_batch * N_HEADS, KV_WIDTH), selt, a_s, ga_s, gb_s, xs2d,
                              w_oa_b, w_ob_b, w_out_b)

    return (y_prompt.reshape(batch, seq, D_MODEL),
            y_sample.reshape(dec_batch, dec_seq, D_MODEL),
            from_feature_major(kwin), from_feature_major(vwin),
            from_feature_major(nk), from_feature_major(nv),
            vn.reshape(depth, dec_batch, dec_seq, D_MODEL))
```

```python
import functools
import math

import numpy as np
import jax
import jax.numpy as jnp
from jax import lax
from jax.experimental import pallas as pl
from jax.experimental.pallas import tpu as pltpu

D_MODEL = 1024
N_HEADS = 16
HEAD_DIM = 64
N_KV_HEADS = 4
GROUP = N_HEADS // N_KV_HEADS
KV_WIDTH = N_KV_HEADS * HEAD_DIM
WINDOW = 128
CHUNK = 128
SGU_GROUPS = 8
SGU_GROUP_DIM = D_MODEL // SGU_GROUPS
D_FF = 4 * D_MODEL
FF_SLAB = 1024
EPS = 1e-6
NEG_BIG = -1e30
ATTN_SCALE = HEAD_DIM ** -0.5
LOG2E = math.log2(math.e)

OFF_U, OFF_V, OFF_Q, OFF_K, OFF_VA, OFF_GA, OFF_GB, IN_WIDTH = 0, 1024, 2048, 3072, 3328, 3584, 4608, 5632
R_K, R_VA, R_GA, R_GB, R_END = (o - OFF_K for o in (OFF_K, OFF_VA, OFF_GA, OFF_GB, IN_WIDTH))
W_HALF = IN_WIDTH // 2

TOKEN_BLOCK = 512
FFN_BLOCK = 1024
SUBLANES = 8
VMEM_LIMIT_BYTES = 58 * 1024 * 1024

F32 = jnp.float32
BF16 = jnp.bfloat16


def _rmsnorm(x, g):
    ms = jnp.mean(x * x, axis=-1, keepdims=True)
    return x * lax.rsqrt(ms + EPS) * g


def _gelu_tanh(x):
    c = math.sqrt(2.0 / math.pi)
    return x * (0.5 * (1.0 + jnp.tanh(c * (x + 0.044715 * (x * x * x)))))


def _dot(a, b):
    return jnp.dot(a, b, preferred_element_type=F32)


def _dot_nt(a, b):
    return lax.dot_general(a, b, (((1,), (1,)), ((), ())), preferred_element_type=F32)


def _resident(shape):
    zeros = (0,) * len(shape)
    return pl.BlockSpec(shape, lambda *_: zeros, pipeline_mode=pl.Buffered(1))


def _whole(shape):
    zeros = (0,) * len(shape)
    return pl.BlockSpec(shape, lambda *_: zeros)


def _params():
    return pltpu.CompilerParams(dimension_semantics=("arbitrary",), vmem_limit_bytes=VMEM_LIMIT_BYTES)


def _mix_prompt_kernel(x_ref, ln1_ref, w_uv_ref, w_rest_ref, w_q_ref, sgu_g_ref, sgu_w_ref, bexp_ref,
                       sink_ref, slope_ref, w_oa_ref, w_ob_ref, w_out_ref, w_up_blk_ref, w_down_blk_ref,
                       s_qsel_ref, s_knew_ref, s_vnew_ref, s_ck_ref, s_cv_ref, s_bias_ref, s_sink_ref,
                       x1_ref, kwin_ref, vwin_ref, w_up_bf_ref, w_down_bf_ref, s_o_ref, s_nk_ref, s_nv_ref,
                       qs_scr, kt_scr, vm_scr, kprev_scr, vprev_scr, vn_scr, u_scr, gate_scr, a_scr, b_scr, wt_scr,
                       bias_ref,
                       *, steps_per_seq):
    step = pl.program_id(0)
    tb = x_ref.shape[0]
    nblk = tb // CHUNK
    first = (step % steps_per_seq) == 0
    rd = step % 2
    wr = 1 - rd

    @pl.when(step == 0)
    def _():
        row = lax.broadcasted_iota(jnp.int32, (CHUNK, CHUNK), 0)
        col = lax.broadcasted_iota(jnp.int32, (CHUNK, CHUNK), 1)
        for g in range(SGU_GROUPS):
            wt_scr[g] = jnp.where(row >= col, sgu_w_ref[g], 0.0).astype(BF16)
        dist = (lax.broadcasted_iota(jnp.int32, (CHUNK, 2 * CHUNK), 0) + CHUNK
                - lax.broadcasted_iota(jnp.int32, (CHUNK, 2 * CHUNK), 1))
        in_band = jnp.logical_and(dist >= 0, dist <= WINDOW)
        dist_f = dist.astype(F32)
        for h in range(N_HEADS):
            bias_ref[h] = jnp.where(in_band, -slope_ref[h] * dist_f, NEG_BIG)
        kt_scr[...] = jnp.zeros(kt_scr.shape, BF16)
        vm_scr[...] = jnp.zeros(vm_scr.shape, BF16)
        kprev_scr[...] = jnp.zeros(kprev_scr.shape, BF16)
        vprev_scr[...] = jnp.zeros(vprev_scr.shape, BF16)

    @pl.when(first)
    def _():
        kprev_scr[rd] = jnp.zeros(kprev_scr.shape[1:], BF16)
        vprev_scr[rd] = jnp.zeros(vprev_scr.shape[1:], BF16)

    w_up_bf_ref[...] = w_up_blk_ref[...].astype(BF16)
    w_down_bf_ref[...] = w_down_blk_ref[...].astype(BF16)

    x = x_ref[...]
    xn = _rmsnorm(x, ln1_ref[...]).astype(BF16)

    q = _dot(xn, w_q_ref[...])
    k = _dot(xn, w_rest_ref[:, R_K:R_VA])
    va = _dot(xn, w_rest_ref[:, R_VA:R_GA])
    h_v = _dot(xn, w_uv_ref[:, OFF_V:OFF_Q])
    h_u0 = _dot(xn, w_uv_ref[:, OFF_U:OFF_U + D_MODEL // 2])

    _sample_attn_kernel(s_qsel_ref, s_knew_ref, s_vnew_ref, s_ck_ref, s_cv_ref, s_bias_ref, s_sink_ref,
                        s_o_ref, s_nk_ref, s_nv_ref)

    q = (q * (ATTN_SCALE * LOG2E)).astype(BF16)
    for c in range(nblk):
        for g in range(GROUP):
            qs_scr[c, g * CHUNK:(g + 1) * CHUNK, :] = q[c * CHUNK:(c + 1) * CHUNK, g * KV_WIDTH:(g + 1) * KV_WIDTH]

    kt_f32 = k.T
    kwin_ref[...] = kt_f32[:, tb - WINDOW:]
    vwin_ref[...] = va[tb - WINDOW:, :].T
    kt = kt_f32.astype(BF16)
    vab = va.astype(BF16)
    for kvh in range(N_KV_HEADS):
        own = slice(kvh * HEAD_DIM, (kvh + 1) * HEAD_DIM)
        for c in range(nblk):
            kt_scr[kvh, c, own, :] = kt[own, c * CHUNK:(c + 1) * CHUNK]
        vm_scr[kvh, :, own] = vab[:, own]
        kprev_scr[wr, kvh, own, :] = kt[own, tb - WINDOW:]
        vprev_scr[wr, kvh, :, own] = vab[tb - WINDOW:, own]

    no_prev = jnp.where(
        jnp.logical_and(first, lax.broadcasted_iota(jnp.int32, (CHUNK, 2 * CHUNK), 1) < CHUNK), NEG_BIG, 0.0)

    def attn_scores(c):
        qs = qs_scr[c]
        out = []
        for kvh in range(N_KV_HEADS):
            k_prev = kprev_scr[rd, kvh] if c == 0 else kt_scr[kvh, c - 1]
            out.append(_dot(qs, jnp.concatenate([k_prev, kt_scr[kvh, c]], axis=1)))
        return out

    def attn_softmax(c, scores):
        out = []
        for kvh in range(N_KV_HEADS):
            ps = []
            for g in range(GROUP):
                h = g * N_KV_HEADS + kvh
                s = scores[kvh][g * CHUNK:(g + 1) * CHUNK, :] + bias_ref[h]
                if c == 0:
                    s = s + no_prev
                sink = sink_ref[h]
                m = jnp.max(s, axis=1, keepdims=True)
                p = jnp.exp2(s - m)
                denom = jnp.sum(p, axis=1, keepdims=True) + jnp.exp2(sink - m)
                ps.append((p * (1.0 / denom)).astype(BF16))
            out.append(jnp.concatenate(ps, axis=0))
        return out

    def attn_values(c, probs):
        rows = slice(c * CHUNK, (c + 1) * CHUNK)
        acc = None
        for kvh in range(N_KV_HEADS):
            if c == 0:
                v_band = jnp.concatenate([vprev_scr[rd, kvh], vm_scr[kvh, 0:CHUNK, :]], axis=0)
            else:
                v_band = vm_scr[kvh, (c - 1) * CHUNK:(c + 1) * CHUNK, :]
            o = _dot(probs[kvh], v_band)
            acc = o if acc is None else acc + o
        for g in range(GROUP):
            b_scr[rows, g * KV_WIDTH:(g + 1) * KV_WIDTH] = acc[g * CHUNK:(g + 1) * CHUNK, :].astype(BF16)

    def sgu_chunk(c):
        rows = slice(c * CHUNK, (c + 1) * CHUNK)
        vn_c = vn_scr[rows, :]
        mixed = jnp.concatenate(
            [_dot(wt_scr[g], vn_c[:, g * SGU_GROUP_DIM:(g + 1) * SGU_GROUP_DIM]) for g in range(SGU_GROUPS)],
            axis=1) + bexp_ref[...]
        a_scr[rows, :] = (u_scr[rows, :] * mixed).astype(BF16)

    half = D_MODEL // 2

    def tail_u(h, lo):
        u_scr[:, lo:lo + half] = _gelu_tanh(h)

    def tail_ga(h, lo):
        gate_scr[0, :, lo:lo + half] = jax.nn.sigmoid(h)

    def tail_gb(h, lo):
        gate_scr[1, :, lo:lo + half] = jax.nn.sigmoid(h)

    fillers = [
        (lambda: _dot(xn, w_uv_ref[:, OFF_U + half:OFF_V]), lambda h: tail_u(h, half)),
        (lambda: _dot(xn, w_rest_ref[:, R_GA:R_GA + half]), lambda h: tail_ga(h, 0)),
        (lambda: _dot(xn, w_rest_ref[:, R_GA + half:R_GB]), lambda h: tail_ga(h, half)),
        (lambda: _dot(xn, w_rest_ref[:, R_GB:R_GB + half]), lambda h: tail_gb(h, 0)),
        (lambda: _dot(xn, w_rest_ref[:, R_GB + half:R_END]), lambda h: tail_gb(h, half)),
    ]
    vn_scr[...] = _rmsnorm(_gelu_tanh(h_v), sgu_g_ref[...]).astype(BF16)
    tail_u(h_u0, 0)
    for c in range(nblk):
        scores = attn_scores(c)
        proj = fillers[c][0]() if c < len(fillers) else None
        probs = attn_softmax(c, scores)
        attn_values(c, probs)
        if proj is not None:
            fillers[c][1](proj)
    for matmul, tail in fillers[nblk:]:
        tail(matmul())
    for c in range(nblk):
        sgu_chunk(c)

    hm = gate_scr[0] * _dot(a_scr[...], w_oa_ref[...]) + gate_scr[1] * _dot(b_scr[...], w_ob_ref[...])
    x1_ref[...] = x + _dot(hm.astype(BF16), w_out_ref[...])


def _mix_prompt(x2d, ln1, w_uv, w_rest, w_q, sgu_g, sgu_w, bexp, sinks, slopes, w_oa, w_ob, w_out,
                w_up_f32, w_down_f32,
                s_qsel, s_knew, s_vnew, s_cache_k, s_cache_v, s_bias, s_sink, *, batch, seq):
    n = x2d.shape[0]
    tb = TOKEN_BLOCK
    nblk = tb // CHUNK
    steps = n // tb
    steps_per_seq = seq // tb
    nb = s_cache_k.shape[0]
    per_step = nb // steps
    assert per_step * steps == nb and s_knew.shape == (steps, SUBLANES, KV_WIDTH) and per_step <= SUBLANES
    s_head_block = pl.BlockSpec((None, per_step * N_HEADS, KV_WIDTH), lambda i: (i, 0, 0))
    s_new_block = pl.BlockSpec((None, SUBLANES, KV_WIDTH), lambda i: (i, 0, 0))
    s_cache_block = pl.BlockSpec((per_step, KV_WIDTH, WINDOW), lambda i: (i, 0, 0))
    row_block = pl.BlockSpec((tb, D_MODEL), lambda i: (i, 0))
    win_block = pl.BlockSpec((None, KV_WIDTH, WINDOW), lambda i: (i // steps_per_seq, 0, 0))
    up_block = pl.BlockSpec((D_MODEL // steps, D_FF), lambda i: (i, 0))
    down_block = pl.BlockSpec((D_FF // steps, D_MODEL), lambda i: (i, 0))
    return pl.pallas_call(
        functools.partial(_mix_prompt_kernel, steps_per_seq=steps_per_seq),
        grid=(n // tb,),
        in_specs=[
            row_block,
            _resident((1, D_MODEL)),
            _resident((D_MODEL, OFF_Q)),
            _resident((D_MODEL, R_END)),
            _resident((D_MODEL, D_MODEL)),
            _resident((1, D_MODEL)),
            _resident((SGU_GROUPS, CHUNK, CHUNK)),
            _resident((CHUNK, D_MODEL)),
            pl.BlockSpec(memory_space=pltpu.SMEM),
            pl.BlockSpec(memory_space=pltpu.SMEM),
            _resident((D_MODEL, D_MODEL)),
            _resident((D_MODEL, D_MODEL)),
            _resident((D_MODEL, D_MODEL)),
            up_block,
            down_block,
            s_head_block, s_new_block, s_new_block, s_cache_block, s_cache_block,
            _resident((N_HEADS, WINDOW)), _resident((N_HEADS, 1)),
        ],
        out_specs=[row_block, win_block, win_block, up_block, down_block,
                   s_head_block, s_cache_block, s_cache_block],
        out_shape=[
            jax.ShapeDtypeStruct((n, D_MODEL), F32),
            jax.ShapeDtypeStruct((batch, KV_WIDTH, WINDOW), F32),
            jax.ShapeDtypeStruct((batch, KV_WIDTH, WINDOW), F32),
            jax.ShapeDtypeStruct((D_MODEL, D_FF), BF16),
            jax.ShapeDtypeStruct((D_FF, D_MODEL), BF16),
            jax.ShapeDtypeStruct((steps, per_step * N_HEADS, KV_WIDTH), BF16),
            jax.ShapeDtypeStruct((nb, KV_WIDTH, WINDOW), F32),
            jax.ShapeDtypeStruct((nb, KV_WIDTH, WINDOW), F32),
        ],
        scratch_shapes=[
            pltpu.VMEM((nblk, GROUP * CHUNK, KV_WIDTH), BF16),
            pltpu.VMEM((N_KV_HEADS, nblk, KV_WIDTH, CHUNK), BF16),
            pltpu.VMEM((N_KV_HEADS, tb, KV_WIDTH), BF16),
            pltpu.VMEM((2, N_KV_HEADS, KV_WIDTH, CHUNK), BF16),
            pltpu.VMEM((2, N_KV_HEADS, WINDOW, KV_WIDTH), BF16),
            pltpu.VMEM((tb, D_MODEL), BF16),
            pltpu.VMEM((tb, D_MODEL), F32),
            pltpu.VMEM((2, tb, D_MODEL), F32),
            pltpu.VMEM((tb, D_MODEL), BF16),
            pltpu.VMEM((tb, D_MODEL), BF16),
            pltpu.VMEM((SGU_GROUPS, CHUNK, CHUNK), BF16),
            pltpu.VMEM((N_HEADS, CHUNK, 2 * CHUNK), F32),
        ],
        compiler_params=_params(),
        name="mix_prompt",
    )(x2d, ln1, w_uv, w_rest, w_q, sgu_g, sgu_w, bexp, sinks, slopes, w_oa, w_ob, w_out, w_up_f32, w_down_f32,
      s_qsel, s_knew, s_vnew, s_cache_k, s_cache_v, s_bias, s_sink)


def _ffn_rows(x, ln2_ref, w_up_ref, w_down_ref, lnf_ref):
    xn = _rmsnorm(x, ln2_ref[...]).astype(BF16)
    n_slabs = D_FF // FF_SLAB

    def up(j):
        return _dot(xn, w_up_ref[:, j * FF_SLAB:(j + 1) * FF_SLAB])

    acc = x
    h_next = up(0)
    for j in range(n_slabs):
        h = h_next
        if j + 1 < n_slabs:
            h_next = up(j + 1)
        h = jnp.square(jnp.maximum(h, 0.0)).astype(BF16)
        acc = acc + _dot(h, w_down_ref[j * FF_SLAB:(j + 1) * FF_SLAB, :])
    return _rmsnorm(acc, lnf_ref[...])


def _ffn_kernel(x_ref, ln2_ref, w_up_ref, w_down_ref, lnf_ref,
                s_o_ref, s_selt_ref, s_a_ref, s_ga_ref, s_gb_ref, s_x_ref, w_oa_ref, w_ob_ref, w_out_ref,
                y_ref, ys_ref):
    i = pl.program_id(0)
    last = pl.num_programs(0) - 1

    @pl.when(i < last)
    def _():
        y_ref[...] = _ffn_rows(x_ref[...], ln2_ref, w_up_ref, w_down_ref, lnf_ref)

    @pl.when(i == last)
    def _():
        nb = s_x_ref.shape[0]
        bst = _dot(s_selt_ref[...], s_o_ref[...]).astype(BF16)
        ob = _dot(bst[0:nb, :], w_ob_ref[0:KV_WIDTH, :])
        for g in range(1, GROUP):
            ob = ob + _dot(bst[g * nb:(g + 1) * nb, :], w_ob_ref[g * KV_WIDTH:(g + 1) * KV_WIDTH, :])
        hm = s_ga_ref[...] * _dot(s_a_ref[...], w_oa_ref[...]) + s_gb_ref[...] * ob
        xs1 = s_x_ref[...] + _dot(hm.astype(BF16), w_out_ref[...])
        ys_ref[...] = _ffn_rows(xs1, ln2_ref, w_up_ref, w_down_ref, lnf_ref)


def _ffn(x2d, ln2, w_up, w_down, lnf, s_o, s_selt, s_a, s_ga, s_gb, xs2d, w_oa, w_ob, w_out):
    n = x2d.shape[0]
    nb = xs2d.shape[0]
    n_prompt_steps = n // FFN_BLOCK
    row_block = pl.BlockSpec((FFN_BLOCK, D_MODEL), lambda i: (jnp.minimum(i, n_prompt_steps - 1), 0))
    return pl.pallas_call(
        _ffn_kernel,
        grid=(n_prompt_steps + 1,),
        in_specs=[row_block, _resident((1, D_MODEL)), _resident((D_MODEL, D_FF)),
                  _resident((D_FF, D_MODEL)), _resident((1, D_MODEL)),
                  _resident((nb * N_HEADS, KV_WIDTH)), _resident((GROUP * nb, nb * N_HEADS)),
                  _resident((nb, D_MODEL)), _resident((nb, D_MODEL)), _resident((nb, D_MODEL)),
                  _resident((nb, D_MODEL)),
                  _resident((D_MODEL, D_MODEL)), _resident((D_MODEL, D_MODEL)), _resident((D_MODEL, D_MODEL))],
        out_specs=[row_block, _whole((nb, D_MODEL))],
        out_shape=[jax.ShapeDtypeStruct((n, D_MODEL), F32), jax.ShapeDtypeStruct((nb, D_MODEL), F32)],
        compiler_params=_params(),
        name="ffn",
    )(x2d, ln2, w_up, w_down, lnf, s_o, s_selt, s_a, s_ga, s_gb, xs2d, w_oa, w_ob, w_out)


def _sample_proj_kernel(x_ref, ln1_ref, w_blk_ref, sgu_g_ref, wdiag_ref, b0_ref, sel_ref,
                        w_uv_bf_ref, w_rest_bf_ref, w_q_bf_ref, qsel_ref, knew_ref, vnew_ref, vn_ref, a_ref,
                        ga_ref, gb_ref, xn_scr, h_scr):
    j = pl.program_id(0)
    nb = x_ref.shape[0]

    @pl.when(j == 0)
    def _():
        xn_scr[...] = _rmsnorm(x_ref[...], ln1_ref[...]).astype(BF16)

    wb = w_blk_ref[...].astype(BF16)
    h_scr[j] = _dot(xn_scr[...], wb)

    def reorder_q(half):
        for head in range(N_HEADS):
            src = OFF_Q + head * HEAD_DIM - half * W_HALF
            if 0 <= src < W_HALF:
                kvh, g = divmod(head, GROUP)
                dst = g * KV_WIDTH + kvh * HEAD_DIM
                w_q_bf_ref[:, dst:dst + HEAD_DIM] = wb[:, src:src + HEAD_DIM]

    assert OFF_Q <= W_HALF <= OFF_K

    @pl.when(j == 0)
    def _():
        w_uv_bf_ref[...] = wb[:, OFF_U:OFF_Q]
        reorder_q(0)

    @pl.when(j == 1)
    def _():
        w_rest_bf_ref[...] = wb[:, OFF_K - W_HALF:]
        reorder_q(1)

        def cols(lo, hi):
            half = lo // W_HALF
            assert half == (hi - 1) // W_HALF
            return h_scr[half, :, lo - half * W_HALF:hi - half * W_HALF]

        u = _gelu_tanh(cols(OFF_U, OFF_V))
        v = _gelu_tanh(cols(OFF_V, OFF_Q))
        vn = _rmsnorm(v, sgu_g_ref[...])
        vn_ref[...] = vn
        a_ref[...] = (u * (vn * wdiag_ref[...] + b0_ref[...])).astype(BF16)
        knew_ref[...] = cols(OFF_K, OFF_VA)
        vnew_ref[...] = cols(OFF_VA, OFF_GA)
        ga_ref[...] = jax.nn.sigmoid(cols(OFF_GA, OFF_GB))
        gb_ref[...] = jax.nn.sigmoid(cols(OFF_GB, IN_WIDTH))
        q = _dot(xn_scr[...], w_q_bf_ref[...]) * ATTN_SCALE
        qstack = jnp.concatenate([q[:, g * KV_WIDTH:(g + 1) * KV_WIDTH] for g in range(GROUP)], axis=0).astype(BF16)
        qrep = _dot(sel_ref[...], qstack)
        row_kvh = lax.broadcasted_iota(jnp.int32, (nb * N_HEADS, KV_WIDTH), 0) % N_KV_HEADS
        lane_kvh = lax.broadcasted_iota(jnp.int32, (nb * N_HEADS, KV_WIDTH), 1) // HEAD_DIM
        qsel_ref[...] = jnp.where(row_kvh == lane_kvh, qrep, 0.0).astype(BF16)


def _sample_proj(xs2d, ln1, w_in_f32, sgu_g, wdiag, b0, sel):
    nb = xs2d.shape[0]
    n_blocks = IN_WIDTH // W_HALF
    w_block = pl.BlockSpec((D_MODEL, W_HALF), lambda j: (0, j))
    return pl.pallas_call(
        _sample_proj_kernel,
        grid=(n_blocks,),
        in_specs=[_whole((nb, D_MODEL)), _whole((1, D_MODEL)), w_block,
                  _whole((1, D_MODEL)), _whole((1, D_MODEL)), _whole((1, D_MODEL)),
                  _resident((nb * N_HEADS, GROUP * nb))],
        out_specs=[_whole((D_MODEL, OFF_Q)), _whole((D_MODEL, IN_WIDTH - OFF_K)), _whole((D_MODEL, D_MODEL)),
                   _whole((nb * N_HEADS, KV_WIDTH)), _whole((nb, KV_WIDTH)), _whole((nb, KV_WIDTH)),
                   _whole((nb, D_MODEL)), _whole((nb, D_MODEL)),
                   _whole((nb, D_MODEL)), _whole((nb, D_MODEL))],
        out_shape=[
            jax.ShapeDtypeStruct((D_MODEL, OFF_Q), BF16),
            jax.ShapeDtypeStruct((D_MODEL, IN_WIDTH - OFF_K), BF16),
            jax.ShapeDtypeStruct((D_MODEL, D_MODEL), BF16),
            jax.ShapeDtypeStruct((nb * N_HEADS, KV_WIDTH), BF16),
            jax.ShapeDtypeStruct((nb, KV_WIDTH), F32),
            jax.ShapeDtypeStruct((nb, KV_WIDTH), F32),
            jax.ShapeDtypeStruct((nb, D_MODEL), F32),
            jax.ShapeDtypeStruct((nb, D_MODEL), BF16),
            jax.ShapeDtypeStruct((nb, D_MODEL), F32),
            jax.ShapeDtypeStruct((nb, D_MODEL), F32),
        ],
        scratch_shapes=[pltpu.VMEM((nb, D_MODEL), BF16), pltpu.VMEM((n_blocks, nb, W_HALF), F32)],
        compiler_params=_params(),
        name="sample_proj",
    )(xs2d, ln1, w_in_f32, sgu_g, wdiag, b0, sel)


def _sample_attn_kernel(qsel_ref, knew_ref, vnew_ref, ck_ref, cv_ref, bias_ref, sink_ref,
                        o_ref, nk_ref, nv_ref):
    bs = ck_ref.shape[0]
    row_kvh = lax.broadcasted_iota(jnp.int32, (N_HEADS, KV_WIDTH), 0) % N_KV_HEADS
    lane_kvh = lax.broadcasted_iota(jnp.int32, (N_HEADS, KV_WIDTH), 1) // HEAD_DIM
    own = row_kvh == lane_kvh
    bias = bias_ref[...]
    sink = sink_ref[...]

    qss = [qsel_ref[i * N_HEADS:(i + 1) * N_HEADS, :] for i in range(bs)]
    kns = [knew_ref[i:i + 1, :] for i in range(bs)]
    vws = [vnew_ref[i:i + 1, :] for i in range(bs)]
    scores = [_dot(qss[i], ck_ref[i].astype(BF16)) + bias for i in range(bs)]
    probs = []
    for i in range(bs):
        s = scores[i]
        s_new = jnp.sum(qss[i].astype(F32) * kns[i], axis=1, keepdims=True)
        m = jnp.maximum(jnp.maximum(jnp.max(s, axis=1, keepdims=True), s_new), sink)
        p = jnp.exp(s - m)
        p_new = jnp.exp(s_new - m)
        denom = jnp.sum(p, axis=1, keepdims=True) + p_new + jnp.exp(sink - m)
        probs.append((p.astype(BF16), p_new, denom))
    for i in range(bs):
        p, p_new, denom = probs[i]
        o = (_dot_nt(p, cv_ref[i].astype(BF16)) + p_new * vws[i]) / denom
        o_ref[i * N_HEADS:(i + 1) * N_HEADS, :] = jnp.where(own, o, 0.0).astype(BF16)

    kn_t = knew_ref[...].T
    vw_t = vnew_ref[...].T
    last_lane = lax.broadcasted_iota(jnp.int32, (KV_WIDTH, WINDOW), 1) == WINDOW - 1
    for i in range(bs):
        nk_ref[i] = jnp.where(last_lane, kn_t[:, i:i + 1], pltpu.roll(ck_ref[i], WINDOW - 1, 1))
        nv_ref[i] = jnp.where(last_lane, vw_t[:, i:i + 1], pltpu.roll(cv_ref[i], WINDOW - 1, 1))


def _head_perm(v):
    return v.reshape(N_KV_HEADS, GROUP).T.reshape(N_HEADS)


def _alibi_slopes():
    h = np.arange(1, N_HEADS + 1, dtype=np.float32)
    return np.exp2(-8.0 * h / N_HEADS).astype(np.float32)


def _selection_matrix(nb):
    r = np.arange(nb * N_HEADS)
    c = np.arange(GROUP * nb)
    same_sample = (r[:, None] // N_HEADS) == (c[None, :] % nb)
    same_member = ((r[:, None] % N_HEADS) // N_KV_HEADS) == (c[None, :] // nb)
    return (same_sample & same_member).astype(np.float32)


def kernel(x_prompt, x_sample, cache_k_win, cache_v_win, ln1_g, w_in, sgu_norm_g, sgu_w, sgu_b, attn_sinks,
           w_oa, w_ob, w_out, ln2_g, w_up, w_down, lnf_g):
    batch, seq, _ = x_prompt.shape
    dec_batch, dec_seq, _ = x_sample.shape
    depth = w_in.shape[0]
    assert depth == 1 and dec_seq == 1
    assert seq % TOKEN_BLOCK == 0 and TOKEN_BLOCK % CHUNK == 0
    assert (batch * seq) % FFN_BLOCK == 0
    assert w_in.shape[-1] == IN_WIDTH

    wi = w_in[0]
    w_ob_b = w_ob[0].reshape(N_KV_HEADS, GROUP, HEAD_DIM, D_MODEL).transpose(1, 0, 2, 3).reshape(
        D_MODEL, D_MODEL).astype(BF16)
    w_oa_b = w_oa[0].astype(BF16)
    w_out_b = w_out[0].astype(BF16)
    ln1 = ln1_g[0].reshape(1, D_MODEL)
    ln2 = ln2_g[0].reshape(1, D_MODEL)
    lnf = lnf_g.reshape(1, D_MODEL)
    sgu_g = sgu_norm_g[0].reshape(1, D_MODEL)
    sinks_p = _head_perm(attn_sinks[0].astype(F32))
    slopes_p = _head_perm(_alibi_slopes())

    xs2d = x_sample.reshape(dec_batch, D_MODEL)
    wdiag = jnp.repeat(sgu_w[0][:, 0, 0], SGU_GROUP_DIM).reshape(1, D_MODEL)
    b0 = jnp.repeat(sgu_b[0][:, 0], SGU_GROUP_DIM).reshape(1, D_MODEL)
    sel_np = _selection_matrix(dec_batch)
    sel = jnp.asarray(sel_np, BF16)
    selt = jnp.asarray(sel_np.T, BF16)
    bias_s = -slopes_p[:, None] * jnp.asarray(WINDOW - np.arange(WINDOW), F32)[None, :]

    w_uv_b, w_rest_b, w_q_b, qsel, knew, vnew, vn, a_s, ga_s, gb_s = _sample_proj(
        xs2d, ln1, wi, sgu_g, wdiag, b0, sel)
    def to_feature_major(c):
        return c[0].transpose(0, 2, 3, 1).reshape(dec_batch, KV_WIDTH, WINDOW)

    def from_feature_major(c):
        return c.reshape(c.shape[0], N_KV_HEADS, HEAD_DIM, WINDOW).transpose(0, 3, 1, 2)[None]

    steps = (batch * seq) // TOKEN_BLOCK
    per_step = dec_batch // steps

    def per_step_rows(a):
        return jnp.pad(a.reshape(steps, per_step, KV_WIDTH), ((0, 0), (0, SUBLANES - per_step), (0, 0)))

    bexp = jnp.repeat(sgu_b[0].T, SGU_GROUP_DIM, axis=1)
    x1, kwin, vwin, w_up_b, w_down_b, o_s, nk, nv = _mix_prompt(
        x_prompt.reshape(batch * seq, D_MODEL), ln1, w_uv_b, w_rest_b, w_q_b, sgu_g, sgu_w[0], bexp,
        sinks_p * LOG2E, slopes_p * LOG2E, w_oa_b, w_ob_b, w_out_b, w_up[0], w_down[0],
        qsel.reshape(steps, per_step * N_HEADS, KV_WIDTH), per_step_rows(knew), per_step_rows(vnew),
        to_feature_major(cache_k_win), to_feature_major(cache_v_win), bias_s, sinks_p.reshape(N_HEADS, 1),
        batch=batch, seq=seq)

    y_prompt, y_sample = _ffn(x1, ln2, w_up_b, w_down_b, lnf,
                              o_s.reshape(dec_batch * N_HEADS, KV_WIDTH), selt, a_s, ga_s, gb_s, xs2d,
                              w_oa_b, w_ob_b, w_out_b)

    return (y_prompt.reshape(batch, seq, D_MODEL),
            y_sample.reshape(dec_batch, dec_seq, D_MODEL),
            from_feature_major(kwin), from_feature_major(vwin),
            from_feature_major(nk), from_feature_major(nv),
            vn.reshape(depth, dec_batch, dec_seq, D_MODEL))
```

```python
import functools
import math

import numpy as np
import jax
import jax.numpy as jnp
from jax import lax
from jax.experimental import pallas as pl
from jax.experimental.pallas import tpu as pltpu

D_MODEL = 1024
N_HEADS = 16
HEAD_DIM = 64
N_KV_HEADS = 4
GROUP = N_HEADS // N_KV_HEADS
KV_WIDTH = N_KV_HEADS * HEAD_DIM
WINDOW = 128
CHUNK = 128
SGU_GROUPS = 8
SGU_GROUP_DIM = D_MODEL // SGU_GROUPS
D_FF = 4 * D_MODEL
FF_SLAB = 1024
EPS = 1e-6
NEG_BIG = -1e30
ATTN_SCALE = HEAD_DIM ** -0.5
LOG2E = math.log2(math.e)

OFF_U, OFF_V, OFF_Q, OFF_K, OFF_VA, OFF_GA, OFF_GB, IN_WIDTH = 0, 1024, 2048, 3072, 3328, 3584, 4608, 5632
R_K, R_VA, R_GA, R_GB, R_END = (o - OFF_K for o in (OFF_K, OFF_VA, OFF_GA, OFF_GB, IN_WIDTH))
W_HALF = IN_WIDTH // 2

TOKEN_BLOCK = 512
FFN_BLOCK = 1024
SUBLANES = 8
VMEM_LIMIT_BYTES = 58 * 1024 * 1024

F32 = jnp.float32
BF16 = jnp.bfloat16


def _rmsnorm(x, g):
    ms = jnp.mean(x * x, axis=-1, keepdims=True)
    return x * lax.rsqrt(ms + EPS) * g


def _gelu_tanh(x):
    c = math.sqrt(2.0 / math.pi)
    return x * (0.5 * (1.0 + jnp.tanh(c * (x + 0.044715 * (x * x * x)))))


def _dot(a, b):
    return jnp.dot(a, b, preferred_element_type=F32)


def _dot_nt(a, b):
    return lax.dot_general(a, b, (((1,), (1,)), ((), ())), preferred_element_type=F32)


def _resident(shape):
    zeros = (0,) * len(shape)
    return pl.BlockSpec(shape, lambda *_: zeros, pipeline_mode=pl.Buffered(1))


def _whole(shape):
    zeros = (0,) * len(shape)
    return pl.BlockSpec(shape, lambda *_: zeros)


def _params():
    return pltpu.CompilerParams(dimension_semantics=("arbitrary",), vmem_limit_bytes=VMEM_LIMIT_BYTES)


def _mix_prompt_kernel(x_ref, ln1_ref, w_uv_ref, w_rest_ref, w_q_ref, sgu_g_ref, sgu_w_ref, sgu_b_ref,
                       sink_ref, slope_ref, w_oa_ref, w_ob_ref, w_out_ref, w_up_blk_ref, w_down_blk_ref,
                       s_qsel_ref, s_knew_ref, s_vnew_ref, s_ck_ref, s_cv_ref, s_bias_ref,
                       x1_ref, kwin_ref, vwin_ref, w_up_bf_ref, w_down_bf_ref, s_o_ref, s_nk_ref, s_nv_ref,
                       qs_scr, kt_scr, vm_scr, kprev_scr, vprev_scr, vn_scr, u_scr, gate_scr, a_scr, b_scr, wt_scr,
                       bias_ref, bexp_ref,
                       *, steps_per_seq):
    step = pl.program_id(0)
    tb = x_ref.shape[0]
    nblk = tb // CHUNK
    first = (step % steps_per_seq) == 0
    rd = step % 2
    wr = 1 - rd

    @pl.when(step == 0)
    def _():
        row = lax.broadcasted_iota(jnp.int32, (CHUNK, CHUNK), 0)
        col = lax.broadcasted_iota(jnp.int32, (CHUNK, CHUNK), 1)
        b_t = sgu_b_ref[...].T
        for g in range(SGU_GROUPS):
            wt_scr[g] = jnp.where(row >= col, sgu_w_ref[g], 0.0).astype(BF16)
            bexp_ref[:, g * SGU_GROUP_DIM:(g + 1) * SGU_GROUP_DIM] = jnp.broadcast_to(
                b_t[:, g:g + 1], (CHUNK, SGU_GROUP_DIM))
        dist = (lax.broadcasted_iota(jnp.int32, (CHUNK, 2 * CHUNK), 0) + CHUNK
                - lax.broadcasted_iota(jnp.int32, (CHUNK, 2 * CHUNK), 1))
        in_band = jnp.logical_and(dist >= 0, dist <= WINDOW)
        dist_f = dist.astype(F32)
        for h in range(N_HEADS):
            bias_ref[h] = jnp.where(in_band, -slope_ref[h] * dist_f, NEG_BIG)
        kt_scr[...] = jnp.zeros(kt_scr.shape, BF16)
        vm_scr[...] = jnp.zeros(vm_scr.shape, BF16)
        kprev_scr[...] = jnp.zeros(kprev_scr.shape, BF16)
        vprev_scr[...] = jnp.zeros(vprev_scr.shape, BF16)

    @pl.when(first)
    def _():
        kprev_scr[rd] = jnp.zeros(kprev_scr.shape[1:], BF16)
        vprev_scr[rd] = jnp.zeros(vprev_scr.shape[1:], BF16)

    w_up_bf_ref[...] = w_up_blk_ref[...].astype(BF16)
    w_down_bf_ref[...] = w_down_blk_ref[...].astype(BF16)

    x = x_ref[...]
    xn = _rmsnorm(x, ln1_ref[...]).astype(BF16)

    q = _dot(xn, w_q_ref[...])
    k = _dot(xn, w_rest_ref[:, R_K:R_VA])
    va = _dot(xn, w_rest_ref[:, R_VA:R_GA])
    h_v = _dot(xn, w_uv_ref[:, OFF_V:OFF_Q])
    h_u0 = _dot(xn, w_uv_ref[:, OFF_U:OFF_U + D_MODEL // 2])

    _sample_attn_kernel(s_qsel_ref, s_knew_ref, s_vnew_ref, s_ck_ref, s_cv_ref, s_bias_ref, sink_ref, step % 2 == 1,
                        s_o_ref, s_nk_ref, s_nv_ref)

    q = (q * (ATTN_SCALE * LOG2E)).astype(BF16)
    for c in range(nblk):
        for g in range(GROUP):
            qs_scr[c, g * CHUNK:(g + 1) * CHUNK, :] = q[c * CHUNK:(c + 1) * CHUNK, g * KV_WIDTH:(g + 1) * KV_WIDTH]

    kt_f32 = k.T
    kwin_ref[...] = kt_f32[:, tb - WINDOW:]
    vwin_ref[...] = va[tb - WINDOW:, :].T
    kt = kt_f32.astype(BF16)
    vab = va.astype(BF16)
    for kvh in range(N_KV_HEADS):
        own = slice(kvh * HEAD_DIM, (kvh + 1) * HEAD_DIM)
        for c in range(nblk):
            kt_scr[kvh, c, own, :] = kt[own, c * CHUNK:(c + 1) * CHUNK]
        vm_scr[kvh, :, own] = vab[:, own]
        kprev_scr[wr, kvh, own, :] = kt[own, tb - WINDOW:]
        vprev_scr[wr, kvh, :, own] = vab[tb - WINDOW:, own]

    no_prev = jnp.where(
        jnp.logical_and(first, lax.broadcasted_iota(jnp.int32, (CHUNK, 2 * CHUNK), 1) < CHUNK), NEG_BIG, 0.0)

    def attn_scores(c):
        qs = qs_scr[c]
        out = []
        for kvh in range(N_KV_HEADS):
            k_prev = kprev_scr[rd, kvh] if c == 0 else kt_scr[kvh, c - 1]
            out.append(_dot(qs, jnp.concatenate([k_prev, kt_scr[kvh, c]], axis=1)))
        return out

    def attn_softmax(c, scores):
        out = []
        for kvh in range(N_KV_HEADS):
            ps = []
            for g in range(GROUP):
                h = g * N_KV_HEADS + kvh
                s = scores[kvh][g * CHUNK:(g + 1) * CHUNK, :] + bias_ref[h]
                if c == 0:
                    s = s + no_prev
                sink = sink_ref[0, kvh * GROUP + g] * LOG2E
                m = jnp.max(s, axis=1, keepdims=True)
                p = jnp.exp2(s - m)
                denom = jnp.sum(p, axis=1, keepdims=True) + jnp.exp2(sink - m)
                ps.append((p * (1.0 / denom)).astype(BF16))
            out.append(jnp.concatenate(ps, axis=0))
        return out

    def attn_values(c, probs):
        rows = slice(c * CHUNK, (c + 1) * CHUNK)
        acc = None
        for kvh in range(N_KV_HEADS):
            if c == 0:
                v_band = jnp.concatenate([vprev_scr[rd, kvh], vm_scr[kvh, 0:CHUNK, :]], axis=0)
            else:
                v_band = vm_scr[kvh, (c - 1) * CHUNK:(c + 1) * CHUNK, :]
            o = _dot(probs[kvh], v_band)
            acc = o if acc is None else acc + o
        for g in range(GROUP):
            b_scr[rows, g * KV_WIDTH:(g + 1) * KV_WIDTH] = acc[g * CHUNK:(g + 1) * CHUNK, :].astype(BF16)

    def sgu_chunk(c):
        rows = slice(c * CHUNK, (c + 1) * CHUNK)
        vn_c = vn_scr[rows, :]
        mixed = jnp.concatenate(
            [_dot(wt_scr[g], vn_c[:, g * SGU_GROUP_DIM:(g + 1) * SGU_GROUP_DIM]) for g in range(SGU_GROUPS)],
            axis=1) + bexp_ref[...]
        a_scr[rows, :] = (u_scr[rows, :] * mixed).astype(BF16)

    half = D_MODEL // 2

    def tail_u(h, lo):
        u_scr[:, lo:lo + half] = _gelu_tanh(h)

    def tail_ga(h, lo):
        gate_scr[0, :, lo:lo + half] = jax.nn.sigmoid(h)

    def tail_gb(h, lo):
        gate_scr[1, :, lo:lo + half] = jax.nn.sigmoid(h)

    fillers = [
        (lambda: _dot(xn, w_uv_ref[:, OFF_U + half:OFF_V]), lambda h: tail_u(h, half)),
        (lambda: _dot(xn, w_rest_ref[:, R_GA:R_GA + half]), lambda h: tail_ga(h, 0)),
        (lambda: _dot(xn, w_rest_ref[:, R_GA + half:R_GB]), lambda h: tail_ga(h, half)),
        (lambda: _dot(xn, w_rest_ref[:, R_GB:R_GB + half]), lambda h: tail_gb(h, 0)),
        (lambda: _dot(xn, w_rest_ref[:, R_GB + half:R_END]), lambda h: tail_gb(h, half)),
    ]
    vn_scr[...] = _rmsnorm(_gelu_tanh(h_v), sgu_g_ref[...]).astype(BF16)
    tail_u(h_u0, 0)
    for c in range(nblk):
        scores = attn_scores(c)
        proj = fillers[c][0]() if c < len(fillers) else None
        probs = attn_softmax(c, scores)
        attn_values(c, probs)
        if proj is not None:
            fillers[c][1](proj)
    for matmul, tail in fillers[nblk:]:
        tail(matmul())
    for c in range(nblk):
        sgu_chunk(c)

    hm = gate_scr[0] * _dot(a_scr[...], w_oa_ref[...]) + gate_scr[1] * _dot(b_scr[...], w_ob_ref[...])
    x1_ref[...] = x + _dot(hm.astype(BF16), w_out_ref[...])


def _mix_prompt(x2d, ln1, w_uv, w_rest, w_q, sgu_g, sgu_w, sgu_b, sinks, slopes, w_oa, w_ob, w_out,
                w_up_f32, w_down_f32,
                s_qsel, s_knew, s_vnew, s_cache_k, s_cache_v, s_bias, *, batch, seq):
    n = x2d.shape[0]
    tb = TOKEN_BLOCK
    nblk = tb // CHUNK
    steps = n // tb
    steps_per_seq = seq // tb
    nb = s_cache_k.shape[0]
    per_step = nb // steps
    assert per_step * steps == nb and 2 * per_step == SUBLANES and s_knew.shape == (nb, KV_WIDTH)
    s_head_block = pl.BlockSpec((None, per_step * N_HEADS, KV_WIDTH), lambda i: (i, 0, 0))
    s_new_block = pl.BlockSpec((SUBLANES, KV_WIDTH), lambda i: (i // 2, 0))
    s_cache_block = pl.BlockSpec((per_step, KV_WIDTH, WINDOW), lambda i: (i, 0, 0))
    row_block = pl.BlockSpec((tb, D_MODEL), lambda i: (i, 0))
    win_block = pl.BlockSpec((None, KV_WIDTH, WINDOW), lambda i: (i // steps_per_seq, 0, 0))
    up_block = pl.BlockSpec((D_MODEL // steps, D_FF), lambda i: (i, 0))
    down_block = pl.BlockSpec((D_FF // steps, D_MODEL), lambda i: (i, 0))
    return pl.pallas_call(
        functools.partial(_mix_prompt_kernel, steps_per_seq=steps_per_seq),
        grid=(n // tb,),
        in_specs=[
            row_block,
            _resident((1, D_MODEL)),
            _resident((D_MODEL, OFF_Q)),
            _resident((D_MODEL, R_END)),
            _resident((D_MODEL, D_MODEL)),
            _resident((1, D_MODEL)),
            _resident((SGU_GROUPS, CHUNK, CHUNK)),
            _resident((SGU_GROUPS, CHUNK)),
            pl.BlockSpec(memory_space=pltpu.SMEM),
            pl.BlockSpec(memory_space=pltpu.SMEM),
            _resident((D_MODEL, D_MODEL)),
            _resident((D_MODEL, D_MODEL)),
            _resident((D_MODEL, D_MODEL)),
            up_block,
            down_block,
            s_head_block, s_new_block, s_new_block, s_cache_block, s_cache_block,
            _resident((N_HEADS, WINDOW)),
        ],
        out_specs=[row_block, win_block, win_block, up_block, down_block,
                   s_head_block, s_cache_block, s_cache_block],
        out_shape=[
            jax.ShapeDtypeStruct((n, D_MODEL), F32),
            jax.ShapeDtypeStruct((batch, KV_WIDTH, WINDOW), F32),
            jax.ShapeDtypeStruct((batch, KV_WIDTH, WINDOW), F32),
            jax.ShapeDtypeStruct((D_MODEL, D_FF), BF16),
            jax.ShapeDtypeStruct((D_FF, D_MODEL), BF16),
            jax.ShapeDtypeStruct((steps, per_step * N_HEADS, KV_WIDTH), BF16),
            jax.ShapeDtypeStruct((nb, KV_WIDTH, WINDOW), F32),
            jax.ShapeDtypeStruct((nb, KV_WIDTH, WINDOW), F32),
        ],
        scratch_shapes=[
            pltpu.VMEM((nblk, GROUP * CHUNK, KV_WIDTH), BF16),
            pltpu.VMEM((N_KV_HEADS, nblk, KV_WIDTH, CHUNK), BF16),
            pltpu.VMEM((N_KV_HEADS, tb, KV_WIDTH), BF16),
            pltpu.VMEM((2, N_KV_HEADS, KV_WIDTH, CHUNK), BF16),
            pltpu.VMEM((2, N_KV_HEADS, WINDOW, KV_WIDTH), BF16),
            pltpu.VMEM((tb, D_MODEL), BF16),
            pltpu.VMEM((tb, D_MODEL), F32),
            pltpu.VMEM((2, tb, D_MODEL), F32),
            pltpu.VMEM((tb, D_MODEL), BF16),
            pltpu.VMEM((tb, D_MODEL), BF16),
            pltpu.VMEM((SGU_GROUPS, CHUNK, CHUNK), BF16),
            pltpu.VMEM((N_HEADS, CHUNK, 2 * CHUNK), F32),
            pltpu.VMEM((CHUNK, D_MODEL), F32),
        ],
        compiler_params=_params(),
        name="mix_prompt",
    )(x2d, ln1, w_uv, w_rest, w_q, sgu_g, sgu_w, sgu_b, sinks, slopes, w_oa, w_ob, w_out, w_up_f32, w_down_f32,
      s_qsel, s_knew, s_vnew, s_cache_k, s_cache_v, s_bias)


def _ffn_rows(x, ln2_ref, w_up_ref, w_down_ref, lnf_ref):
    xn = _rmsnorm(x, ln2_ref[...]).astype(BF16)
    n_slabs = D_FF // FF_SLAB

    def up(j):
        return _dot(xn, w_up_ref[:, j * FF_SLAB:(j + 1) * FF_SLAB])

    acc = x
    h_next = up(0)
    for j in range(n_slabs):
        h = h_next
        if j + 1 < n_slabs:
            h_next = up(j + 1)
        h = jnp.square(jnp.maximum(h, 0.0)).astype(BF16)
        acc = acc + _dot(h, w_down_ref[j * FF_SLAB:(j + 1) * FF_SLAB, :])
    return _rmsnorm(acc, lnf_ref[...])


def _ffn_kernel(x_ref, ln2_ref, w_up_ref, w_down_ref, lnf_ref,
                s_o_ref, s_selt_ref, s_a_ref, s_ga_ref, s_gb_ref, s_x_ref, w_oa_ref, w_ob_ref, w_out_ref,
                y_ref, ys_ref):
    i = pl.program_id(0)
    last = pl.num_programs(0) - 1

    @pl.when(i < last)
    def _():
        y_ref[...] = _ffn_rows(x_ref[...], ln2_ref, w_up_ref, w_down_ref, lnf_ref)

    @pl.when(i == last)
    def _():
        nb = s_x_ref.shape[0]
        bst = _dot(s_selt_ref[...], s_o_ref[...]).astype(BF16)
        ob = _dot(bst[0:nb, :], w_ob_ref[0:KV_WIDTH, :])
        for g in range(1, GROUP):
            ob = ob + _dot(bst[g * nb:(g + 1) * nb, :], w_ob_ref[g * KV_WIDTH:(g + 1) * KV_WIDTH, :])
        hm = s_ga_ref[...] * _dot(s_a_ref[...], w_oa_ref[...]) + s_gb_ref[...] * ob
        xs1 = s_x_ref[...] + _dot(hm.astype(BF16), w_out_ref[...])
        ys_ref[...] = _ffn_rows(xs1, ln2_ref, w_up_ref, w_down_ref, lnf_ref)


def _ffn(x2d, ln2, w_up, w_down, lnf, s_o, s_selt, s_a, s_ga, s_gb, xs2d, w_oa, w_ob, w_out):
    n = x2d.shape[0]
    nb = xs2d.shape[0]
    n_prompt_steps = n // FFN_BLOCK
    row_block = pl.BlockSpec((FFN_BLOCK, D_MODEL), lambda i: (jnp.minimum(i, n_prompt_steps - 1), 0))
    return pl.pallas_call(
        _ffn_kernel,
        grid=(n_prompt_steps + 1,),
        in_specs=[row_block, _resident((1, D_MODEL)), _resident((D_MODEL, D_FF)),
                  _resident((D_FF, D_MODEL)), _resident((1, D_MODEL)),
                  _resident((nb * N_HEADS, KV_WIDTH)), _resident((GROUP * nb, nb * N_HEADS)),
                  _resident((nb, D_MODEL)), _resident((nb, D_MODEL)), _resident((nb, D_MODEL)),
                  _resident((nb, D_MODEL)),
                  _resident((D_MODEL, D_MODEL)), _resident((D_MODEL, D_MODEL)), _resident((D_MODEL, D_MODEL))],
        out_specs=[row_block, _whole((nb, D_MODEL))],
        out_shape=[jax.ShapeDtypeStruct((n, D_MODEL), F32), jax.ShapeDtypeStruct((nb, D_MODEL), F32)],
        compiler_params=_params(),
        name="ffn",
    )(x2d, ln2, w_up, w_down, lnf, s_o, s_selt, s_a, s_ga, s_gb, xs2d, w_oa, w_ob, w_out)


def _sample_proj_kernel(x_ref, ln1_ref, w_blk_ref, sgu_g_ref, sgu_w_ref, sgu_b_ref, sel_ref,
                        w_uv_bf_ref, w_rest_bf_ref, w_q_bf_ref, qsel_ref, knew_ref, vnew_ref, vn_ref, a_ref,
                        ga_ref, gb_ref, xn_scr, h_scr):
    j = pl.program_id(0)
    nb = x_ref.shape[0]

    @pl.when(j == 0)
    def _():
        xn_scr[...] = _rmsnorm(x_ref[...], ln1_ref[...]).astype(BF16)

    wb = w_blk_ref[...].astype(BF16)
    h_scr[j] = _dot(xn_scr[...], wb)

    def reorder_q(half):
        for head in range(N_HEADS):
            src = OFF_Q + head * HEAD_DIM - half * W_HALF
            if 0 <= src < W_HALF:
                kvh, g = divmod(head, GROUP)
                dst = g * KV_WIDTH + kvh * HEAD_DIM
                w_q_bf_ref[:, dst:dst + HEAD_DIM] = wb[:, src:src + HEAD_DIM]

    assert OFF_Q <= W_HALF <= OFF_K

    @pl.when(j == 0)
    def _():
        w_uv_bf_ref[...] = wb[:, OFF_U:OFF_Q]
        reorder_q(0)

    @pl.when(j == 1)
    def _():
        w_rest_bf_ref[...] = wb[:, OFF_K - W_HALF:]
        reorder_q(1)

        def cols(lo, hi):
            half = lo // W_HALF
            assert half == (hi - 1) // W_HALF
            return h_scr[half, :, lo - half * W_HALF:hi - half * W_HALF]

        u = _gelu_tanh(cols(OFF_U, OFF_V))
        v = _gelu_tanh(cols(OFF_V, OFF_Q))
        vn = _rmsnorm(v, sgu_g_ref[...])
        vn_ref[...] = vn
        def over_groups(entry):
            return jnp.concatenate(
                [jnp.broadcast_to(entry(g), (1, SGU_GROUP_DIM)) for g in range(SGU_GROUPS)], axis=1)

        w_diag = over_groups(lambda g: sgu_w_ref[g, 0:1, 0:1])
        b_first = over_groups(lambda g: sgu_b_ref[g:g + 1, 0:1])
        a_ref[...] = (u * (vn * w_diag + b_first)).astype(BF16)
        knew_ref[...] = cols(OFF_K, OFF_VA)
        vnew_ref[...] = cols(OFF_VA, OFF_GA)
        ga_ref[...] = jax.nn.sigmoid(cols(OFF_GA, OFF_GB))
        gb_ref[...] = jax.nn.sigmoid(cols(OFF_GB, IN_WIDTH))
        q = _dot(xn_scr[...], w_q_bf_ref[...]) * ATTN_SCALE
        qstack = jnp.concatenate([q[:, g * KV_WIDTH:(g + 1) * KV_WIDTH] for g in range(GROUP)], axis=0).astype(BF16)
        qrep = _dot(sel_ref[...], qstack)
        row_kvh = lax.broadcasted_iota(jnp.int32, (nb * N_HEADS, KV_WIDTH), 0) % N_KV_HEADS
        lane_kvh = lax.broadcasted_iota(jnp.int32, (nb * N_HEADS, KV_WIDTH), 1) // HEAD_DIM
        qsel_ref[...] = jnp.where(row_kvh == lane_kvh, qrep, 0.0).astype(BF16)


def _sample_proj(xs2d, ln1, w_in_f32, sgu_g, sgu_w, sgu_b, sel):
    nb = xs2d.shape[0]
    n_blocks = IN_WIDTH // W_HALF
    w_block = pl.BlockSpec((D_MODEL, W_HALF), lambda j: (0, j))
    return pl.pallas_call(
        _sample_proj_kernel,
        grid=(n_blocks,),
        in_specs=[_whole((nb, D_MODEL)), _whole((1, D_MODEL)), w_block,
                  _whole((1, D_MODEL)),
                  pl.BlockSpec((SGU_GROUPS, SUBLANES, CHUNK), lambda j: (0, 0, 0)),
                  _whole((SGU_GROUPS, CHUNK)),
                  _resident((nb * N_HEADS, GROUP * nb))],
        out_specs=[_whole((D_MODEL, OFF_Q)), _whole((D_MODEL, IN_WIDTH - OFF_K)), _whole((D_MODEL, D_MODEL)),
                   _whole((nb * N_HEADS, KV_WIDTH)), _whole((nb, KV_WIDTH)), _whole((nb, KV_WIDTH)),
                   _whole((nb, D_MODEL)), _whole((nb, D_MODEL)),
                   _whole((nb, D_MODEL)), _whole((nb, D_MODEL))],
        out_shape=[
            jax.ShapeDtypeStruct((D_MODEL, OFF_Q), BF16),
            jax.ShapeDtypeStruct((D_MODEL, IN_WIDTH - OFF_K), BF16),
            jax.ShapeDtypeStruct((D_MODEL, D_MODEL), BF16),
            jax.ShapeDtypeStruct((nb * N_HEADS, KV_WIDTH), BF16),
            jax.ShapeDtypeStruct((nb, KV_WIDTH), F32),
            jax.ShapeDtypeStruct((nb, KV_WIDTH), F32),
            jax.ShapeDtypeStruct((nb, D_MODEL), F32),
            jax.ShapeDtypeStruct((nb, D_MODEL), BF16),
            jax.ShapeDtypeStruct((nb, D_MODEL), F32),
            jax.ShapeDtypeStruct((nb, D_MODEL), F32),
        ],
        scratch_shapes=[pltpu.VMEM((nb, D_MODEL), BF16), pltpu.VMEM((n_blocks, nb, W_HALF), F32)],
        compiler_params=_params(),
        name="sample_proj",
    )(xs2d, ln1, w_in_f32, sgu_g, sgu_w, sgu_b, sel)


def _sample_attn_kernel(qsel_ref, knew_ref, vnew_ref, ck_ref, cv_ref, bias_ref, sink_ref, upper,
                        o_ref, nk_ref, nv_ref):
    bs = ck_ref.shape[0]
    assert knew_ref.shape[0] == 2 * bs
    row0 = jnp.where(upper, bs, 0)
    row_kvh = lax.broadcasted_iota(jnp.int32, (N_HEADS, KV_WIDTH), 0) % N_KV_HEADS
    lane_kvh = lax.broadcasted_iota(jnp.int32, (N_HEADS, KV_WIDTH), 1) // HEAD_DIM
    own = row_kvh == lane_kvh
    bias = bias_ref[...]
    head_row = lax.broadcasted_iota(jnp.int32, (N_HEADS, 1), 0)
    sink = jnp.zeros((N_HEADS, 1), F32)
    for h in range(N_HEADS):
        g, kvh = divmod(h, N_KV_HEADS)
        sink = jnp.where(head_row == h, sink_ref[0, kvh * GROUP + g], sink)

    qss = [qsel_ref[i * N_HEADS:(i + 1) * N_HEADS, :] for i in range(bs)]
    kns = [knew_ref[pl.ds(row0 + i, 1), :] for i in range(bs)]
    vws = [vnew_ref[pl.ds(row0 + i, 1), :] for i in range(bs)]
    scores = [_dot(qss[i], ck_ref[i].astype(BF16)) + bias for i in range(bs)]
    probs = []
    for i in range(bs):
        s = scores[i]
        s_new = jnp.sum(qss[i].astype(F32) * kns[i], axis=1, keepdims=True)
        m = jnp.maximum(jnp.maximum(jnp.max(s, axis=1, keepdims=True), s_new), sink)
        p = jnp.exp(s - m)
        p_new = jnp.exp(s_new - m)
        denom = jnp.sum(p, axis=1, keepdims=True) + p_new + jnp.exp(sink - m)
        probs.append((p.astype(BF16), p_new, denom))
    for i in range(bs):
        p, p_new, denom = probs[i]
        o = (_dot_nt(p, cv_ref[i].astype(BF16)) + p_new * vws[i]) / denom
        o_ref[i * N_HEADS:(i + 1) * N_HEADS, :] = jnp.where(own, o, 0.0).astype(BF16)

    kn_t = knew_ref[...].T
    vw_t = vnew_ref[...].T
    last_lane = lax.broadcasted_iota(jnp.int32, (KV_WIDTH, WINDOW), 1) == WINDOW - 1
    for i in range(bs):
        kn_col = jnp.where(upper, kn_t[:, bs + i:bs + i + 1], kn_t[:, i:i + 1])
        vw_col = jnp.where(upper, vw_t[:, bs + i:bs + i + 1], vw_t[:, i:i + 1])
        nk_ref[i] = jnp.where(last_lane, kn_col, pltpu.roll(ck_ref[i], WINDOW - 1, 1))
        nv_ref[i] = jnp.where(last_lane, vw_col, pltpu.roll(cv_ref[i], WINDOW - 1, 1))


def _head_perm(v):
    return v.reshape(N_KV_HEADS, GROUP).T.reshape(N_HEADS)


def _alibi_slopes():
    h = np.arange(1, N_HEADS + 1, dtype=np.float32)
    return np.exp2(-8.0 * h / N_HEADS).astype(np.float32)


def _selection_matrix(nb):
    r = np.arange(nb * N_HEADS)
    c = np.arange(GROUP * nb)
    same_sample = (r[:, None] // N_HEADS) == (c[None, :] % nb)
    same_member = ((r[:, None] % N_HEADS) // N_KV_HEADS) == (c[None, :] // nb)
    return (same_sample & same_member).astype(np.float32)


def kernel(x_prompt, x_sample, cache_k_win, cache_v_win, ln1_g, w_in, sgu_norm_g, sgu_w, sgu_b, attn_sinks,
           w_oa, w_ob, w_out, ln2_g, w_up, w_down, lnf_g):
    batch, seq, _ = x_prompt.shape
    dec_batch, dec_seq, _ = x_sample.shape
    depth = w_in.shape[0]
    assert depth == 1 and dec_seq == 1
    assert seq % TOKEN_BLOCK == 0 and TOKEN_BLOCK % CHUNK == 0
    assert (batch * seq) % FFN_BLOCK == 0
    assert w_in.shape[-1] == IN_WIDTH

    wi = w_in[0]
    w_ob_b = w_ob[0].reshape(N_KV_HEADS, GROUP, HEAD_DIM, D_MODEL).transpose(1, 0, 2, 3).reshape(
        D_MODEL, D_MODEL).astype(BF16)
    w_oa_b = w_oa[0].astype(BF16)
    w_out_b = w_out[0].astype(BF16)
    ln1 = ln1_g[0].reshape(1, D_MODEL)
    ln2 = ln2_g[0].reshape(1, D_MODEL)
    lnf = lnf_g.reshape(1, D_MODEL)
    sgu_g = sgu_norm_g[0].reshape(1, D_MODEL)
    slopes_p = _head_perm(_alibi_slopes())

    xs2d = x_sample.reshape(dec_batch, D_MODEL)
    sel_np = _selection_matrix(dec_batch)
    sel = jnp.asarray(sel_np, BF16)
    selt = jnp.asarray(sel_np.T, BF16)
    bias_s = -slopes_p[:, None] * (WINDOW - np.arange(WINDOW, dtype=np.float32))[None, :]

    w_uv_b, w_rest_b, w_q_b, qsel, knew, vnew, vn, a_s, ga_s, gb_s = _sample_proj(
        xs2d, ln1, wi, sgu_g, sgu_w[0], sgu_b[0], sel)
    def to_feature_major(c):
        return c[0].transpose(0, 2, 3, 1).reshape(dec_batch, KV_WIDTH, WINDOW)

    def from_feature_major(c):
        return c.reshape(c.shape[0], N_KV_HEADS, HEAD_DIM, WINDOW).transpose(0, 3, 1, 2)[None]

    steps = (batch * seq) // TOKEN_BLOCK
    per_step = dec_batch // steps

    x1, kwin, vwin, w_up_b, w_down_b, o_s, nk, nv = _mix_prompt(
        x_prompt.reshape(batch * seq, D_MODEL), ln1, w_uv_b, w_rest_b, w_q_b, sgu_g, sgu_w[0], sgu_b[0],
        attn_sinks, slopes_p * LOG2E, w_oa_b, w_ob_b, w_out_b, w_up[0], w_down[0],
        qsel.reshape(steps, per_step * N_HEADS, KV_WIDTH), knew, vnew,
        to_feature_major(cache_k_win), to_feature_major(cache_v_win), bias_s,
        batch=batch, seq=seq)

    y_prompt, y_sample = _ffn(x1, ln2, w_up_b, w_down_b, lnf,
                              o_s.reshape(dec_batch * N_HEADS, KV_WIDTH), selt, a_s, ga_s, gb_s, xs2d,
                              w_oa_b, w_ob_b, w_out_b)

    return (y_prompt.reshape(batch, seq, D_MODEL),
            y_sample.reshape(dec_batch, dec_seq, D_MODEL),
            from_feature_major(kwin), from_feature_major(vwin),
            from_feature_major(nk), from_feature_major(nv),
            vn.reshape(depth, dec_batch, dec_seq, D_MODEL))
```

```python
import functools
import math

import numpy as np
import jax
import jax.numpy as jnp
from jax import lax
from jax.experimental import pallas as pl
from jax.experimental.pallas import tpu as pltpu

D_MODEL = 1024
N_HEADS = 16
HEAD_DIM = 64
N_KV_HEADS = 4
GROUP = N_HEADS // N_KV_HEADS
KV_WIDTH = N_KV_HEADS * HEAD_DIM
WINDOW = 128
CHUNK = 128
SGU_GROUPS = 8
SGU_GROUP_DIM = D_MODEL // SGU_GROUPS
D_FF = 4 * D_MODEL
FF_SLAB = 1024
EPS = 1e-6
NEG_BIG = -1e30
ATTN_SCALE = HEAD_DIM ** -0.5
LOG2E = math.log2(math.e)

OFF_U, OFF_V, OFF_Q, OFF_K, OFF_VA, OFF_GA, OFF_GB, IN_WIDTH = 0, 1024, 2048, 3072, 3328, 3584, 4608, 5632
R_K, R_VA, R_GA, R_GB, R_END = (o - OFF_K for o in (OFF_K, OFF_VA, OFF_GA, OFF_GB, IN_WIDTH))
W_HALF = IN_WIDTH // 2

TOKEN_BLOCK = 512
FFN_BLOCK = 1024
SUBLANES = 8
VMEM_LIMIT_BYTES = 58 * 1024 * 1024

F32 = jnp.float32
BF16 = jnp.bfloat16


def _rmsnorm(x, g):
    ms = jnp.mean(x * x, axis=-1, keepdims=True)
    return x * lax.rsqrt(ms + EPS) * g


def _gelu_tanh(x):
    c = math.sqrt(2.0 / math.pi)
    return x * (0.5 * (1.0 + jnp.tanh(c * (x + 0.044715 * (x * x * x)))))


def _dot(a, b):
    return jnp.dot(a, b, preferred_element_type=F32)


def _dot_nt(a, b):
    return lax.dot_general(a, b, (((1,), (1,)), ((), ())), preferred_element_type=F32)


def _resident(shape):
    zeros = (0,) * len(shape)
    return pl.BlockSpec(shape, lambda *_: zeros, pipeline_mode=pl.Buffered(1))


def _whole(shape):
    zeros = (0,) * len(shape)
    return pl.BlockSpec(shape, lambda *_: zeros)


def _params():
    return pltpu.CompilerParams(dimension_semantics=("arbitrary",), vmem_limit_bytes=VMEM_LIMIT_BYTES)


def _mix_prompt_kernel(x_ref, ln1_ref, w_uv_ref, w_rest_ref, w_q_ref, sgu_g_ref, sgu_w_ref, sgu_b_ref,
                       sink_ref, slope_ref, w_oa_ref, w_ob_ref, w_out_ref, w_up_blk_ref, w_down_blk_ref,
                       s_qsel_ref, s_knew_ref, s_vnew_ref, s_ck_ref, s_cv_ref, s_bias_ref,
                       x1_ref, kwin_ref, vwin_ref, w_up_bf_ref, w_down_bf_ref, s_o_ref, s_nk_ref, s_nv_ref,
                       qs_scr, kt_scr, vm_scr, kprev_scr, vprev_scr, vn_scr, u_scr, gate_scr, a_scr, b_scr, wt_scr,
                       bias_ref, bexp_ref,
                       *, steps_per_seq):
    step = pl.program_id(0)
    tb = x_ref.shape[0]
    nblk = tb // CHUNK
    first = (step % steps_per_seq) == 0
    rd = step % 2
    wr = 1 - rd

    @pl.when(step == 0)
    def _():
        row = lax.broadcasted_iota(jnp.int32, (CHUNK, CHUNK), 0)
        col = lax.broadcasted_iota(jnp.int32, (CHUNK, CHUNK), 1)
        b_t = sgu_b_ref[...].T
        for g in range(SGU_GROUPS):
            wt_scr[g] = jnp.where(row >= col, sgu_w_ref[g], 0.0).astype(BF16)
            bexp_ref[:, g * SGU_GROUP_DIM:(g + 1) * SGU_GROUP_DIM] = jnp.broadcast_to(
                b_t[:, g:g + 1], (CHUNK, SGU_GROUP_DIM))
        dist = (lax.broadcasted_iota(jnp.int32, (CHUNK, 2 * CHUNK), 0) + CHUNK
                - lax.broadcasted_iota(jnp.int32, (CHUNK, 2 * CHUNK), 1))
        in_band = jnp.logical_and(dist >= 0, dist <= WINDOW)
        dist_f = dist.astype(F32)
        for h in range(N_HEADS):
            bias_ref[h] = jnp.where(in_band, -slope_ref[h] * dist_f, NEG_BIG)
        kt_scr[...] = jnp.zeros(kt_scr.shape, BF16)
        vm_scr[...] = jnp.zeros(vm_scr.shape, BF16)
        kprev_scr[...] = jnp.zeros(kprev_scr.shape, BF16)
        vprev_scr[...] = jnp.zeros(vprev_scr.shape, BF16)

    @pl.when(first)
    def _():
        kprev_scr[rd] = jnp.zeros(kprev_scr.shape[1:], BF16)
        vprev_scr[rd] = jnp.zeros(vprev_scr.shape[1:], BF16)

    w_up_bf_ref[...] = w_up_blk_ref[...].astype(BF16)
    w_down_bf_ref[...] = w_down_blk_ref[...].astype(BF16)

    x = x_ref[...]
    quarters = [slice(i * tb // 4, (i + 1) * tb // 4) for i in range(4)]
    xn_parts = [_rmsnorm(x[p], ln1_ref[...]).astype(BF16) for p in quarters]
    xn = jnp.concatenate(xn_parts, axis=0)

    q = jnp.concatenate([_dot(xn_p, w_q_ref[...]) for xn_p in xn_parts], axis=0)
    k = _dot(xn, w_rest_ref[:, R_K:R_VA])
    va = _dot(xn, w_rest_ref[:, R_VA:R_GA])
    h_v = _dot(xn, w_uv_ref[:, OFF_V:OFF_Q])
    h_u0 = _dot(xn, w_uv_ref[:, OFF_U:OFF_U + D_MODEL // 2])

    _sample_attn_kernel(s_qsel_ref, s_knew_ref, s_vnew_ref, s_ck_ref, s_cv_ref, s_bias_ref, sink_ref, step % 2 == 1,
                        s_o_ref, s_nk_ref, s_nv_ref)

    q = (q * (ATTN_SCALE * LOG2E)).astype(BF16)
    for c in range(nblk):
        for g in range(GROUP):
            qs_scr[c, g * CHUNK:(g + 1) * CHUNK, :] = q[c * CHUNK:(c + 1) * CHUNK, g * KV_WIDTH:(g + 1) * KV_WIDTH]

    kt_f32 = k.T
    kwin_ref[...] = kt_f32[:, tb - WINDOW:]
    vwin_ref[...] = va[tb - WINDOW:, :].T
    kt = kt_f32.astype(BF16)
    vab = va.astype(BF16)
    for kvh in range(N_KV_HEADS):
        own = slice(kvh * HEAD_DIM, (kvh + 1) * HEAD_DIM)
        for c in range(nblk):
            kt_scr[kvh, c, own, :] = kt[own, c * CHUNK:(c + 1) * CHUNK]
        vm_scr[kvh, :, own] = vab[:, own]
        kprev_scr[wr, kvh, own, :] = kt[own, tb - WINDOW:]
        vprev_scr[wr, kvh, :, own] = vab[tb - WINDOW:, own]

    no_prev = jnp.where(
        jnp.logical_and(first, lax.broadcasted_iota(jnp.int32, (CHUNK, 2 * CHUNK), 1) < CHUNK), NEG_BIG, 0.0)

    def attn_scores(c):
        qs = qs_scr[c]
        out = []
        for kvh in range(N_KV_HEADS):
            k_prev = kprev_scr[rd, kvh] if c == 0 else kt_scr[kvh, c - 1]
            out.append(_dot(qs, jnp.concatenate([k_prev, kt_scr[kvh, c]], axis=1)))
        return out

    def attn_softmax(c, scores):
        out = []
        for kvh in range(N_KV_HEADS):
            ps = []
            for g in range(GROUP):
                h = g * N_KV_HEADS + kvh
                s = scores[kvh][g * CHUNK:(g + 1) * CHUNK, :] + bias_ref[h]
                if c == 0:
                    s = s + no_prev
                sink = sink_ref[0, kvh * GROUP + g] * LOG2E
                m = jnp.max(s, axis=1, keepdims=True)
                p = jnp.exp2(s - m)
                denom = jnp.sum(p, axis=1, keepdims=True) + jnp.exp2(sink - m)
                ps.append((p * (1.0 / denom)).astype(BF16))
            out.append(jnp.concatenate(ps, axis=0))
        return out

    def attn_values(c, probs):
        rows = slice(c * CHUNK, (c + 1) * CHUNK)
        acc = None
        for kvh in range(N_KV_HEADS):
            if c == 0:
                v_band = jnp.concatenate([vprev_scr[rd, kvh], vm_scr[kvh, 0:CHUNK, :]], axis=0)
            else:
                v_band = vm_scr[kvh, (c - 1) * CHUNK:(c + 1) * CHUNK, :]
            o = _dot(probs[kvh], v_band)
            acc = o if acc is None else acc + o
        for g in range(GROUP):
            b_scr[rows, g * KV_WIDTH:(g + 1) * KV_WIDTH] = acc[g * CHUNK:(g + 1) * CHUNK, :].astype(BF16)

    def sgu_chunk(c):
        rows = slice(c * CHUNK, (c + 1) * CHUNK)
        vn_c = vn_scr[rows, :]
        mixed = jnp.concatenate(
            [_dot(wt_scr[g], vn_c[:, g * SGU_GROUP_DIM:(g + 1) * SGU_GROUP_DIM]) for g in range(SGU_GROUPS)],
            axis=1) + bexp_ref[...]
        a_scr[rows, :] = (u_scr[rows, :] * mixed).astype(BF16)

    half = D_MODEL // 2

    def tail_u(h, lo):
        u_scr[:, lo:lo + half] = _gelu_tanh(h)

    def tail_ga(h, lo):
        gate_scr[0, :, lo:lo + half] = jax.nn.sigmoid(h)

    def tail_gb(h, lo):
        gate_scr[1, :, lo:lo + half] = jax.nn.sigmoid(h)

    fillers = [
        (lambda: _dot(xn, w_uv_ref[:, OFF_U + half:OFF_V]), lambda h: tail_u(h, half)),
        (lambda: _dot(xn, w_rest_ref[:, R_GA:R_GA + half]), lambda h: tail_ga(h, 0)),
        (lambda: _dot(xn, w_rest_ref[:, R_GA + half:R_GB]), lambda h: tail_ga(h, half)),
        (lambda: _dot(xn, w_rest_ref[:, R_GB:R_GB + half]), lambda h: tail_gb(h, 0)),
        (lambda: _dot(xn, w_rest_ref[:, R_GB + half:R_END]), lambda h: tail_gb(h, half)),
    ]
    vn_scr[...] = _rmsnorm(_gelu_tanh(h_v), sgu_g_ref[...]).astype(BF16)
    tail_u(h_u0, 0)
    for c in range(nblk):
        scores = attn_scores(c)
        proj = fillers[c][0]() if c < len(fillers) else None
        probs = attn_softmax(c, scores)
        attn_values(c, probs)
        if proj is not None:
            fillers[c][1](proj)
    for matmul, tail in fillers[nblk:]:
        tail(matmul())
    for c in range(nblk):
        sgu_chunk(c)

    hm = gate_scr[0] * _dot(a_scr[...], w_oa_ref[...]) + gate_scr[1] * _dot(b_scr[...], w_ob_ref[...])
    x1_ref[...] = x + _dot(hm.astype(BF16), w_out_ref[...])


def _mix_prompt(x2d, ln1, w_uv, w_rest, w_q, sgu_g, sgu_w, sgu_b, sinks, slopes, w_oa, w_ob, w_out,
                w_up_f32, w_down_f32,
                s_qsel, s_knew, s_vnew, s_cache_k, s_cache_v, s_bias, *, batch, seq):
    n = x2d.shape[0]
    tb = TOKEN_BLOCK
    nblk = tb // CHUNK
    steps = n // tb
    steps_per_seq = seq // tb
    nb = s_cache_k.shape[0]
    per_step = nb // steps
    assert per_step * steps == nb and 2 * per_step == SUBLANES and s_knew.shape == (nb, KV_WIDTH)
    s_head_block = pl.BlockSpec((None, per_step * N_HEADS, KV_WIDTH), lambda i: (i, 0, 0))
    s_new_block = pl.BlockSpec((SUBLANES, KV_WIDTH), lambda i: (i // 2, 0))
    s_cache_block = pl.BlockSpec((per_step, KV_WIDTH, WINDOW), lambda i: (i, 0, 0))
    row_block = pl.BlockSpec((tb, D_MODEL), lambda i: (i, 0))
    win_block = pl.BlockSpec((None, KV_WIDTH, WINDOW), lambda i: (i // steps_per_seq, 0, 0))
    up_block = pl.BlockSpec((D_MODEL // steps, D_FF), lambda i: (i, 0))
    down_block = pl.BlockSpec((D_FF // steps, D_MODEL), lambda i: (i, 0))
    return pl.pallas_call(
        functools.partial(_mix_prompt_kernel, steps_per_seq=steps_per_seq),
        grid=(n // tb,),
        in_specs=[
            row_block,
            _resident((1, D_MODEL)),
            _resident((D_MODEL, OFF_Q)),
            _resident((D_MODEL, R_END)),
            _resident((D_MODEL, D_MODEL)),
            _resident((1, D_MODEL)),
            _resident((SGU_GROUPS, CHUNK, CHUNK)),
            _resident((SGU_GROUPS, CHUNK)),
            pl.BlockSpec(memory_space=pltpu.SMEM),
            pl.BlockSpec(memory_space=pltpu.SMEM),
            _resident((D_MODEL, D_MODEL)),
            _resident((D_MODEL, D_MODEL)),
            _resident((D_MODEL, D_MODEL)),
            up_block,
            down_block,
            s_head_block, s_new_block, s_new_block, s_cache_block, s_cache_block,
            _resident((N_HEADS, WINDOW)),
        ],
        out_specs=[row_block, win_block, win_block, up_block, down_block,
                   s_head_block, s_cache_block, s_cache_block],
        out_shape=[
            jax.ShapeDtypeStruct((n, D_MODEL), F32),
            jax.ShapeDtypeStruct((batch, KV_WIDTH, WINDOW), F32),
            jax.ShapeDtypeStruct((batch, KV_WIDTH, WINDOW), F32),
            jax.ShapeDtypeStruct((D_MODEL, D_FF), BF16),
            jax.ShapeDtypeStruct((D_FF, D_MODEL), BF16),
            jax.ShapeDtypeStruct((steps, per_step * N_HEADS, KV_WIDTH), BF16),
            jax.ShapeDtypeStruct((nb, KV_WIDTH, WINDOW), F32),
            jax.ShapeDtypeStruct((nb, KV_WIDTH, WINDOW), F32),
        ],
        scratch_shapes=[
            pltpu.VMEM((nblk, GROUP * CHUNK, KV_WIDTH), BF16),
            pltpu.VMEM((N_KV_HEADS, nblk, KV_WIDTH, CHUNK), BF16),
            pltpu.VMEM((N_KV_HEADS, tb, KV_WIDTH), BF16),
            pltpu.VMEM((2, N_KV_HEADS, KV_WIDTH, CHUNK), BF16),
            pltpu.VMEM((2, N_KV_HEADS, WINDOW, KV_WIDTH), BF16),
            pltpu.VMEM((tb, D_MODEL), BF16),
            pltpu.VMEM((tb, D_MODEL), F32),
            pltpu.VMEM((2, tb, D_MODEL), F32),
            pltpu.VMEM((tb, D_MODEL), BF16),
            pltpu.VMEM((tb, D_MODEL), BF16),
            pltpu.VMEM((SGU_GROUPS, CHUNK, CHUNK), BF16),
            pltpu.VMEM((N_HEADS, CHUNK, 2 * CHUNK), F32),
            pltpu.VMEM((CHUNK, D_MODEL), F32),
        ],
        compiler_params=_params(),
        name="mix_prompt",
    )(x2d, ln1, w_uv, w_rest, w_q, sgu_g, sgu_w, sgu_b, sinks, slopes, w_oa, w_ob, w_out, w_up_f32, w_down_f32,
      s_qsel, s_knew, s_vnew, s_cache_k, s_cache_v, s_bias)


def _ffn_rows(x, y_ref, ln2_ref, w_up_ref, w_down_ref, lnf_ref, *, row_parts):
    n_slabs = D_FF // FF_SLAB
    m = x.shape[0]
    parts = [slice(i * m // row_parts, (i + 1) * m // row_parts) for i in range(row_parts)]

    def up(j, xn):
        return _dot(xn, w_up_ref[:, j * FF_SLAB:(j + 1) * FF_SLAB])

    xns = [_rmsnorm(x[p], ln2_ref[...]).astype(BF16) for p in parts]
    h_next = jnp.concatenate([up(0, xn_p) for xn_p in xns], axis=0) if row_parts > 1 else up(0, xns[0])
    xn = jnp.concatenate(xns, axis=0) if row_parts > 1 else xns[0]

    acc = x
    for j in range(n_slabs):
        h = h_next
        if j + 1 < n_slabs:
            h_next = up(j + 1, xn)
        h = jnp.square(jnp.maximum(h, 0.0)).astype(BF16)
        w_d = w_down_ref[j * FF_SLAB:(j + 1) * FF_SLAB, :]
        if j + 1 < n_slabs:
            acc = acc + _dot(h, w_d)
        else:
            for p in parts:
                y_ref[p, :] = _rmsnorm(acc[p] + _dot(h[p], w_d), lnf_ref[...])


def _ffn_kernel(x_ref, ln2_ref, w_up_ref, w_down_ref, lnf_ref,
                s_o_ref, s_selt_ref, s_a_ref, s_ga_ref, s_gb_ref, s_x_ref, w_oa_ref, w_ob_ref, w_out_ref,
                y_ref, ys_ref):
    i = pl.program_id(0)
    last = pl.num_programs(0) - 1

    @pl.when(i < last)
    def _():
        _ffn_rows(x_ref[...], y_ref, ln2_ref, w_up_ref, w_down_ref, lnf_ref, row_parts=4)

    @pl.when(i == last)
    def _():
        nb = s_x_ref.shape[0]
        bst = _dot(s_selt_ref[...], s_o_ref[...]).astype(BF16)
        ob = _dot(bst[0:nb, :], w_ob_ref[0:KV_WIDTH, :])
        for g in range(1, GROUP):
            ob = ob + _dot(bst[g * nb:(g + 1) * nb, :], w_ob_ref[g * KV_WIDTH:(g + 1) * KV_WIDTH, :])
        hm = s_ga_ref[...] * _dot(s_a_ref[...], w_oa_ref[...]) + s_gb_ref[...] * ob
        xs1 = s_x_ref[...] + _dot(hm.astype(BF16), w_out_ref[...])
        _ffn_rows(xs1, ys_ref, ln2_ref, w_up_ref, w_down_ref, lnf_ref, row_parts=1)


def _ffn(x2d, ln2, w_up, w_down, lnf, s_o, s_selt, s_a, s_ga, s_gb, xs2d, w_oa, w_ob, w_out):
    n = x2d.shape[0]
    nb = xs2d.shape[0]
    n_prompt_steps = n // FFN_BLOCK
    row_block = pl.BlockSpec((FFN_BLOCK, D_MODEL), lambda i: (jnp.minimum(i, n_prompt_steps - 1), 0))
    return pl.pallas_call(
        _ffn_kernel,
        grid=(n_prompt_steps + 1,),
        in_specs=[row_block, _resident((1, D_MODEL)), _resident((D_MODEL, D_FF)),
                  _resident((D_FF, D_MODEL)), _resident((1, D_MODEL)),
                  _resident((nb * N_HEADS, KV_WIDTH)), _resident((GROUP * nb, nb * N_HEADS)),
                  _resident((nb, D_MODEL)), _resident((nb, D_MODEL)), _resident((nb, D_MODEL)),
                  _resident((nb, D_MODEL)),
                  _resident((D_MODEL, D_MODEL)), _resident((D_MODEL, D_MODEL)), _resident((D_MODEL, D_MODEL))],
        out_specs=[row_block, _whole((nb, D_MODEL))],
        out_shape=[jax.ShapeDtypeStruct((n, D_MODEL), F32), jax.ShapeDtypeStruct((nb, D_MODEL), F32)],
        compiler_params=_params(),
        name="ffn",
    )(x2d, ln2, w_up, w_down, lnf, s_o, s_selt, s_a, s_ga, s_gb, xs2d, w_oa, w_ob, w_out)


def _sample_proj_kernel(x_ref, ln1_ref, w_blk_ref, sgu_g_ref, sgu_w_ref, sgu_b_ref, sel_ref,
                        w_uv_bf_ref, w_rest_bf_ref, w_q_bf_ref, qsel_ref, knew_ref, vnew_ref, vn_ref, a_ref,
                        ga_ref, gb_ref, xn_scr, h_scr):
    j = pl.program_id(0)
    nb = x_ref.shape[0]

    @pl.when(j == 0)
    def _():
        xn_scr[...] = _rmsnorm(x_ref[...], ln1_ref[...]).astype(BF16)

    wb = w_blk_ref[...].astype(BF16)
    h_scr[j] = _dot(xn_scr[...], wb)

    def reorder_q(half):
        for head in range(N_HEADS):
            src = OFF_Q + head * HEAD_DIM - half * W_HALF
            if 0 <= src < W_HALF:
                kvh, g = divmod(head, GROUP)
                dst = g * KV_WIDTH + kvh * HEAD_DIM
                w_q_bf_ref[:, dst:dst + HEAD_DIM] = wb[:, src:src + HEAD_DIM]

    assert OFF_Q <= W_HALF <= OFF_K

    @pl.when(j == 0)
    def _():
        w_uv_bf_ref[...] = wb[:, OFF_U:OFF_Q]
        reorder_q(0)

    @pl.when(j == 1)
    def _():
        w_rest_bf_ref[...] = wb[:, OFF_K - W_HALF:]
        reorder_q(1)

        def cols(lo, hi):
            half = lo // W_HALF
            assert half == (hi - 1) // W_HALF
            return h_scr[half, :, lo - half * W_HALF:hi - half * W_HALF]

        u = _gelu_tanh(cols(OFF_U, OFF_V))
        v = _gelu_tanh(cols(OFF_V, OFF_Q))
        vn = _rmsnorm(v, sgu_g_ref[...])
        vn_ref[...] = vn
        def over_groups(entry):
            return jnp.concatenate(
                [jnp.broadcast_to(entry(g), (1, SGU_GROUP_DIM)) for g in range(SGU_GROUPS)], axis=1)

        w_diag = over_groups(lambda g: sgu_w_ref[g, 0:1, 0:1])
        b_first = over_groups(lambda g: sgu_b_ref[g:g + 1, 0:1])
        a_ref[...] = (u * (vn * w_diag + b_first)).astype(BF16)
        knew_ref[...] = cols(OFF_K, OFF_VA)
        vnew_ref[...] = cols(OFF_VA, OFF_GA)
        ga_ref[...] = jax.nn.sigmoid(cols(OFF_GA, OFF_GB))
        gb_ref[...] = jax.nn.sigmoid(cols(OFF_GB, IN_WIDTH))
        q = _dot(xn_scr[...], w_q_bf_ref[...]) * ATTN_SCALE
        qstack = jnp.concatenate([q[:, g * KV_WIDTH:(g + 1) * KV_WIDTH] for g in range(GROUP)], axis=0).astype(BF16)
        qrep = _dot(sel_ref[...], qstack)
        row_kvh = lax.broadcasted_iota(jnp.int32, (nb * N_HEADS, KV_WIDTH), 0) % N_KV_HEADS
        lane_kvh = lax.broadcasted_iota(jnp.int32, (nb * N_HEADS, KV_WIDTH), 1) // HEAD_DIM
        qsel_ref[...] = jnp.where(row_kvh == lane_kvh, qrep, 0.0).astype(BF16)


def _sample_proj(xs2d, ln1, w_in_f32, sgu_g, sgu_w, sgu_b, sel):
    nb = xs2d.shape[0]
    n_blocks = IN_WIDTH // W_HALF
    w_block = pl.BlockSpec((D_MODEL, W_HALF), lambda j: (0, j))
    return pl.pallas_call(
        _sample_proj_kernel,
        grid=(n_blocks,),
        in_specs=[_whole((nb, D_MODEL)), _whole((1, D_MODEL)), w_block,
                  _whole((1, D_MODEL)),
                  pl.BlockSpec((SGU_GROUPS, SUBLANES, CHUNK), lambda j: (0, 0, 0)),
                  _whole((SGU_GROUPS, CHUNK)),
                  _resident((nb * N_HEADS, GROUP * nb))],
        out_specs=[_whole((D_MODEL, OFF_Q)), _whole((D_MODEL, IN_WIDTH - OFF_K)), _whole((D_MODEL, D_MODEL)),
                   _whole((nb * N_HEADS, KV_WIDTH)), _whole((nb, KV_WIDTH)), _whole((nb, KV_WIDTH)),
                   _whole((nb, D_MODEL)), _whole((nb, D_MODEL)),
                   _whole((nb, D_MODEL)), _whole((nb, D_MODEL))],
        out_shape=[
            jax.ShapeDtypeStruct((D_MODEL, OFF_Q), BF16),
            jax.ShapeDtypeStruct((D_MODEL, IN_WIDTH - OFF_K), BF16),
            jax.ShapeDtypeStruct((D_MODEL, D_MODEL), BF16),
            jax.ShapeDtypeStruct((nb * N_HEADS, KV_WIDTH), BF16),
            jax.ShapeDtypeStruct((nb, KV_WIDTH), F32),
            jax.ShapeDtypeStruct((nb, KV_WIDTH), F32),
            jax.ShapeDtypeStruct((nb, D_MODEL), F32),
            jax.ShapeDtypeStruct((nb, D_MODEL), BF16),
            jax.ShapeDtypeStruct((nb, D_MODEL), F32),
            jax.ShapeDtypeStruct((nb, D_MODEL), F32),
        ],
        scratch_shapes=[pltpu.VMEM((nb, D_MODEL), BF16), pltpu.VMEM((n_blocks, nb, W_HALF), F32)],
        compiler_params=_params(),
        name="sample_proj",
    )(xs2d, ln1, w_in_f32, sgu_g, sgu_w, sgu_b, sel)


def _sample_attn_kernel(qsel_ref, knew_ref, vnew_ref, ck_ref, cv_ref, bias_ref, sink_ref, upper,
                        o_ref, nk_ref, nv_ref):
    bs = ck_ref.shape[0]
    assert knew_ref.shape[0] == 2 * bs
    row0 = jnp.where(upper, bs, 0)
    row_kvh = lax.broadcasted_iota(jnp.int32, (N_HEADS, KV_WIDTH), 0) % N_KV_HEADS
    lane_kvh = lax.broadcasted_iota(jnp.int32, (N_HEADS, KV_WIDTH), 1) // HEAD_DIM
    own = row_kvh == lane_kvh
    bias = bias_ref[...]
    head_row = lax.broadcasted_iota(jnp.int32, (N_HEADS, 1), 0)
    sink = jnp.zeros((N_HEADS, 1), F32)
    for h in range(N_HEADS):
        g, kvh = divmod(h, N_KV_HEADS)
        sink = jnp.where(head_row == h, sink_ref[0, kvh * GROUP + g], sink)

    qss = [qsel_ref[i * N_HEADS:(i + 1) * N_HEADS, :] for i in range(bs)]
    kns = [knew_ref[pl.ds(row0 + i, 1), :] for i in range(bs)]
    vws = [vnew_ref[pl.ds(row0 + i, 1), :] for i in range(bs)]
    scores = [_dot(qss[i], ck_ref[i].astype(BF16)) + bias for i in range(bs)]
    probs = []
    for i in range(bs):
        s = scores[i]
        s_new = jnp.sum(qss[i].astype(F32) * kns[i], axis=1, keepdims=True)
        m = jnp.maximum(jnp.maximum(jnp.max(s, axis=1, keepdims=True), s_new), sink)
        p = jnp.exp(s - m)
        p_new = jnp.exp(s_new - m)
        denom = jnp.sum(p, axis=1, keepdims=True) + p_new + jnp.exp(sink - m)
        probs.append((p.astype(BF16), p_new, denom))
    for i in range(bs):
        p, p_new, denom = probs[i]
        o = (_dot_nt(p, cv_ref[i].astype(BF16)) + p_new * vws[i]) / denom
        o_ref[i * N_HEADS:(i + 1) * N_HEADS, :] = jnp.where(own, o, 0.0).astype(BF16)

    kn_t = knew_ref[...].T
    vw_t = vnew_ref[...].T
    last_lane = lax.broadcasted_iota(jnp.int32, (KV_WIDTH, WINDOW), 1) == WINDOW - 1
    for i in range(bs):
        kn_col = jnp.where(upper, kn_t[:, bs + i:bs + i + 1], kn_t[:, i:i + 1])
        vw_col = jnp.where(upper, vw_t[:, bs + i:bs + i + 1], vw_t[:, i:i + 1])
        nk_ref[i] = jnp.where(last_lane, kn_col, pltpu.roll(ck_ref[i], WINDOW - 1, 1))
        nv_ref[i] = jnp.where(last_lane, vw_col, pltpu.roll(cv_ref[i], WINDOW - 1, 1))


def _head_perm(v):
    return v.reshape(N_KV_HEADS, GROUP).T.reshape(N_HEADS)


def _alibi_slopes():
    h = np.arange(1, N_HEADS + 1, dtype=np.float32)
    return np.exp2(-8.0 * h / N_HEADS).astype(np.float32)


def _selection_matrix(nb):
    r = np.arange(nb * N_HEADS)
    c = np.arange(GROUP * nb)
    same_sample = (r[:, None] // N_HEADS) == (c[None, :] % nb)
    same_member = ((r[:, None] % N_HEADS) // N_KV_HEADS) == (c[None, :] // nb)
    return (same_sample & same_member).astype(np.float32)


def kernel(x_prompt, x_sample, cache_k_win, cache_v_win, ln1_g, w_in, sgu_norm_g, sgu_w, sgu_b, attn_sinks,
           w_oa, w_ob, w_out, ln2_g, w_up, w_down, lnf_g):
    batch, seq, _ = x_prompt.shape
    dec_batch, dec_seq, _ = x_sample.shape
    depth = w_in.shape[0]
    assert depth == 1 and dec_seq == 1
    assert seq % TOKEN_BLOCK == 0 and TOKEN_BLOCK % CHUNK == 0
    assert (batch * seq) % FFN_BLOCK == 0
    assert w_in.shape[-1] == IN_WIDTH

    wi = w_in[0]
    w_ob_b = w_ob[0].reshape(N_KV_HEADS, GROUP, HEAD_DIM, D_MODEL).transpose(1, 0, 2, 3).reshape(
        D_MODEL, D_MODEL).astype(BF16)
    w_oa_b = w_oa[0].astype(BF16)
    w_out_b = w_out[0].astype(BF16)
    ln1 = ln1_g[0].reshape(1, D_MODEL)
    ln2 = ln2_g[0].reshape(1, D_MODEL)
    lnf = lnf_g.reshape(1, D_MODEL)
    sgu_g = sgu_norm_g[0].reshape(1, D_MODEL)
    slopes_p = _head_perm(_alibi_slopes())

    xs2d = x_sample.reshape(dec_batch, D_MODEL)
    sel_np = _selection_matrix(dec_batch)
    sel = jnp.asarray(sel_np, BF16)
    selt = jnp.asarray(sel_np.T, BF16)
    bias_s = -slopes_p[:, None] * (WINDOW - np.arange(WINDOW, dtype=np.float32))[None, :]

    w_uv_b, w_rest_b, w_q_b, qsel, knew, vnew, vn, a_s, ga_s, gb_s = _sample_proj(
        xs2d, ln1, wi, sgu_g, sgu_w[0], sgu_b[0], sel)
    def to_feature_major(c):
        return c[0].transpose(0, 2, 3, 1).reshape(dec_batch, KV_WIDTH, WINDOW)

    def from_feature_major(c):
        return c.reshape(c.shape[0], N_KV_HEADS, HEAD_DIM, WINDOW).transpose(0, 3, 1, 2)[None]

    steps = (batch * seq) // TOKEN_BLOCK
    per_step = dec_batch // steps

    x1, kwin, vwin, w_up_b, w_down_b, o_s, nk, nv = _mix_prompt(
        x_prompt.reshape(batch * seq, D_MODEL), ln1, w_uv_b, w_rest_b, w_q_b, sgu_g, sgu_w[0], sgu_b[0],
        attn_sinks, slopes_p * LOG2E, w_oa_b, w_ob_b, w_out_b, w_up[0], w_down[0],
        qsel.reshape(steps, per_step * N_HEADS, KV_WIDTH), knew, vnew,
        to_feature_major(cache_k_win), to_feature_major(cache_v_win), bias_s,
        batch=batch, seq=seq)

    y_prompt, y_sample = _ffn(x1, ln2, w_up_b, w_down_b, lnf,
                              o_s.reshape(dec_batch * N_HEADS, KV_WIDTH), selt, a_s, ga_s, gb_s, xs2d,
                              w_oa_b, w_ob_b, w_out_b)

    return (y_prompt.reshape(batch, seq, D_MODEL),
            y_sample.reshape(dec_batch, dec_seq, D_MODEL),
            from_feature_major(kwin), from_feature_major(vwin),
            from_feature_major(nk), from_feature_major(nv),
            vn.reshape(depth, dec_batch, dec_seq, D_MODEL))
```

```python
import functools
import math

import numpy as np
import jax
import jax.numpy as jnp
from jax import lax
from jax.experimental import pallas as pl
from jax.experimental.pallas import tpu as pltpu

D_MODEL = 1024
N_HEADS = 16
HEAD_DIM = 64
N_KV_HEADS = 4
GROUP = N_HEADS // N_KV_HEADS
KV_WIDTH = N_KV_HEADS * HEAD_DIM
WINDOW = 128
CHUNK = 128
SGU_GROUPS = 8
SGU_GROUP_DIM = D_MODEL // SGU_GROUPS
D_FF = 4 * D_MODEL
FF_SLAB = 1024
EPS = 1e-6
NEG_BIG = -1e30
ATTN_SCALE = HEAD_DIM ** -0.5
LOG2E = math.log2(math.e)

OFF_U, OFF_V, OFF_Q, OFF_K, OFF_VA, OFF_GA, OFF_GB, IN_WIDTH = 0, 1024, 2048, 3072, 3328, 3584, 4608, 5632
R_K, R_VA, R_GA, R_GB, R_END = (o - OFF_K for o in (OFF_K, OFF_VA, OFF_GA, OFF_GB, IN_WIDTH))
W_HALF = IN_WIDTH // 2

TOKEN_BLOCK = 512
FFN_BLOCK = 1024
SUBLANES = 8
VMEM_LIMIT_BYTES = 58 * 1024 * 1024

F32 = jnp.float32
BF16 = jnp.bfloat16


def _rmsnorm(x, g):
    ms = jnp.mean(x * x, axis=-1, keepdims=True)
    return x * lax.rsqrt(ms + EPS) * g


def _gelu_tanh(x):
    c = math.sqrt(2.0 / math.pi)
    return x * (0.5 * (1.0 + jnp.tanh(c * (x + 0.044715 * (x * x * x)))))


def _dot(a, b):
    return jnp.dot(a, b, preferred_element_type=F32)


def _dot_nt(a, b):
    return lax.dot_general(a, b, (((1,), (1,)), ((), ())), preferred_element_type=F32)


def _resident(shape):
    zeros = (0,) * len(shape)
    return pl.BlockSpec(shape, lambda *_: zeros, pipeline_mode=pl.Buffered(1))


def _whole(shape):
    zeros = (0,) * len(shape)
    return pl.BlockSpec(shape, lambda *_: zeros)


def _params():
    return pltpu.CompilerParams(dimension_semantics=("arbitrary",), vmem_limit_bytes=VMEM_LIMIT_BYTES)


def _mix_prompt_kernel(x_ref, ln1_ref, w_uv_ref, w_rest_ref, w_q_ref, sgu_g_ref, sgu_w_ref, sgu_b_ref,
                       sink_ref, slope_ref, w_oa_ref, w_ob_ref, w_out_ref, w_up_blk_ref, w_down_blk_ref,
                       s_qsel_ref, s_knew_ref, s_vnew_ref, s_ck_ref, s_cv_ref, s_bias_ref,
                       x1_ref, kwin_ref, vwin_ref, w_up_bf_ref, w_down_bf_ref, s_o_ref, s_nk_ref, s_nv_ref,
                       qs_scr, kt_scr, vm_scr, kprev_scr, vprev_scr, vn_scr, u_scr, gate_scr, a_scr, b_scr, wt_scr,
                       bias_ref, bexp_ref,
                       *, steps_per_seq):
    step = pl.program_id(0)
    tb = x_ref.shape[0]
    nblk = tb // CHUNK
    first = (step % steps_per_seq) == 0
    rd = step % 2
    wr = 1 - rd

    @pl.when(step == 0)
    def _():
        row = lax.broadcasted_iota(jnp.int32, (CHUNK, CHUNK), 0)
        col = lax.broadcasted_iota(jnp.int32, (CHUNK, CHUNK), 1)
        b_t = sgu_b_ref[...].T
        for g in range(SGU_GROUPS):
            wt_scr[g] = jnp.where(row >= col, sgu_w_ref[g], 0.0).astype(BF16)
            bexp_ref[:, g * SGU_GROUP_DIM:(g + 1) * SGU_GROUP_DIM] = jnp.broadcast_to(
                b_t[:, g:g + 1], (CHUNK, SGU_GROUP_DIM))
        dist = (lax.broadcasted_iota(jnp.int32, (CHUNK, 2 * CHUNK), 0) + CHUNK
                - lax.broadcasted_iota(jnp.int32, (CHUNK, 2 * CHUNK), 1))
        in_band = jnp.logical_and(dist >= 0, dist <= WINDOW)
        dist_f = dist.astype(F32)
        for h in range(N_HEADS):
            bias_ref[h] = jnp.where(in_band, -slope_ref[h] * dist_f, NEG_BIG)
        kt_scr[...] = jnp.zeros(kt_scr.shape, BF16)
        vm_scr[...] = jnp.zeros(vm_scr.shape, BF16)
        kprev_scr[...] = jnp.zeros(kprev_scr.shape, BF16)
        vprev_scr[...] = jnp.zeros(vprev_scr.shape, BF16)

    @pl.when(first)
    def _():
        kprev_scr[rd] = jnp.zeros(kprev_scr.shape[1:], BF16)
        vprev_scr[rd] = jnp.zeros(vprev_scr.shape[1:], BF16)

    w_up_bf_ref[...] = w_up_blk_ref[...].astype(BF16)
    w_down_bf_ref[...] = w_down_blk_ref[...].astype(BF16)

    x = x_ref[...]
    quarters = [slice(i * tb // 4, (i + 1) * tb // 4) for i in range(4)]
    xn_parts = [_rmsnorm(x[p], ln1_ref[...]).astype(BF16) for p in quarters]
    xn = jnp.concatenate(xn_parts, axis=0)

    q = jnp.concatenate([_dot(xn_p, w_q_ref[...]) for xn_p in xn_parts], axis=0)
    k = _dot(xn, w_rest_ref[:, R_K:R_VA])
    va = _dot(xn, w_rest_ref[:, R_VA:R_GA])
    h_v = _dot(xn, w_uv_ref[:, OFF_V:OFF_Q])
    h_u0 = _dot(xn, w_uv_ref[:, OFF_U:OFF_U + D_MODEL // 2])

    q = (q * (ATTN_SCALE * LOG2E)).astype(BF16)
    for c in range(nblk):
        for g in range(GROUP):
            qs_scr[c, g * CHUNK:(g + 1) * CHUNK, :] = q[c * CHUNK:(c + 1) * CHUNK, g * KV_WIDTH:(g + 1) * KV_WIDTH]

    kt_f32 = k.T
    kwin_ref[...] = kt_f32[:, tb - WINDOW:]
    vwin_ref[...] = va[tb - WINDOW:, :].T
    kt = kt_f32.astype(BF16)
    vab = va.astype(BF16)
    for kvh in range(N_KV_HEADS):
        own = slice(kvh * HEAD_DIM, (kvh + 1) * HEAD_DIM)
        for c in range(nblk):
            kt_scr[kvh, c, own, :] = kt[own, c * CHUNK:(c + 1) * CHUNK]
        vm_scr[kvh, :, own] = vab[:, own]
        kprev_scr[wr, kvh, own, :] = kt[own, tb - WINDOW:]
        vprev_scr[wr, kvh, :, own] = vab[tb - WINDOW:, own]

    no_prev = jnp.where(
        jnp.logical_and(first, lax.broadcasted_iota(jnp.int32, (CHUNK, 2 * CHUNK), 1) < CHUNK), NEG_BIG, 0.0)

    def attn_scores(c):
        qs = qs_scr[c]
        out = []
        for kvh in range(N_KV_HEADS):
            k_prev = kprev_scr[rd, kvh] if c == 0 else kt_scr[kvh, c - 1]
            out.append(_dot(qs, jnp.concatenate([k_prev, kt_scr[kvh, c]], axis=1)))
        return out

    def attn_softmax(c, scores):
        out = []
        for kvh in range(N_KV_HEADS):
            ps = []
            for g in range(GROUP):
                h = g * N_KV_HEADS + kvh
                s = scores[kvh][g * CHUNK:(g + 1) * CHUNK, :] + bias_ref[h]
                if c == 0:
                    s = s + no_prev
                sink = sink_ref[0, kvh * GROUP + g] * LOG2E
                m = jnp.max(s, axis=1, keepdims=True)
                p = jnp.exp2(s - m)
                denom = jnp.sum(p, axis=1, keepdims=True) + jnp.exp2(sink - m)
                ps.append((p * (1.0 / denom)).astype(BF16))
            out.append(jnp.concatenate(ps, axis=0))
        return out

    def attn_values(c, probs):
        rows = slice(c * CHUNK, (c + 1) * CHUNK)
        acc = None
        for kvh in range(N_KV_HEADS):
            if c == 0:
                v_band = jnp.concatenate([vprev_scr[rd, kvh], vm_scr[kvh, 0:CHUNK, :]], axis=0)
            else:
                v_band = vm_scr[kvh, (c - 1) * CHUNK:(c + 1) * CHUNK, :]
            o = _dot(probs[kvh], v_band)
            acc = o if acc is None else acc + o
        for g in range(GROUP):
            b_scr[rows, g * KV_WIDTH:(g + 1) * KV_WIDTH] = acc[g * CHUNK:(g + 1) * CHUNK, :].astype(BF16)

    def sgu_chunk(c):
        rows = slice(c * CHUNK, (c + 1) * CHUNK)
        vn_c = vn_scr[rows, :]
        mixed = jnp.concatenate(
            [_dot(wt_scr[g], vn_c[:, g * SGU_GROUP_DIM:(g + 1) * SGU_GROUP_DIM]) for g in range(SGU_GROUPS)],
            axis=1) + bexp_ref[...]
        a_scr[rows, :] = (u_scr[rows, :] * mixed).astype(BF16)

    half = D_MODEL // 2

    def tail_u(h, lo):
        u_scr[:, lo:lo + half] = _gelu_tanh(h)

    def tail_ga(h, lo):
        gate_scr[0, :, lo:lo + half] = jax.nn.sigmoid(h)

    def tail_gb(h, lo):
        gate_scr[1, :, lo:lo + half] = jax.nn.sigmoid(h)

    fillers = [
        (lambda: _dot(xn, w_uv_ref[:, OFF_U + half:OFF_V]), lambda h: tail_u(h, half)),
        (lambda: _dot(xn, w_rest_ref[:, R_GA:R_GA + half]), lambda h: tail_ga(h, 0)),
        (lambda: _dot(xn, w_rest_ref[:, R_GA + half:R_GB]), lambda h: tail_ga(h, half)),
        (lambda: _dot(xn, w_rest_ref[:, R_GB:R_GB + half]), lambda h: tail_gb(h, 0)),
        (lambda: _dot(xn, w_rest_ref[:, R_GB + half:R_END]), lambda h: tail_gb(h, half)),
    ]
    vn_scr[...] = _rmsnorm(_gelu_tanh(h_v), sgu_g_ref[...]).astype(BF16)
    tail_u(h_u0, 0)
    for c in range(nblk):
        scores = attn_scores(c)
        proj = fillers[c][0]() if c < len(fillers) else None
        probs = attn_softmax(c, scores)
        attn_values(c, probs)
        if proj is not None:
            fillers[c][1](proj)
    for matmul, tail in fillers[nblk:]:
        tail(matmul())
    for c in range(nblk):
        sgu_chunk(c)

    sample = _sample_attn_stages(s_qsel_ref, s_knew_ref, s_vnew_ref, s_ck_ref, s_cv_ref, s_bias_ref, sink_ref,
                                 step % 2 == 1, s_o_ref, s_nk_ref, s_nv_ref)
    next(sample)
    branch_a = _dot(a_scr[...], w_oa_ref[...])
    next(sample)
    branch_b = _dot(b_scr[...], w_ob_ref[...])
    for _ in sample:
        pass
    hm = gate_scr[0] * branch_a + gate_scr[1] * branch_b
    x1_ref[...] = x + _dot(hm.astype(BF16), w_out_ref[...])


def _mix_prompt(x2d, ln1, w_uv, w_rest, w_q, sgu_g, sgu_w, sgu_b, sinks, slopes, w_oa, w_ob, w_out,
                w_up_f32, w_down_f32,
                s_qsel, s_knew, s_vnew, s_cache_k, s_cache_v, s_bias, *, batch, seq):
    n = x2d.shape[0]
    tb = TOKEN_BLOCK
    nblk = tb // CHUNK
    steps = n // tb
    steps_per_seq = seq // tb
    nb = s_cache_k.shape[0]
    per_step = nb // steps
    assert per_step * steps == nb and 2 * per_step == SUBLANES and s_knew.shape == (nb, KV_WIDTH)
    s_head_block = pl.BlockSpec((None, per_step * N_HEADS, KV_WIDTH), lambda i: (i, 0, 0))
    s_new_block = pl.BlockSpec((SUBLANES, KV_WIDTH), lambda i: (i // 2, 0))
    s_cache_block = pl.BlockSpec((per_step, KV_WIDTH, WINDOW), lambda i: (i, 0, 0))
    row_block = pl.BlockSpec((tb, D_MODEL), lambda i: (i, 0))
    win_block = pl.BlockSpec((None, KV_WIDTH, WINDOW), lambda i: (i // steps_per_seq, 0, 0))
    up_block = pl.BlockSpec((D_MODEL // steps, D_FF), lambda i: (i, 0))
    down_block = pl.BlockSpec((D_FF // steps, D_MODEL), lambda i: (i, 0))
    return pl.pallas_call(
        functools.partial(_mix_prompt_kernel, steps_per_seq=steps_per_seq),
        grid=(n // tb,),
        in_specs=[
            row_block,
            _resident((1, D_MODEL)),
            _resident((D_MODEL, OFF_Q)),
            _resident((D_MODEL, R_END)),
            _resident((D_MODEL, D_MODEL)),
            _resident((1, D_MODEL)),
            _resident((SGU_GROUPS, CHUNK, CHUNK)),
            _resident((SGU_GROUPS, CHUNK)),
            pl.BlockSpec(memory_space=pltpu.SMEM),
            pl.BlockSpec(memory_space=pltpu.SMEM),
            _resident((D_MODEL, D_MODEL)),
            _resident((D_MODEL, D_MODEL)),
            _resident((D_MODEL, D_MODEL)),
            up_block,
            down_block,
            s_head_block, s_new_block, s_new_block, s_cache_block, s_cache_block,
            _resident((N_HEADS, WINDOW)),
        ],
        out_specs=[row_block, win_block, win_block, up_block, down_block,
                   s_head_block, s_cache_block, s_cache_block],
        out_shape=[
            jax.ShapeDtypeStruct((n, D_MODEL), F32),
            jax.ShapeDtypeStruct((batch, KV_WIDTH, WINDOW), F32),
            jax.ShapeDtypeStruct((batch, KV_WIDTH, WINDOW), F32),
            jax.ShapeDtypeStruct((D_MODEL, D_FF), BF16),
            jax.ShapeDtypeStruct((D_FF, D_MODEL), BF16),
            jax.ShapeDtypeStruct((steps, per_step * N_HEADS, KV_WIDTH), BF16),
            jax.ShapeDtypeStruct((nb, KV_WIDTH, WINDOW), F32),
            jax.ShapeDtypeStruct((nb, KV_WIDTH, WINDOW), F32),
        ],
        scratch_shapes=[
            pltpu.VMEM((nblk, GROUP * CHUNK, KV_WIDTH), BF16),
            pltpu.VMEM((N_KV_HEADS, nblk, KV_WIDTH, CHUNK), BF16),
            pltpu.VMEM((N_KV_HEADS, tb, KV_WIDTH), BF16),
            pltpu.VMEM((2, N_KV_HEADS, KV_WIDTH, CHUNK), BF16),
            pltpu.VMEM((2, N_KV_HEADS, WINDOW, KV_WIDTH), BF16),
            pltpu.VMEM((tb, D_MODEL), BF16),
            pltpu.VMEM((tb, D_MODEL), F32),
            pltpu.VMEM((2, tb, D_MODEL), F32),
            pltpu.VMEM((tb, D_MODEL), BF16),
            pltpu.VMEM((tb, D_MODEL), BF16),
            pltpu.VMEM((SGU_GROUPS, CHUNK, CHUNK), BF16),
            pltpu.VMEM((N_HEADS, CHUNK, 2 * CHUNK), F32),
            pltpu.VMEM((CHUNK, D_MODEL), F32),
        ],
        compiler_params=_params(),
        name="mix_prompt",
    )(x2d, ln1, w_uv, w_rest, w_q, sgu_g, sgu_w, sgu_b, sinks, slopes, w_oa, w_ob, w_out, w_up_f32, w_down_f32,
      s_qsel, s_knew, s_vnew, s_cache_k, s_cache_v, s_bias)


def _ffn_rows(x, y_ref, ln2_ref, w_up_ref, w_down_ref, lnf_ref, *, row_parts):
    n_slabs = D_FF // FF_SLAB
    m = x.shape[0]
    parts = [slice(i * m // row_parts, (i + 1) * m // row_parts) for i in range(row_parts)]

    def up(j, xn):
        return _dot(xn, w_up_ref[:, j * FF_SLAB:(j + 1) * FF_SLAB])

    xns = [_rmsnorm(x[p], ln2_ref[...]).astype(BF16) for p in parts]
    h_next = jnp.concatenate([up(0, xn_p) for xn_p in xns], axis=0) if row_parts > 1 else up(0, xns[0])
    xn = jnp.concatenate(xns, axis=0) if row_parts > 1 else xns[0]

    acc = x
    for j in range(n_slabs):
        h = h_next
        if j + 1 < n_slabs:
            h_next = up(j + 1, xn)
        h = jnp.square(jnp.maximum(h, 0.0)).astype(BF16)
        w_d = w_down_ref[j * FF_SLAB:(j + 1) * FF_SLAB, :]
        if j + 1 < n_slabs:
            acc = acc + _dot(h, w_d)
        else:
            for p in parts:
                y_ref[p, :] = _rmsnorm(acc[p] + _dot(h[p], w_d), lnf_ref[...])


def _ffn_kernel(x_ref, ln2_ref, w_up_ref, w_down_ref, lnf_ref,
                s_o_ref, s_selt_ref, s_a_ref, s_ga_ref, s_gb_ref, s_x_ref, w_oa_ref, w_ob_ref, w_out_ref,
                y_ref, ys_ref):
    i = pl.program_id(0)
    last = pl.num_programs(0) - 1

    @pl.when(i < last)
    def _():
        _ffn_rows(x_ref[...], y_ref, ln2_ref, w_up_ref, w_down_ref, lnf_ref, row_parts=4)

    @pl.when(i == last)
    def _():
        nb = s_x_ref.shape[0]
        bst = _dot(s_selt_ref[...], s_o_ref[...]).astype(BF16)
        ob = _dot(bst[0:nb, :], w_ob_ref[0:KV_WIDTH, :])
        for g in range(1, GROUP):
            ob = ob + _dot(bst[g * nb:(g + 1) * nb, :], w_ob_ref[g * KV_WIDTH:(g + 1) * KV_WIDTH, :])
        hm = s_ga_ref[...] * _dot(s_a_ref[...], w_oa_ref[...]) + s_gb_ref[...] * ob
        xs1 = s_x_ref[...] + _dot(hm.astype(BF16), w_out_ref[...])
        _ffn_rows(xs1, ys_ref, ln2_ref, w_up_ref, w_down_ref, lnf_ref, row_parts=1)


def _ffn(x2d, ln2, w_up, w_down, lnf, s_o, s_selt, s_a, s_ga, s_gb, xs2d, w_oa, w_ob, w_out):
    n = x2d.shape[0]
    nb = xs2d.shape[0]
    n_prompt_steps = n // FFN_BLOCK
    row_block = pl.BlockSpec((FFN_BLOCK, D_MODEL), lambda i: (jnp.minimum(i, n_prompt_steps - 1), 0))
    return pl.pallas_call(
        _ffn_kernel,
        grid=(n_prompt_steps + 1,),
        in_specs=[row_block, _resident((1, D_MODEL)), _resident((D_MODEL, D_FF)),
                  _resident((D_FF, D_MODEL)), _resident((1, D_MODEL)),
                  _resident((nb * N_HEADS, KV_WIDTH)), _resident((GROUP * nb, nb * N_HEADS)),
                  _resident((nb, D_MODEL)), _resident((nb, D_MODEL)), _resident((nb, D_MODEL)),
                  _resident((nb, D_MODEL)),
                  _resident((D_MODEL, D_MODEL)), _resident((D_MODEL, D_MODEL)), _resident((D_MODEL, D_MODEL))],
        out_specs=[row_block, _whole((nb, D_MODEL))],
        out_shape=[jax.ShapeDtypeStruct((n, D_MODEL), F32), jax.ShapeDtypeStruct((nb, D_MODEL), F32)],
        compiler_params=_params(),
        name="ffn",
    )(x2d, ln2, w_up, w_down, lnf, s_o, s_selt, s_a, s_ga, s_gb, xs2d, w_oa, w_ob, w_out)


def _sample_proj_kernel(x_ref, ln1_ref, w_blk_ref, sgu_g_ref, sgu_w_ref, sgu_b_ref, sel_ref,
                        w_uv_bf_ref, w_rest_bf_ref, w_q_bf_ref, qsel_ref, knew_ref, vnew_ref, vn_ref, a_ref,
                        ga_ref, gb_ref, xn_scr, h_scr):
    j = pl.program_id(0)
    nb = x_ref.shape[0]

    @pl.when(j == 0)
    def _():
        xn_scr[...] = _rmsnorm(x_ref[...], ln1_ref[...]).astype(BF16)

    wb = w_blk_ref[...].astype(BF16)
    h_scr[j] = _dot(xn_scr[...], wb)

    def reorder_q(half):
        for head in range(N_HEADS):
            src = OFF_Q + head * HEAD_DIM - half * W_HALF
            if 0 <= src < W_HALF:
                kvh, g = divmod(head, GROUP)
                dst = g * KV_WIDTH + kvh * HEAD_DIM
                w_q_bf_ref[:, dst:dst + HEAD_DIM] = wb[:, src:src + HEAD_DIM]

    assert OFF_Q <= W_HALF <= OFF_K

    @pl.when(j == 0)
    def _():
        w_uv_bf_ref[...] = wb[:, OFF_U:OFF_Q]
        reorder_q(0)

    @pl.when(j == 1)
    def _():
        w_rest_bf_ref[...] = wb[:, OFF_K - W_HALF:]
        reorder_q(1)

        def cols(lo, hi):
            half = lo // W_HALF
            assert half == (hi - 1) // W_HALF
            return h_scr[half, :, lo - half * W_HALF:hi - half * W_HALF]

        u = _gelu_tanh(cols(OFF_U, OFF_V))
        v = _gelu_tanh(cols(OFF_V, OFF_Q))
        vn = _rmsnorm(v, sgu_g_ref[...])
        vn_ref[...] = vn
        def over_groups(entry):
            return jnp.concatenate(
                [jnp.broadcast_to(entry(g), (1, SGU_GROUP_DIM)) for g in range(SGU_GROUPS)], axis=1)

        w_diag = over_groups(lambda g: sgu_w_ref[g, 0:1, 0:1])
        b_first = over_groups(lambda g: sgu_b_ref[g:g + 1, 0:1])
        a_ref[...] = (u * (vn * w_diag + b_first)).astype(BF16)
        knew_ref[...] = cols(OFF_K, OFF_VA)
        vnew_ref[...] = cols(OFF_VA, OFF_GA)
        ga_ref[...] = jax.nn.sigmoid(cols(OFF_GA, OFF_GB))
        gb_ref[...] = jax.nn.sigmoid(cols(OFF_GB, IN_WIDTH))
        q = _dot(xn_scr[...], w_q_bf_ref[...]) * ATTN_SCALE
        qstack = jnp.concatenate([q[:, g * KV_WIDTH:(g + 1) * KV_WIDTH] for g in range(GROUP)], axis=0).astype(BF16)
        qrep = _dot(sel_ref[...], qstack)
        row_kvh = lax.broadcasted_iota(jnp.int32, (nb * N_HEADS, KV_WIDTH), 0) % N_KV_HEADS
        lane_kvh = lax.broadcasted_iota(jnp.int32, (nb * N_HEADS, KV_WIDTH), 1) // HEAD_DIM
        qsel_ref[...] = jnp.where(row_kvh == lane_kvh, qrep, 0.0).astype(BF16)


def _sample_proj(xs2d, ln1, w_in_f32, sgu_g, sgu_w, sgu_b, sel):
    nb = xs2d.shape[0]
    n_blocks = IN_WIDTH // W_HALF
    w_block = pl.BlockSpec((D_MODEL, W_HALF), lambda j: (0, j))
    return pl.pallas_call(
        _sample_proj_kernel,
        grid=(n_blocks,),
        in_specs=[_whole((nb, D_MODEL)), _whole((1, D_MODEL)), w_block,
                  _whole((1, D_MODEL)),
                  pl.BlockSpec((SGU_GROUPS, SUBLANES, CHUNK), lambda j: (0, 0, 0)),
                  _whole((SGU_GROUPS, CHUNK)),
                  _resident((nb * N_HEADS, GROUP * nb))],
        out_specs=[_whole((D_MODEL, OFF_Q)), _whole((D_MODEL, IN_WIDTH - OFF_K)), _whole((D_MODEL, D_MODEL)),
                   _whole((nb * N_HEADS, KV_WIDTH)), _whole((nb, KV_WIDTH)), _whole((nb, KV_WIDTH)),
                   _whole((nb, D_MODEL)), _whole((nb, D_MODEL)),
                   _whole((nb, D_MODEL)), _whole((nb, D_MODEL))],
        out_shape=[
            jax.ShapeDtypeStruct((D_MODEL, OFF_Q), BF16),
            jax.ShapeDtypeStruct((D_MODEL, IN_WIDTH - OFF_K), BF16),
            jax.ShapeDtypeStruct((D_MODEL, D_MODEL), BF16),
            jax.ShapeDtypeStruct((nb * N_HEADS, KV_WIDTH), BF16),
            jax.ShapeDtypeStruct((nb, KV_WIDTH), F32),
            jax.ShapeDtypeStruct((nb, KV_WIDTH), F32),
            jax.ShapeDtypeStruct((nb, D_MODEL), F32),
            jax.ShapeDtypeStruct((nb, D_MODEL), BF16),
            jax.ShapeDtypeStruct((nb, D_MODEL), F32),
            jax.ShapeDtypeStruct((nb, D_MODEL), F32),
        ],
        scratch_shapes=[pltpu.VMEM((nb, D_MODEL), BF16), pltpu.VMEM((n_blocks, nb, W_HALF), F32)],
        compiler_params=_params(),
        name="sample_proj",
    )(xs2d, ln1, w_in_f32, sgu_g, sgu_w, sgu_b, sel)


def _sample_attn_stages(qsel_ref, knew_ref, vnew_ref, ck_ref, cv_ref, bias_ref, sink_ref, upper,
                        o_ref, nk_ref, nv_ref):
    bs = ck_ref.shape[0]
    assert knew_ref.shape[0] == 2 * bs
    row0 = jnp.where(upper, bs, 0)
    row_kvh = lax.broadcasted_iota(jnp.int32, (N_HEADS, KV_WIDTH), 0) % N_KV_HEADS
    lane_kvh = lax.broadcasted_iota(jnp.int32, (N_HEADS, KV_WIDTH), 1) // HEAD_DIM
    own = row_kvh == lane_kvh
    bias = bias_ref[...]
    head_row = lax.broadcasted_iota(jnp.int32, (N_HEADS, 1), 0)
    sink = jnp.zeros((N_HEADS, 1), F32)
    for h in range(N_HEADS):
        g, kvh = divmod(h, N_KV_HEADS)
        sink = jnp.where(head_row == h, sink_ref[0, kvh * GROUP + g], sink)

    qss = [qsel_ref[i * N_HEADS:(i + 1) * N_HEADS, :] for i in range(bs)]
    kns = [knew_ref[pl.ds(row0 + i, 1), :] for i in range(bs)]
    vws = [vnew_ref[pl.ds(row0 + i, 1), :] for i in range(bs)]
    scores = [_dot(qss[i], ck_ref[i].astype(BF16)) + bias for i in range(bs)]
    yield
    probs = []
    for i in range(bs):
        s = scores[i]
        s_new = jnp.sum(qss[i].astype(F32) * kns[i], axis=1, keepdims=True)
        m = jnp.maximum(jnp.maximum(jnp.max(s, axis=1, keepdims=True), s_new), sink)
        p = jnp.exp(s - m)
        p_new = jnp.exp(s_new - m)
        denom = jnp.sum(p, axis=1, keepdims=True) + p_new + jnp.exp(sink - m)
        probs.append((p.astype(BF16), p_new, denom))

    kn_t = knew_ref[...].T
    vw_t = vnew_ref[...].T
    last_lane = lax.broadcasted_iota(jnp.int32, (KV_WIDTH, WINDOW), 1) == WINDOW - 1
    for i in range(bs):
        kn_col = jnp.where(upper, kn_t[:, bs + i:bs + i + 1], kn_t[:, i:i + 1])
        vw_col = jnp.where(upper, vw_t[:, bs + i:bs + i + 1], vw_t[:, i:i + 1])
        nk_ref[i] = jnp.where(last_lane, kn_col, pltpu.roll(ck_ref[i], WINDOW - 1, 1))
        nv_ref[i] = jnp.where(last_lane, vw_col, pltpu.roll(cv_ref[i], WINDOW - 1, 1))
    yield
    for i in range(bs):
        p, p_new, denom = probs[i]
        o = (_dot_nt(p, cv_ref[i].astype(BF16)) + p_new * vws[i]) / denom
        o_ref[i * N_HEADS:(i + 1) * N_HEADS, :] = jnp.where(own, o, 0.0).astype(BF16)


def _head_perm(v):
    return v.reshape(N_KV_HEADS, GROUP).T.reshape(N_HEADS)


def _alibi_slopes():
    h = np.arange(1, N_HEADS + 1, dtype=np.float32)
    return np.exp2(-8.0 * h / N_HEADS).astype(np.float32)


def _selection_matrix(nb):
    r = np.arange(nb * N_HEADS)
    c = np.arange(GROUP * nb)
    same_sample = (r[:, None] // N_HEADS) == (c[None, :] % nb)
    same_member = ((r[:, None] % N_HEADS) // N_KV_HEADS) == (c[None, :] // nb)
    return (same_sample & same_member).astype(np.float32)


def kernel(x_prompt, x_sample, cache_k_win, cache_v_win, ln1_g, w_in, sgu_norm_g, sgu_w, sgu_b, attn_sinks,
           w_oa, w_ob, w_out, ln2_g, w_up, w_down, lnf_g):
    batch, seq, _ = x_prompt.shape
    dec_batch, dec_seq, _ = x_sample.shape
    depth = w_in.shape[0]
    assert depth == 1 and dec_seq == 1
    assert seq % TOKEN_BLOCK == 0 and TOKEN_BLOCK % CHUNK == 0
    assert (batch * seq) % FFN_BLOCK == 0
    assert w_in.shape[-1] == IN_WIDTH

    wi = w_in[0]
    w_ob_b = w_ob[0].reshape(N_KV_HEADS, GROUP, HEAD_DIM, D_MODEL).transpose(1, 0, 2, 3).reshape(
        D_MODEL, D_MODEL).astype(BF16)
    w_oa_b = w_oa[0].astype(BF16)
    w_out_b = w_out[0].astype(BF16)
    ln1 = ln1_g[0].reshape(1, D_MODEL)
    ln2 = ln2_g[0].reshape(1, D_MODEL)
    lnf = lnf_g.reshape(1, D_MODEL)
    sgu_g = sgu_norm_g[0].reshape(1, D_MODEL)
    slopes_p = _head_perm(_alibi_slopes())

    xs2d = x_sample.reshape(dec_batch, D_MODEL)
    sel_np = _selection_matrix(dec_batch)
    sel = jnp.asarray(sel_np, BF16)
    selt = jnp.asarray(sel_np.T, BF16)
    bias_s = -slopes_p[:, None] * (WINDOW - np.arange(WINDOW, dtype=np.float32))[None, :]

    w_uv_b, w_rest_b, w_q_b, qsel, knew, vnew, vn, a_s, ga_s, gb_s = _sample_proj(
        xs2d, ln1, wi, sgu_g, sgu_w[0], sgu_b[0], sel)
    def to_feature_major(c):
        return c[0].transpose(0, 2, 3, 1).reshape(dec_batch, KV_WIDTH, WINDOW)

    def from_feature_major(c):
        return c.reshape(c.shape[0], N_KV_HEADS, HEAD_DIM, WINDOW).transpose(0, 3, 1, 2)[None]

    steps = (batch * seq) // TOKEN_BLOCK
    per_step = dec_batch // steps

    x1, kwin, vwin, w_up_b, w_down_b, o_s, nk, nv = _mix_prompt(
        x_prompt.reshape(batch * seq, D_MODEL), ln1, w_uv_b, w_rest_b, w_q_b, sgu_g, sgu_w[0], sgu_b[0],
        attn_sinks, slopes_p * LOG2E, w_oa_b, w_ob_b, w_out_b, w_up[0], w_down[0],
        qsel.reshape(steps, per_step * N_HEADS, KV_WIDTH), knew, vnew,
        to_feature_major(cache_k_win), to_feature_major(cache_v_win), bias_s,
        batch=batch, seq=seq)

    y_prompt, y_sample = _ffn(x1, ln2, w_up_b, w_down_b, lnf,
                              o_s.reshape(dec_batch * N_HEADS, KV_WIDTH), selt, a_s, ga_s, gb_s, xs2d,
                              w_oa_b, w_ob_b, w_out_b)

    return (y_prompt.reshape(batch, seq, D_MODEL),
            y_sample.reshape(dec_batch, dec_seq, D_MODEL),
            from_feature_major(kwin), from_feature_major(vwin),
            from_feature_major(nk), from_feature_major(nv),
            vn.reshape(depth, dec_batch, dec_seq, D_MODEL))
```

```python
import functools
import math

import numpy as np
import jax
import jax.numpy as jnp
from jax import lax
from jax.experimental import pallas as pl
from jax.experimental.pallas import tpu as pltpu

D_MODEL = 1024
N_HEADS = 16
HEAD_DIM = 64
N_KV_HEADS = 4
GROUP = N_HEADS // N_KV_HEADS
KV_WIDTH = N_KV_HEADS * HEAD_DIM
WINDOW = 128
CHUNK = 128
SGU_GROUPS = 8
SGU_GROUP_DIM = D_MODEL // SGU_GROUPS
D_FF = 4 * D_MODEL
FF_SLAB = 1024
EPS = 1e-6
NEG_BIG = -1e30
ATTN_SCALE = HEAD_DIM ** -0.5
LOG2E = math.log2(math.e)

OFF_U, OFF_V, OFF_Q, OFF_K, OFF_VA, OFF_GA, OFF_GB, IN_WIDTH = 0, 1024, 2048, 3072, 3328, 3584, 4608, 5632
R_K, R_VA, R_GA, R_GB, R_END = (o - OFF_K for o in (OFF_K, OFF_VA, OFF_GA, OFF_GB, IN_WIDTH))
W_HALF = IN_WIDTH // 2

TOKEN_BLOCK = 512
FFN_BLOCK = 1024
SUBLANES = 8
VMEM_LIMIT_BYTES = 58 * 1024 * 1024

F32 = jnp.float32
BF16 = jnp.bfloat16


def _rmsnorm(x, g):
    ms = jnp.mean(x * x, axis=-1, keepdims=True)
    return x * lax.rsqrt(ms + EPS) * g


def _gelu_tanh(x):
    c = math.sqrt(2.0 / math.pi)
    return x * (0.5 * (1.0 + jnp.tanh(c * (x + 0.044715 * (x * x * x)))))


def _dot(a, b):
    return jnp.dot(a, b, preferred_element_type=F32)


def _dot_nt(a, b):
    return lax.dot_general(a, b, (((1,), (1,)), ((), ())), preferred_element_type=F32)


def _resident(shape):
    zeros = (0,) * len(shape)
    return pl.BlockSpec(shape, lambda *_: zeros, pipeline_mode=pl.Buffered(1))


def _whole(shape):
    zeros = (0,) * len(shape)
    return pl.BlockSpec(shape, lambda *_: zeros)


def _params():
    return pltpu.CompilerParams(dimension_semantics=("arbitrary",), vmem_limit_bytes=VMEM_LIMIT_BYTES)


def _mix_prompt_kernel(x_ref, ln1_ref, w_uv_ref, w_rest_ref, w_q_ref, sgu_g_ref, sgu_w_ref, sgu_b_ref,
                       sink_ref, slope_ref, w_oa_ref, w_ob_ref, w_out_ref, w_up_blk_ref, w_down_blk_ref,
                       s_qsel_ref, s_knew_ref, s_vnew_ref, s_ck_ref, s_cv_ref, s_bias_ref,
                       x1_ref, kwin_ref, vwin_ref, w_up_bf_ref, w_down_bf_ref, s_o_ref, s_nk_ref, s_nv_ref,
                       qs_scr, kt_scr, vm_scr, kprev_scr, vprev_scr, vn_scr, u_scr, gate_scr, a_scr, b_scr, wt_scr,
                       bias_ref, bexp_ref,
                       *, steps_per_seq):
    step = pl.program_id(0)
    tb = x_ref.shape[0]
    nblk = tb // CHUNK
    first = (step % steps_per_seq) == 0
    rd = step % 2
    wr = 1 - rd

    @pl.when(step == 0)
    def _():
        row = lax.broadcasted_iota(jnp.int32, (CHUNK, CHUNK), 0)
        col = lax.broadcasted_iota(jnp.int32, (CHUNK, CHUNK), 1)
        b_t = sgu_b_ref[...].T
        for g in range(SGU_GROUPS):
            wt_scr[g] = jnp.where(row >= col, sgu_w_ref[g], 0.0).astype(BF16)
            bexp_ref[:, g * SGU_GROUP_DIM:(g + 1) * SGU_GROUP_DIM] = jnp.broadcast_to(
                b_t[:, g:g + 1], (CHUNK, SGU_GROUP_DIM))
        dist = (lax.broadcasted_iota(jnp.int32, (CHUNK, 2 * CHUNK), 0) + CHUNK
                - lax.broadcasted_iota(jnp.int32, (CHUNK, 2 * CHUNK), 1))
        in_band = jnp.logical_and(dist >= 0, dist <= WINDOW)
        dist_f = dist.astype(F32)
        for h in range(N_HEADS):
            bias_ref[h] = jnp.where(in_band, -slope_ref[h] * dist_f, NEG_BIG)
        kt_scr[...] = jnp.zeros(kt_scr.shape, BF16)
        vm_scr[...] = jnp.zeros(vm_scr.shape, BF16)
        kprev_scr[...] = jnp.zeros(kprev_scr.shape, BF16)
        vprev_scr[...] = jnp.zeros(vprev_scr.shape, BF16)

    @pl.when(first)
    def _():
        kprev_scr[rd] = jnp.zeros(kprev_scr.shape[1:], BF16)
        vprev_scr[rd] = jnp.zeros(vprev_scr.shape[1:], BF16)

    w_up_bf_ref[...] = w_up_blk_ref[...].astype(BF16)
    w_down_bf_ref[...] = w_down_blk_ref[...].astype(BF16)

    x = x_ref[...]
    quarters = [slice(i * tb // 4, (i + 1) * tb // 4) for i in range(4)]
    xn_parts = [_rmsnorm(x[p], ln1_ref[...]).astype(BF16) for p in quarters]
    xn = jnp.concatenate(xn_parts, axis=0)

    q = jnp.concatenate([_dot(xn_p, w_q_ref[...]) for xn_p in xn_parts], axis=0)
    k = _dot(xn, w_rest_ref[:, R_K:R_VA])
    va = _dot(xn, w_rest_ref[:, R_VA:R_GA])
    h_v = _dot(xn, w_uv_ref[:, OFF_V:OFF_Q])
    h_u0 = _dot(xn, w_uv_ref[:, OFF_U:OFF_U + D_MODEL // 2])

    q = (q * (ATTN_SCALE * LOG2E)).astype(BF16)
    for c in range(nblk):
        for g in range(GROUP):
            qs_scr[c, g * CHUNK:(g + 1) * CHUNK, :] = q[c * CHUNK:(c + 1) * CHUNK, g * KV_WIDTH:(g + 1) * KV_WIDTH]

    kt_f32 = k.T
    kwin_ref[...] = kt_f32[:, tb - WINDOW:]
    vwin_ref[...] = va[tb - WINDOW:, :].T
    kt = kt_f32.astype(BF16)
    vab = va.astype(BF16)
    for kvh in range(N_KV_HEADS):
        own = slice(kvh * HEAD_DIM, (kvh + 1) * HEAD_DIM)
        for c in range(nblk):
            kt_scr[kvh, c, own, :] = kt[own, c * CHUNK:(c + 1) * CHUNK]
        vm_scr[kvh, :, own] = vab[:, own]
        kprev_scr[wr, kvh, own, :] = kt[own, tb - WINDOW:]
        vprev_scr[wr, kvh, :, own] = vab[tb - WINDOW:, own]

    no_prev = jnp.where(
        jnp.logical_and(first, lax.broadcasted_iota(jnp.int32, (CHUNK, 2 * CHUNK), 1) < CHUNK), NEG_BIG, 0.0)

    def attn_scores(c):
        qs = qs_scr[c]
        out = []
        for kvh in range(N_KV_HEADS):
            k_prev = kprev_scr[rd, kvh] if c == 0 else kt_scr[kvh, c - 1]
            out.append(_dot(qs, jnp.concatenate([k_prev, kt_scr[kvh, c]], axis=1)))
        return out

    def attn_softmax(c, scores):
        out = []
        for kvh in range(N_KV_HEADS):
            ps = []
            for g in range(GROUP):
                h = g * N_KV_HEADS + kvh
                s = scores[kvh][g * CHUNK:(g + 1) * CHUNK, :] + bias_ref[h]
                if c == 0:
                    s = s + no_prev
                sink = sink_ref[0, kvh * GROUP + g] * LOG2E
                m = jnp.max(s, axis=1, keepdims=True)
                p = jnp.exp2(s - m)
                denom = jnp.sum(p, axis=1, keepdims=True) + jnp.exp2(sink - m)
                ps.append((p * (1.0 / denom)).astype(BF16))
            out.append(jnp.concatenate(ps, axis=0))
        return out

    def attn_values(c, probs):
        rows = slice(c * CHUNK, (c + 1) * CHUNK)
        acc = None
        for kvh in range(N_KV_HEADS):
            if c == 0:
                v_band = jnp.concatenate([vprev_scr[rd, kvh], vm_scr[kvh, 0:CHUNK, :]], axis=0)
            else:
                v_band = vm_scr[kvh, (c - 1) * CHUNK:(c + 1) * CHUNK, :]
            o = _dot(probs[kvh], v_band)
            acc = o if acc is None else acc + o
        for g in range(GROUP):
            b_scr[rows, g * KV_WIDTH:(g + 1) * KV_WIDTH] = acc[g * CHUNK:(g + 1) * CHUNK, :].astype(BF16)

    def sgu_chunk(c):
        rows = slice(c * CHUNK, (c + 1) * CHUNK)
        vn_c = vn_scr[rows, :]
        mixed = jnp.concatenate(
            [_dot(wt_scr[g], vn_c[:, g * SGU_GROUP_DIM:(g + 1) * SGU_GROUP_DIM]) for g in range(SGU_GROUPS)],
            axis=1) + bexp_ref[...]
        a_scr[rows, :] = (u_scr[rows, :] * mixed).astype(BF16)

    half = D_MODEL // 2

    def tail_u(h, lo):
        u_scr[:, lo:lo + half] = _gelu_tanh(h)

    def tail_ga(h, lo):
        gate_scr[0, :, lo:lo + half] = jax.nn.sigmoid(h)

    def tail_gb(h, lo):
        gate_scr[1, :, lo:lo + half] = jax.nn.sigmoid(h)

    fillers = [
        (lambda: _dot(xn, w_uv_ref[:, OFF_U + half:OFF_V]), lambda h: tail_u(h, half)),
        (lambda: _dot(xn, w_rest_ref[:, R_GA:R_GA + half]), lambda h: tail_ga(h, 0)),
        (lambda: _dot(xn, w_rest_ref[:, R_GA + half:R_GB]), lambda h: tail_ga(h, half)),
        (lambda: _dot(xn, w_rest_ref[:, R_GB:R_GB + half]), lambda h: tail_gb(h, 0)),
        (lambda: _dot(xn, w_rest_ref[:, R_GB + half:R_END]), lambda h: tail_gb(h, half)),
    ]
    vn_scr[...] = _rmsnorm(_gelu_tanh(h_v), sgu_g_ref[...]).astype(BF16)
    tail_u(h_u0, 0)
    sgu_after = {nblk - 2: range(0, nblk // 2), nblk - 1: range(nblk // 2, nblk)}
    for c in range(nblk):
        scores = attn_scores(c)
        proj = fillers[c][0]() if c < len(fillers) else None
        for cc in sgu_after.get(c, ()):
            sgu_chunk(cc)
        probs = attn_softmax(c, scores)
        attn_values(c, probs)
        if proj is not None:
            fillers[c][1](proj)
    for matmul, tail in fillers[nblk:]:
        tail(matmul())

    sample = _sample_attn_stages(s_qsel_ref, s_knew_ref, s_vnew_ref, s_ck_ref, s_cv_ref, s_bias_ref, sink_ref,
                                 step % 2 == 1, s_o_ref, s_nk_ref, s_nv_ref)
    next(sample)
    branch_a = _dot(a_scr[...], w_oa_ref[...])
    next(sample)
    branch_b = _dot(b_scr[...], w_ob_ref[...])
    for _ in sample:
        pass
    hm = gate_scr[0] * branch_a + gate_scr[1] * branch_b
    x1_ref[...] = x + _dot(hm.astype(BF16), w_out_ref[...])


def _mix_prompt(x2d, ln1, w_uv, w_rest, w_q, sgu_g, sgu_w, sgu_b, sinks, slopes, w_oa, w_ob, w_out,
                w_up_f32, w_down_f32,
                s_qsel, s_knew, s_vnew, s_cache_k, s_cache_v, s_bias, *, batch, seq):
    n = x2d.shape[0]
    tb = TOKEN_BLOCK
    nblk = tb // CHUNK
    steps = n // tb
    steps_per_seq = seq // tb
    nb = s_cache_k.shape[0]
    per_step = nb // steps
    assert per_step * steps == nb and 2 * per_step == SUBLANES and s_knew.shape == (nb, KV_WIDTH)
    s_head_block = pl.BlockSpec((None, per_step * N_HEADS, KV_WIDTH), lambda i: (i, 0, 0))
    s_new_block = pl.BlockSpec((SUBLANES, KV_WIDTH), lambda i: (i // 2, 0))
    s_cache_block = pl.BlockSpec((per_step, KV_WIDTH, WINDOW), lambda i: (i, 0, 0))
    row_block = pl.BlockSpec((tb, D_MODEL), lambda i: (i, 0))
    win_block = pl.BlockSpec((None, KV_WIDTH, WINDOW), lambda i: (i // steps_per_seq, 0, 0))
    up_block = pl.BlockSpec((D_MODEL // steps, D_FF), lambda i: (i, 0))
    down_block = pl.BlockSpec((D_FF // steps, D_MODEL), lambda i: (i, 0))
    return pl.pallas_call(
        functools.partial(_mix_prompt_kernel, steps_per_seq=steps_per_seq),
        grid=(n // tb,),
        in_specs=[
            row_block,
            _resident((1, D_MODEL)),
            _resident((D_MODEL, OFF_Q)),
            _resident((D_MODEL, R_END)),
            _resident((D_MODEL, D_MODEL)),
            _resident((1, D_MODEL)),
            _resident((SGU_GROUPS, CHUNK, CHUNK)),
            _resident((SGU_GROUPS, CHUNK)),
            pl.BlockSpec(memory_space=pltpu.SMEM),
            pl.BlockSpec(memory_space=pltpu.SMEM),
            _resident((D_MODEL, D_MODEL)),
            _resident((D_MODEL, D_MODEL)),
            _resident((D_MODEL, D_MODEL)),
            up_block,
            down_block,
            s_head_block, s_new_block, s_new_block, s_cache_block, s_cache_block,
            _resident((N_HEADS, WINDOW)),
        ],
        out_specs=[row_block, win_block, win_block, up_block, down_block,
                   s_head_block, s_cache_block, s_cache_block],
        out_shape=[
            jax.ShapeDtypeStruct((n, D_MODEL), F32),
            jax.ShapeDtypeStruct((batch, KV_WIDTH, WINDOW), F32),
            jax.ShapeDtypeStruct((batch, KV_WIDTH, WINDOW), F32),
            jax.ShapeDtypeStruct((D_MODEL, D_FF), BF16),
            jax.ShapeDtypeStruct((D_FF, D_MODEL), BF16),
            jax.ShapeDtypeStruct((steps, per_step * N_HEADS, KV_WIDTH), BF16),
            jax.ShapeDtypeStruct((nb, KV_WIDTH, WINDOW), F32),
            jax.ShapeDtypeStruct((nb, KV_WIDTH, WINDOW), F32),
        ],
        scratch_shapes=[
            pltpu.VMEM((nblk, GROUP * CHUNK, KV_WIDTH), BF16),
            pltpu.VMEM((N_KV_HEADS, nblk, KV_WIDTH, CHUNK), BF16),
            pltpu.VMEM((N_KV_HEADS, tb, KV_WIDTH), BF16),
            pltpu.VMEM((2, N_KV_HEADS, KV_WIDTH, CHUNK), BF16),
            pltpu.VMEM((2, N_KV_HEADS, WINDOW, KV_WIDTH), BF16),
            pltpu.VMEM((tb, D_MODEL), BF16),
            pltpu.VMEM((tb, D_MODEL), F32),
            pltpu.VMEM((2, tb, D_MODEL), F32),
            pltpu.VMEM((tb, D_MODEL), BF16),
            pltpu.VMEM((tb, D_MODEL), BF16),
            pltpu.VMEM((SGU_GROUPS, CHUNK, CHUNK), BF16),
            pltpu.VMEM((N_HEADS, CHUNK, 2 * CHUNK), F32),
            pltpu.VMEM((CHUNK, D_MODEL), F32),
        ],
        compiler_params=_params(),
        name="mix_prompt",
    )(x2d, ln1, w_uv, w_rest, w_q, sgu_g, sgu_w, sgu_b, sinks, slopes, w_oa, w_ob, w_out, w_up_f32, w_down_f32,
      s_qsel, s_knew, s_vnew, s_cache_k, s_cache_v, s_bias)


def _ffn_rows(x, y_ref, ln2_ref, w_up_ref, w_down_ref, lnf_ref, *, row_parts):
    n_slabs = D_FF // FF_SLAB
    m = x.shape[0]
    parts = [slice(i * m // row_parts, (i + 1) * m // row_parts) for i in range(row_parts)]

    def up(j, xn):
        return _dot(xn, w_up_ref[:, j * FF_SLAB:(j + 1) * FF_SLAB])

    xns = [_rmsnorm(x[p], ln2_ref[...]).astype(BF16) for p in parts]
    h_next = jnp.concatenate([up(0, xn_p) for xn_p in xns], axis=0) if row_parts > 1 else up(0, xns[0])
    xn = jnp.concatenate(xns, axis=0) if row_parts > 1 else xns[0]

    acc = x
    for j in range(n_slabs):
        h = h_next
        if j + 1 < n_slabs:
            h_next = up(j + 1, xn)
        h = jnp.square(jnp.maximum(h, 0.0)).astype(BF16)
        w_d = w_down_ref[j * FF_SLAB:(j + 1) * FF_SLAB, :]
        if j + 1 < n_slabs:
            acc = acc + _dot(h, w_d)
        else:
            for p in parts:
                y_ref[p, :] = _rmsnorm(acc[p] + _dot(h[p], w_d), lnf_ref[...])


def _ffn_kernel(x_ref, ln2_ref, w_up_ref, w_down_ref, lnf_ref,
                s_o_ref, s_selt_ref, s_a_ref, s_ga_ref, s_gb_ref, s_x_ref, w_oa_ref, w_ob_ref, w_out_ref,
                y_ref, ys_ref):
    i = pl.program_id(0)
    last = pl.num_programs(0) - 1

    @pl.when(i < last)
    def _():
        _ffn_rows(x_ref[...], y_ref, ln2_ref, w_up_ref, w_down_ref, lnf_ref, row_parts=4)

    @pl.when(i == last)
    def _():
        nb = s_x_ref.shape[0]
        bst = _dot(s_selt_ref[...], s_o_ref[...]).astype(BF16)
        ob = _dot(bst[0:nb, :], w_ob_ref[0:KV_WIDTH, :])
        for g in range(1, GROUP):
            ob = ob + _dot(bst[g * nb:(g + 1) * nb, :], w_ob_ref[g * KV_WIDTH:(g + 1) * KV_WIDTH, :])
        hm = s_ga_ref[...] * _dot(s_a_ref[...], w_oa_ref[...]) + s_gb_ref[...] * ob
        xs1 = s_x_ref[...] + _dot(hm.astype(BF16), w_out_ref[...])
        _ffn_rows(xs1, ys_ref, ln2_ref, w_up_ref, w_down_ref, lnf_ref, row_parts=1)


def _ffn(x2d, ln2, w_up, w_down, lnf, s_o, s_selt, s_a, s_ga, s_gb, xs2d, w_oa, w_ob, w_out):
    n = x2d.shape[0]
    nb = xs2d.shape[0]
    n_prompt_steps = n // FFN_BLOCK
    row_block = pl.BlockSpec((FFN_BLOCK, D_MODEL), lambda i: (jnp.minimum(i, n_prompt_steps - 1), 0))
    return pl.pallas_call(
        _ffn_kernel,
        grid=(n_prompt_steps + 1,),
        in_specs=[row_block, _resident((1, D_MODEL)), _resident((D_MODEL, D_FF)),
                  _resident((D_FF, D_MODEL)), _resident((1, D_MODEL)),
                  _resident((nb * N_HEADS, KV_WIDTH)), _resident((GROUP * nb, nb * N_HEADS)),
                  _resident((nb, D_MODEL)), _resident((nb, D_MODEL)), _resident((nb, D_MODEL)),
                  _resident((nb, D_MODEL)),
                  _resident((D_MODEL, D_MODEL)), _resident((D_MODEL, D_MODEL)), _resident((D_MODEL, D_MODEL))],
        out_specs=[row_block, _whole((nb, D_MODEL))],
        out_shape=[jax.ShapeDtypeStruct((n, D_MODEL), F32), jax.ShapeDtypeStruct((nb, D_MODEL), F32)],
        compiler_params=_params(),
        name="ffn",
    )(x2d, ln2, w_up, w_down, lnf, s_o, s_selt, s_a, s_ga, s_gb, xs2d, w_oa, w_ob, w_out)


def _sample_proj_kernel(x_ref, ln1_ref, w_blk_ref, sgu_g_ref, sgu_w_ref, sgu_b_ref, sel_ref,
                        w_uv_bf_ref, w_rest_bf_ref, w_q_bf_ref, qsel_ref, knew_ref, vnew_ref, vn_ref, a_ref,
                        ga_ref, gb_ref, xn_scr, h_scr):
    j = pl.program_id(0)
    nb = x_ref.shape[0]

    @pl.when(j == 0)
    def _():
        xn_scr[...] = _rmsnorm(x_ref[...], ln1_ref[...]).astype(BF16)

    wb = w_blk_ref[...].astype(BF16)
    h_scr[j] = _dot(xn_scr[...], wb)

    def reorder_q(half):
        for head in range(N_HEADS):
            src = OFF_Q + head * HEAD_DIM - half * W_HALF
            if 0 <= src < W_HALF:
                kvh, g = divmod(head, GROUP)
                dst = g * KV_WIDTH + kvh * HEAD_DIM
                w_q_bf_ref[:, dst:dst + HEAD_DIM] = wb[:, src:src + HEAD_DIM]

    assert OFF_Q <= W_HALF <= OFF_K

    @pl.when(j == 0)
    def _():
        w_uv_bf_ref[...] = wb[:, OFF_U:OFF_Q]
        reorder_q(0)

    @pl.when(j == 1)
    def _():
        w_rest_bf_ref[...] = wb[:, OFF_K - W_HALF:]
        reorder_q(1)

        def cols(lo, hi):
            half = lo // W_HALF
            assert half == (hi - 1) // W_HALF
            return h_scr[half, :, lo - half * W_HALF:hi - half * W_HALF]

        u = _gelu_tanh(cols(OFF_U, OFF_V))
        v = _gelu_tanh(cols(OFF_V, OFF_Q))
        vn = _rmsnorm(v, sgu_g_ref[...])
        vn_ref[...] = vn
        def over_groups(entry):
            return jnp.concatenate(
                [jnp.broadcast_to(entry(g), (1, SGU_GROUP_DIM)) for g in range(SGU_GROUPS)], axis=1)

        w_diag = over_groups(lambda g: sgu_w_ref[g, 0:1, 0:1])
        b_first = over_groups(lambda g: sgu_b_ref[g:g + 1, 0:1])
        a_ref[...] = (u * (vn * w_diag + b_first)).astype(BF16)
        knew_ref[...] = cols(OFF_K, OFF_VA)
        vnew_ref[...] = cols(OFF_VA, OFF_GA)
        ga_ref[...] = jax.nn.sigmoid(cols(OFF_GA, OFF_GB))
        gb_ref[...] = jax.nn.sigmoid(cols(OFF_GB, IN_WIDTH))
        q = _dot(xn_scr[...], w_q_bf_ref[...]) * ATTN_SCALE
        qstack = jnp.concatenate([q[:, g * KV_WIDTH:(g + 1) * KV_WIDTH] for g in range(GROUP)], axis=0).astype(BF16)
        qrep = _dot(sel_ref[...], qstack)
        row_kvh = lax.broadcasted_iota(jnp.int32, (nb * N_HEADS, KV_WIDTH), 0) % N_KV_HEADS
        lane_kvh = lax.broadcasted_iota(jnp.int32, (nb * N_HEADS, KV_WIDTH), 1) // HEAD_DIM
        qsel_ref[...] = jnp.where(row_kvh == lane_kvh, qrep, 0.0).astype(BF16)


def _sample_proj(xs2d, ln1, w_in_f32, sgu_g, sgu_w, sgu_b, sel):
    nb = xs2d.shape[0]
    n_blocks = IN_WIDTH // W_HALF
    w_block = pl.BlockSpec((D_MODEL, W_HALF), lambda j: (0, j))
    return pl.pallas_call(
        _sample_proj_kernel,
        grid=(n_blocks,),
        in_specs=[_whole((nb, D_MODEL)), _whole((1, D_MODEL)), w_block,
                  _whole((1, D_MODEL)),
                  pl.BlockSpec((SGU_GROUPS, SUBLANES, CHUNK), lambda j: (0, 0, 0)),
                  _whole((SGU_GROUPS, CHUNK)),
                  _resident((nb * N_HEADS, GROUP * nb))],
        out_specs=[_whole((D_MODEL, OFF_Q)), _whole((D_MODEL, IN_WIDTH - OFF_K)), _whole((D_MODEL, D_MODEL)),
                   _whole((nb * N_HEADS, KV_WIDTH)), _whole((nb, KV_WIDTH)), _whole((nb, KV_WIDTH)),
                   _whole((nb, D_MODEL)), _whole((nb, D_MODEL)),
                   _whole((nb, D_MODEL)), _whole((nb, D_MODEL))],
        out_shape=[
            jax.ShapeDtypeStruct((D_MODEL, OFF_Q), BF16),
            jax.ShapeDtypeStruct((D_MODEL, IN_WIDTH - OFF_K), BF16),
            jax.ShapeDtypeStruct((D_MODEL, D_MODEL), BF16),
            jax.ShapeDtypeStruct((nb * N_HEADS, KV_WIDTH), BF16),
            jax.ShapeDtypeStruct((nb, KV_WIDTH), F32),
            jax.ShapeDtypeStruct((nb, KV_WIDTH), F32),
            jax.ShapeDtypeStruct((nb, D_MODEL), F32),
            jax.ShapeDtypeStruct((nb, D_MODEL), BF16),
            jax.ShapeDtypeStruct((nb, D_MODEL), F32),
            jax.ShapeDtypeStruct((nb, D_MODEL), F32),
        ],
        scratch_shapes=[pltpu.VMEM((nb, D_MODEL), BF16), pltpu.VMEM((n_blocks, nb, W_HALF), F32)],
        compiler_params=_params(),
        name="sample_proj",
    )(xs2d, ln1, w_in_f32, sgu_g, sgu_w, sgu_b, sel)


def _sample_attn_stages(qsel_ref, knew_ref, vnew_ref, ck_ref, cv_ref, bias_ref, sink_ref, upper,
                        o_ref, nk_ref, nv_ref):
    bs = ck_ref.shape[0]
    assert knew_ref.shape[0] == 2 * bs
    row0 = jnp.where(upper, bs, 0)
    row_kvh = lax.broadcasted_iota(jnp.int32, (N_HEADS, KV_WIDTH), 0) % N_KV_HEADS
    lane_kvh = lax.broadcasted_iota(jnp.int32, (N_HEADS, KV_WIDTH), 1) // HEAD_DIM
    own = row_kvh == lane_kvh
    bias = bias_ref[...]
    head_row = lax.broadcasted_iota(jnp.int32, (N_HEADS, 1), 0)
    sink = jnp.zeros((N_HEADS, 1), F32)
    for h in range(N_HEADS):
        g, kvh = divmod(h, N_KV_HEADS)
        sink = jnp.where(head_row == h, sink_ref[0, kvh * GROUP + g], sink)

    qss = [qsel_ref[i * N_HEADS:(i + 1) * N_HEADS, :] for i in range(bs)]
    kns = [knew_ref[pl.ds(row0 + i, 1), :] for i in range(bs)]
    vws = [vnew_ref[pl.ds(row0 + i, 1), :] for i in range(bs)]
    scores = [_dot(qss[i], ck_ref[i].astype(BF16)) + bias for i in range(bs)]
    yield
    probs = []
    for i in range(bs):
        s = scores[i]
        s_new = jnp.sum(qss[i].astype(F32) * kns[i], axis=1, keepdims=True)
        m = jnp.maximum(jnp.maximum(jnp.max(s, axis=1, keepdims=True), s_new), sink)
        p = jnp.exp(s - m)
        p_new = jnp.exp(s_new - m)
        denom = jnp.sum(p, axis=1, keepdims=True) + p_new + jnp.exp(sink - m)
        probs.append((p.astype(BF16), p_new, denom))

    kn_t = knew_ref[...].T
    vw_t = vnew_ref[...].T
    last_lane = lax.broadcasted_iota(jnp.int32, (KV_WIDTH, WINDOW), 1) == WINDOW - 1
    for i in range(bs):
        kn_col = jnp.where(upper, kn_t[:, bs + i:bs + i + 1], kn_t[:, i:i + 1])
        vw_col = jnp.where(upper, vw_t[:, bs + i:bs + i + 1], vw_t[:, i:i + 1])
        nk_ref[i] = jnp.where(last_lane, kn_col, pltpu.roll(ck_ref[i], WINDOW - 1, 1))
        nv_ref[i] = jnp.where(last_lane, vw_col, pltpu.roll(cv_ref[i], WINDOW - 1, 1))
    yield
    for i in range(bs):
        p, p_new, denom = probs[i]
        o = (_dot_nt(p, cv_ref[i].astype(BF16)) + p_new * vws[i]) / denom
        o_ref[i * N_HEADS:(i + 1) * N_HEADS, :] = jnp.where(own, o, 0.0).astype(BF16)


def _head_perm(v):
    return v.reshape(N_KV_HEADS, GROUP).T.reshape(N_HEADS)


def _alibi_slopes():
    h = np.arange(1, N_HEADS + 1, dtype=np.float32)
    return np.exp2(-8.0 * h / N_HEADS).astype(np.float32)


def _selection_matrix(nb):
    r = np.arange(nb * N_HEADS)
    c = np.arange(GROUP * nb)
    same_sample = (r[:, None] // N_HEADS) == (c[None, :] % nb)
    same_member = ((r[:, None] % N_HEADS) // N_KV_HEADS) == (c[None, :] // nb)
    return (same_sample & same_member).astype(np.float32)


def kernel(x_prompt, x_sample, cache_k_win, cache_v_win, ln1_g, w_in, sgu_norm_g, sgu_w, sgu_b, attn_sinks,
           w_oa, w_ob, w_out, ln2_g, w_up, w_down, lnf_g):
    batch, seq, _ = x_prompt.shape
    dec_batch, dec_seq, _ = x_sample.shape
    depth = w_in.shape[0]
    assert depth == 1 and dec_seq == 1
    assert seq % TOKEN_BLOCK == 0 and TOKEN_BLOCK % CHUNK == 0
    assert (batch * seq) % FFN_BLOCK == 0
    assert w_in.shape[-1] == IN_WIDTH

    wi = w_in[0]
    w_ob_b = w_ob[0].reshape(N_KV_HEADS, GROUP, HEAD_DIM, D_MODEL).transpose(1, 0, 2, 3).reshape(
        D_MODEL, D_MODEL).astype(BF16)
    w_oa_b = w_oa[0].astype(BF16)
    w_out_b = w_out[0].astype(BF16)
    ln1 = ln1_g[0].reshape(1, D_MODEL)
    ln2 = ln2_g[0].reshape(1, D_MODEL)
    lnf = lnf_g.reshape(1, D_MODEL)
    sgu_g = sgu_norm_g[0].reshape(1, D_MODEL)
    slopes_p = _head_perm(_alibi_slopes())

    xs2d = x_sample.reshape(dec_batch, D_MODEL)
    sel_np = _selection_matrix(dec_batch)
    sel = jnp.asarray(sel_np, BF16)
    selt = jnp.asarray(sel_np.T, BF16)
    bias_s = -slopes_p[:, None] * (WINDOW - np.arange(WINDOW, dtype=np.float32))[None, :]

    w_uv_b, w_rest_b, w_q_b, qsel, knew, vnew, vn, a_s, ga_s, gb_s = _sample_proj(
        xs2d, ln1, wi, sgu_g, sgu_w[0], sgu_b[0], sel)
    def to_feature_major(c):
        return c[0].transpose(0, 2, 3, 1).reshape(dec_batch, KV_WIDTH, WINDOW)

    def from_feature_major(c):
        return c.reshape(c.shape[0], N_KV_HEADS, HEAD_DIM, WINDOW).transpose(0, 3, 1, 2)[None]

    steps = (batch * seq) // TOKEN_BLOCK
    per_step = dec_batch // steps

    x1, kwin, vwin, w_up_b, w_down_b, o_s, nk, nv = _mix_prompt(
        x_prompt.reshape(batch * seq, D_MODEL), ln1, w_uv_b, w_rest_b, w_q_b, sgu_g, sgu_w[0], sgu_b[0],
        attn_sinks, slopes_p * LOG2E, w_oa_b, w_ob_b, w_out_b, w_up[0], w_down[0],
        qsel.reshape(steps, per_step * N_HEADS, KV_WIDTH), knew, vnew,
        to_feature_major(cache_k_win), to_feature_major(cache_v_win), bias_s,
        batch=batch, seq=seq)

    y_prompt, y_sample = _ffn(x1, ln2, w_up_b, w_down_b, lnf,
                              o_s.reshape(dec_batch * N_HEADS, KV_WIDTH), selt, a_s, ga_s, gb_s, xs2d,
                              w_oa_b, w_ob_b, w_out_b)

    return (y_prompt.reshape(batch, seq, D_MODEL),
            y_sample.reshape(dec_batch, dec_seq, D_MODEL),
            from_feature_major(kwin), from_feature_major(vwin),
            from_feature_major(nk), from_feature_major(nv),
            vn.reshape(depth, dec_batch, dec_seq, D_MODEL))
```

```python
import functools
import math

import numpy as np
import jax
import jax.numpy as jnp
from jax import lax
from jax.experimental import pallas as pl
from jax.experimental.pallas import tpu as pltpu

D_MODEL = 1024
N_HEADS = 16
HEAD_DIM = 64
N_KV_HEADS = 4
GROUP = N_HEADS // N_KV_HEADS
KV_WIDTH = N_KV_HEADS * HEAD_DIM
WINDOW = 128
CHUNK = 128
SGU_GROUPS = 8
SGU_GROUP_DIM = D_MODEL // SGU_GROUPS
D_FF = 4 * D_MODEL
FF_SLAB = 1024
EPS = 1e-6
NEG_BIG = -1e30
ATTN_SCALE = HEAD_DIM ** -0.5
LOG2E = math.log2(math.e)

OFF_U, OFF_V, OFF_Q, OFF_K, OFF_VA, OFF_GA, OFF_GB, IN_WIDTH = 0, 1024, 2048, 3072, 3328, 3584, 4608, 5632
W_STEP = IN_WIDTH // 4

TOKEN_BLOCK = 512
FFN_BLOCK = 1024
SUBLANES = 8
VMEM_LIMIT_BYTES = 58 * 1024 * 1024

F32 = jnp.float32
BF16 = jnp.bfloat16


def _rmsnorm(x, g):
    ms = jnp.mean(x * x, axis=-1, keepdims=True)
    return x * lax.rsqrt(ms + EPS) * g


def _gelu_tanh(x):
    c = math.sqrt(2.0 / math.pi)
    return x * (0.5 * (1.0 + jnp.tanh(c * (x + 0.044715 * (x * x * x)))))


def _dot(a, b):
    return jnp.dot(a, b, preferred_element_type=F32)


def _dot_nt(a, b):
    return lax.dot_general(a, b, (((1,), (1,)), ((), ())), preferred_element_type=F32)


def _resident(shape):
    zeros = (0,) * len(shape)
    return pl.BlockSpec(shape, lambda *_: zeros, pipeline_mode=pl.Buffered(1))


def _whole(shape):
    zeros = (0,) * len(shape)
    return pl.BlockSpec(shape, lambda *_: zeros)


def _params():
    return pltpu.CompilerParams(dimension_semantics=("arbitrary",), vmem_limit_bytes=VMEM_LIMIT_BYTES)


def _mix_prompt_kernel(x_ref, ln1_ref, w_in_ref, w_q_ref, sgu_g_ref, sgu_w_ref, sgu_b_ref,
                       sink_ref, slope_ref, w_oa_ref, w_ob_ref, w_out_ref, w_up_blk_ref, w_down_blk_ref,
                       s_qsel_ref, s_knew_ref, s_vnew_ref, s_ck_ref, s_cv_ref, s_bias_ref,
                       x1_ref, kwin_ref, vwin_ref, w_up_bf_ref, w_down_bf_ref, s_o_ref, s_nk_ref, s_nv_ref,
                       qs_scr, kt_scr, vm_scr, kprev_scr, vprev_scr, vn_scr, u_scr, gate_scr, a_scr, b_scr, wt_scr,
                       bias_ref, bexp_ref,
                       *, steps_per_seq):
    step = pl.program_id(0)
    tb = x_ref.shape[0]
    nblk = tb // CHUNK
    first = (step % steps_per_seq) == 0
    rd = step % 2
    wr = 1 - rd

    @pl.when(step == 0)
    def _():
        row = lax.broadcasted_iota(jnp.int32, (CHUNK, CHUNK), 0)
        col = lax.broadcasted_iota(jnp.int32, (CHUNK, CHUNK), 1)
        b_t = sgu_b_ref[...].T
        for g in range(SGU_GROUPS):
            wt_scr[g] = jnp.where(row >= col, sgu_w_ref[g], 0.0).astype(BF16)
            bexp_ref[:, g * SGU_GROUP_DIM:(g + 1) * SGU_GROUP_DIM] = jnp.broadcast_to(
                b_t[:, g:g + 1], (CHUNK, SGU_GROUP_DIM))
        dist = (lax.broadcasted_iota(jnp.int32, (CHUNK, 2 * CHUNK), 0) + CHUNK
                - lax.broadcasted_iota(jnp.int32, (CHUNK, 2 * CHUNK), 1))
        in_band = jnp.logical_and(dist >= 0, dist <= WINDOW)
        dist_f = dist.astype(F32)
        for h in range(N_HEADS):
            bias_ref[h] = jnp.where(in_band, -slope_ref[h] * dist_f, NEG_BIG)
        kt_scr[...] = jnp.zeros(kt_scr.shape, BF16)
        vm_scr[...] = jnp.zeros(vm_scr.shape, BF16)
        kprev_scr[...] = jnp.zeros(kprev_scr.shape, BF16)
        vprev_scr[...] = jnp.zeros(vprev_scr.shape, BF16)

    @pl.when(first)
    def _():
        kprev_scr[rd] = jnp.zeros(kprev_scr.shape[1:], BF16)
        vprev_scr[rd] = jnp.zeros(vprev_scr.shape[1:], BF16)

    w_up_bf_ref[...] = w_up_blk_ref[...].astype(BF16)
    w_down_bf_ref[...] = w_down_blk_ref[...].astype(BF16)

    x = x_ref[...]
    quarters = [slice(i * tb // 4, (i + 1) * tb // 4) for i in range(4)]
    xn_parts = [_rmsnorm(x[p], ln1_ref[...]).astype(BF16) for p in quarters]
    xn = jnp.concatenate(xn_parts, axis=0)

    q = jnp.concatenate([_dot(xn_p, w_q_ref[...]) for xn_p in xn_parts], axis=0)
    k = _dot(xn, w_in_ref[:, OFF_K:OFF_VA])
    va = _dot(xn, w_in_ref[:, OFF_VA:OFF_GA])
    h_v = _dot(xn, w_in_ref[:, OFF_V:OFF_Q])
    h_u0 = _dot(xn, w_in_ref[:, OFF_U:OFF_U + D_MODEL // 2])

    q = (q * (ATTN_SCALE * LOG2E)).astype(BF16)
    for c in range(nblk):
        for g in range(GROUP):
            qs_scr[c, g * CHUNK:(g + 1) * CHUNK, :] = q[c * CHUNK:(c + 1) * CHUNK, g * KV_WIDTH:(g + 1) * KV_WIDTH]

    kt_f32 = k.T
    kwin_ref[...] = kt_f32[:, tb - WINDOW:]
    vwin_ref[...] = va[tb - WINDOW:, :].T
    kt = kt_f32.astype(BF16)
    vab = va.astype(BF16)
    for kvh in range(N_KV_HEADS):
        own = slice(kvh * HEAD_DIM, (kvh + 1) * HEAD_DIM)
        for c in range(nblk):
            kt_scr[kvh, c, own, :] = kt[own, c * CHUNK:(c + 1) * CHUNK]
        vm_scr[kvh, :, own] = vab[:, own]
        kprev_scr[wr, kvh, own, :] = kt[own, tb - WINDOW:]
        vprev_scr[wr, kvh, :, own] = vab[tb - WINDOW:, own]

    no_prev = jnp.where(
        jnp.logical_and(first, lax.broadcasted_iota(jnp.int32, (CHUNK, 2 * CHUNK), 1) < CHUNK), NEG_BIG, 0.0)

    def attn_scores(c):
        qs = qs_scr[c]
        out = []
        for kvh in range(N_KV_HEADS):
            k_prev = kprev_scr[rd, kvh] if c == 0 else kt_scr[kvh, c - 1]
            out.append(_dot(qs, jnp.concatenate([k_prev, kt_scr[kvh, c]], axis=1)))
        return out

    def attn_softmax(c, scores):
        out = []
        for kvh in range(N_KV_HEADS):
            ps = []
            for g in range(GROUP):
                h = g * N_KV_HEADS + kvh
                s = scores[kvh][g * CHUNK:(g + 1) * CHUNK, :] + bias_ref[h]
                if c == 0:
                    s = s + no_prev
                sink = sink_ref[0, kvh * GROUP + g] * LOG2E
                m = jnp.max(s, axis=1, keepdims=True)
                p = jnp.exp2(s - m)
                denom = jnp.sum(p, axis=1, keepdims=True) + jnp.exp2(sink - m)
                ps.append((p * (1.0 / denom)).astype(BF16))
            out.append(jnp.concatenate(ps, axis=0))
        return out

    def attn_values(c, probs):
        rows = slice(c * CHUNK, (c + 1) * CHUNK)
        acc = None
        for kvh in range(N_KV_HEADS):
            if c == 0:
                v_band = jnp.concatenate([vprev_scr[rd, kvh], vm_scr[kvh, 0:CHUNK, :]], axis=0)
            else:
                v_band = vm_scr[kvh, (c - 1) * CHUNK:(c + 1) * CHUNK, :]
            o = _dot(probs[kvh], v_band)
            acc = o if acc is None else acc + o
        for g in range(GROUP):
            b_scr[rows, g * KV_WIDTH:(g + 1) * KV_WIDTH] = acc[g * CHUNK:(g + 1) * CHUNK, :].astype(BF16)

    def sgu_chunk(c):
        rows = slice(c * CHUNK, (c + 1) * CHUNK)
        vn_c = vn_scr[rows, :]
        mixed = jnp.concatenate(
            [_dot(wt_scr[g], vn_c[:, g * SGU_GROUP_DIM:(g + 1) * SGU_GROUP_DIM]) for g in range(SGU_GROUPS)],
            axis=1) + bexp_ref[...]
        a_scr[rows, :] = (u_scr[rows, :] * mixed).astype(BF16)

    half = D_MODEL // 2

    def tail_u(h, lo):
        u_scr[:, lo:lo + half] = _gelu_tanh(h)

    def tail_ga(h, lo):
        gate_scr[0, :, lo:lo + half] = jax.nn.sigmoid(h)

    def tail_gb(h, lo):
        gate_scr[1, :, lo:lo + half] = jax.nn.sigmoid(h)

    fillers = [
        (lambda: _dot(xn, w_in_ref[:, OFF_U + half:OFF_V]), lambda h: tail_u(h, half)),
        (lambda: _dot(xn, w_in_ref[:, OFF_GA:OFF_GA + half]), lambda h: tail_ga(h, 0)),
        (lambda: _dot(xn, w_in_ref[:, OFF_GA + half:OFF_GB]), lambda h: tail_ga(h, half)),
        (lambda: _dot(xn, w_in_ref[:, OFF_GB:OFF_GB + half]), lambda h: tail_gb(h, 0)),
        (lambda: _dot(xn, w_in_ref[:, OFF_GB + half:IN_WIDTH]), lambda h: tail_gb(h, half)),
    ]
    vn_scr[...] = _rmsnorm(_gelu_tanh(h_v), sgu_g_ref[...]).astype(BF16)
    tail_u(h_u0, 0)
    sgu_after = {nblk - 2: range(0, nblk // 2), nblk - 1: range(nblk // 2, nblk)}
    for c in range(nblk):
        scores = attn_scores(c)
        proj = fillers[c][0]() if c < len(fillers) else None
        for cc in sgu_after.get(c, ()):
            sgu_chunk(cc)
        probs = attn_softmax(c, scores)
        attn_values(c, probs)
        if proj is not None:
            fillers[c][1](proj)
    for matmul, tail in fillers[nblk:]:
        tail(matmul())

    sample = _sample_attn_stages(s_qsel_ref, s_knew_ref, s_vnew_ref, s_ck_ref, s_cv_ref, s_bias_ref, sink_ref,
                                 step % 2 == 1, s_o_ref, s_nk_ref, s_nv_ref)
    next(sample)
    branch_a = _dot(a_scr[...], w_oa_ref[...])
    next(sample)
    branch_b = _dot(b_scr[...], w_ob_ref[...])
    for _ in sample:
        pass
    hm = gate_scr[0] * branch_a + gate_scr[1] * branch_b
    x1_ref[...] = x + _dot(hm.astype(BF16), w_out_ref[...])


def _mix_prompt(x2d, ln1, w_in, w_q, sgu_g, sgu_w, sgu_b, sinks, slopes, w_oa, w_ob, w_out,
                w_up_f32, w_down_f32,
                s_qsel, s_knew, s_vnew, s_cache_k, s_cache_v, s_bias, *, batch, seq):
    n = x2d.shape[0]
    tb = TOKEN_BLOCK
    nblk = tb // CHUNK
    steps = n // tb
    steps_per_seq = seq // tb
    nb = s_cache_k.shape[0]
    per_step = nb // steps
    assert per_step * steps == nb and 2 * per_step == SUBLANES and s_knew.shape == (nb, KV_WIDTH)
    s_head_block = pl.BlockSpec((None, per_step * N_HEADS, KV_WIDTH), lambda i: (i, 0, 0))
    s_new_block = pl.BlockSpec((SUBLANES, KV_WIDTH), lambda i: (i // 2, 0))
    s_cache_block = pl.BlockSpec((per_step, KV_WIDTH, WINDOW), lambda i: (i, 0, 0))
    row_block = pl.BlockSpec((tb, D_MODEL), lambda i: (i, 0))
    win_block = pl.BlockSpec((None, KV_WIDTH, WINDOW), lambda i: (i // steps_per_seq, 0, 0))
    up_block = pl.BlockSpec((D_MODEL // steps, D_FF), lambda i: (i, 0))
    down_block = pl.BlockSpec((D_FF // steps, D_MODEL), lambda i: (i, 0))
    return pl.pallas_call(
        functools.partial(_mix_prompt_kernel, steps_per_seq=steps_per_seq),
        grid=(n // tb,),
        in_specs=[
            row_block,
            _resident((1, D_MODEL)),
            _resident((D_MODEL, IN_WIDTH)),
            _resident((D_MODEL, D_MODEL)),
            _resident((1, D_MODEL)),
            _resident((SGU_GROUPS, CHUNK, CHUNK)),
            _resident((SGU_GROUPS, CHUNK)),
            pl.BlockSpec(memory_space=pltpu.SMEM),
            pl.BlockSpec(memory_space=pltpu.SMEM),
            _resident((D_MODEL, D_MODEL)),
            _resident((D_MODEL, D_MODEL)),
            _resident((D_MODEL, D_MODEL)),
            up_block,
            down_block,
            s_head_block, s_new_block, s_new_block, s_cache_block, s_cache_block,
            _resident((N_HEADS, WINDOW)),
        ],
        out_specs=[row_block, win_block, win_block, up_block, down_block,
                   s_head_block, s_cache_block, s_cache_block],
        out_shape=[
            jax.ShapeDtypeStruct((n, D_MODEL), F32),
            jax.ShapeDtypeStruct((batch, KV_WIDTH, WINDOW), F32),
            jax.ShapeDtypeStruct((batch, KV_WIDTH, WINDOW), F32),
            jax.ShapeDtypeStruct((D_MODEL, D_FF), BF16),
            jax.ShapeDtypeStruct((D_FF, D_MODEL), BF16),
            jax.ShapeDtypeStruct((steps, per_step * N_HEADS, KV_WIDTH), BF16),
            jax.ShapeDtypeStruct((nb, KV_WIDTH, WINDOW), F32),
            jax.ShapeDtypeStruct((nb, KV_WIDTH, WINDOW), F32),
        ],
        scratch_shapes=[
            pltpu.VMEM((nblk, GROUP * CHUNK, KV_WIDTH), BF16),
            pltpu.VMEM((N_KV_HEADS, nblk, KV_WIDTH, CHUNK), BF16),
            pltpu.VMEM((N_KV_HEADS, tb, KV_WIDTH), BF16),
            pltpu.VMEM((2, N_KV_HEADS, KV_WIDTH, CHUNK), BF16),
            pltpu.VMEM((2, N_KV_HEADS, WINDOW, KV_WIDTH), BF16),
            pltpu.VMEM((tb, D_MODEL), BF16),
            pltpu.VMEM((tb, D_MODEL), F32),
            pltpu.VMEM((2, tb, D_MODEL), F32),
            pltpu.VMEM((tb, D_MODEL), BF16),
            pltpu.VMEM((tb, D_MODEL), BF16),
            pltpu.VMEM((SGU_GROUPS, CHUNK, CHUNK), BF16),
            pltpu.VMEM((N_HEADS, CHUNK, 2 * CHUNK), F32),
            pltpu.VMEM((CHUNK, D_MODEL), F32),
        ],
        compiler_params=_params(),
        name="mix_prompt",
    )(x2d, ln1, w_in, w_q, sgu_g, sgu_w, sgu_b, sinks, slopes, w_oa, w_ob, w_out, w_up_f32, w_down_f32,
      s_qsel, s_knew, s_vnew, s_cache_k, s_cache_v, s_bias)


def _ffn_rows(x, y_ref, ln2_ref, w_up_ref, w_down_ref, lnf_ref, *, row_parts):
    n_slabs = D_FF // FF_SLAB
    m = x.shape[0]
    parts = [slice(i * m // row_parts, (i + 1) * m // row_parts) for i in range(row_parts)]

    def up(j, xn):
        return _dot(xn, w_up_ref[:, j * FF_SLAB:(j + 1) * FF_SLAB])

    xns = [_rmsnorm(x[p], ln2_ref[...]).astype(BF16) for p in parts]
    h_next = jnp.concatenate([up(0, xn_p) for xn_p in xns], axis=0) if row_parts > 1 else up(0, xns[0])
    xn = jnp.concatenate(xns, axis=0) if row_parts > 1 else xns[0]

    acc = x
    for j in range(n_slabs):
        h = h_next
        if j + 1 < n_slabs:
            h_next = up(j + 1, xn)
        h = jnp.square(jnp.maximum(h, 0.0)).astype(BF16)
        w_d = w_down_ref[j * FF_SLAB:(j + 1) * FF_SLAB, :]
        if j + 1 < n_slabs:
            acc = acc + _dot(h, w_d)
        else:
            for p in parts:
                y_ref[p, :] = _rmsnorm(acc[p] + _dot(h[p], w_d), lnf_ref[...])


def _ffn_kernel(x_ref, ln2_ref, w_up_ref, w_down_ref, lnf_ref,
                s_o_ref, s_selt_ref, s_a_ref, s_ga_ref, s_gb_ref, s_x_ref, w_oa_ref, w_ob_ref, w_out_ref,
                y_ref, ys_ref):
    i = pl.program_id(0)
    last = pl.num_programs(0) - 1

    @pl.when(i < last)
    def _():
        _ffn_rows(x_ref[...], y_ref, ln2_ref, w_up_ref, w_down_ref, lnf_ref, row_parts=4)

    @pl.when(i == last)
    def _():
        nb = s_x_ref.shape[0]
        bst = _dot(s_selt_ref[...], s_o_ref[...]).astype(BF16)
        ob = _dot(bst[0:nb, :], w_ob_ref[0:KV_WIDTH, :])
        for g in range(1, GROUP):
            ob = ob + _dot(bst[g * nb:(g + 1) * nb, :], w_ob_ref[g * KV_WIDTH:(g + 1) * KV_WIDTH, :])
        hm = s_ga_ref[...] * _dot(s_a_ref[...], w_oa_ref[...]) + s_gb_ref[...] * ob
        xs1 = s_x_ref[...] + _dot(hm.astype(BF16), w_out_ref[...])
        _ffn_rows(xs1, ys_ref, ln2_ref, w_up_ref, w_down_ref, lnf_ref, row_parts=1)


def _ffn(x2d, ln2, w_up, w_down, lnf, s_o, s_selt, s_a, s_ga, s_gb, xs2d, w_oa, w_ob, w_out):
    n = x2d.shape[0]
    nb = xs2d.shape[0]
    n_prompt_steps = n // FFN_BLOCK
    row_block = pl.BlockSpec((FFN_BLOCK, D_MODEL), lambda i: (jnp.minimum(i, n_prompt_steps - 1), 0))
    return pl.pallas_call(
        _ffn_kernel,
        grid=(n_prompt_steps + 1,),
        in_specs=[row_block, _resident((1, D_MODEL)), _resident((D_MODEL, D_FF)),
                  _resident((D_FF, D_MODEL)), _resident((1, D_MODEL)),
                  _resident((nb * N_HEADS, KV_WIDTH)), _resident((GROUP * nb, nb * N_HEADS)),
                  _resident((nb, D_MODEL)), _resident((nb, D_MODEL)), _resident((nb, D_MODEL)),
                  _resident((nb, D_MODEL)),
                  _resident((D_MODEL, D_MODEL)), _resident((D_MODEL, D_MODEL)), _resident((D_MODEL, D_MODEL))],
        out_specs=[row_block, _whole((nb, D_MODEL))],
        out_shape=[jax.ShapeDtypeStruct((n, D_MODEL), F32), jax.ShapeDtypeStruct((nb, D_MODEL), F32)],
        compiler_params=_params(),
        name="ffn",
    )(x2d, ln2, w_up, w_down, lnf, s_o, s_selt, s_a, s_ga, s_gb, xs2d, w_oa, w_ob, w_out)


def _sample_proj_kernel(x_ref, ln1_ref, w_blk_ref, sgu_g_ref, sgu_w_ref, sgu_b_ref, sel_ref,
                        w_bf_ref, w_q_bf_ref, qsel_ref, knew_ref, vnew_ref, vn_ref, a_ref,
                        ga_ref, gb_ref, xn_scr, h_scr):
    j = pl.program_id(0)
    nb = x_ref.shape[0]
    n_steps = IN_WIDTH // W_STEP

    @pl.when(j == 0)
    def _():
        xn_scr[...] = _rmsnorm(x_ref[...], ln1_ref[...]).astype(BF16)

    wb = w_blk_ref[...].astype(BF16)
    w_bf_ref[...] = wb
    h_scr[j] = _dot(xn_scr[...], wb)

    assert W_STEP % HEAD_DIM == 0
    for step in range(n_steps):
        heads = [h for h in range(N_HEADS) if step * W_STEP <= OFF_Q + h * HEAD_DIM < (step + 1) * W_STEP]
        if heads:
            @pl.when(j == step)
            def _(step=step, heads=heads):
                for head in heads:
                    src = OFF_Q + head * HEAD_DIM - step * W_STEP
                    kvh, g = divmod(head, GROUP)
                    dst = g * KV_WIDTH + kvh * HEAD_DIM
                    w_q_bf_ref[:, dst:dst + HEAD_DIM] = wb[:, src:src + HEAD_DIM]

    @pl.when(j == n_steps - 1)
    def _():
        def cols(lo, hi):
            pieces = []
            for step in range(n_steps):
                a, b = max(lo, step * W_STEP), min(hi, (step + 1) * W_STEP)
                if a < b:
                    pieces.append(h_scr[step, :, a - step * W_STEP:b - step * W_STEP])
            return pieces[0] if len(pieces) == 1 else jnp.concatenate(pieces, axis=1)

        u = _gelu_tanh(cols(OFF_U, OFF_V))
        v = _gelu_tanh(cols(OFF_V, OFF_Q))
        vn = _rmsnorm(v, sgu_g_ref[...])
        vn_ref[...] = vn
        def over_groups(entry):
            return jnp.concatenate(
                [jnp.broadcast_to(entry(g), (1, SGU_GROUP_DIM)) for g in range(SGU_GROUPS)], axis=1)

        w_diag = over_groups(lambda g: sgu_w_ref[g, 0:1, 0:1])
        b_first = over_groups(lambda g: sgu_b_ref[g:g + 1, 0:1])
        a_ref[...] = (u * (vn * w_diag + b_first)).astype(BF16)
        knew_ref[...] = cols(OFF_K, OFF_VA)
        vnew_ref[...] = cols(OFF_VA, OFF_GA)
        ga_ref[...] = jax.nn.sigmoid(cols(OFF_GA, OFF_GB))
        gb_ref[...] = jax.nn.sigmoid(cols(OFF_GB, IN_WIDTH))
        q = _dot(xn_scr[...], w_q_bf_ref[...]) * ATTN_SCALE
        qstack = jnp.concatenate([q[:, g * KV_WIDTH:(g + 1) * KV_WIDTH] for g in range(GROUP)], axis=0).astype(BF16)
        qrep = _dot(sel_ref[...], qstack)
        row_kvh = lax.broadcasted_iota(jnp.int32, (nb * N_HEADS, KV_WIDTH), 0) % N_KV_HEADS
        lane_kvh = lax.broadcasted_iota(jnp.int32, (nb * N_HEADS, KV_WIDTH), 1) // HEAD_DIM
        qsel_ref[...] = jnp.where(row_kvh == lane_kvh, qrep, 0.0).astype(BF16)


def _sample_proj(xs2d, ln1, w_in_f32, sgu_g, sgu_w, sgu_b, sel):
    nb = xs2d.shape[0]
    n_blocks = IN_WIDTH // W_STEP
    w_block = pl.BlockSpec((D_MODEL, W_STEP), lambda j: (0, j))
    return pl.pallas_call(
        _sample_proj_kernel,
        grid=(n_blocks,),
        in_specs=[_whole((nb, D_MODEL)), _whole((1, D_MODEL)), w_block,
                  _whole((1, D_MODEL)),
                  pl.BlockSpec((SGU_GROUPS, SUBLANES, CHUNK), lambda j: (0, 0, 0)),
                  _whole((SGU_GROUPS, CHUNK)),
                  _resident((nb * N_HEADS, GROUP * nb))],
        out_specs=[w_block, _whole((D_MODEL, D_MODEL)),
                   _whole((nb * N_HEADS, KV_WIDTH)), _whole((nb, KV_WIDTH)), _whole((nb, KV_WIDTH)),
                   _whole((nb, D_MODEL)), _whole((nb, D_MODEL)),
                   _whole((nb, D_MODEL)), _whole((nb, D_MODEL))],
        out_shape=[
            jax.ShapeDtypeStruct((D_MODEL, IN_WIDTH), BF16),
            jax.ShapeDtypeStruct((D_MODEL, D_MODEL), BF16),
            jax.ShapeDtypeStruct((nb * N_HEADS, KV_WIDTH), BF16),
            jax.ShapeDtypeStruct((nb, KV_WIDTH), F32),
            jax.ShapeDtypeStruct((nb, KV_WIDTH), F32),
            jax.ShapeDtypeStruct((nb, D_MODEL), F32),
            jax.ShapeDtypeStruct((nb, D_MODEL), BF16),
            jax.ShapeDtypeStruct((nb, D_MODEL), F32),
            jax.ShapeDtypeStruct((nb, D_MODEL), F32),
        ],
        scratch_shapes=[pltpu.VMEM((nb, D_MODEL), BF16), pltpu.VMEM((n_blocks, nb, W_STEP), F32)],
        compiler_params=_params(),
        name="sample_proj",
    )(xs2d, ln1, w_in_f32, sgu_g, sgu_w, sgu_b, sel)


def _sample_attn_stages(qsel_ref, knew_ref, vnew_ref, ck_ref, cv_ref, bias_ref, sink_ref, upper,
                        o_ref, nk_ref, nv_ref):
    bs = ck_ref.shape[0]
    assert knew_ref.shape[0] == 2 * bs
    row0 = jnp.where(upper, bs, 0)
    row_kvh = lax.broadcasted_iota(jnp.int32, (N_HEADS, KV_WIDTH), 0) % N_KV_HEADS
    lane_kvh = lax.broadcasted_iota(jnp.int32, (N_HEADS, KV_WIDTH), 1) // HEAD_DIM
    own = row_kvh == lane_kvh
    bias = bias_ref[...]
    head_row = lax.broadcasted_iota(jnp.int32, (N_HEADS, 1), 0)
    sink = jnp.zeros((N_HEADS, 1), F32)
    for h in range(N_HEADS):
        g, kvh = divmod(h, N_KV_HEADS)
        sink = jnp.where(head_row == h, sink_ref[0, kvh * GROUP + g], sink)

    qss = [qsel_ref[i * N_HEADS:(i + 1) * N_HEADS, :] for i in range(bs)]
    kns = [knew_ref[pl.ds(row0 + i, 1), :] for i in range(bs)]
    vws = [vnew_ref[pl.ds(row0 + i, 1), :] for i in range(bs)]
    scores = [_dot(qss[i], ck_ref[i].astype(BF16)) + bias for i in range(bs)]
    yield
    probs = []
    for i in range(bs):
        s = scores[i]
        s_new = jnp.sum(qss[i].astype(F32) * kns[i], axis=1, keepdims=True)
        m = jnp.maximum(jnp.maximum(jnp.max(s, axis=1, keepdims=True), s_new), sink)
        p = jnp.exp(s - m)
        p_new = jnp.exp(s_new - m)
        denom = jnp.sum(p, axis=1, keepdims=True) + p_new + jnp.exp(sink - m)
        probs.append((p.astype(BF16), p_new, denom))

    kn_t = knew_ref[...].T
    vw_t = vnew_ref[...].T
    last_lane = lax.broadcasted_iota(jnp.int32, (KV_WIDTH, WINDOW), 1) == WINDOW - 1
    for i in range(bs):
        kn_col = jnp.where(upper, kn_t[:, bs + i:bs + i + 1], kn_t[:, i:i + 1])
        vw_col = jnp.where(upper, vw_t[:, bs + i:bs + i + 1], vw_t[:, i:i + 1])
        nk_ref[i] = jnp.where(last_lane, kn_col, pltpu.roll(ck_ref[i], WINDOW - 1, 1))
        nv_ref[i] = jnp.where(last_lane, vw_col, pltpu.roll(cv_ref[i], WINDOW - 1, 1))
    yield
    for i in range(bs):
        p, p_new, denom = probs[i]
        o = (_dot_nt(p, cv_ref[i].astype(BF16)) + p_new * vws[i]) / denom
        o_ref[i * N_HEADS:(i + 1) * N_HEADS, :] = jnp.where(own, o, 0.0).astype(BF16)


def _head_perm(v):
    return v.reshape(N_KV_HEADS, GROUP).T.reshape(N_HEADS)


def _alibi_slopes():
    h = np.arange(1, N_HEADS + 1, dtype=np.float32)
    return np.exp2(-8.0 * h / N_HEADS).astype(np.float32)


def _selection_matrix(nb):
    r = np.arange(nb * N_HEADS)
    c = np.arange(GROUP * nb)
    same_sample = (r[:, None] // N_HEADS) == (c[None, :] % nb)
    same_member = ((r[:, None] % N_HEADS) // N_KV_HEADS) == (c[None, :] // nb)
    return (same_sample & same_member).astype(np.float32)


def kernel(x_prompt, x_sample, cache_k_win, cache_v_win, ln1_g, w_in, sgu_norm_g, sgu_w, sgu_b, attn_sinks,
           w_oa, w_ob, w_out, ln2_g, w_up, w_down, lnf_g):
    batch, seq, _ = x_prompt.shape
    dec_batch, dec_seq, _ = x_sample.shape
    depth = w_in.shape[0]
    assert depth == 1 and dec_seq == 1
    assert seq % TOKEN_BLOCK == 0 and TOKEN_BLOCK % CHUNK == 0
    assert (batch * seq) % FFN_BLOCK == 0
    assert w_in.shape[-1] == IN_WIDTH

    wi = w_in[0]
    w_ob_b = w_ob[0].reshape(N_KV_HEADS, GROUP, HEAD_DIM, D_MODEL).transpose(1, 0, 2, 3).reshape(
        D_MODEL, D_MODEL).astype(BF16)
    w_oa_b = w_oa[0].astype(BF16)
    w_out_b = w_out[0].astype(BF16)
    ln1 = ln1_g[0].reshape(1, D_MODEL)
    ln2 = ln2_g[0].reshape(1, D_MODEL)
    lnf = lnf_g.reshape(1, D_MODEL)
    sgu_g = sgu_norm_g[0].reshape(1, D_MODEL)
    slopes_p = _head_perm(_alibi_slopes())

    xs2d = x_sample.reshape(dec_batch, D_MODEL)
    sel_np = _selection_matrix(dec_batch)
    sel = jnp.asarray(sel_np, BF16)
    selt = jnp.asarray(sel_np.T, BF16)
    bias_s = -slopes_p[:, None] * (WINDOW - np.arange(WINDOW, dtype=np.float32))[None, :]

    w_in_b, w_q_b, qsel, knew, vnew, vn, a_s, ga_s, gb_s = _sample_proj(
        xs2d, ln1, wi, sgu_g, sgu_w[0], sgu_b[0], sel)
    def to_feature_major(c):
        return c[0].transpose(0, 2, 3, 1).reshape(dec_batch, KV_WIDTH, WINDOW)

    def from_feature_major(c):
        return c.reshape(c.shape[0], N_KV_HEADS, HEAD_DIM, WINDOW).transpose(0, 3, 1, 2)[None]

    steps = (batch * seq) // TOKEN_BLOCK
    per_step = dec_batch // steps

    x1, kwin, vwin, w_up_b, w_down_b, o_s, nk, nv = _mix_prompt(
        x_prompt.reshape(batch * seq, D_MODEL), ln1, w_in_b, w_q_b, sgu_g, sgu_w[0], sgu_b[0],
        attn_sinks, slopes_p * LOG2E, w_oa_b, w_ob_b, w_out_b, w_up[0], w_down[0],
        qsel.reshape(steps, per_step * N_HEADS, KV_WIDTH), knew, vnew,
        to_feature_major(cache_k_win), to_feature_major(cache_v_win), bias_s,
        batch=batch, seq=seq)

    y_prompt, y_sample = _ffn(x1, ln2, w_up_b, w_down_b, lnf,
                              o_s.reshape(dec_batch * N_HEADS, KV_WIDTH), selt, a_s, ga_s, gb_s, xs2d,
                              w_oa_b, w_ob_b, w_out_b)

    return (y_prompt.reshape(batch, seq, D_MODEL),
            y_sample.reshape(dec_batch, dec_seq, D_MODEL),
            from_feature_major(kwin), from_feature_major(vwin),
            from_feature_major(nk), from_feature_major(nv),
            vn.reshape(depth, dec_batch, dec_seq, D_MODEL))
```

```python
import functools
import math

import numpy as np
import jax
import jax.numpy as jnp
from jax import lax
from jax.experimental import pallas as pl
from jax.experimental.pallas import tpu as pltpu

D_MODEL = 1024
N_HEADS = 16
HEAD_DIM = 64
N_KV_HEADS = 4
GROUP = N_HEADS // N_KV_HEADS
KV_WIDTH = N_KV_HEADS * HEAD_DIM
WINDOW = 128
CHUNK = 128
SGU_GROUPS = 8
SGU_GROUP_DIM = D_MODEL // SGU_GROUPS
D_FF = 4 * D_MODEL
FF_SLAB = 1024
EPS = 1e-6
NEG_BIG = -1e30
ATTN_SCALE = HEAD_DIM ** -0.5
LOG2E = math.log2(math.e)

OFF_U, OFF_V, OFF_Q, OFF_K, OFF_VA, OFF_GA, OFF_GB, IN_WIDTH = 0, 1024, 2048, 3072, 3328, 3584, 4608, 5632
W_STEP = IN_WIDTH // 4

TOKEN_BLOCK = 512
FFN_BLOCK = 1024
SUBLANES = 8
VMEM_LIMIT_BYTES = 58 * 1024 * 1024

F32 = jnp.float32
BF16 = jnp.bfloat16


def _rmsnorm(x, g):
    ms = jnp.mean(x * x, axis=-1, keepdims=True)
    return x * lax.rsqrt(ms + EPS) * g


def _gelu_tanh(x):
    c = math.sqrt(2.0 / math.pi)
    return x * (0.5 * (1.0 + jnp.tanh(c * (x + 0.044715 * (x * x * x)))))


def _dot(a, b):
    return jnp.dot(a, b, preferred_element_type=F32)


def _dot_nt(a, b):
    return lax.dot_general(a, b, (((1,), (1,)), ((), ())), preferred_element_type=F32)


def _resident(shape):
    zeros = (0,) * len(shape)
    return pl.BlockSpec(shape, lambda *_: zeros, pipeline_mode=pl.Buffered(1))


def _whole(shape):
    zeros = (0,) * len(shape)
    return pl.BlockSpec(shape, lambda *_: zeros)


def _params():
    return pltpu.CompilerParams(dimension_semantics=("arbitrary",), vmem_limit_bytes=VMEM_LIMIT_BYTES)


def _mix_prompt_kernel(x_ref, ln1_ref, w_in_ref, w_q_ref, sgu_g_ref, sgu_w_ref, sgu_b_ref,
                       sink_ref, slope_ref, w_oa_ref, w_ob_ref, w_out_ref, w_up_blk_ref, w_down_blk_ref,
                       s_qsel_ref, s_knew_ref, s_vnew_ref, s_ck_ref, s_cv_ref, s_bias_ref,
                       x1_ref, kwin_ref, vwin_ref, w_up_bf_ref, w_down_bf_ref, s_o_ref, s_nk_ref, s_nv_ref,
                       qs_scr, kt_scr, vm_scr, kprev_scr, vprev_scr, vn_scr, u_scr, gate_scr, a_scr, b_scr, wt_scr,
                       bias_ref, bexp_ref,
                       *, steps_per_seq):
    step = pl.program_id(0)
    tb = x_ref.shape[0]
    nblk = tb // CHUNK
    first = (step % steps_per_seq) == 0
    rd = step % 2
    wr = 1 - rd

    @pl.when(step == 0)
    def _():
        row = lax.broadcasted_iota(jnp.int32, (CHUNK, CHUNK), 0)
        col = lax.broadcasted_iota(jnp.int32, (CHUNK, CHUNK), 1)
        b_t = sgu_b_ref[...].T
        for g in range(SGU_GROUPS):
            wt_scr[g] = jnp.where(row >= col, sgu_w_ref[g], 0.0).astype(BF16)
            bexp_ref[:, g * SGU_GROUP_DIM:(g + 1) * SGU_GROUP_DIM] = jnp.broadcast_to(
                b_t[:, g:g + 1], (CHUNK, SGU_GROUP_DIM))
        dist = (lax.broadcasted_iota(jnp.int32, (CHUNK, 2 * CHUNK), 0) + CHUNK
                - lax.broadcasted_iota(jnp.int32, (CHUNK, 2 * CHUNK), 1))
        in_band = jnp.logical_and(dist >= 0, dist <= WINDOW)
        dist_f = dist.astype(F32)
        for h in range(N_HEADS):
            bias_ref[h] = jnp.where(in_band, -slope_ref[h] * dist_f, NEG_BIG)
        kt_scr[...] = jnp.zeros(kt_scr.shape, BF16)
        vm_scr[...] = jnp.zeros(vm_scr.shape, BF16)
        kprev_scr[...] = jnp.zeros(kprev_scr.shape, BF16)
        vprev_scr[...] = jnp.zeros(vprev_scr.shape, BF16)

    @pl.when(first)
    def _():
        kprev_scr[rd] = jnp.zeros(kprev_scr.shape[1:], BF16)
        vprev_scr[rd] = jnp.zeros(vprev_scr.shape[1:], BF16)

    w_up_bf_ref[...] = w_up_blk_ref[...].astype(BF16)
    w_down_bf_ref[...] = w_down_blk_ref[...].astype(BF16)

    x = x_ref[...]
    quarters = [slice(i * tb // 4, (i + 1) * tb // 4) for i in range(4)]
    xn_parts = [_rmsnorm(x[p], ln1_ref[...]).astype(BF16) for p in quarters]
    xn = jnp.concatenate(xn_parts, axis=0)

    q = jnp.concatenate([_dot(xn_p, w_q_ref[...]) for xn_p in xn_parts], axis=0)
    k = _dot(xn, w_in_ref[:, OFF_K:OFF_VA])
    va = _dot(xn, w_in_ref[:, OFF_VA:OFF_GA])
    h_v = _dot(xn, w_in_ref[:, OFF_V:OFF_Q])
    h_u0 = _dot(xn, w_in_ref[:, OFF_U:OFF_U + D_MODEL // 2])

    q = (q * (ATTN_SCALE * LOG2E)).astype(BF16)
    for c in range(nblk):
        for g in range(GROUP):
            qs_scr[c, g * CHUNK:(g + 1) * CHUNK, :] = q[c * CHUNK:(c + 1) * CHUNK, g * KV_WIDTH:(g + 1) * KV_WIDTH]

    kt_f32 = k.T
    kwin_ref[...] = kt_f32[:, tb - WINDOW:]
    vwin_ref[...] = va[tb - WINDOW:, :].T
    kt = kt_f32.astype(BF16)
    vab = va.astype(BF16)
    for kvh in range(N_KV_HEADS):
        own = slice(kvh * HEAD_DIM, (kvh + 1) * HEAD_DIM)
        for c in range(nblk):
            kt_scr[kvh, c, own, :] = kt[own, c * CHUNK:(c + 1) * CHUNK]
        vm_scr[kvh, :, own] = vab[:, own]
        kprev_scr[wr, kvh, own, :] = kt[own, tb - WINDOW:]
        vprev_scr[wr, kvh, :, own] = vab[tb - WINDOW:, own]

    no_prev = jnp.where(
        jnp.logical_and(first, lax.broadcasted_iota(jnp.int32, (CHUNK, 2 * CHUNK), 1) < CHUNK), NEG_BIG, 0.0)

    def attn_scores(c):
        qs = qs_scr[c]
        out = []
        for kvh in range(N_KV_HEADS):
            k_prev = kprev_scr[rd, kvh] if c == 0 else kt_scr[kvh, c - 1]
            out.append(_dot(qs, jnp.concatenate([k_prev, kt_scr[kvh, c]], axis=1)))
        return out

    def attn_softmax(c, scores):
        out = []
        for kvh in range(N_KV_HEADS):
            ps = []
            for g in range(GROUP):
                h = g * N_KV_HEADS + kvh
                s = scores[kvh][g * CHUNK:(g + 1) * CHUNK, :] + bias_ref[h]
                if c == 0:
                    s = s + no_prev
                sink = sink_ref[0, kvh * GROUP + g] * LOG2E
                m = jnp.max(s, axis=1, keepdims=True)
                p = jnp.exp2(s - m)
                denom = jnp.sum(p, axis=1, keepdims=True) + jnp.exp2(sink - m)
                ps.append((p * (1.0 / denom)).astype(BF16))
            out.append(jnp.concatenate(ps, axis=0))
        return out

    def attn_values(c, probs):
        rows = slice(c * CHUNK, (c + 1) * CHUNK)
        acc = None
        for kvh in range(N_KV_HEADS):
            if c == 0:
                v_band = jnp.concatenate([vprev_scr[rd, kvh], vm_scr[kvh, 0:CHUNK, :]], axis=0)
            else:
                v_band = vm_scr[kvh, (c - 1) * CHUNK:(c + 1) * CHUNK, :]
            o = _dot(probs[kvh], v_band)
            acc = o if acc is None else acc + o
        for g in range(GROUP):
            b_scr[rows, g * KV_WIDTH:(g + 1) * KV_WIDTH] = acc[g * CHUNK:(g + 1) * CHUNK, :].astype(BF16)

    def sgu_chunk(c):
        rows = slice(c * CHUNK, (c + 1) * CHUNK)
        vn_c = vn_scr[rows, :]
        mixed = jnp.concatenate(
            [_dot(wt_scr[g], vn_c[:, g * SGU_GROUP_DIM:(g + 1) * SGU_GROUP_DIM]) for g in range(SGU_GROUPS)],
            axis=1) + bexp_ref[...]
        a_scr[rows, :] = (u_scr[rows, :] * mixed).astype(BF16)

    half = D_MODEL // 2

    def tail_u(h, lo):
        u_scr[:, lo:lo + half] = _gelu_tanh(h)

    def tail_ga(h, lo):
        gate_scr[0, :, lo:lo + half] = jax.nn.sigmoid(h)

    def tail_gb(h, lo):
        gate_scr[1, :, lo:lo + half] = jax.nn.sigmoid(h)

    fillers = [
        (lambda: _dot(xn, w_in_ref[:, OFF_U + half:OFF_V]), lambda h: tail_u(h, half)),
        (lambda: _dot(xn, w_in_ref[:, OFF_GA:OFF_GA + half]), lambda h: tail_ga(h, 0)),
        (lambda: _dot(xn, w_in_ref[:, OFF_GA + half:OFF_GB]), lambda h: tail_ga(h, half)),
        (lambda: _dot(xn, w_in_ref[:, OFF_GB:IN_WIDTH]),
         lambda h: (tail_gb(h[:, :half], 0), tail_gb(h[:, half:], half))),
    ]
    vn_scr[...] = _rmsnorm(_gelu_tanh(h_v), sgu_g_ref[...]).astype(BF16)
    tail_u(h_u0, 0)
    sgu_after = {nblk - 2: range(0, nblk // 2), nblk - 1: range(nblk // 2, nblk)}
    for c in range(nblk):
        scores = attn_scores(c)
        proj = fillers[c][0]() if c < len(fillers) else None
        for cc in sgu_after.get(c, ()):
            sgu_chunk(cc)
        probs = attn_softmax(c, scores)
        attn_values(c, probs)
        if proj is not None:
            fillers[c][1](proj)
    for matmul, tail in fillers[nblk:]:
        tail(matmul())

    sample = _sample_attn_stages(s_qsel_ref, s_knew_ref, s_vnew_ref, s_ck_ref, s_cv_ref, s_bias_ref, sink_ref,
                                 step % 2 == 1, s_o_ref, s_nk_ref, s_nv_ref)
    next(sample)
    branch_a = _dot(a_scr[...], w_oa_ref[...])
    next(sample)
    branch_b = _dot(b_scr[...], w_ob_ref[...])
    for _ in sample:
        pass
    hm = gate_scr[0] * branch_a + gate_scr[1] * branch_b
    x1_ref[...] = x + _dot(hm.astype(BF16), w_out_ref[...])


def _mix_prompt(x2d, ln1, w_in, w_q, sgu_g, sgu_w, sgu_b, sinks, slopes, w_oa, w_ob, w_out,
                w_up_f32, w_down_f32,
                s_qsel, s_knew, s_vnew, s_cache_k, s_cache_v, s_bias, *, batch, seq):
    n = x2d.shape[0]
    tb = TOKEN_BLOCK
    nblk = tb // CHUNK
    steps = n // tb
    steps_per_seq = seq // tb
    nb = s_cache_k.shape[0]
    per_step = nb // steps
    assert per_step * steps == nb and 2 * per_step == SUBLANES and s_knew.shape == (nb, KV_WIDTH)
    s_head_block = pl.BlockSpec((None, per_step * N_HEADS, KV_WIDTH), lambda i: (i, 0, 0))
    s_new_block = pl.BlockSpec((SUBLANES, KV_WIDTH), lambda i: (i // 2, 0))
    s_cache_block = pl.BlockSpec((per_step, KV_WIDTH, WINDOW), lambda i: (i, 0, 0))
    row_block = pl.BlockSpec((tb, D_MODEL), lambda i: (i, 0))
    win_block = pl.BlockSpec((None, KV_WIDTH, WINDOW), lambda i: (i // steps_per_seq, 0, 0))
    up_block = pl.BlockSpec((D_MODEL // steps, D_FF), lambda i: (i, 0))
    down_block = pl.BlockSpec((D_FF // steps, D_MODEL), lambda i: (i, 0))
    return pl.pallas_call(
        functools.partial(_mix_prompt_kernel, steps_per_seq=steps_per_seq),
        grid=(n // tb,),
        in_specs=[
            row_block,
            _resident((1, D_MODEL)),
            _resident((D_MODEL, IN_WIDTH)),
            _resident((D_MODEL, D_MODEL)),
            _resident((1, D_MODEL)),
            _resident((SGU_GROUPS, CHUNK, CHUNK)),
            _resident((SGU_GROUPS, CHUNK)),
            pl.BlockSpec(memory_space=pltpu.SMEM),
            pl.BlockSpec(memory_space=pltpu.SMEM),
            _resident((D_MODEL, D_MODEL)),
            _resident((D_MODEL, D_MODEL)),
            _resident((D_MODEL, D_MODEL)),
            up_block,
            down_block,
            s_head_block, s_new_block, s_new_block, s_cache_block, s_cache_block,
            _resident((N_HEADS, WINDOW)),
        ],
        out_specs=[row_block, win_block, win_block, up_block, down_block,
                   s_head_block, s_cache_block, s_cache_block],
        out_shape=[
            jax.ShapeDtypeStruct((n, D_MODEL), F32),
            jax.ShapeDtypeStruct((batch, KV_WIDTH, WINDOW), F32),
            jax.ShapeDtypeStruct((batch, KV_WIDTH, WINDOW), F32),
            jax.ShapeDtypeStruct((D_MODEL, D_FF), BF16),
            jax.ShapeDtypeStruct((D_FF, D_MODEL), BF16),
            jax.ShapeDtypeStruct((steps, per_step * N_HEADS, KV_WIDTH), BF16),
            jax.ShapeDtypeStruct((nb, KV_WIDTH, WINDOW), F32),
            jax.ShapeDtypeStruct((nb, KV_WIDTH, WINDOW), F32),
        ],
        scratch_shapes=[
            pltpu.VMEM((nblk, GROUP * CHUNK, KV_WIDTH), BF16),
            pltpu.VMEM((N_KV_HEADS, nblk, KV_WIDTH, CHUNK), BF16),
            pltpu.VMEM((N_KV_HEADS, tb, KV_WIDTH), BF16),
            pltpu.VMEM((2, N_KV_HEADS, KV_WIDTH, CHUNK), BF16),
            pltpu.VMEM((2, N_KV_HEADS, WINDOW, KV_WIDTH), BF16),
            pltpu.VMEM((tb, D_MODEL), BF16),
            pltpu.VMEM((tb, D_MODEL), F32),
            pltpu.VMEM((2, tb, D_MODEL), F32),
            pltpu.VMEM((tb, D_MODEL), BF16),
            pltpu.VMEM((tb, D_MODEL), BF16),
            pltpu.VMEM((SGU_GROUPS, CHUNK, CHUNK), BF16),
            pltpu.VMEM((N_HEADS, CHUNK, 2 * CHUNK), F32),
            pltpu.VMEM((CHUNK, D_MODEL), F32),
        ],
        compiler_params=_params(),
        name="mix_prompt",
    )(x2d, ln1, w_in, w_q, sgu_g, sgu_w, sgu_b, sinks, slopes, w_oa, w_ob, w_out, w_up_f32, w_down_f32,
      s_qsel, s_knew, s_vnew, s_cache_k, s_cache_v, s_bias)


def _ffn_rows(x, y_ref, ln2_ref, w_up_ref, w_down_ref, lnf_ref, *, row_parts):
    n_slabs = D_FF // FF_SLAB
    m = x.shape[0]
    parts = [slice(i * m // row_parts, (i + 1) * m // row_parts) for i in range(row_parts)]

    def up(j, xn):
        return _dot(xn, w_up_ref[:, j * FF_SLAB:(j + 1) * FF_SLAB])

    xns = [_rmsnorm(x[p], ln2_ref[...]).astype(BF16) for p in parts]
    h_next = jnp.concatenate([up(0, xn_p) for xn_p in xns], axis=0) if row_parts > 1 else up(0, xns[0])
    xn = jnp.concatenate(xns, axis=0) if row_parts > 1 else xns[0]

    acc = x
    for j in range(n_slabs):
        h = h_next
        if j + 1 < n_slabs:
            h_next = up(j + 1, xn)
        h = jnp.square(jnp.maximum(h, 0.0)).astype(BF16)
        w_d = w_down_ref[j * FF_SLAB:(j + 1) * FF_SLAB, :]
        if j + 1 < n_slabs:
            acc = acc + _dot(h, w_d)
        else:
            for p in parts:
                y_ref[p, :] = _rmsnorm(acc[p] + _dot(h[p], w_d), lnf_ref[...])


def _ffn_kernel(x_ref, ln2_ref, w_up_ref, w_down_ref, lnf_ref,
                s_o_ref, s_selt_ref, s_a_ref, s_ga_ref, s_gb_ref, s_x_ref, w_oa_ref, w_ob_ref, w_out_ref,
                y_ref, ys_ref):
    i = pl.program_id(0)
    last = pl.num_programs(0) - 1

    @pl.when(i < last)
    def _():
        _ffn_rows(x_ref[...], y_ref, ln2_ref, w_up_ref, w_down_ref, lnf_ref, row_parts=4)

    @pl.when(i == last)
    def _():
        nb = s_x_ref.shape[0]
        bst = _dot(s_selt_ref[...], s_o_ref[...]).astype(BF16)
        ob = _dot(bst[0:nb, :], w_ob_ref[0:KV_WIDTH, :])
        for g in range(1, GROUP):
            ob = ob + _dot(bst[g * nb:(g + 1) * nb, :], w_ob_ref[g * KV_WIDTH:(g + 1) * KV_WIDTH, :])
        hm = s_ga_ref[...] * _dot(s_a_ref[...], w_oa_ref[...]) + s_gb_ref[...] * ob
        xs1 = s_x_ref[...] + _dot(hm.astype(BF16), w_out_ref[...])
        _ffn_rows(xs1, ys_ref, ln2_ref, w_up_ref, w_down_ref, lnf_ref, row_parts=1)


def _ffn(x2d, ln2, w_up, w_down, lnf, s_o, s_selt, s_a, s_ga, s_gb, xs2d, w_oa, w_ob, w_out):
    n = x2d.shape[0]
    nb = xs2d.shape[0]
    n_prompt_steps = n // FFN_BLOCK
    row_block = pl.BlockSpec((FFN_BLOCK, D_MODEL), lambda i: (jnp.minimum(i, n_prompt_steps - 1), 0))
    return pl.pallas_call(
        _ffn_kernel,
        grid=(n_prompt_steps + 1,),
        in_specs=[row_block, _resident((1, D_MODEL)), _resident((D_MODEL, D_FF)),
                  _resident((D_FF, D_MODEL)), _resident((1, D_MODEL)),
                  _resident((nb * N_HEADS, KV_WIDTH)), _resident((GROUP * nb, nb * N_HEADS)),
                  _resident((nb, D_MODEL)), _resident((nb, D_MODEL)), _resident((nb, D_MODEL)),
                  _resident((nb, D_MODEL)),
                  _resident((D_MODEL, D_MODEL)), _resident((D_MODEL, D_MODEL)), _resident((D_MODEL, D_MODEL))],
        out_specs=[row_block, _whole((nb, D_MODEL))],
        out_shape=[jax.ShapeDtypeStruct((n, D_MODEL), F32), jax.ShapeDtypeStruct((nb, D_MODEL), F32)],
        compiler_params=_params(),
        name="ffn",
    )(x2d, ln2, w_up, w_down, lnf, s_o, s_selt, s_a, s_ga, s_gb, xs2d, w_oa, w_ob, w_out)


def _sample_proj_kernel(x_ref, ln1_ref, w_blk_ref, sgu_g_ref, sgu_w_ref, sgu_b_ref, sel_ref,
                        w_oa_blk_ref, w_ob_blk_ref, w_out_blk_ref,
                        w_bf_ref, w_q_bf_ref, qsel_ref, knew_ref, vnew_ref, vn_ref, a_ref,
                        ga_ref, gb_ref, w_oa_bf_ref, w_ob_bf_ref, w_out_bf_ref, xn_scr, h_scr):
    j = pl.program_id(0)
    nb = x_ref.shape[0]
    n_steps = IN_WIDTH // W_STEP

    @pl.when(j == 0)
    def _():
        xn_scr[...] = _rmsnorm(x_ref[...], ln1_ref[...]).astype(BF16)

    w_oa_bf_ref[...] = w_oa_blk_ref[...].astype(BF16)
    w_out_bf_ref[...] = w_out_blk_ref[...].astype(BF16)
    assert w_ob_blk_ref.shape[0] == GROUP * HEAD_DIM and n_steps == N_KV_HEADS
    for g in range(GROUP):
        dst = pl.multiple_of(g * KV_WIDTH + j * HEAD_DIM, HEAD_DIM)
        w_ob_bf_ref[pl.ds(dst, HEAD_DIM), :] = w_ob_blk_ref[g * HEAD_DIM:(g + 1) * HEAD_DIM, :].astype(BF16)

    wb = w_blk_ref[...].astype(BF16)
    w_bf_ref[...] = wb
    h_scr[j] = _dot(xn_scr[...], wb)

    assert W_STEP % HEAD_DIM == 0
    for step in range(n_steps):
        heads = [h for h in range(N_HEADS) if step * W_STEP <= OFF_Q + h * HEAD_DIM < (step + 1) * W_STEP]
        if heads:
            @pl.when(j == step)
            def _(step=step, heads=heads):
                for head in heads:
                    src = OFF_Q + head * HEAD_DIM - step * W_STEP
                    kvh, g = divmod(head, GROUP)
                    dst = g * KV_WIDTH + kvh * HEAD_DIM
                    w_q_bf_ref[:, dst:dst + HEAD_DIM] = wb[:, src:src + HEAD_DIM]

    @pl.when(j == n_steps - 1)
    def _():
        def cols(lo, hi):
            pieces = []
            for step in range(n_steps):
                a, b = max(lo, step * W_STEP), min(hi, (step + 1) * W_STEP)
                if a < b:
                    pieces.append(h_scr[step, :, a - step * W_STEP:b - step * W_STEP])
            return pieces[0] if len(pieces) == 1 else jnp.concatenate(pieces, axis=1)

        u = _gelu_tanh(cols(OFF_U, OFF_V))
        v = _gelu_tanh(cols(OFF_V, OFF_Q))
        vn = _rmsnorm(v, sgu_g_ref[...])
        vn_ref[...] = vn
        def over_groups(entry):
            return jnp.concatenate(
                [jnp.broadcast_to(entry(g), (1, SGU_GROUP_DIM)) for g in range(SGU_GROUPS)], axis=1)

        w_diag = over_groups(lambda g: sgu_w_ref[g, 0:1, 0:1])
        b_first = over_groups(lambda g: sgu_b_ref[g:g + 1, 0:1])
        a_ref[...] = (u * (vn * w_diag + b_first)).astype(BF16)
        knew_ref[...] = cols(OFF_K, OFF_VA)
        vnew_ref[...] = cols(OFF_VA, OFF_GA)
        ga_ref[...] = jax.nn.sigmoid(cols(OFF_GA, OFF_GB))
        gb_ref[...] = jax.nn.sigmoid(cols(OFF_GB, IN_WIDTH))
        q = _dot(xn_scr[...], w_q_bf_ref[...]) * ATTN_SCALE
        qstack = jnp.concatenate([q[:, g * KV_WIDTH:(g + 1) * KV_WIDTH] for g in range(GROUP)], axis=0).astype(BF16)
        qrep = _dot(sel_ref[...], qstack)
        row_kvh = lax.broadcasted_iota(jnp.int32, (nb * N_HEADS, KV_WIDTH), 0) % N_KV_HEADS
        lane_kvh = lax.broadcasted_iota(jnp.int32, (nb * N_HEADS, KV_WIDTH), 1) // HEAD_DIM
        qsel_ref[...] = jnp.where(row_kvh == lane_kvh, qrep, 0.0).astype(BF16)


def _sample_proj(xs2d, ln1, w_in_f32, sgu_g, sgu_w, sgu_b, sel, w_oa_f32, w_ob_f32, w_out_f32):
    nb = xs2d.shape[0]
    n_blocks = IN_WIDTH // W_STEP
    w_block = pl.BlockSpec((D_MODEL, W_STEP), lambda j: (0, j))
    merge_block = pl.BlockSpec((D_MODEL // n_blocks, D_MODEL), lambda j: (j, 0))
    merge_shape = jax.ShapeDtypeStruct((D_MODEL, D_MODEL), BF16)
    return pl.pallas_call(
        _sample_proj_kernel,
        grid=(n_blocks,),
        in_specs=[_whole((nb, D_MODEL)), _whole((1, D_MODEL)), w_block,
                  _whole((1, D_MODEL)),
                  pl.BlockSpec((SGU_GROUPS, SUBLANES, CHUNK), lambda j: (0, 0, 0)),
                  _whole((SGU_GROUPS, CHUNK)),
                  _resident((nb * N_HEADS, GROUP * nb)),
                  merge_block, merge_block, merge_block],
        out_specs=[w_block, _whole((D_MODEL, D_MODEL)),
                   _whole((nb * N_HEADS, KV_WIDTH)), _whole((nb, KV_WIDTH)), _whole((nb, KV_WIDTH)),
                   _whole((nb, D_MODEL)), _whole((nb, D_MODEL)),
                   _whole((nb, D_MODEL)), _whole((nb, D_MODEL)),
                   merge_block, _whole((D_MODEL, D_MODEL)), merge_block],
        out_shape=[
            jax.ShapeDtypeStruct((D_MODEL, IN_WIDTH), BF16),
            jax.ShapeDtypeStruct((D_MODEL, D_MODEL), BF16),
            jax.ShapeDtypeStruct((nb * N_HEADS, KV_WIDTH), BF16),
            jax.ShapeDtypeStruct((nb, KV_WIDTH), F32),
            jax.ShapeDtypeStruct((nb, KV_WIDTH), F32),
            jax.ShapeDtypeStruct((nb, D_MODEL), F32),
            jax.ShapeDtypeStruct((nb, D_MODEL), BF16),
            jax.ShapeDtypeStruct((nb, D_MODEL), F32),
            jax.ShapeDtypeStruct((nb, D_MODEL), F32),
            merge_shape, merge_shape, merge_shape,
        ],
        scratch_shapes=[pltpu.VMEM((nb, D_MODEL), BF16), pltpu.VMEM((n_blocks, nb, W_STEP), F32)],
        compiler_params=_params(),
        name="sample_proj",
    )(xs2d, ln1, w_in_f32, sgu_g, sgu_w, sgu_b, sel, w_oa_f32, w_ob_f32, w_out_f32)


def _sample_attn_stages(qsel_ref, knew_ref, vnew_ref, ck_ref, cv_ref, bias_ref, sink_ref, upper,
                        o_ref, nk_ref, nv_ref):
    bs = ck_ref.shape[0]
    assert knew_ref.shape[0] == 2 * bs
    row0 = jnp.where(upper, bs, 0)
    row_kvh = lax.broadcasted_iota(jnp.int32, (N_HEADS, KV_WIDTH), 0) % N_KV_HEADS
    lane_kvh = lax.broadcasted_iota(jnp.int32, (N_HEADS, KV_WIDTH), 1) // HEAD_DIM
    own = row_kvh == lane_kvh
    bias = bias_ref[...]
    head_row = lax.broadcasted_iota(jnp.int32, (N_HEADS, 1), 0)
    sink = jnp.zeros((N_HEADS, 1), F32)
    for h in range(N_HEADS):
        g, kvh = divmod(h, N_KV_HEADS)
        sink = jnp.where(head_row == h, sink_ref[0, kvh * GROUP + g], sink)

    qss = [qsel_ref[i * N_HEADS:(i + 1) * N_HEADS, :] for i in range(bs)]
    kns = [knew_ref[pl.ds(row0 + i, 1), :] for i in range(bs)]
    vws = [vnew_ref[pl.ds(row0 + i, 1), :] for i in range(bs)]
    scores = [_dot(qss[i], ck_ref[i].astype(BF16)) + bias for i in range(bs)]
    yield
    probs = []
    for i in range(bs):
        s = scores[i]
        s_new = jnp.sum(qss[i].astype(F32) * kns[i], axis=1, keepdims=True)
        m = jnp.maximum(jnp.maximum(jnp.max(s, axis=1, keepdims=True), s_new), sink)
        p = jnp.exp(s - m)
        p_new = jnp.exp(s_new - m)
        denom = jnp.sum(p, axis=1, keepdims=True) + p_new + jnp.exp(sink - m)
        probs.append((p.astype(BF16), p_new, denom))

    kn_t = knew_ref[...].T
    vw_t = vnew_ref[...].T
    last_lane = lax.broadcasted_iota(jnp.int32, (KV_WIDTH, WINDOW), 1) == WINDOW - 1
    for i in range(bs):
        kn_col = jnp.where(upper, kn_t[:, bs + i:bs + i + 1], kn_t[:, i:i + 1])
        vw_col = jnp.where(upper, vw_t[:, bs + i:bs + i + 1], vw_t[:, i:i + 1])
        nk_ref[i] = jnp.where(last_lane, kn_col, pltpu.roll(ck_ref[i], WINDOW - 1, 1))
        nv_ref[i] = jnp.where(last_lane, vw_col, pltpu.roll(cv_ref[i], WINDOW - 1, 1))
    yield
    for i in range(bs):
        p, p_new, denom = probs[i]
        o = (_dot_nt(p, cv_ref[i].astype(BF16)) + p_new * vws[i]) / denom
        o_ref[i * N_HEADS:(i + 1) * N_HEADS, :] = jnp.where(own, o, 0.0).astype(BF16)


def _head_perm(v):
    return v.reshape(N_KV_HEADS, GROUP).T.reshape(N_HEADS)


def _alibi_slopes():
    h = np.arange(1, N_HEADS + 1, dtype=np.float32)
    return np.exp2(-8.0 * h / N_HEADS).astype(np.float32)


def _selection_matrix(nb):
    r = np.arange(nb * N_HEADS)
    c = np.arange(GROUP * nb)
    same_sample = (r[:, None] // N_HEADS) == (c[None, :] % nb)
    same_member = ((r[:, None] % N_HEADS) // N_KV_HEADS) == (c[None, :] // nb)
    return (same_sample & same_member).astype(np.float32)


def kernel(x_prompt, x_sample, cache_k_win, cache_v_win, ln1_g, w_in, sgu_norm_g, sgu_w, sgu_b, attn_sinks,
           w_oa, w_ob, w_out, ln2_g, w_up, w_down, lnf_g):
    batch, seq, _ = x_prompt.shape
    dec_batch, dec_seq, _ = x_sample.shape
    depth = w_in.shape[0]
    assert depth == 1 and dec_seq == 1
    assert seq % TOKEN_BLOCK == 0 and TOKEN_BLOCK % CHUNK == 0
    assert (batch * seq) % FFN_BLOCK == 0
    assert w_in.shape[-1] == IN_WIDTH

    ln1 = ln1_g[0].reshape(1, D_MODEL)
    ln2 = ln2_g[0].reshape(1, D_MODEL)
    lnf = lnf_g.reshape(1, D_MODEL)
    sgu_g = sgu_norm_g[0].reshape(1, D_MODEL)
    slopes_p = _head_perm(_alibi_slopes())

    xs2d = x_sample.reshape(dec_batch, D_MODEL)
    sel_np = _selection_matrix(dec_batch)
    sel = jnp.asarray(sel_np, BF16)
    selt = jnp.asarray(sel_np.T, BF16)
    bias_s = -slopes_p[:, None] * (WINDOW - np.arange(WINDOW, dtype=np.float32))[None, :]

    w_in_b, w_q_b, qsel, knew, vnew, vn, a_s, ga_s, gb_s, w_oa_b, w_ob_b, w_out_b = _sample_proj(
        xs2d, ln1, w_in[0], sgu_g, sgu_w[0], sgu_b[0], sel, w_oa[0], w_ob[0], w_out[0])
    def to_feature_major(c):
        return c[0].transpose(0, 2, 3, 1).reshape(dec_batch, KV_WIDTH, WINDOW)

    def from_feature_major(c):
        return c.reshape(c.shape[0], N_KV_HEADS, HEAD_DIM, WINDOW).transpose(0, 3, 1, 2)[None]

    steps = (batch * seq) // TOKEN_BLOCK
    per_step = dec_batch // steps

    x1, kwin, vwin, w_up_b, w_down_b, o_s, nk, nv = _mix_prompt(
        x_prompt.reshape(batch * seq, D_MODEL), ln1, w_in_b, w_q_b, sgu_g, sgu_w[0], sgu_b[0],
        attn_sinks, slopes_p * LOG2E, w_oa_b, w_ob_b, w_out_b, w_up[0], w_down[0],
        qsel.reshape(steps, per_step * N_HEADS, KV_WIDTH), knew, vnew,
        to_feature_major(cache_k_win), to_feature_major(cache_v_win), bias_s,
        batch=batch, seq=seq)

    y_prompt, y_sample = _ffn(x1, ln2, w_up_b, w_down_b, lnf,
                              o_s.reshape(dec_batch * N_HEADS, KV_WIDTH), selt, a_s, ga_s, gb_s, xs2d,
                              w_oa_b, w_ob_b, w_out_b)

    return (y_prompt.reshape(batch, seq, D_MODEL),
            y_sample.reshape(dec_batch, dec_seq, D_MODEL),
            from_feature_major(kwin), from_feature_major(vwin),
            from_feature_major(nk), from_feature_major(nv),
            vn.reshape(depth, dec_batch, dec_seq, D_MODEL))
```

```python
import functools
import math

import numpy as np
import jax
import jax.numpy as jnp
from jax import lax
from jax.experimental import pallas as pl
from jax.experimental.pallas import tpu as pltpu

D_MODEL = 1024
N_HEADS = 16
HEAD_DIM = 64
N_KV_HEADS = 4
GROUP = N_HEADS // N_KV_HEADS
KV_WIDTH = N_KV_HEADS * HEAD_DIM
WINDOW = 128
CHUNK = 128
SGU_GROUPS = 8
SGU_GROUP_DIM = D_MODEL // SGU_GROUPS
D_FF = 4 * D_MODEL
FF_SLAB = 1024
EPS = 1e-6
NEG_BIG = -1e30
ATTN_SCALE = HEAD_DIM ** -0.5
LOG2E = math.log2(math.e)

OFF_U, OFF_V, OFF_Q, OFF_K, OFF_VA, OFF_GA, OFF_GB, IN_WIDTH = 0, 1024, 2048, 3072, 3328, 3584, 4608, 5632
W_STEP = IN_WIDTH // 4

TOKEN_BLOCK = 512
FFN_BLOCK = 1024
SUBLANES = 8
LANES = 128
VMEM_LIMIT_BYTES = 58 * 1024 * 1024

F32 = jnp.float32
BF16 = jnp.bfloat16


def _rmsnorm(x, g):
    ms = jnp.mean(x * x, axis=-1, keepdims=True)
    return x * lax.rsqrt(ms + EPS) * g


def _gelu_tanh(x):
    c = math.sqrt(2.0 / math.pi)
    return x * (0.5 * (1.0 + jnp.tanh(c * (x + 0.044715 * (x * x * x)))))


def _dot(a, b):
    return jnp.dot(a, b, preferred_element_type=F32)


def _dot_nt(a, b):
    return lax.dot_general(a, b, (((1,), (1,)), ((), ())), preferred_element_type=F32)


def _resident(shape):
    zeros = (0,) * len(shape)
    return pl.BlockSpec(shape, lambda *_: zeros, pipeline_mode=pl.Buffered(1))


def _whole(shape):
    zeros = (0,) * len(shape)
    return pl.BlockSpec(shape, lambda *_: zeros)


def _params():
    return pltpu.CompilerParams(dimension_semantics=("arbitrary",), vmem_limit_bytes=VMEM_LIMIT_BYTES)


def _mix_prompt_kernel(x_ref, ln1_ref, w_in_ref, w_q_ref, sgu_g_ref, sgu_w_ref, sgu_b_ref,
                       sink_ref, slope_ref, w_oa_ref, w_ob_ref, w_out_ref, w_up_blk_ref, w_down_blk_ref,
                       s_qsel_ref, s_knew_ref, s_vnew_ref, s_ck_ref, s_cv_ref, s_bias_ref,
                       x1_ref, kwin_ref, vwin_ref, w_up_bf_ref, w_down_bf_ref, s_o_ref, s_nk_ref, s_nv_ref,
                       qs_scr, kt_scr, vm_scr, kprev_scr, vprev_scr, vn_scr, u_scr, gate_scr, a_scr, b_scr, wt_scr,
                       bias_ref, bexp_ref,
                       *, steps_per_seq):
    step = pl.program_id(0)
    tb = x_ref.shape[0]
    nblk = tb // CHUNK
    first = (step % steps_per_seq) == 0
    rd = step % 2
    wr = 1 - rd

    @pl.when(step == 0)
    def _():
        row = lax.broadcasted_iota(jnp.int32, (CHUNK, CHUNK), 0)
        col = lax.broadcasted_iota(jnp.int32, (CHUNK, CHUNK), 1)
        b_t = sgu_b_ref[...].T
        for g in range(SGU_GROUPS):
            wt_scr[g] = jnp.where(row >= col, sgu_w_ref[g], 0.0).astype(BF16)
            bexp_ref[:, g * SGU_GROUP_DIM:(g + 1) * SGU_GROUP_DIM] = jnp.broadcast_to(
                b_t[:, g:g + 1], (CHUNK, SGU_GROUP_DIM))
        dist = (lax.broadcasted_iota(jnp.int32, (CHUNK, 2 * CHUNK), 0) + CHUNK
                - lax.broadcasted_iota(jnp.int32, (CHUNK, 2 * CHUNK), 1))
        in_band = jnp.logical_and(dist >= 0, dist <= WINDOW)
        dist_f = dist.astype(F32)
        for h in range(N_HEADS):
            bias_ref[h] = jnp.where(in_band, -slope_ref[h] * dist_f, NEG_BIG)
        kt_scr[...] = jnp.zeros(kt_scr.shape, BF16)
        vm_scr[...] = jnp.zeros(vm_scr.shape, BF16)
        kprev_scr[...] = jnp.zeros(kprev_scr.shape, BF16)
        vprev_scr[...] = jnp.zeros(vprev_scr.shape, BF16)

    @pl.when(first)
    def _():
        kprev_scr[rd] = jnp.zeros(kprev_scr.shape[1:], BF16)
        vprev_scr[rd] = jnp.zeros(vprev_scr.shape[1:], BF16)

    w_up_bf_ref[...] = w_up_blk_ref[...].astype(BF16)
    w_down_bf_ref[...] = w_down_blk_ref[...].astype(BF16)

    x = x_ref[...]
    quarters = [slice(i * tb // 4, (i + 1) * tb // 4) for i in range(4)]
    xn_parts = [_rmsnorm(x[p], ln1_ref[...]).astype(BF16) for p in quarters]
    xn = jnp.concatenate(xn_parts, axis=0)

    q = jnp.concatenate([_dot(xn_p, w_q_ref[...]) for xn_p in xn_parts], axis=0)
    k = _dot(xn, w_in_ref[:, OFF_K:OFF_VA])
    va = _dot(xn, w_in_ref[:, OFF_VA:OFF_GA])
    h_v = _dot(xn, w_in_ref[:, OFF_V:OFF_Q])
    h_u0 = _dot(xn, w_in_ref[:, OFF_U:OFF_U + D_MODEL // 2])

    q = (q * (ATTN_SCALE * LOG2E)).astype(BF16)
    for c in range(nblk):
        for g in range(GROUP):
            qs_scr[c, g * CHUNK:(g + 1) * CHUNK, :] = q[c * CHUNK:(c + 1) * CHUNK, g * KV_WIDTH:(g + 1) * KV_WIDTH]

    kt_f32 = k.T
    kwin_ref[...] = kt_f32[:, tb - WINDOW:]
    vwin_ref[...] = va[tb - WINDOW:, :].T
    kt = kt_f32.astype(BF16)
    vab = va.astype(BF16)
    for kvh in range(N_KV_HEADS):
        own = slice(kvh * HEAD_DIM, (kvh + 1) * HEAD_DIM)
        for c in range(nblk):
            kt_scr[kvh, c, own, :] = kt[own, c * CHUNK:(c + 1) * CHUNK]
        vm_scr[kvh, :, own] = vab[:, own]
        kprev_scr[wr, kvh, own, :] = kt[own, tb - WINDOW:]
        vprev_scr[wr, kvh, :, own] = vab[tb - WINDOW:, own]

    no_prev = jnp.where(
        jnp.logical_and(first, lax.broadcasted_iota(jnp.int32, (CHUNK, 2 * CHUNK), 1) < CHUNK), NEG_BIG, 0.0)

    def attn_scores(c):
        qs = qs_scr[c]
        out = []
        for kvh in range(N_KV_HEADS):
            k_prev = kprev_scr[rd, kvh] if c == 0 else kt_scr[kvh, c - 1]
            out.append(_dot(qs, jnp.concatenate([k_prev, kt_scr[kvh, c]], axis=1)))
        return out

    def attn_softmax(c, scores):
        out = []
        for kvh in range(N_KV_HEADS):
            ps = []
            for g in range(GROUP):
                h = g * N_KV_HEADS + kvh
                s = scores[kvh][g * CHUNK:(g + 1) * CHUNK, :] + bias_ref[h]
                if c == 0:
                    s = s + no_prev
                sink = sink_ref[0, kvh * GROUP + g] * LOG2E
                m = jnp.max(s, axis=1, keepdims=True)
                p = jnp.exp2(s - m)
                denom = jnp.sum(p, axis=1, keepdims=True) + jnp.exp2(sink - m)
                ps.append((p * (1.0 / denom)).astype(BF16))
            out.append(jnp.concatenate(ps, axis=0))
        return out

    def attn_values(c, probs):
        rows = slice(c * CHUNK, (c + 1) * CHUNK)
        acc = None
        for kvh in range(N_KV_HEADS):
            if c == 0:
                v_band = jnp.concatenate([vprev_scr[rd, kvh], vm_scr[kvh, 0:CHUNK, :]], axis=0)
            else:
                v_band = vm_scr[kvh, (c - 1) * CHUNK:(c + 1) * CHUNK, :]
            o = _dot(probs[kvh], v_band)
            acc = o if acc is None else acc + o
        for g in range(GROUP):
            b_scr[rows, g * KV_WIDTH:(g + 1) * KV_WIDTH] = acc[g * CHUNK:(g + 1) * CHUNK, :].astype(BF16)

    def sgu_chunk(c):
        rows = slice(c * CHUNK, (c + 1) * CHUNK)
        vn_c = vn_scr[rows, :]
        mixed = jnp.concatenate(
            [_dot(wt_scr[g], vn_c[:, g * SGU_GROUP_DIM:(g + 1) * SGU_GROUP_DIM]) for g in range(SGU_GROUPS)],
            axis=1) + bexp_ref[...]
        a_scr[rows, :] = (u_scr[rows, :] * mixed).astype(BF16)

    half = D_MODEL // 2

    def tail_u(h, lo):
        u_scr[:, lo:lo + half] = _gelu_tanh(h)

    def tail_ga(h, lo):
        gate_scr[0, :, lo:lo + half] = jax.nn.sigmoid(h)

    def tail_gb(h, lo):
        gate_scr[1, :, lo:lo + half] = jax.nn.sigmoid(h)

    fillers = [
        (lambda: _dot(xn, w_in_ref[:, OFF_U + half:OFF_V]), lambda h: tail_u(h, half)),
        (lambda: _dot(xn, w_in_ref[:, OFF_GA:OFF_GA + half]), lambda h: tail_ga(h, 0)),
        (lambda: _dot(xn, w_in_ref[:, OFF_GA + half:OFF_GB]), lambda h: tail_ga(h, half)),
        (lambda: _dot(xn, w_in_ref[:, OFF_GB:IN_WIDTH]),
         lambda h: (tail_gb(h[:, :half], 0), tail_gb(h[:, half:], half))),
    ]
    vn_scr[...] = _rmsnorm(_gelu_tanh(h_v), sgu_g_ref[...]).astype(BF16)
    tail_u(h_u0, 0)
    sgu_after = {nblk - 2: range(0, nblk // 2), nblk - 1: range(nblk // 2, nblk)}
    for c in range(nblk):
        scores = attn_scores(c)
        proj = fillers[c][0]() if c < len(fillers) else None
        for cc in sgu_after.get(c, ()):
            sgu_chunk(cc)
        probs = attn_softmax(c, scores)
        attn_values(c, probs)
        if proj is not None:
            fillers[c][1](proj)
    for matmul, tail in fillers[nblk:]:
        tail(matmul())

    sample = _sample_attn_stages(s_qsel_ref, s_knew_ref, s_vnew_ref, s_ck_ref, s_cv_ref, s_bias_ref, sink_ref,
                                 step % 2 == 1, s_o_ref, s_nk_ref, s_nv_ref)
    next(sample)
    branch_a = _dot(a_scr[...], w_oa_ref[...])
    next(sample)
    branch_b = _dot(b_scr[...], w_ob_ref[...])
    for _ in sample:
        pass
    hm = gate_scr[0] * branch_a + gate_scr[1] * branch_b
    x1_ref[...] = x + _dot(hm.astype(BF16), w_out_ref[...])


def _mix_prompt(x2d, ln1, w_in, w_q, sgu_g, sgu_w, sgu_b, sinks, slopes, w_oa, w_ob, w_out,
                w_up_f32, w_down_f32,
                s_qsel, s_knew, s_vnew, s_cache_k, s_cache_v, s_bias, *, batch, seq):
    n = x2d.shape[0]
    tb = TOKEN_BLOCK
    nblk = tb // CHUNK
    steps = n // tb
    steps_per_seq = seq // tb
    nb = s_cache_k.shape[0]
    per_step = nb // steps
    assert per_step * steps == nb and 2 * per_step == SUBLANES and s_knew.shape == (nb, KV_WIDTH)
    s_head_block = pl.BlockSpec((None, per_step * N_HEADS, KV_WIDTH), lambda i: (i, 0, 0))
    s_new_block = pl.BlockSpec((SUBLANES, KV_WIDTH), lambda i: (i // 2, 0))
    s_cache_block = pl.BlockSpec((per_step, KV_WIDTH, WINDOW), lambda i: (i, 0, 0))
    row_block = pl.BlockSpec((tb, D_MODEL), lambda i: (i, 0))
    win_block = pl.BlockSpec((None, KV_WIDTH, WINDOW), lambda i: (i // steps_per_seq, 0, 0))
    up_block = pl.BlockSpec((D_MODEL // steps, D_FF), lambda i: (i, 0))
    down_block = pl.BlockSpec((D_FF // steps, D_MODEL), lambda i: (i, 0))
    return pl.pallas_call(
        functools.partial(_mix_prompt_kernel, steps_per_seq=steps_per_seq),
        grid=(n // tb,),
        in_specs=[
            row_block,
            _resident((1, D_MODEL)),
            _resident((D_MODEL, IN_WIDTH)),
            _resident((D_MODEL, D_MODEL)),
            _resident((1, D_MODEL)),
            _resident((SGU_GROUPS, CHUNK, CHUNK)),
            _resident((SGU_GROUPS, CHUNK)),
            pl.BlockSpec(memory_space=pltpu.SMEM),
            pl.BlockSpec(memory_space=pltpu.SMEM),
            _resident((D_MODEL, D_MODEL)),
            _resident((D_MODEL, D_MODEL)),
            _resident((D_MODEL, D_MODEL)),
            up_block,
            down_block,
            s_head_block, s_new_block, s_new_block, s_cache_block, s_cache_block,
            _resident((N_HEADS, WINDOW)),
        ],
        out_specs=[row_block, win_block, win_block, up_block, down_block,
                   s_head_block, s_cache_block, s_cache_block],
        out_shape=[
            jax.ShapeDtypeStruct((n, D_MODEL), F32),
            jax.ShapeDtypeStruct((batch, KV_WIDTH, WINDOW), F32),
            jax.ShapeDtypeStruct((batch, KV_WIDTH, WINDOW), F32),
            jax.ShapeDtypeStruct((D_MODEL, D_FF), BF16),
            jax.ShapeDtypeStruct((D_FF, D_MODEL), BF16),
            jax.ShapeDtypeStruct((steps, per_step * N_HEADS, KV_WIDTH), BF16),
            jax.ShapeDtypeStruct((nb, KV_WIDTH, WINDOW), F32),
            jax.ShapeDtypeStruct((nb, KV_WIDTH, WINDOW), F32),
        ],
        scratch_shapes=[
            pltpu.VMEM((nblk, GROUP * CHUNK, KV_WIDTH), BF16),
            pltpu.VMEM((N_KV_HEADS, nblk, KV_WIDTH, CHUNK), BF16),
            pltpu.VMEM((N_KV_HEADS, tb, KV_WIDTH), BF16),
            pltpu.VMEM((2, N_KV_HEADS, KV_WIDTH, CHUNK), BF16),
            pltpu.VMEM((2, N_KV_HEADS, WINDOW, KV_WIDTH), BF16),
            pltpu.VMEM((tb, D_MODEL), BF16),
            pltpu.VMEM((tb, D_MODEL), F32),
            pltpu.VMEM((2, tb, D_MODEL), F32),
            pltpu.VMEM((tb, D_MODEL), BF16),
            pltpu.VMEM((tb, D_MODEL), BF16),
            pltpu.VMEM((SGU_GROUPS, CHUNK, CHUNK), BF16),
            pltpu.VMEM((N_HEADS, CHUNK, 2 * CHUNK), F32),
            pltpu.VMEM((CHUNK, D_MODEL), F32),
        ],
        compiler_params=_params(),
        name="mix_prompt",
    )(x2d, ln1, w_in, w_q, sgu_g, sgu_w, sgu_b, sinks, slopes, w_oa, w_ob, w_out, w_up_f32, w_down_f32,
      s_qsel, s_knew, s_vnew, s_cache_k, s_cache_v, s_bias)


def _store_rows_as_tiles(ref, rows):
    for c in range(rows.shape[1] // LANES):
        ref[:, c, :] = rows[:, c * LANES:(c + 1) * LANES]


def _rows_from_tiles(ref):
    return jnp.concatenate([ref[:, c, :] for c in range(ref.shape[1])], axis=1)


def _ffn_rows(x, store, ln2_ref, w_up_ref, w_down_ref, lnf_ref, *, row_parts):
    n_slabs = D_FF // FF_SLAB
    m = x.shape[0]
    parts = [slice(i * m // row_parts, (i + 1) * m // row_parts) for i in range(row_parts)]

    def up(j, xn):
        return _dot(xn, w_up_ref[:, j * FF_SLAB:(j + 1) * FF_SLAB])

    xns = [_rmsnorm(x[p], ln2_ref[...]).astype(BF16) for p in parts]
    h_next = jnp.concatenate([up(0, xn_p) for xn_p in xns], axis=0) if row_parts > 1 else up(0, xns[0])
    xn = jnp.concatenate(xns, axis=0) if row_parts > 1 else xns[0]

    acc = x
    for j in range(n_slabs):
        h = h_next
        if j + 1 < n_slabs:
            h_next = up(j + 1, xn)
        h = jnp.square(jnp.maximum(h, 0.0)).astype(BF16)
        w_d = w_down_ref[j * FF_SLAB:(j + 1) * FF_SLAB, :]
        if j + 1 < n_slabs:
            acc = acc + _dot(h, w_d)
        else:
            for p in parts:
                store(p, _rmsnorm(acc[p] + _dot(h[p], w_d), lnf_ref[...]))


def _ffn_kernel(x_ref, ln2_ref, w_up_ref, w_down_ref, lnf_ref,
                s_o_ref, s_selt_ref, s_a_ref, s_ga_ref, s_gb_ref, s_x_ref, w_oa_ref, w_ob_ref, w_out_ref,
                y_ref, ys_ref):
    i = pl.program_id(0)
    last = pl.num_programs(0) - 1

    @pl.when(i < last)
    def _():
        def store(p, rows):
            y_ref[p, :] = rows

        _ffn_rows(x_ref[...], store, ln2_ref, w_up_ref, w_down_ref, lnf_ref, row_parts=4)

    @pl.when(i == last)
    def _():
        nb = s_x_ref.shape[0]
        bst = _dot(s_selt_ref[...], s_o_ref[...]).astype(BF16)
        ob = _dot(bst[0:nb, :], w_ob_ref[0:KV_WIDTH, :])
        for g in range(1, GROUP):
            ob = ob + _dot(bst[g * nb:(g + 1) * nb, :], w_ob_ref[g * KV_WIDTH:(g + 1) * KV_WIDTH, :])
        hm = s_ga_ref[...] * _dot(s_a_ref[...], w_oa_ref[...]) + s_gb_ref[...] * ob
        xs1 = _rows_from_tiles(s_x_ref) + _dot(hm.astype(BF16), w_out_ref[...])
        _ffn_rows(xs1, lambda p, rows: _store_rows_as_tiles(ys_ref, rows), ln2_ref, w_up_ref, w_down_ref, lnf_ref,
                  row_parts=1)


def _ffn(x2d, ln2, w_up, w_down, lnf, s_o, s_selt, s_a, s_ga, s_gb, xs2d, w_oa, w_ob, w_out):
    n = x2d.shape[0]
    nb = xs2d.shape[0]
    n_prompt_steps = n // FFN_BLOCK
    row_block = pl.BlockSpec((FFN_BLOCK, D_MODEL), lambda i: (jnp.minimum(i, n_prompt_steps - 1), 0))
    return pl.pallas_call(
        _ffn_kernel,
        grid=(n_prompt_steps + 1,),
        in_specs=[row_block, _resident((1, D_MODEL)), _resident((D_MODEL, D_FF)),
                  _resident((D_FF, D_MODEL)), _resident((1, D_MODEL)),
                  _resident((nb * N_HEADS, KV_WIDTH)), _resident((GROUP * nb, nb * N_HEADS)),
                  _resident((nb, D_MODEL)), _resident((nb, D_MODEL)), _resident((nb, D_MODEL)),
                  _resident((nb, D_MODEL // LANES, LANES)),
                  _resident((D_MODEL, D_MODEL)), _resident((D_MODEL, D_MODEL)), _resident((D_MODEL, D_MODEL))],
        out_specs=[row_block, _whole((nb, D_MODEL // LANES, LANES))],
        out_shape=[jax.ShapeDtypeStruct((n, D_MODEL), F32),
                   jax.ShapeDtypeStruct((nb, D_MODEL // LANES, LANES), F32)],
        compiler_params=_params(),
        name="ffn",
    )(x2d, ln2, w_up, w_down, lnf, s_o, s_selt, s_a, s_ga, s_gb, xs2d, w_oa, w_ob, w_out)


def _sample_proj_kernel(x_ref, ln1_ref, w_blk_ref, sgu_g_ref, sgu_w_ref, sgu_b_ref, sel_ref,
                        w_oa_blk_ref, w_ob_blk_ref, w_out_blk_ref,
                        w_bf_ref, w_q_bf_ref, qsel_ref, knew_ref, vnew_ref, vn_ref, a_ref,
                        ga_ref, gb_ref, w_oa_bf_ref, w_ob_bf_ref, w_out_bf_ref, xn_scr, h_scr):
    j = pl.program_id(0)
    nb = x_ref.shape[0]
    n_steps = IN_WIDTH // W_STEP

    @pl.when(j == 0)
    def _():
        xn_scr[...] = _rmsnorm(_rows_from_tiles(x_ref), ln1_ref[...]).astype(BF16)

    w_oa_bf_ref[...] = w_oa_blk_ref[...].astype(BF16)
    w_out_bf_ref[...] = w_out_blk_ref[...].astype(BF16)
    assert w_ob_blk_ref.shape[0] == GROUP * HEAD_DIM and n_steps == N_KV_HEADS
    for g in range(GROUP):
        dst = pl.multiple_of(g * KV_WIDTH + j * HEAD_DIM, HEAD_DIM)
        w_ob_bf_ref[pl.ds(dst, HEAD_DIM), :] = w_ob_blk_ref[g * HEAD_DIM:(g + 1) * HEAD_DIM, :].astype(BF16)

    wb = w_blk_ref[...].astype(BF16)
    w_bf_ref[...] = wb
    h_scr[j] = _dot(xn_scr[...], wb)

    assert W_STEP % HEAD_DIM == 0
    for step in range(n_steps):
        heads = [h for h in range(N_HEADS) if step * W_STEP <= OFF_Q + h * HEAD_DIM < (step + 1) * W_STEP]
        if heads:
            @pl.when(j == step)
            def _(step=step, heads=heads):
                for head in heads:
                    src = OFF_Q + head * HEAD_DIM - step * W_STEP
                    kvh, g = divmod(head, GROUP)
                    dst = g * KV_WIDTH + kvh * HEAD_DIM
                    w_q_bf_ref[:, dst:dst + HEAD_DIM] = wb[:, src:src + HEAD_DIM]

    @pl.when(j == n_steps - 1)
    def _():
        def cols(lo, hi):
            pieces = []
            for step in range(n_steps):
                a, b = max(lo, step * W_STEP), min(hi, (step + 1) * W_STEP)
                if a < b:
                    pieces.append(h_scr[step, :, a - step * W_STEP:b - step * W_STEP])
            return pieces[0] if len(pieces) == 1 else jnp.concatenate(pieces, axis=1)

        u = _gelu_tanh(cols(OFF_U, OFF_V))
        v = _gelu_tanh(cols(OFF_V, OFF_Q))
        vn = _rmsnorm(v, sgu_g_ref[...])
        _store_rows_as_tiles(vn_ref, vn)
        def over_groups(entry):
            return jnp.concatenate(
                [jnp.broadcast_to(entry(g), (1, SGU_GROUP_DIM)) for g in range(SGU_GROUPS)], axis=1)

        w_diag = over_groups(lambda g: sgu_w_ref[g, 0:1, 0:1])
        b_first = over_groups(lambda g: sgu_b_ref[g:g + 1, 0:1])
        a_ref[...] = (u * (vn * w_diag + b_first)).astype(BF16)
        knew_ref[...] = cols(OFF_K, OFF_VA)
        vnew_ref[...] = cols(OFF_VA, OFF_GA)
        ga_ref[...] = jax.nn.sigmoid(cols(OFF_GA, OFF_GB))
        gb_ref[...] = jax.nn.sigmoid(cols(OFF_GB, IN_WIDTH))
        q = _dot(xn_scr[...], w_q_bf_ref[...]) * ATTN_SCALE
        qstack = jnp.concatenate([q[:, g * KV_WIDTH:(g + 1) * KV_WIDTH] for g in range(GROUP)], axis=0).astype(BF16)
        qrep = _dot(sel_ref[...], qstack)
        row_kvh = lax.broadcasted_iota(jnp.int32, (nb * N_HEADS, KV_WIDTH), 0) % N_KV_HEADS
        lane_kvh = lax.broadcasted_iota(jnp.int32, (nb * N_HEADS, KV_WIDTH), 1) // HEAD_DIM
        qsel_ref[...] = jnp.where(row_kvh == lane_kvh, qrep, 0.0).astype(BF16)


def _sample_proj(xs2d, ln1, w_in_f32, sgu_g, sgu_w, sgu_b, sel, w_oa_f32, w_ob_f32, w_out_f32):
    nb = xs2d.shape[0]
    n_blocks = IN_WIDTH // W_STEP
    w_block = pl.BlockSpec((D_MODEL, W_STEP), lambda j: (0, j))
    merge_block = pl.BlockSpec((D_MODEL // n_blocks, D_MODEL), lambda j: (j, 0))
    merge_shape = jax.ShapeDtypeStruct((D_MODEL, D_MODEL), BF16)
    return pl.pallas_call(
        _sample_proj_kernel,
        grid=(n_blocks,),
        in_specs=[_whole((nb, D_MODEL // LANES, LANES)), _whole((1, D_MODEL)), w_block,
                  _whole((1, D_MODEL)),
                  pl.BlockSpec((SGU_GROUPS, SUBLANES, CHUNK), lambda j: (0, 0, 0)),
                  _whole((SGU_GROUPS, CHUNK)),
                  _resident((nb * N_HEADS, GROUP * nb)),
                  merge_block, merge_block, merge_block],
        out_specs=[w_block, _whole((D_MODEL, D_MODEL)),
                   _whole((nb * N_HEADS, KV_WIDTH)), _whole((nb, KV_WIDTH)), _whole((nb, KV_WIDTH)),
                   _whole((nb, D_MODEL // LANES, LANES)), _whole((nb, D_MODEL)),
                   _whole((nb, D_MODEL)), _whole((nb, D_MODEL)),
                   merge_block, _whole((D_MODEL, D_MODEL)), merge_block],
        out_shape=[
            jax.ShapeDtypeStruct((D_MODEL, IN_WIDTH), BF16),
            jax.ShapeDtypeStruct((D_MODEL, D_MODEL), BF16),
            jax.ShapeDtypeStruct((nb * N_HEADS, KV_WIDTH), BF16),
            jax.ShapeDtypeStruct((nb, KV_WIDTH), F32),
            jax.ShapeDtypeStruct((nb, KV_WIDTH), F32),
            jax.ShapeDtypeStruct((nb, D_MODEL // LANES, LANES), F32),
            jax.ShapeDtypeStruct((nb, D_MODEL), BF16),
            jax.ShapeDtypeStruct((nb, D_MODEL), F32),
            jax.ShapeDtypeStruct((nb, D_MODEL), F32),
            merge_shape, merge_shape, merge_shape,
        ],
        scratch_shapes=[pltpu.VMEM((nb, D_MODEL), BF16), pltpu.VMEM((n_blocks, nb, W_STEP), F32)],
        compiler_params=_params(),
        name="sample_proj",
    )(xs2d, ln1, w_in_f32, sgu_g, sgu_w, sgu_b, sel, w_oa_f32, w_ob_f32, w_out_f32)


def _sample_attn_stages(qsel_ref, knew_ref, vnew_ref, ck_ref, cv_ref, bias_ref, sink_ref, upper,
                        o_ref, nk_ref, nv_ref):
    bs = ck_ref.shape[0]
    assert knew_ref.shape[0] == 2 * bs
    row0 = jnp.where(upper, bs, 0)
    row_kvh = lax.broadcasted_iota(jnp.int32, (N_HEADS, KV_WIDTH), 0) % N_KV_HEADS
    lane_kvh = lax.broadcasted_iota(jnp.int32, (N_HEADS, KV_WIDTH), 1) // HEAD_DIM
    own = row_kvh == lane_kvh
    bias = bias_ref[...]
    head_row = lax.broadcasted_iota(jnp.int32, (N_HEADS, 1), 0)
    sink = jnp.zeros((N_HEADS, 1), F32)
    for h in range(N_HEADS):
        g, kvh = divmod(h, N_KV_HEADS)
        sink = jnp.where(head_row == h, sink_ref[0, kvh * GROUP + g], sink)

    qss = [qsel_ref[i * N_HEADS:(i + 1) * N_HEADS, :] for i in range(bs)]
    kns = [knew_ref[pl.ds(row0 + i, 1), :] for i in range(bs)]
    vws = [vnew_ref[pl.ds(row0 + i, 1), :] for i in range(bs)]
    scores = [_dot(qss[i], ck_ref[i].astype(BF16)) + bias for i in range(bs)]
    yield
    probs = []
    for i in range(bs):
        s = scores[i]
        s_new = jnp.sum(qss[i].astype(F32) * kns[i], axis=1, keepdims=True)
        m = jnp.maximum(jnp.maximum(jnp.max(s, axis=1, keepdims=True), s_new), sink)
        p = jnp.exp(s - m)
        p_new = jnp.exp(s_new - m)
        denom = jnp.sum(p, axis=1, keepdims=True) + p_new + jnp.exp(sink - m)
        probs.append((p.astype(BF16), p_new, denom))

    kn_t = knew_ref[...].T
    vw_t = vnew_ref[...].T
    last_lane = lax.broadcasted_iota(jnp.int32, (KV_WIDTH, WINDOW), 1) == WINDOW - 1
    for i in range(bs):
        kn_col = jnp.where(upper, kn_t[:, bs + i:bs + i + 1], kn_t[:, i:i + 1])
        vw_col = jnp.where(upper, vw_t[:, bs + i:bs + i + 1], vw_t[:, i:i + 1])
        nk_ref[i] = jnp.where(last_lane, kn_col, pltpu.roll(ck_ref[i], WINDOW - 1, 1))
        nv_ref[i] = jnp.where(last_lane, vw_col, pltpu.roll(cv_ref[i], WINDOW - 1, 1))
    yield
    for i in range(bs):
        p, p_new, denom = probs[i]
        o = (_dot_nt(p, cv_ref[i].astype(BF16)) + p_new * vws[i]) / denom
        o_ref[i * N_HEADS:(i + 1) * N_HEADS, :] = jnp.where(own, o, 0.0).astype(BF16)


def _head_perm(v):
    return v.reshape(N_KV_HEADS, GROUP).T.reshape(N_HEADS)


def _alibi_slopes():
    h = np.arange(1, N_HEADS + 1, dtype=np.float32)
    return np.exp2(-8.0 * h / N_HEADS).astype(np.float32)


def _selection_matrix(nb):
    r = np.arange(nb * N_HEADS)
    c = np.arange(GROUP * nb)
    same_sample = (r[:, None] // N_HEADS) == (c[None, :] % nb)
    same_member = ((r[:, None] % N_HEADS) // N_KV_HEADS) == (c[None, :] // nb)
    return (same_sample & same_member).astype(np.float32)


def kernel(x_prompt, x_sample, cache_k_win, cache_v_win, ln1_g, w_in, sgu_norm_g, sgu_w, sgu_b, attn_sinks,
           w_oa, w_ob, w_out, ln2_g, w_up, w_down, lnf_g):
    batch, seq, _ = x_prompt.shape
    dec_batch, dec_seq, _ = x_sample.shape
    depth = w_in.shape[0]
    assert depth == 1 and dec_seq == 1
    assert seq % TOKEN_BLOCK == 0 and TOKEN_BLOCK % CHUNK == 0
    assert (batch * seq) % FFN_BLOCK == 0
    assert w_in.shape[-1] == IN_WIDTH

    ln1 = ln1_g[0].reshape(1, D_MODEL)
    ln2 = ln2_g[0].reshape(1, D_MODEL)
    lnf = lnf_g.reshape(1, D_MODEL)
    sgu_g = sgu_norm_g[0].reshape(1, D_MODEL)
    slopes_p = _head_perm(_alibi_slopes())

    xs2d = x_sample.reshape(dec_batch, D_MODEL // LANES, LANES)
    sel_np = _selection_matrix(dec_batch)
    sel = jnp.asarray(sel_np, BF16)
    selt = jnp.asarray(sel_np.T, BF16)
    bias_s = -slopes_p[:, None] * (WINDOW - np.arange(WINDOW, dtype=np.float32))[None, :]

    w_in_b, w_q_b, qsel, knew, vnew, vn, a_s, ga_s, gb_s, w_oa_b, w_ob_b, w_out_b = _sample_proj(
        xs2d, ln1, w_in[0], sgu_g, sgu_w[0], sgu_b[0], sel, w_oa[0], w_ob[0], w_out[0])
    def to_feature_major(c):
        return c[0].transpose(0, 2, 3, 1).reshape(dec_batch, KV_WIDTH, WINDOW)

    def from_feature_major(c):
        return c.reshape(c.shape[0], N_KV_HEADS, HEAD_DIM, WINDOW).transpose(0, 3, 1, 2)[None]

    steps = (batch * seq) // TOKEN_BLOCK
    per_step = dec_batch // steps

    x1, kwin, vwin, w_up_b, w_down_b, o_s, nk, nv = _mix_prompt(
        x_prompt.reshape(batch * seq, D_MODEL), ln1, w_in_b, w_q_b, sgu_g, sgu_w[0], sgu_b[0],
        attn_sinks, slopes_p * LOG2E, w_oa_b, w_ob_b, w_out_b, w_up[0], w_down[0],
        qsel.reshape(steps, per_step * N_HEADS, KV_WIDTH), knew, vnew,
        to_feature_major(cache_k_win), to_feature_major(cache_v_win), bias_s,
        batch=batch, seq=seq)

    y_prompt, y_sample = _ffn(x1, ln2, w_up_b, w_down_b, lnf,
                              o_s.reshape(dec_batch * N_HEADS, KV_WIDTH), selt, a_s, ga_s, gb_s, xs2d,
                              w_oa_b, w_ob_b, w_out_b)

    return (y_prompt.reshape(batch, seq, D_MODEL),
            y_sample.reshape(dec_batch, dec_seq, D_MODEL),
            from_feature_major(kwin), from_feature_major(vwin),
            from_feature_major(nk), from_feature_major(nv),
            vn.reshape(depth, dec_batch, dec_seq, D_MODEL))
```

```python
import functools
import math

import numpy as np
import jax
import jax.numpy as jnp
from jax import lax
from jax.experimental import pallas as pl
from jax.experimental.pallas import tpu as pltpu

D_MODEL = 1024
N_HEADS = 16
HEAD_DIM = 64
N_KV_HEADS = 4
GROUP = N_HEADS // N_KV_HEADS
KV_WIDTH = N_KV_HEADS * HEAD_DIM
WINDOW = 128
CHUNK = 128
SGU_GROUPS = 8
SGU_GROUP_DIM = D_MODEL // SGU_GROUPS
D_FF = 4 * D_MODEL
FF_SLAB = 1024
EPS = 1e-6
NEG_BIG = -1e30
ATTN_SCALE = HEAD_DIM ** -0.5
LOG2E = math.log2(math.e)

OFF_U, OFF_V, OFF_Q, OFF_K, OFF_VA, OFF_GA, OFF_GB, IN_WIDTH = 0, 1024, 2048, 3072, 3328, 3584, 4608, 5632
COL_BLOCK = 512
MERGE_PARTS = 4
STAGE_SLOTS = 4

TOKEN_BLOCK = 512
FFN_BLOCK = 1024
SUBLANES = 8
LANES = 128
VMEM_LIMIT_BYTES = 58 * 1024 * 1024

F32 = jnp.float32
BF16 = jnp.bfloat16


def _rmsnorm(x, g):
    ms = jnp.mean(x * x, axis=-1, keepdims=True)
    return x * lax.rsqrt(ms + EPS) * g


def _gelu_tanh(x):
    c = math.sqrt(2.0 / math.pi)
    return x * (0.5 * (1.0 + jnp.tanh(c * (x + 0.044715 * (x * x * x)))))


def _dot(a, b):
    return jnp.dot(a, b, preferred_element_type=F32)


def _dot_nt(a, b):
    return lax.dot_general(a, b, (((1,), (1,)), ((), ())), preferred_element_type=F32)


def _resident(shape):
    zeros = (0,) * len(shape)
    return pl.BlockSpec(shape, lambda *_: zeros, pipeline_mode=pl.Buffered(1))


def _whole(shape):
    zeros = (0,) * len(shape)
    return pl.BlockSpec(shape, lambda *_: zeros)


def _params():
    return pltpu.CompilerParams(dimension_semantics=("arbitrary",), vmem_limit_bytes=VMEM_LIMIT_BYTES)


def _mix_prompt_kernel(x_ref, ln1_ref, w_in_ref, w_q_ref, sgu_g_ref, sgu_w_ref, sgu_b_ref,
                       sink_ref, slope_ref, w_oa_ref, w_ob_ref, w_out_ref, w_up_blk_ref, w_down_blk_ref,
                       s_qsel_ref, s_knew_ref, s_vnew_ref, s_ck_ref, s_cv_ref, s_bias_ref,
                       x1_ref, kwin_ref, vwin_ref, w_up_bf_ref, w_down_bf_ref, s_o_ref, s_nk_ref, s_nv_ref,
                       qs_scr, kt_scr, vm_scr, kprev_scr, vprev_scr, vn_scr, u_scr, gate_scr, a_scr, b_scr, wt_scr,
                       bias_ref, bexp_ref,
                       *, steps_per_seq):
    step = pl.program_id(0)
    tb = x_ref.shape[0]
    nblk = tb // CHUNK
    first = (step % steps_per_seq) == 0
    rd = step % 2
    wr = 1 - rd

    @pl.when(step == 0)
    def _():
        row = lax.broadcasted_iota(jnp.int32, (CHUNK, CHUNK), 0)
        col = lax.broadcasted_iota(jnp.int32, (CHUNK, CHUNK), 1)
        b_t = sgu_b_ref[...].T
        for g in range(SGU_GROUPS):
            wt_scr[g] = jnp.where(row >= col, sgu_w_ref[g], 0.0).astype(BF16)
            bexp_ref[:, g * SGU_GROUP_DIM:(g + 1) * SGU_GROUP_DIM] = jnp.broadcast_to(
                b_t[:, g:g + 1], (CHUNK, SGU_GROUP_DIM))
        dist = (lax.broadcasted_iota(jnp.int32, (CHUNK, 2 * CHUNK), 0) + CHUNK
                - lax.broadcasted_iota(jnp.int32, (CHUNK, 2 * CHUNK), 1))
        in_band = jnp.logical_and(dist >= 0, dist <= WINDOW)
        dist_f = dist.astype(F32)
        for h in range(N_HEADS):
            bias_ref[h] = jnp.where(in_band, -slope_ref[h] * dist_f, NEG_BIG)
        kt_scr[...] = jnp.zeros(kt_scr.shape, BF16)
        vm_scr[...] = jnp.zeros(vm_scr.shape, BF16)
        kprev_scr[...] = jnp.zeros(kprev_scr.shape, BF16)
        vprev_scr[...] = jnp.zeros(vprev_scr.shape, BF16)

    @pl.when(first)
    def _():
        kprev_scr[rd] = jnp.zeros(kprev_scr.shape[1:], BF16)
        vprev_scr[rd] = jnp.zeros(vprev_scr.shape[1:], BF16)

    w_up_bf_ref[...] = w_up_blk_ref[...].astype(BF16)
    w_down_bf_ref[...] = w_down_blk_ref[...].astype(BF16)

    x = x_ref[...]
    quarters = [slice(i * tb // 4, (i + 1) * tb // 4) for i in range(4)]
    xn_parts = [_rmsnorm(x[p], ln1_ref[...]).astype(BF16) for p in quarters]
    xn = jnp.concatenate(xn_parts, axis=0)

    q = jnp.concatenate([_dot(xn_p, w_q_ref[...]) for xn_p in xn_parts], axis=0)
    k = _dot(xn, w_in_ref[:, OFF_K:OFF_VA])
    va = _dot(xn, w_in_ref[:, OFF_VA:OFF_GA])
    h_v = _dot(xn, w_in_ref[:, OFF_V:OFF_Q])
    h_u0 = _dot(xn, w_in_ref[:, OFF_U:OFF_U + D_MODEL // 2])

    q = (q * (ATTN_SCALE * LOG2E)).astype(BF16)
    for c in range(nblk):
        for g in range(GROUP):
            qs_scr[c, g * CHUNK:(g + 1) * CHUNK, :] = q[c * CHUNK:(c + 1) * CHUNK, g * KV_WIDTH:(g + 1) * KV_WIDTH]

    kt_f32 = k.T
    kwin_ref[...] = kt_f32[:, tb - WINDOW:]
    vwin_ref[...] = va[tb - WINDOW:, :].T
    kt = kt_f32.astype(BF16)
    vab = va.astype(BF16)
    for kvh in range(N_KV_HEADS):
        own = slice(kvh * HEAD_DIM, (kvh + 1) * HEAD_DIM)
        for c in range(nblk):
            kt_scr[kvh, c, own, :] = kt[own, c * CHUNK:(c + 1) * CHUNK]
        vm_scr[kvh, :, own] = vab[:, own]
        kprev_scr[wr, kvh, own, :] = kt[own, tb - WINDOW:]
        vprev_scr[wr, kvh, :, own] = vab[tb - WINDOW:, own]

    no_prev = jnp.where(
        jnp.logical_and(first, lax.broadcasted_iota(jnp.int32, (CHUNK, 2 * CHUNK), 1) < CHUNK), NEG_BIG, 0.0)

    def attn_scores(c):
        qs = qs_scr[c]
        out = []
        for kvh in range(N_KV_HEADS):
            k_prev = kprev_scr[rd, kvh] if c == 0 else kt_scr[kvh, c - 1]
            out.append(_dot(qs, jnp.concatenate([k_prev, kt_scr[kvh, c]], axis=1)))
        return out

    def attn_softmax(c, scores):
        out = []
        for kvh in range(N_KV_HEADS):
            ps = []
            for g in range(GROUP):
                h = g * N_KV_HEADS + kvh
                s = scores[kvh][g * CHUNK:(g + 1) * CHUNK, :] + bias_ref[h]
                if c == 0:
                    s = s + no_prev
                sink = sink_ref[0, kvh * GROUP + g] * LOG2E
                m = jnp.max(s, axis=1, keepdims=True)
                p = jnp.exp2(s - m)
                denom = jnp.sum(p, axis=1, keepdims=True) + jnp.exp2(sink - m)
                ps.append((p * (1.0 / denom)).astype(BF16))
            out.append(jnp.concatenate(ps, axis=0))
        return out

    def attn_values(c, probs):
        rows = slice(c * CHUNK, (c + 1) * CHUNK)
        acc = None
        for kvh in range(N_KV_HEADS):
            if c == 0:
                v_band = jnp.concatenate([vprev_scr[rd, kvh], vm_scr[kvh, 0:CHUNK, :]], axis=0)
            else:
                v_band = vm_scr[kvh, (c - 1) * CHUNK:(c + 1) * CHUNK, :]
            o = _dot(probs[kvh], v_band)
            acc = o if acc is None else acc + o
        for g in range(GROUP):
            b_scr[rows, g * KV_WIDTH:(g + 1) * KV_WIDTH] = acc[g * CHUNK:(g + 1) * CHUNK, :].astype(BF16)

    def sgu_chunk(c):
        rows = slice(c * CHUNK, (c + 1) * CHUNK)
        vn_c = vn_scr[rows, :]
        mixed = jnp.concatenate(
            [_dot(wt_scr[g], vn_c[:, g * SGU_GROUP_DIM:(g + 1) * SGU_GROUP_DIM]) for g in range(SGU_GROUPS)],
            axis=1) + bexp_ref[...]
        a_scr[rows, :] = (u_scr[rows, :] * mixed).astype(BF16)

    half = D_MODEL // 2

    def tail_u(h, lo):
        u_scr[:, lo:lo + half] = _gelu_tanh(h)

    def tail_ga(h, lo):
        gate_scr[0, :, lo:lo + half] = jax.nn.sigmoid(h)

    def tail_gb(h, lo):
        gate_scr[1, :, lo:lo + half] = jax.nn.sigmoid(h)

    fillers = [
        (lambda: _dot(xn, w_in_ref[:, OFF_U + half:OFF_V]), lambda h: tail_u(h, half)),
        (lambda: _dot(xn, w_in_ref[:, OFF_GA:OFF_GA + half]), lambda h: tail_ga(h, 0)),
        (lambda: _dot(xn, w_in_ref[:, OFF_GA + half:OFF_GB]), lambda h: tail_ga(h, half)),
        (lambda: _dot(xn, w_in_ref[:, OFF_GB:IN_WIDTH]),
         lambda h: (tail_gb(h[:, :half], 0), tail_gb(h[:, half:], half))),
    ]
    vn_scr[...] = _rmsnorm(_gelu_tanh(h_v), sgu_g_ref[...]).astype(BF16)
    tail_u(h_u0, 0)
    sgu_after = {nblk - 2: range(0, nblk // 2), nblk - 1: range(nblk // 2, nblk)}
    for c in range(nblk):
        scores = attn_scores(c)
        proj = fillers[c][0]() if c < len(fillers) else None
        for cc in sgu_after.get(c, ()):
            sgu_chunk(cc)
        probs = attn_softmax(c, scores)
        attn_values(c, probs)
        if proj is not None:
            fillers[c][1](proj)
    for matmul, tail in fillers[nblk:]:
        tail(matmul())

    sample = _sample_attn_stages(s_qsel_ref, s_knew_ref, s_vnew_ref, s_ck_ref, s_cv_ref, s_bias_ref, sink_ref,
                                 step % 2 == 1, s_o_ref, s_nk_ref, s_nv_ref)
    next(sample)
    branch_a = _dot(a_scr[...], w_oa_ref[...])
    next(sample)
    branch_b = _dot(b_scr[...], w_ob_ref[...])
    for _ in sample:
        pass
    hm = gate_scr[0] * branch_a + gate_scr[1] * branch_b
    x1_ref[...] = x + _dot(hm.astype(BF16), w_out_ref[...])


def _mix_prompt(x2d, ln1, w_in, w_q, sgu_g, sgu_w, sgu_b, sinks, slopes, w_oa, w_ob, w_out,
                w_up_f32, w_down_f32,
                s_qsel, s_knew, s_vnew, s_cache_k, s_cache_v, s_bias, *, batch, seq):
    n = x2d.shape[0]
    tb = TOKEN_BLOCK
    nblk = tb // CHUNK
    steps = n // tb
    steps_per_seq = seq // tb
    nb = s_cache_k.shape[0]
    per_step = nb // steps
    assert per_step * steps == nb and 2 * per_step == SUBLANES and s_knew.shape == (nb, KV_WIDTH)
    s_head_block = pl.BlockSpec((None, per_step * N_HEADS, KV_WIDTH), lambda i: (i, 0, 0))
    s_new_block = pl.BlockSpec((SUBLANES, KV_WIDTH), lambda i: (i // 2, 0))
    s_cache_block = pl.BlockSpec((per_step, KV_WIDTH, WINDOW), lambda i: (i, 0, 0))
    row_block = pl.BlockSpec((tb, D_MODEL), lambda i: (i, 0))
    win_block = pl.BlockSpec((None, KV_WIDTH, WINDOW), lambda i: (i // steps_per_seq, 0, 0))
    up_block = pl.BlockSpec((D_MODEL // steps, D_FF), lambda i: (i, 0))
    down_block = pl.BlockSpec((D_FF // steps, D_MODEL), lambda i: (i, 0))
    return pl.pallas_call(
        functools.partial(_mix_prompt_kernel, steps_per_seq=steps_per_seq),
        grid=(n // tb,),
        in_specs=[
            row_block,
            _resident((1, D_MODEL)),
            _resident((D_MODEL, IN_WIDTH)),
            _resident((D_MODEL, D_MODEL)),
            _resident((1, D_MODEL)),
            _resident((SGU_GROUPS, CHUNK, CHUNK)),
            _resident((SGU_GROUPS, CHUNK)),
            pl.BlockSpec(memory_space=pltpu.SMEM),
            pl.BlockSpec(memory_space=pltpu.SMEM),
            _resident((D_MODEL, D_MODEL)),
            _resident((D_MODEL, D_MODEL)),
            _resident((D_MODEL, D_MODEL)),
            up_block,
            down_block,
            s_head_block, s_new_block, s_new_block, s_cache_block, s_cache_block,
            _resident((N_HEADS, WINDOW)),
        ],
        out_specs=[row_block, win_block, win_block, up_block, down_block,
                   s_head_block, s_cache_block, s_cache_block],
        out_shape=[
            jax.ShapeDtypeStruct((n, D_MODEL), F32),
            jax.ShapeDtypeStruct((batch, KV_WIDTH, WINDOW), F32),
            jax.ShapeDtypeStruct((batch, KV_WIDTH, WINDOW), F32),
            jax.ShapeDtypeStruct((D_MODEL, D_FF), BF16),
            jax.ShapeDtypeStruct((D_FF, D_MODEL), BF16),
            jax.ShapeDtypeStruct((steps, per_step * N_HEADS, KV_WIDTH), BF16),
            jax.ShapeDtypeStruct((nb, KV_WIDTH, WINDOW), F32),
            jax.ShapeDtypeStruct((nb, KV_WIDTH, WINDOW), F32),
        ],
        scratch_shapes=[
            pltpu.VMEM((nblk, GROUP * CHUNK, KV_WIDTH), BF16),
            pltpu.VMEM((N_KV_HEADS, nblk, KV_WIDTH, CHUNK), BF16),
            pltpu.VMEM((N_KV_HEADS, tb, KV_WIDTH), BF16),
            pltpu.VMEM((2, N_KV_HEADS, KV_WIDTH, CHUNK), BF16),
            pltpu.VMEM((2, N_KV_HEADS, WINDOW, KV_WIDTH), BF16),
            pltpu.VMEM((tb, D_MODEL), BF16),
            pltpu.VMEM((tb, D_MODEL), F32),
            pltpu.VMEM((2, tb, D_MODEL), F32),
            pltpu.VMEM((tb, D_MODEL), BF16),
            pltpu.VMEM((tb, D_MODEL), BF16),
            pltpu.VMEM((SGU_GROUPS, CHUNK, CHUNK), BF16),
            pltpu.VMEM((N_HEADS, CHUNK, 2 * CHUNK), F32),
            pltpu.VMEM((CHUNK, D_MODEL), F32),
        ],
        compiler_params=_params(),
        name="mix_prompt",
    )(x2d, ln1, w_in, w_q, sgu_g, sgu_w, sgu_b, sinks, slopes, w_oa, w_ob, w_out, w_up_f32, w_down_f32,
      s_qsel, s_knew, s_vnew, s_cache_k, s_cache_v, s_bias)


def _store_rows_as_tiles(ref, rows):
    for c in range(rows.shape[1] // LANES):
        ref[:, c, :] = rows[:, c * LANES:(c + 1) * LANES]


def _rows_from_tiles(ref):
    return jnp.concatenate([ref[:, c, :] for c in range(ref.shape[1])], axis=1)


def _ffn_rows(x, store, ln2_ref, w_up_ref, w_down_ref, lnf_ref, *, row_parts):
    n_slabs = D_FF // FF_SLAB
    m = x.shape[0]
    parts = [slice(i * m // row_parts, (i + 1) * m // row_parts) for i in range(row_parts)]

    def up(j, xn):
        return _dot(xn, w_up_ref[:, j * FF_SLAB:(j + 1) * FF_SLAB])

    xns = [_rmsnorm(x[p], ln2_ref[...]).astype(BF16) for p in parts]
    h_next = jnp.concatenate([up(0, xn_p) for xn_p in xns], axis=0) if row_parts > 1 else up(0, xns[0])
    xn = jnp.concatenate(xns, axis=0) if row_parts > 1 else xns[0]

    acc = x
    for j in range(n_slabs):
        h = h_next
        if j + 1 < n_slabs:
            h_next = up(j + 1, xn)
        h = jnp.square(jnp.maximum(h, 0.0)).astype(BF16)
        w_d = w_down_ref[j * FF_SLAB:(j + 1) * FF_SLAB, :]
        if j + 1 < n_slabs:
            acc = acc + _dot(h, w_d)
        else:
            for p in parts:
                store(p, _rmsnorm(acc[p] + _dot(h[p], w_d), lnf_ref[...]))


def _ffn_kernel(x_ref, ln2_ref, w_up_ref, w_down_ref, lnf_ref,
                s_o_ref, s_selt_ref, s_a_ref, s_ga_ref, s_gb_ref, s_x_ref, w_oa_ref, w_ob_ref, w_out_ref,
                y_ref, ys_ref):
    i = pl.program_id(0)
    last = pl.num_programs(0) - 1

    @pl.when(i < last)
    def _():
        def store(p, rows):
            y_ref[p, :] = rows

        _ffn_rows(x_ref[...], store, ln2_ref, w_up_ref, w_down_ref, lnf_ref, row_parts=4)

    @pl.when(i == last)
    def _():
        nb = s_x_ref.shape[0]
        bst = _dot(s_selt_ref[...], s_o_ref[...]).astype(BF16)
        ob = _dot(bst[0:nb, :], w_ob_ref[0:KV_WIDTH, :])
        for g in range(1, GROUP):
            ob = ob + _dot(bst[g * nb:(g + 1) * nb, :], w_ob_ref[g * KV_WIDTH:(g + 1) * KV_WIDTH, :])
        hm = s_ga_ref[...] * _dot(s_a_ref[...], w_oa_ref[...]) + s_gb_ref[...] * ob
        xs1 = _rows_from_tiles(s_x_ref) + _dot(hm.astype(BF16), w_out_ref[...])
        _ffn_rows(xs1, lambda p, rows: _store_rows_as_tiles(ys_ref, rows), ln2_ref, w_up_ref, w_down_ref, lnf_ref,
                  row_parts=1)


def _ffn(x2d, ln2, w_up, w_down, lnf, s_o, s_selt, s_a, s_ga, s_gb, xs2d, w_oa, w_ob, w_out):
    n = x2d.shape[0]
    nb = xs2d.shape[0]
    n_prompt_steps = n // FFN_BLOCK
    row_block = pl.BlockSpec((FFN_BLOCK, D_MODEL), lambda i: (jnp.minimum(i, n_prompt_steps - 1), 0))
    return pl.pallas_call(
        _ffn_kernel,
        grid=(n_prompt_steps + 1,),
        in_specs=[row_block, _resident((1, D_MODEL)), _resident((D_MODEL, D_FF)),
                  _resident((D_FF, D_MODEL)), _resident((1, D_MODEL)),
                  _resident((nb * N_HEADS, KV_WIDTH)), _resident((GROUP * nb, nb * N_HEADS)),
                  _resident((nb, D_MODEL)), _resident((nb, D_MODEL)), _resident((nb, D_MODEL)),
                  _resident((nb, D_MODEL // LANES, LANES)),
                  _resident((D_MODEL, D_MODEL)), _resident((D_MODEL, D_MODEL)), _resident((D_MODEL, D_MODEL))],
        out_specs=[row_block, _whole((nb, D_MODEL // LANES, LANES))],
        out_shape=[jax.ShapeDtypeStruct((n, D_MODEL), F32),
                   jax.ShapeDtypeStruct((nb, D_MODEL // LANES, LANES), F32)],
        compiler_params=_params(),
        name="ffn",
    )(x2d, ln2, w_up, w_down, lnf, s_o, s_selt, s_a, s_ga, s_gb, xs2d, w_oa, w_ob, w_out)


def _sample_proj_kernel(x_ref, ln1_ref, w_in_hbm, sgu_g_ref, sgu_w_ref, sgu_b_ref, sel_ref,
                        w_oa_hbm, w_ob_hbm, w_out_hbm,
                        w_bf_hbm, w_q_bf_hbm, qsel_ref, knew_ref, vnew_ref, vn_ref, a_ref,
                        ga_ref, gb_ref, w_oa_bf_hbm, w_ob_bf_hbm, w_out_bf_hbm,
                        col_stage, row_stage, w_bf_scr, w_q_scr, merge_scr, h_scr, in_sem, out_sem):
    nb = x_ref.shape[0]
    merge_srcs = (w_oa_hbm, w_ob_hbm, w_out_hbm)
    merge_dsts = (w_oa_bf_hbm, w_ob_bf_hbm, w_out_bf_hbm)
    n_merge = len(merge_srcs) * MERGE_PARTS
    n_col = IN_WIDTH // COL_BLOCK
    merge_rows = D_MODEL // MERGE_PARTS

    def load(job):
        if job < n_merge:
            w, r = divmod(job, MERGE_PARTS)
            src = merge_srcs[w].at[r * merge_rows:(r + 1) * merge_rows, :]
            dst = row_stage.at[job % STAGE_SLOTS]
        else:
            c = job - n_merge
            src = w_in_hbm.at[:, c * COL_BLOCK:(c + 1) * COL_BLOCK]
            dst = col_stage.at[c % STAGE_SLOTS]
        return pltpu.make_async_copy(src, dst, in_sem.at[job])

    def store(idx):
        if idx < len(merge_dsts):
            src, dst = merge_scr.at[idx], merge_dsts[idx]
        elif idx < len(merge_dsts) + n_col:
            c = idx - len(merge_dsts)
            src = w_bf_scr.at[:, c * COL_BLOCK:(c + 1) * COL_BLOCK]
            dst = w_bf_hbm.at[:, c * COL_BLOCK:(c + 1) * COL_BLOCK]
        else:
            src, dst = w_q_scr, w_q_bf_hbm
        return pltpu.make_async_copy(src, dst, out_sem.at[idx])

    n_stores = len(merge_dsts) + n_col + 1
    for job in range(STAGE_SLOTS):
        load(job).start()
    for c in range(STAGE_SLOTS):
        load(n_merge + c).start()

    xn = _rmsnorm(_rows_from_tiles(x_ref), ln1_ref[...]).astype(BF16)

    assert merge_rows == GROUP * HEAD_DIM and MERGE_PARTS == N_KV_HEADS
    for job in range(n_merge):
        w, r = divmod(job, MERGE_PARTS)
        load(job).wait()
        blk = row_stage[job % STAGE_SLOTS].astype(BF16)
        if w == 1:
            for g in range(GROUP):
                dst = g * KV_WIDTH + r * HEAD_DIM
                merge_scr[w, dst:dst + HEAD_DIM, :] = blk[g * HEAD_DIM:(g + 1) * HEAD_DIM, :]
        else:
            merge_scr[w, r * merge_rows:(r + 1) * merge_rows, :] = blk
        if job + STAGE_SLOTS < n_merge:
            load(job + STAGE_SLOTS).start()
        if r == MERGE_PARTS - 1:
            store(w).start()

    assert COL_BLOCK % HEAD_DIM == 0 and IN_WIDTH % COL_BLOCK == 0
    for c in range(n_col):
        lo = c * COL_BLOCK
        load(n_merge + c).wait()
        wb = col_stage[c % STAGE_SLOTS].astype(BF16)
        w_bf_scr[:, lo:lo + COL_BLOCK] = wb
        h_scr[:, lo:lo + COL_BLOCK] = _dot(xn, wb)
        for head in range(N_HEADS):
            src = OFF_Q + head * HEAD_DIM - lo
            if 0 <= src < COL_BLOCK:
                kvh, g = divmod(head, GROUP)
                dst = g * KV_WIDTH + kvh * HEAD_DIM
                w_q_scr[:, dst:dst + HEAD_DIM] = wb[:, src:src + HEAD_DIM]
        if c + STAGE_SLOTS < n_col:
            load(n_merge + c + STAGE_SLOTS).start()
        store(len(merge_dsts) + c).start()
    store(n_stores - 1).start()

    u = _gelu_tanh(h_scr[:, OFF_U:OFF_V])
    v = _gelu_tanh(h_scr[:, OFF_V:OFF_Q])
    vn = _rmsnorm(v, sgu_g_ref[...])
    _store_rows_as_tiles(vn_ref, vn)
    def over_groups(entry):
        return jnp.concatenate(
            [jnp.broadcast_to(entry(g), (1, SGU_GROUP_DIM)) for g in range(SGU_GROUPS)], axis=1)

    w_diag = over_groups(lambda g: sgu_w_ref[g, 0:1, 0:1])
    b_first = over_groups(lambda g: sgu_b_ref[g:g + 1, 0:1])
    a_ref[...] = (u * (vn * w_diag + b_first)).astype(BF16)
    knew_ref[...] = h_scr[:, OFF_K:OFF_VA]
    vnew_ref[...] = h_scr[:, OFF_VA:OFF_GA]
    ga_ref[...] = jax.nn.sigmoid(h_scr[:, OFF_GA:OFF_GB])
    gb_ref[...] = jax.nn.sigmoid(h_scr[:, OFF_GB:IN_WIDTH])
    q = _dot(xn, w_q_scr[...]) * ATTN_SCALE
    qstack = jnp.concatenate([q[:, g * KV_WIDTH:(g + 1) * KV_WIDTH] for g in range(GROUP)], axis=0).astype(BF16)
    qrep = _dot(sel_ref[...], qstack)
    row_kvh = lax.broadcasted_iota(jnp.int32, (nb * N_HEADS, KV_WIDTH), 0) % N_KV_HEADS
    lane_kvh = lax.broadcasted_iota(jnp.int32, (nb * N_HEADS, KV_WIDTH), 1) // HEAD_DIM
    qsel_ref[...] = jnp.where(row_kvh == lane_kvh, qrep, 0.0).astype(BF16)

    for idx in range(n_stores):
        store(idx).wait()


def _sample_proj(xs2d, ln1, w_in_f32, sgu_g, sgu_w, sgu_b, sel, w_oa_f32, w_ob_f32, w_out_f32):
    nb = xs2d.shape[0]
    n_col = IN_WIDTH // COL_BLOCK
    n_merge = 3 * MERGE_PARTS
    in_hbm = pl.BlockSpec(memory_space=pl.ANY)
    merge_shape = jax.ShapeDtypeStruct((D_MODEL, D_MODEL), BF16)
    return pl.pallas_call(
        _sample_proj_kernel,
        grid=(1,),
        in_specs=[_whole((nb, D_MODEL // LANES, LANES)), _whole((1, D_MODEL)), in_hbm,
                  _whole((1, D_MODEL)),
                  pl.BlockSpec((SGU_GROUPS, SUBLANES, CHUNK), lambda j: (0, 0, 0)),
                  _whole((SGU_GROUPS, CHUNK)),
                  _resident((nb * N_HEADS, GROUP * nb)),
                  in_hbm, in_hbm, in_hbm],
        out_specs=[in_hbm, in_hbm,
                   _whole((nb * N_HEADS, KV_WIDTH)), _whole((nb, KV_WIDTH)), _whole((nb, KV_WIDTH)),
                   _whole((nb, D_MODEL // LANES, LANES)), _whole((nb, D_MODEL)),
                   _whole((nb, D_MODEL)), _whole((nb, D_MODEL)),
                   in_hbm, in_hbm, in_hbm],
        out_shape=[
            jax.ShapeDtypeStruct((D_MODEL, IN_WIDTH), BF16),
            jax.ShapeDtypeStruct((D_MODEL, D_MODEL), BF16),
            jax.ShapeDtypeStruct((nb * N_HEADS, KV_WIDTH), BF16),
            jax.ShapeDtypeStruct((nb, KV_WIDTH), F32),
            jax.ShapeDtypeStruct((nb, KV_WIDTH), F32),
            jax.ShapeDtypeStruct((nb, D_MODEL // LANES, LANES), F32),
            jax.ShapeDtypeStruct((nb, D_MODEL), BF16),
            jax.ShapeDtypeStruct((nb, D_MODEL), F32),
            jax.ShapeDtypeStruct((nb, D_MODEL), F32),
            merge_shape, merge_shape, merge_shape,
        ],
        scratch_shapes=[pltpu.VMEM((STAGE_SLOTS, D_MODEL, COL_BLOCK), F32),
                        pltpu.VMEM((STAGE_SLOTS, D_MODEL // MERGE_PARTS, D_MODEL), F32),
                        pltpu.VMEM((D_MODEL, IN_WIDTH), BF16),
                        pltpu.VMEM((D_MODEL, D_MODEL), BF16),
                        pltpu.VMEM((3, D_MODEL, D_MODEL), BF16),
                        pltpu.VMEM((nb, IN_WIDTH), F32),
                        pltpu.SemaphoreType.DMA((n_merge + n_col,)),
                        pltpu.SemaphoreType.DMA((3 + n_col + 1,))],
        compiler_params=_params(),
        name="sample_proj",
    )(xs2d, ln1, w_in_f32, sgu_g, sgu_w, sgu_b, sel, w_oa_f32, w_ob_f32, w_out_f32)


def _sample_attn_stages(qsel_ref, knew_ref, vnew_ref, ck_ref, cv_ref, bias_ref, sink_ref, upper,
                        o_ref, nk_ref, nv_ref):
    bs = ck_ref.shape[0]
    assert knew_ref.shape[0] == 2 * bs
    row0 = jnp.where(upper, bs, 0)
    row_kvh = lax.broadcasted_iota(jnp.int32, (N_HEADS, KV_WIDTH), 0) % N_KV_HEADS
    lane_kvh = lax.broadcasted_iota(jnp.int32, (N_HEADS, KV_WIDTH), 1) // HEAD_DIM
    own = row_kvh == lane_kvh
    bias = bias_ref[...]
    head_row = lax.broadcasted_iota(jnp.int32, (N_HEADS, 1), 0)
    sink = jnp.zeros((N_HEADS, 1), F32)
    for h in range(N_HEADS):
        g, kvh = divmod(h, N_KV_HEADS)
        sink = jnp.where(head_row == h, sink_ref[0, kvh * GROUP + g], sink)

    qss = [qsel_ref[i * N_HEADS:(i + 1) * N_HEADS, :] for i in range(bs)]
    kns = [knew_ref[pl.ds(row0 + i, 1), :] for i in range(bs)]
    vws = [vnew_ref[pl.ds(row0 + i, 1), :] for i in range(bs)]
    scores = [_dot(qss[i], ck_ref[i].astype(BF16)) + bias for i in range(bs)]
    yield
    probs = []
    for i in range(bs):
        s = scores[i]
        s_new = jnp.sum(qss[i].astype(F32) * kns[i], axis=1, keepdims=True)
        m = jnp.maximum(jnp.maximum(jnp.max(s, axis=1, keepdims=True), s_new), sink)
        p = jnp.exp(s - m)
        p_new = jnp.exp(s_new - m)
        denom = jnp.sum(p, axis=1, keepdims=True) + p_new + jnp.exp(sink - m)
        probs.append((p.astype(BF16), p_new, denom))

    kn_t = knew_ref[...].T
    vw_t = vnew_ref[...].T
    last_lane = lax.broadcasted_iota(jnp.int32, (KV_WIDTH, WINDOW), 1) == WINDOW - 1
    for i in range(bs):
        kn_col = jnp.where(upper, kn_t[:, bs + i:bs + i + 1], kn_t[:, i:i + 1])
        vw_col = jnp.where(upper, vw_t[:, bs + i:bs + i + 1], vw_t[:, i:i + 1])
        nk_ref[i] = jnp.where(last_lane, kn_col, pltpu.roll(ck_ref[i], WINDOW - 1, 1))
        nv_ref[i] = jnp.where(last_lane, vw_col, pltpu.roll(cv_ref[i], WINDOW - 1, 1))
    yield
    for i in range(bs):
        p, p_new, denom = probs[i]
        o = (_dot_nt(p, cv_ref[i].astype(BF16)) + p_new * vws[i]) / denom
        o_ref[i * N_HEADS:(i + 1) * N_HEADS, :] = jnp.where(own, o, 0.0).astype(BF16)


def _head_perm(v):
    return v.reshape(N_KV_HEADS, GROUP).T.reshape(N_HEADS)


def _alibi_slopes():
    h = np.arange(1, N_HEADS + 1, dtype=np.float32)
    return np.exp2(-8.0 * h / N_HEADS).astype(np.float32)


def _selection_matrix(nb):
    r = np.arange(nb * N_HEADS)
    c = np.arange(GROUP * nb)
    same_sample = (r[:, None] // N_HEADS) == (c[None, :] % nb)
    same_member = ((r[:, None] % N_HEADS) // N_KV_HEADS) == (c[None, :] // nb)
    return (same_sample & same_member).astype(np.float32)


def kernel(x_prompt, x_sample, cache_k_win, cache_v_win, ln1_g, w_in, sgu_norm_g, sgu_w, sgu_b, attn_sinks,
           w_oa, w_ob, w_out, ln2_g, w_up, w_down, lnf_g):
    batch, seq, _ = x_prompt.shape
    dec_batch, dec_seq, _ = x_sample.shape
    depth = w_in.shape[0]
    assert depth == 1 and dec_seq == 1
    assert seq % TOKEN_BLOCK == 0 and TOKEN_BLOCK % CHUNK == 0
    assert (batch * seq) % FFN_BLOCK == 0
    assert w_in.shape[-1] == IN_WIDTH

    ln1 = ln1_g[0].reshape(1, D_MODEL)
    ln2 = ln2_g[0].reshape(1, D_MODEL)
    lnf = lnf_g.reshape(1, D_MODEL)
    sgu_g = sgu_norm_g[0].reshape(1, D_MODEL)
    slopes_p = _head_perm(_alibi_slopes())

    xs2d = x_sample.reshape(dec_batch, D_MODEL // LANES, LANES)
    sel_np = _selection_matrix(dec_batch)
    sel = jnp.asarray(sel_np, BF16)
    selt = jnp.asarray(sel_np.T, BF16)
    bias_s = -slopes_p[:, None] * (WINDOW - np.arange(WINDOW, dtype=np.float32))[None, :]

    w_in_b, w_q_b, qsel, knew, vnew, vn, a_s, ga_s, gb_s, w_oa_b, w_ob_b, w_out_b = _sample_proj(
        xs2d, ln1, w_in[0], sgu_g, sgu_w[0], sgu_b[0], sel, w_oa[0], w_ob[0], w_out[0])
    def to_feature_major(c):
        return c[0].transpose(0, 2, 3, 1).reshape(dec_batch, KV_WIDTH, WINDOW)

    def from_feature_major(c):
        return c.reshape(c.shape[0], N_KV_HEADS, HEAD_DIM, WINDOW).transpose(0, 3, 1, 2)[None]

    steps = (batch * seq) // TOKEN_BLOCK
    per_step = dec_batch // steps

    x1, kwin, vwin, w_up_b, w_down_b, o_s, nk, nv = _mix_prompt(
        x_prompt.reshape(batch * seq, D_MODEL), ln1, w_in_b, w_q_b, sgu_g, sgu_w[0], sgu_b[0],
        attn_sinks, slopes_p * LOG2E, w_oa_b, w_ob_b, w_out_b, w_up[0], w_down[0],
        qsel.reshape(steps, per_step * N_HEADS, KV_WIDTH), knew, vnew,
        to_feature_major(cache_k_win), to_feature_major(cache_v_win), bias_s,
        batch=batch, seq=seq)

    y_prompt, y_sample = _ffn(x1, ln2, w_up_b, w_down_b, lnf,
                              o_s.reshape(dec_batch * N_HEADS, KV_WIDTH), selt, a_s, ga_s, gb_s, xs2d,
                              w_oa_b, w_ob_b, w_out_b)

    return (y_prompt.reshape(batch, seq, D_MODEL),
            y_sample.reshape(dec_batch, dec_seq, D_MODEL),
            from_feature_major(kwin), from_feature_major(vwin),
            from_feature_major(nk), from_feature_major(nv),
            vn.reshape(depth, dec_batch, dec_seq, D_MODEL))
```

```python
import functools
import math

import numpy as np
import jax
import jax.numpy as jnp
from jax import lax
from jax.experimental import pallas as pl
from jax.experimental.pallas import tpu as pltpu

D_MODEL = 1024
N_HEADS = 16
HEAD_DIM = 64
N_KV_HEADS = 4
GROUP = N_HEADS // N_KV_HEADS
KV_WIDTH = N_KV_HEADS * HEAD_DIM
WINDOW = 128
CHUNK = 128
SGU_GROUPS = 8
SGU_GROUP_DIM = D_MODEL // SGU_GROUPS
D_FF = 4 * D_MODEL
FF_SLAB = 1024
EPS = 1e-6
NEG_BIG = -1e30
ATTN_SCALE = HEAD_DIM ** -0.5
LOG2E = math.log2(math.e)

OFF_U, OFF_V, OFF_Q, OFF_K, OFF_VA, OFF_GA, OFF_GB, IN_WIDTH = 0, 1024, 2048, 3072, 3328, 3584, 4608, 5632
W_STEP = IN_WIDTH // 4

TOKEN_BLOCK = 512
FFN_BLOCK = 1024
SUBLANES = 8
LANES = 128
VMEM_LIMIT_BYTES = 58 * 1024 * 1024

F32 = jnp.float32
BF16 = jnp.bfloat16


def _rmsnorm(x, g):
    ms = jnp.mean(x * x, axis=-1, keepdims=True)
    return x * lax.rsqrt(ms + EPS) * g


def _gelu_tanh(x):
    c = math.sqrt(2.0 / math.pi)
    return x * (0.5 * (1.0 + jnp.tanh(c * (x + 0.044715 * (x * x * x)))))


def _dot(a, b):
    return jnp.dot(a, b, preferred_element_type=F32)


def _dot_nt(a, b):
    return lax.dot_general(a, b, (((1,), (1,)), ((), ())), preferred_element_type=F32)


def _resident(shape):
    zeros = (0,) * len(shape)
    return pl.BlockSpec(shape, lambda *_: zeros, pipeline_mode=pl.Buffered(1))


def _whole(shape):
    zeros = (0,) * len(shape)
    return pl.BlockSpec(shape, lambda *_: zeros)


def _params():
    return pltpu.CompilerParams(dimension_semantics=("arbitrary",), vmem_limit_bytes=VMEM_LIMIT_BYTES)


def _mix_prompt_kernel(x_ref, ln1_ref, w_in_ref, w_q_ref, sgu_g_ref, sgu_w_ref, sgu_b_ref,
                       sink_ref, slope_ref, w_oa_ref, w_ob_ref, w_out_ref, w_up_blk_ref, w_down_blk_ref,
                       s_qsel_ref, s_knew_ref, s_vnew_ref, s_ck_ref, s_cv_ref, s_bias_ref,
                       x1_ref, kwin_ref, vwin_ref, w_up_bf_ref, w_down_bf_ref, s_o_ref, s_nk_ref, s_nv_ref,
                       qs_scr, kt_scr, vm_scr, kprev_scr, vprev_scr, vn_scr, u_scr, gate_scr, a_scr, b_scr, wt_scr,
                       bias_ref, bexp_ref,
                       *, steps_per_seq):
    step = pl.program_id(0)
    tb = x_ref.shape[0]
    nblk = tb // CHUNK
    first = (step % steps_per_seq) == 0
    rd = step % 2
    wr = 1 - rd

    @pl.when(step == 0)
    def _():
        row = lax.broadcasted_iota(jnp.int32, (CHUNK, CHUNK), 0)
        col = lax.broadcasted_iota(jnp.int32, (CHUNK, CHUNK), 1)
        b_t = sgu_b_ref[...].T
        for g in range(SGU_GROUPS):
            wt_scr[g] = jnp.where(row >= col, sgu_w_ref[g], 0.0).astype(BF16)
            bexp_ref[:, g * SGU_GROUP_DIM:(g + 1) * SGU_GROUP_DIM] = jnp.broadcast_to(
                b_t[:, g:g + 1], (CHUNK, SGU_GROUP_DIM))
        dist = (lax.broadcasted_iota(jnp.int32, (CHUNK, 2 * CHUNK), 0) + CHUNK
                - lax.broadcasted_iota(jnp.int32, (CHUNK, 2 * CHUNK), 1))
        in_band = jnp.logical_and(dist >= 0, dist <= WINDOW)
        dist_f = dist.astype(F32)
        for h in range(N_HEADS):
            bias_ref[h] = jnp.where(in_band, -slope_ref[h] * dist_f, NEG_BIG)
        kt_scr[...] = jnp.zeros(kt_scr.shape, BF16)
        vm_scr[...] = jnp.zeros(vm_scr.shape, BF16)
        kprev_scr[...] = jnp.zeros(kprev_scr.shape, BF16)
        vprev_scr[...] = jnp.zeros(vprev_scr.shape, BF16)

    @pl.when(first)
    def _():
        kprev_scr[rd] = jnp.zeros(kprev_scr.shape[1:], BF16)
        vprev_scr[rd] = jnp.zeros(vprev_scr.shape[1:], BF16)

    w_up_bf_ref[...] = w_up_blk_ref[...].astype(BF16)
    w_down_bf_ref[...] = w_down_blk_ref[...].astype(BF16)

    x = x_ref[...]
    quarters = [slice(i * tb // 4, (i + 1) * tb // 4) for i in range(4)]
    xn_parts = [_rmsnorm(x[p], ln1_ref[...]).astype(BF16) for p in quarters]
    xn = jnp.concatenate(xn_parts, axis=0)

    q = jnp.concatenate([_dot(xn_p, w_q_ref[...]) for xn_p in xn_parts], axis=0)
    k = _dot(xn, w_in_ref[:, OFF_K:OFF_VA])
    va = _dot(xn, w_in_ref[:, OFF_VA:OFF_GA])
    h_v = _dot(xn, w_in_ref[:, OFF_V:OFF_Q])
    h_u0 = _dot(xn, w_in_ref[:, OFF_U:OFF_U + D_MODEL // 2])

    q = (q * (ATTN_SCALE * LOG2E)).astype(BF16)
    for c in range(nblk):
        for g in range(GROUP):
            qs_scr[c, g * CHUNK:(g + 1) * CHUNK, :] = q[c * CHUNK:(c + 1) * CHUNK, g * KV_WIDTH:(g + 1) * KV_WIDTH]

    kt_f32 = k.T
    kwin_ref[...] = kt_f32[:, tb - WINDOW:]
    vwin_ref[...] = va[tb - WINDOW:, :].T
    kt = kt_f32.astype(BF16)
    vab = va.astype(BF16)
    for kvh in range(N_KV_HEADS):
        own = slice(kvh * HEAD_DIM, (kvh + 1) * HEAD_DIM)
        for c in range(nblk):
            kt_scr[kvh, c, own, :] = kt[own, c * CHUNK:(c + 1) * CHUNK]
        vm_scr[kvh, :, own] = vab[:, own]
        kprev_scr[wr, kvh, own, :] = kt[own, tb - WINDOW:]
        vprev_scr[wr, kvh, :, own] = vab[tb - WINDOW:, own]

    no_prev = jnp.where(
        jnp.logical_and(first, lax.broadcasted_iota(jnp.int32, (CHUNK, 2 * CHUNK), 1) < CHUNK), NEG_BIG, 0.0)

    def attn_scores(c):
        qs = qs_scr[c]
        out = []
        for kvh in range(N_KV_HEADS):
            k_prev = kprev_scr[rd, kvh] if c == 0 else kt_scr[kvh, c - 1]
            out.append(_dot(qs, jnp.concatenate([k_prev, kt_scr[kvh, c]], axis=1)))
        return out

    def attn_softmax(c, scores):
        out = []
        for kvh in range(N_KV_HEADS):
            ps = []
            for g in range(GROUP):
                h = g * N_KV_HEADS + kvh
                s = scores[kvh][g * CHUNK:(g + 1) * CHUNK, :] + bias_ref[h]
                if c == 0:
                    s = s + no_prev
                sink = sink_ref[0, kvh * GROUP + g] * LOG2E
                m = jnp.max(s, axis=1, keepdims=True)
                p = jnp.exp2(s - m)
                denom = jnp.sum(p, axis=1, keepdims=True) + jnp.exp2(sink - m)
                ps.append((p * (1.0 / denom)).astype(BF16))
            out.append(jnp.concatenate(ps, axis=0))
        return out

    def attn_values(c, probs):
        rows = slice(c * CHUNK, (c + 1) * CHUNK)
        acc = None
        for kvh in range(N_KV_HEADS):
            if c == 0:
                v_band = jnp.concatenate([vprev_scr[rd, kvh], vm_scr[kvh, 0:CHUNK, :]], axis=0)
            else:
                v_band = vm_scr[kvh, (c - 1) * CHUNK:(c + 1) * CHUNK, :]
            o = _dot(probs[kvh], v_band)
            acc = o if acc is None else acc + o
        for g in range(GROUP):
            b_scr[rows, g * KV_WIDTH:(g + 1) * KV_WIDTH] = acc[g * CHUNK:(g + 1) * CHUNK, :].astype(BF16)

    def sgu_chunk(c):
        rows = slice(c * CHUNK, (c + 1) * CHUNK)
        vn_c = vn_scr[rows, :]
        mixed = jnp.concatenate(
            [_dot(wt_scr[g], vn_c[:, g * SGU_GROUP_DIM:(g + 1) * SGU_GROUP_DIM]) for g in range(SGU_GROUPS)],
            axis=1) + bexp_ref[...]
        a_scr[rows, :] = (u_scr[rows, :] * mixed).astype(BF16)

    half = D_MODEL // 2

    def tail_u(h, lo):
        u_scr[:, lo:lo + half] = _gelu_tanh(h)

    def tail_ga(h, lo):
        gate_scr[0, :, lo:lo + half] = jax.nn.sigmoid(h)

    def tail_gb(h, lo):
        gate_scr[1, :, lo:lo + half] = jax.nn.sigmoid(h)

    fillers = [
        (lambda: _dot(xn, w_in_ref[:, OFF_U + half:OFF_V]), lambda h: tail_u(h, half)),
        (lambda: _dot(xn, w_in_ref[:, OFF_GA:OFF_GA + half]), lambda h: tail_ga(h, 0)),
        (lambda: _dot(xn, w_in_ref[:, OFF_GA + half:OFF_GB]), lambda h: tail_ga(h, half)),
        (lambda: _dot(xn, w_in_ref[:, OFF_GB:IN_WIDTH]),
         lambda h: (tail_gb(h[:, :half], 0), tail_gb(h[:, half:], half))),
    ]
    vn_scr[...] = _rmsnorm(_gelu_tanh(h_v), sgu_g_ref[...]).astype(BF16)
    tail_u(h_u0, 0)
    sgu_after = {nblk - 2: range(0, nblk // 2), nblk - 1: range(nblk // 2, nblk)}
    for c in range(nblk):
        scores = attn_scores(c)
        proj = fillers[c][0]() if c < len(fillers) else None
        for cc in sgu_after.get(c, ()):
            sgu_chunk(cc)
        probs = attn_softmax(c, scores)
        attn_values(c, probs)
        if proj is not None:
            fillers[c][1](proj)
    for matmul, tail in fillers[nblk:]:
        tail(matmul())

    sample = _sample_attn_stages(s_qsel_ref, s_knew_ref, s_vnew_ref, s_ck_ref, s_cv_ref, s_bias_ref, sink_ref,
                                 step % 2 == 1, s_o_ref, s_nk_ref, s_nv_ref)
    next(sample)
    branch_a = _dot(a_scr[...], w_oa_ref[...])
    next(sample)
    branch_b = _dot(b_scr[...], w_ob_ref[...])
    for _ in sample:
        pass
    hm = gate_scr[0] * branch_a + gate_scr[1] * branch_b
    x1_ref[...] = x + _dot(hm.astype(BF16), w_out_ref[...])


def _mix_prompt(x2d, ln1, w_in, w_q, sgu_g, sgu_w, sgu_b, sinks, slopes, w_oa, w_ob, w_out,
                w_up_f32, w_down_f32,
                s_qsel, s_knew, s_vnew, s_cache_k, s_cache_v, s_bias, *, batch, seq):
    n = x2d.shape[0]
    tb = TOKEN_BLOCK
    nblk = tb // CHUNK
    steps = n // tb
    steps_per_seq = seq // tb
    nb = s_cache_k.shape[0]
    per_step = nb // steps
    assert per_step * steps == nb and 2 * per_step == SUBLANES and s_knew.shape == (nb, KV_WIDTH)
    s_head_block = pl.BlockSpec((None, per_step * N_HEADS, KV_WIDTH), lambda i: (i, 0, 0))
    s_new_block = pl.BlockSpec((SUBLANES, KV_WIDTH), lambda i: (i // 2, 0))
    s_cache_block = pl.BlockSpec((per_step, KV_WIDTH, WINDOW), lambda i: (i, 0, 0))
    row_block = pl.BlockSpec((tb, D_MODEL), lambda i: (i, 0))
    win_block = pl.BlockSpec((None, KV_WIDTH, WINDOW), lambda i: (i // steps_per_seq, 0, 0))
    up_block = pl.BlockSpec((D_MODEL // steps, D_FF), lambda i: (i, 0))
    down_block = pl.BlockSpec((D_FF // steps, D_MODEL), lambda i: (i, 0))
    return pl.pallas_call(
        functools.partial(_mix_prompt_kernel, steps_per_seq=steps_per_seq),
        grid=(n // tb,),
        in_specs=[
            row_block,
            _resident((1, D_MODEL)),
            _resident((D_MODEL, IN_WIDTH)),
            _resident((D_MODEL, D_MODEL)),
            _resident((1, D_MODEL)),
            _resident((SGU_GROUPS, CHUNK, CHUNK)),
            _resident((SGU_GROUPS, CHUNK)),
            pl.BlockSpec(memory_space=pltpu.SMEM),
            pl.BlockSpec(memory_space=pltpu.SMEM),
            _resident((D_MODEL, D_MODEL)),
            _resident((D_MODEL, D_MODEL)),
            _resident((D_MODEL, D_MODEL)),
            up_block,
            down_block,
            s_head_block, s_new_block, s_new_block, s_cache_block, s_cache_block,
            _resident((N_HEADS, WINDOW)),
        ],
        out_specs=[row_block, win_block, win_block, up_block, down_block,
                   s_head_block, s_cache_block, s_cache_block],
        out_shape=[
            jax.ShapeDtypeStruct((n, D_MODEL), F32),
            jax.ShapeDtypeStruct((batch, KV_WIDTH, WINDOW), F32),
            jax.ShapeDtypeStruct((batch, KV_WIDTH, WINDOW), F32),
            jax.ShapeDtypeStruct((D_MODEL, D_FF), BF16),
            jax.ShapeDtypeStruct((D_FF, D_MODEL), BF16),
            jax.ShapeDtypeStruct((steps, per_step * N_HEADS, KV_WIDTH), BF16),
            jax.ShapeDtypeStruct((nb, KV_WIDTH, WINDOW), F32),
            jax.ShapeDtypeStruct((nb, KV_WIDTH, WINDOW), F32),
        ],
        scratch_shapes=[
            pltpu.VMEM((nblk, GROUP * CHUNK, KV_WIDTH), BF16),
            pltpu.VMEM((N_KV_HEADS, nblk, KV_WIDTH, CHUNK), BF16),
            pltpu.VMEM((N_KV_HEADS, tb, KV_WIDTH), BF16),
            pltpu.VMEM((2, N_KV_HEADS, KV_WIDTH, CHUNK), BF16),
            pltpu.VMEM((2, N_KV_HEADS, WINDOW, KV_WIDTH), BF16),
            pltpu.VMEM((tb, D_MODEL), BF16),
            pltpu.VMEM((tb, D_MODEL), F32),
            pltpu.VMEM((2, tb, D_MODEL), F32),
            pltpu.VMEM((tb, D_MODEL), BF16),
            pltpu.VMEM((tb, D_MODEL), BF16),
            pltpu.VMEM((SGU_GROUPS, CHUNK, CHUNK), BF16),
            pltpu.VMEM((N_HEADS, CHUNK, 2 * CHUNK), F32),
            pltpu.VMEM((CHUNK, D_MODEL), F32),
        ],
        compiler_params=_params(),
        name="mix_prompt",
    )(x2d, ln1, w_in, w_q, sgu_g, sgu_w, sgu_b, sinks, slopes, w_oa, w_ob, w_out, w_up_f32, w_down_f32,
      s_qsel, s_knew, s_vnew, s_cache_k, s_cache_v, s_bias)


def _store_rows_as_tiles(ref, rows):
    for c in range(rows.shape[1] // LANES):
        ref[:, c, :] = rows[:, c * LANES:(c + 1) * LANES]


def _rows_from_tiles(ref):
    return jnp.concatenate([ref[:, c, :] for c in range(ref.shape[1])], axis=1)


def _ffn_rows(x, store, ln2_ref, w_up_ref, w_down_ref, lnf_ref, *, row_parts):
    n_slabs = D_FF // FF_SLAB
    m = x.shape[0]
    parts = [slice(i * m // row_parts, (i + 1) * m // row_parts) for i in range(row_parts)]

    def up(j, xn):
        return _dot(xn, w_up_ref[:, j * FF_SLAB:(j + 1) * FF_SLAB])

    xns = [_rmsnorm(x[p], ln2_ref[...]).astype(BF16) for p in parts]
    h_next = jnp.concatenate([up(0, xn_p) for xn_p in xns], axis=0) if row_parts > 1 else up(0, xns[0])
    xn = jnp.concatenate(xns, axis=0) if row_parts > 1 else xns[0]

    acc = x
    for j in range(n_slabs):
        h = h_next
        if j + 1 < n_slabs:
            h_next = up(j + 1, xn)
        h = jnp.square(jnp.maximum(h, 0.0)).astype(BF16)
        w_d = w_down_ref[j * FF_SLAB:(j + 1) * FF_SLAB, :]
        if j + 1 < n_slabs:
            acc = acc + _dot(h, w_d)
        else:
            for p in parts:
                store(p, _rmsnorm(acc[p] + _dot(h[p], w_d), lnf_ref[...]))


def _ffn_kernel(x_ref, ln2_ref, w_up_ref, w_down_ref, lnf_ref,
                s_o_ref, s_selt_hbm, s_a_ref, s_ga_ref, s_gb_ref, s_x_ref, w_oa_hbm, w_ob_hbm, w_out_hbm,
                y_ref, ys_ref, s_selt_ref, w_oa_ref, w_ob_ref, w_out_ref, late_sem):
    i = pl.program_id(0)
    last = pl.num_programs(0) - 1

    def late_copies():
        pairs = ((s_selt_hbm, s_selt_ref), (w_oa_hbm, w_oa_ref), (w_ob_hbm, w_ob_ref), (w_out_hbm, w_out_ref))
        return [pltpu.make_async_copy(src, dst, late_sem.at[k]) for k, (src, dst) in enumerate(pairs)]

    @pl.when(i == 0)
    def _():
        for copy in late_copies():
            copy.start()

    @pl.when(i < last)
    def _():
        def store(p, rows):
            y_ref[p, :] = rows

        _ffn_rows(x_ref[...], store, ln2_ref, w_up_ref, w_down_ref, lnf_ref, row_parts=4)

    @pl.when(i == last)
    def _():
        nb = s_x_ref.shape[0]
        for copy in late_copies():
            copy.wait()
        bst = _dot(s_selt_ref[...], s_o_ref[...]).astype(BF16)
        ob = _dot(bst[0:nb, :], w_ob_ref[0:KV_WIDTH, :])
        for g in range(1, GROUP):
            ob = ob + _dot(bst[g * nb:(g + 1) * nb, :], w_ob_ref[g * KV_WIDTH:(g + 1) * KV_WIDTH, :])
        hm = s_ga_ref[...] * _dot(s_a_ref[...], w_oa_ref[...]) + s_gb_ref[...] * ob
        xs1 = _rows_from_tiles(s_x_ref) + _dot(hm.astype(BF16), w_out_ref[...])
        _ffn_rows(xs1, lambda p, rows: _store_rows_as_tiles(ys_ref, rows), ln2_ref, w_up_ref, w_down_ref, lnf_ref,
                  row_parts=1)


def _ffn(x2d, ln2, w_up, w_down, lnf, s_o, s_selt, s_a, s_ga, s_gb, xs2d, w_oa, w_ob, w_out):
    n = x2d.shape[0]
    nb = xs2d.shape[0]
    n_prompt_steps = n // FFN_BLOCK
    row_block = pl.BlockSpec((FFN_BLOCK, D_MODEL), lambda i: (jnp.minimum(i, n_prompt_steps - 1), 0))
    assert n_prompt_steps >= 1
    in_hbm = pl.BlockSpec(memory_space=pl.ANY)
    return pl.pallas_call(
        _ffn_kernel,
        grid=(n_prompt_steps + 1,),
        in_specs=[row_block, _resident((1, D_MODEL)), _resident((D_MODEL, D_FF)),
                  _resident((D_FF, D_MODEL)), _resident((1, D_MODEL)),
                  _resident((nb * N_HEADS, KV_WIDTH)), in_hbm,
                  _resident((nb, D_MODEL)), _resident((nb, D_MODEL)), _resident((nb, D_MODEL)),
                  _resident((nb, D_MODEL // LANES, LANES)),
                  in_hbm, in_hbm, in_hbm],
        out_specs=[row_block, _whole((nb, D_MODEL // LANES, LANES))],
        out_shape=[jax.ShapeDtypeStruct((n, D_MODEL), F32),
                   jax.ShapeDtypeStruct((nb, D_MODEL // LANES, LANES), F32)],
        scratch_shapes=[pltpu.VMEM((GROUP * nb, nb * N_HEADS), BF16),
                        pltpu.VMEM((D_MODEL, D_MODEL), BF16), pltpu.VMEM((D_MODEL, D_MODEL), BF16),
                        pltpu.VMEM((D_MODEL, D_MODEL), BF16),
                        pltpu.SemaphoreType.DMA((4,))],
        compiler_params=_params(),
        name="ffn",
    )(x2d, ln2, w_up, w_down, lnf, s_o, s_selt, s_a, s_ga, s_gb, xs2d, w_oa, w_ob, w_out)


def _sample_proj_kernel(x_ref, ln1_ref, w_blk_ref, sgu_g_ref, sgu_w_ref, sgu_b_ref, sel_ref,
                        w_oa_blk_ref, w_ob_blk_ref, w_out_blk_ref,
                        w_bf_ref, w_q_bf_ref, qsel_ref, knew_ref, vnew_ref, vn_ref, a_ref,
                        ga_ref, gb_ref, w_oa_bf_ref, w_ob_bf_ref, w_out_bf_ref, xn_scr, h_scr):
    j = pl.program_id(0)
    nb = x_ref.shape[0]
    n_steps = IN_WIDTH // W_STEP

    @pl.when(j == 0)
    def _():
        xn_scr[...] = _rmsnorm(_rows_from_tiles(x_ref), ln1_ref[...]).astype(BF16)

    w_oa_bf_ref[...] = w_oa_blk_ref[...].astype(BF16)
    w_out_bf_ref[...] = w_out_blk_ref[...].astype(BF16)
    assert w_ob_blk_ref.shape[0] == GROUP * HEAD_DIM and n_steps == N_KV_HEADS
    for g in range(GROUP):
        dst = pl.multiple_of(g * KV_WIDTH + j * HEAD_DIM, HEAD_DIM)
        w_ob_bf_ref[pl.ds(dst, HEAD_DIM), :] = w_ob_blk_ref[g * HEAD_DIM:(g + 1) * HEAD_DIM, :].astype(BF16)

    wb = w_blk_ref[...].astype(BF16)
    w_bf_ref[...] = wb
    h_scr[j] = _dot(xn_scr[...], wb)

    assert W_STEP % HEAD_DIM == 0
    for step in range(n_steps):
        heads = [h for h in range(N_HEADS) if step * W_STEP <= OFF_Q + h * HEAD_DIM < (step + 1) * W_STEP]
        if heads:
            @pl.when(j == step)
            def _(step=step, heads=heads):
                for head in heads:
                    src = OFF_Q + head * HEAD_DIM - step * W_STEP
                    kvh, g = divmod(head, GROUP)
                    dst = g * KV_WIDTH + kvh * HEAD_DIM
                    w_q_bf_ref[:, dst:dst + HEAD_DIM] = wb[:, src:src + HEAD_DIM]

    @pl.when(j == n_steps - 1)
    def _():
        def cols(lo, hi):
            pieces = []
            for step in range(n_steps):
                a, b = max(lo, step * W_STEP), min(hi, (step + 1) * W_STEP)
                if a < b:
                    pieces.append(h_scr[step, :, a - step * W_STEP:b - step * W_STEP])
            return pieces[0] if len(pieces) == 1 else jnp.concatenate(pieces, axis=1)

        u = _gelu_tanh(cols(OFF_U, OFF_V))
        v = _gelu_tanh(cols(OFF_V, OFF_Q))
        vn = _rmsnorm(v, sgu_g_ref[...])
        _store_rows_as_tiles(vn_ref, vn)
        def over_groups(entry):
            return jnp.concatenate(
                [jnp.broadcast_to(entry(g), (1, SGU_GROUP_DIM)) for g in range(SGU_GROUPS)], axis=1)

        w_diag = over_groups(lambda g: sgu_w_ref[g, 0:1, 0:1])
        b_first = over_groups(lambda g: sgu_b_ref[g:g + 1, 0:1])
        a_ref[...] = (u * (vn * w_diag + b_first)).astype(BF16)
        knew_ref[...] = cols(OFF_K, OFF_VA)
        vnew_ref[...] = cols(OFF_VA, OFF_GA)
        ga_ref[...] = jax.nn.sigmoid(cols(OFF_GA, OFF_GB))
        gb_ref[...] = jax.nn.sigmoid(cols(OFF_GB, IN_WIDTH))
        q = _dot(xn_scr[...], w_q_bf_ref[...]) * ATTN_SCALE
        qstack = jnp.concatenate([q[:, g * KV_WIDTH:(g + 1) * KV_WIDTH] for g in range(GROUP)], axis=0).astype(BF16)
        qrep = _dot(sel_ref[...], qstack)
        row_kvh = lax.broadcasted_iota(jnp.int32, (nb * N_HEADS, KV_WIDTH), 0) % N_KV_HEADS
        lane_kvh = lax.broadcasted_iota(jnp.int32, (nb * N_HEADS, KV_WIDTH), 1) // HEAD_DIM
        qsel_ref[...] = jnp.where(row_kvh == lane_kvh, qrep, 0.0).astype(BF16)


def _sample_proj(xs2d, ln1, w_in_f32, sgu_g, sgu_w, sgu_b, sel, w_oa_f32, w_ob_f32, w_out_f32):
    nb = xs2d.shape[0]
    n_blocks = IN_WIDTH // W_STEP
    w_block = pl.BlockSpec((D_MODEL, W_STEP), lambda j: (0, j))
    merge_block = pl.BlockSpec((D_MODEL // n_blocks, D_MODEL), lambda j: (j, 0))
    merge_shape = jax.ShapeDtypeStruct((D_MODEL, D_MODEL), BF16)
    return pl.pallas_call(
        _sample_proj_kernel,
        grid=(n_blocks,),
        in_specs=[_whole((nb, D_MODEL // LANES, LANES)), _whole((1, D_MODEL)), w_block,
                  _whole((1, D_MODEL)),
                  pl.BlockSpec((SGU_GROUPS, SUBLANES, CHUNK), lambda j: (0, 0, 0)),
                  _whole((SGU_GROUPS, CHUNK)),
                  _resident((nb * N_HEADS, GROUP * nb)),
                  merge_block, merge_block, merge_block],
        out_specs=[w_block, _whole((D_MODEL, D_MODEL)),
                   _whole((nb * N_HEADS, KV_WIDTH)), _whole((nb, KV_WIDTH)), _whole((nb, KV_WIDTH)),
                   _whole((nb, D_MODEL // LANES, LANES)), _whole((nb, D_MODEL)),
                   _whole((nb, D_MODEL)), _whole((nb, D_MODEL)),
                   merge_block, _whole((D_MODEL, D_MODEL)), merge_block],
        out_shape=[
            jax.ShapeDtypeStruct((D_MODEL, IN_WIDTH), BF16),
            jax.ShapeDtypeStruct((D_MODEL, D_MODEL), BF16),
            jax.ShapeDtypeStruct((nb * N_HEADS, KV_WIDTH), BF16),
            jax.ShapeDtypeStruct((nb, KV_WIDTH), F32),
            jax.ShapeDtypeStruct((nb, KV_WIDTH), F32),
            jax.ShapeDtypeStruct((nb, D_MODEL // LANES, LANES), F32),
            jax.ShapeDtypeStruct((nb, D_MODEL), BF16),
            jax.ShapeDtypeStruct((nb, D_MODEL), F32),
            jax.ShapeDtypeStruct((nb, D_MODEL), F32),
            merge_shape, merge_shape, merge_shape,
        ],
        scratch_shapes=[pltpu.VMEM((nb, D_MODEL), BF16), pltpu.VMEM((n_blocks, nb, W_STEP), F32)],
        compiler_params=_params(),
        name="sample_proj",
    )(xs2d, ln1, w_in_f32, sgu_g, sgu_w, sgu_b, sel, w_oa_f32, w_ob_f32, w_out_f32)


def _sample_attn_stages(qsel_ref, knew_ref, vnew_ref, ck_ref, cv_ref, bias_ref, sink_ref, upper,
                        o_ref, nk_ref, nv_ref):
    bs = ck_ref.shape[0]
    assert knew_ref.shape[0] == 2 * bs
    row0 = jnp.where(upper, bs, 0)
    row_kvh = lax.broadcasted_iota(jnp.int32, (N_HEADS, KV_WIDTH), 0) % N_KV_HEADS
    lane_kvh = lax.broadcasted_iota(jnp.int32, (N_HEADS, KV_WIDTH), 1) // HEAD_DIM
    own = row_kvh == lane_kvh
    bias = bias_ref[...]
    head_row = lax.broadcasted_iota(jnp.int32, (N_HEADS, 1), 0)
    sink = jnp.zeros((N_HEADS, 1), F32)
    for h in range(N_HEADS):
        g, kvh = divmod(h, N_KV_HEADS)
        sink = jnp.where(head_row == h, sink_ref[0, kvh * GROUP + g], sink)

    qss = [qsel_ref[i * N_HEADS:(i + 1) * N_HEADS, :] for i in range(bs)]
    kns = [knew_ref[pl.ds(row0 + i, 1), :] for i in range(bs)]
    vws = [vnew_ref[pl.ds(row0 + i, 1), :] for i in range(bs)]
    scores = [_dot(qss[i], ck_ref[i].astype(BF16)) + bias for i in range(bs)]
    yield
    probs = []
    for i in range(bs):
        s = scores[i]
        s_new = jnp.sum(qss[i].astype(F32) * kns[i], axis=1, keepdims=True)
        m = jnp.maximum(jnp.maximum(jnp.max(s, axis=1, keepdims=True), s_new), sink)
        p = jnp.exp(s - m)
        p_new = jnp.exp(s_new - m)
        denom = jnp.sum(p, axis=1, keepdims=True) + p_new + jnp.exp(sink - m)
        probs.append((p.astype(BF16), p_new, denom))

    kn_t = knew_ref[...].T
    vw_t = vnew_ref[...].T
    last_lane = lax.broadcasted_iota(jnp.int32, (KV_WIDTH, WINDOW), 1) == WINDOW - 1
    for i in range(bs):
        kn_col = jnp.where(upper, kn_t[:, bs + i:bs + i + 1], kn_t[:, i:i + 1])
        vw_col = jnp.where(upper, vw_t[:, bs + i:bs + i + 1], vw_t[:, i:i + 1])
        nk_ref[i] = jnp.where(last_lane, kn_col, pltpu.roll(ck_ref[i], WINDOW - 1, 1))
        nv_ref[i] = jnp.where(last_lane, vw_col, pltpu.roll(cv_ref[i], WINDOW - 1, 1))
    yield
    for i in range(bs):
        p, p_new, denom = probs[i]
        o = (_dot_nt(p, cv_ref[i].astype(BF16)) + p_new * vws[i]) / denom
        o_ref[i * N_HEADS:(i + 1) * N_HEADS, :] = jnp.where(own, o, 0.0).astype(BF16)


def _head_perm(v):
    return v.reshape(N_KV_HEADS, GROUP).T.reshape(N_HEADS)


def _alibi_slopes():
    h = np.arange(1, N_HEADS + 1, dtype=np.float32)
    return np.exp2(-8.0 * h / N_HEADS).astype(np.float32)


def _selection_matrix(nb):
    r = np.arange(nb * N_HEADS)
    c = np.arange(GROUP * nb)
    same_sample = (r[:, None] // N_HEADS) == (c[None, :] % nb)
    same_member = ((r[:, None] % N_HEADS) // N_KV_HEADS) == (c[None, :] // nb)
    return (same_sample & same_member).astype(np.float32)


def kernel(x_prompt, x_sample, cache_k_win, cache_v_win, ln1_g, w_in, sgu_norm_g, sgu_w, sgu_b, attn_sinks,
           w_oa, w_ob, w_out, ln2_g, w_up, w_down, lnf_g):
    batch, seq, _ = x_prompt.shape
    dec_batch, dec_seq, _ = x_sample.shape
    depth = w_in.shape[0]
    assert depth == 1 and dec_seq == 1
    assert seq % TOKEN_BLOCK == 0 and TOKEN_BLOCK % CHUNK == 0
    assert (batch * seq) % FFN_BLOCK == 0
    assert w_in.shape[-1] == IN_WIDTH

    ln1 = ln1_g[0].reshape(1, D_MODEL)
    ln2 = ln2_g[0].reshape(1, D_MODEL)
    lnf = lnf_g.reshape(1, D_MODEL)
    sgu_g = sgu_norm_g[0].reshape(1, D_MODEL)
    slopes_p = _head_perm(_alibi_slopes())

    xs2d = x_sample.reshape(dec_batch, D_MODEL // LANES, LANES)
    sel_np = _selection_matrix(dec_batch)
    sel = jnp.asarray(sel_np, BF16)
    selt = jnp.asarray(sel_np.T, BF16)
    bias_s = -slopes_p[:, None] * (WINDOW - np.arange(WINDOW, dtype=np.float32))[None, :]

    w_in_b, w_q_b, qsel, knew, vnew, vn, a_s, ga_s, gb_s, w_oa_b, w_ob_b, w_out_b = _sample_proj(
        xs2d, ln1, w_in[0], sgu_g, sgu_w[0], sgu_b[0], sel, w_oa[0], w_ob[0], w_out[0])
    def to_feature_major(c):
        return c[0].transpose(0, 2, 3, 1).reshape(dec_batch, KV_WIDTH, WINDOW)

    def from_feature_major(c):
        return c.reshape(c.shape[0], N_KV_HEADS, HEAD_DIM, WINDOW).transpose(0, 3, 1, 2)[None]

    steps = (batch * seq) // TOKEN_BLOCK
    per_step = dec_batch // steps

    x1, kwin, vwin, w_up_b, w_down_b, o_s, nk, nv = _mix_prompt(
        x_prompt.reshape(batch * seq, D_MODEL), ln1, w_in_b, w_q_b, sgu_g, sgu_w[0], sgu_b[0],
        attn_sinks, slopes_p * LOG2E, w_oa_b, w_ob_b, w_out_b, w_up[0], w_down[0],
        qsel.reshape(steps, per_step * N_HEADS, KV_WIDTH), knew, vnew,
        to_feature_major(cache_k_win), to_feature_major(cache_v_win), bias_s,
        batch=batch, seq=seq)

    y_prompt, y_sample = _ffn(x1, ln2, w_up_b, w_down_b, lnf,
                              o_s.reshape(dec_batch * N_HEADS, KV_WIDTH), selt, a_s, ga_s, gb_s, xs2d,
                              w_oa_b, w_ob_b, w_out_b)

    return (y_prompt.reshape(batch, seq, D_MODEL),
            y_sample.reshape(dec_batch, dec_seq, D_MODEL),
            from_feature_major(kwin), from_feature_major(vwin),
            from_feature_major(nk), from_feature_major(nv),
            vn.reshape(depth, dec_batch, dec_seq, D_MODEL))
```

```python
import functools
import math

import numpy as np
import jax
import jax.numpy as jnp
from jax import lax
from jax.experimental import pallas as pl
from jax.experimental.pallas import tpu as pltpu

D_MODEL = 1024
N_HEADS = 16
HEAD_DIM = 64
N_KV_HEADS = 4
GROUP = N_HEADS // N_KV_HEADS
KV_WIDTH = N_KV_HEADS * HEAD_DIM
WINDOW = 128
CHUNK = 128
SGU_GROUPS = 8
SGU_GROUP_DIM = D_MODEL // SGU_GROUPS
D_FF = 4 * D_MODEL
FF_SLAB = 1024
EPS = 1e-6
NEG_BIG = -1e30
ATTN_SCALE = HEAD_DIM ** -0.5
LOG2E = math.log2(math.e)

OFF_U, OFF_V, OFF_Q, OFF_K, OFF_VA, OFF_GA, OFF_GB, IN_WIDTH = 0, 1024, 2048, 3072, 3328, 3584, 4608, 5632
W_STEP = IN_WIDTH // 4

TOKEN_BLOCK = 512
FFN_BLOCK = 1024
SUBLANES = 8
LANES = 128
VMEM_LIMIT_BYTES = 58 * 1024 * 1024

F32 = jnp.float32
BF16 = jnp.bfloat16


def _rmsnorm(x, g):
    ms = jnp.mean(x * x, axis=-1, keepdims=True)
    return x * lax.rsqrt(ms + EPS) * g


def _gelu_tanh(x):
    c = math.sqrt(2.0 / math.pi)
    return x * (0.5 * (1.0 + jnp.tanh(c * (x + 0.044715 * (x * x * x)))))


def _dot(a, b):
    return jnp.dot(a, b, preferred_element_type=F32)


def _dot_nt(a, b):
    return lax.dot_general(a, b, (((1,), (1,)), ((), ())), preferred_element_type=F32)


def _resident(shape):
    zeros = (0,) * len(shape)
    return pl.BlockSpec(shape, lambda *_: zeros, pipeline_mode=pl.Buffered(1))


def _whole(shape):
    zeros = (0,) * len(shape)
    return pl.BlockSpec(shape, lambda *_: zeros)


def _params():
    return pltpu.CompilerParams(dimension_semantics=("arbitrary",), vmem_limit_bytes=VMEM_LIMIT_BYTES)


def _mix_prompt_kernel(x_ref, ln1_ref, w_in_ref, w_q_ref, sgu_g_ref, sgu_w_ref, sgu_b_ref,
                       sink_ref, slope_ref, w_oa_ref, w_ob_ref, w_out_ref, w_up_blk_ref, w_down_blk_ref,
                       s_qsel_ref, s_knew_ref, s_vnew_ref, s_ck_ref, s_cv_ref, s_bias_ref,
                       x1_ref, kwin_ref, vwin_ref, w_up_bf_ref, w_down_bf_ref, s_o_ref, s_nk_ref, s_nv_ref,
                       qs_scr, kt_scr, vm_scr, kprev_scr, vprev_scr, vn_scr, u_scr, gate_scr, a_scr, b_scr, wt_scr,
                       bias_ref, bexp_ref,
                       *, steps_per_seq):
    step = pl.program_id(0)
    tb = x_ref.shape[0]
    nblk = tb // CHUNK
    first = (step % steps_per_seq) == 0
    rd = step % 2
    wr = 1 - rd

    @pl.when(step == 0)
    def _():
        row = lax.broadcasted_iota(jnp.int32, (CHUNK, CHUNK), 0)
        col = lax.broadcasted_iota(jnp.int32, (CHUNK, CHUNK), 1)
        b_t = sgu_b_ref[...].T
        for g in range(SGU_GROUPS):
            wt_scr[g] = jnp.where(row >= col, sgu_w_ref[g], 0.0).astype(BF16)
            bexp_ref[:, g * SGU_GROUP_DIM:(g + 1) * SGU_GROUP_DIM] = jnp.broadcast_to(
                b_t[:, g:g + 1], (CHUNK, SGU_GROUP_DIM))
        dist = (lax.broadcasted_iota(jnp.int32, (CHUNK, 2 * CHUNK), 0) + CHUNK
                - lax.broadcasted_iota(jnp.int32, (CHUNK, 2 * CHUNK), 1))
        in_band = jnp.logical_and(dist >= 0, dist <= WINDOW)
        dist_f = dist.astype(F32)
        for h in range(N_HEADS):
            bias_ref[h] = jnp.where(in_band, -slope_ref[h] * dist_f, NEG_BIG)
        kt_scr[...] = jnp.zeros(kt_scr.shape, BF16)
        vm_scr[...] = jnp.zeros(vm_scr.shape, BF16)
        kprev_scr[...] = jnp.zeros(kprev_scr.shape, BF16)
        vprev_scr[...] = jnp.zeros(vprev_scr.shape, BF16)

    @pl.when(first)
    def _():
        kprev_scr[rd] = jnp.zeros(kprev_scr.shape[1:], BF16)
        vprev_scr[rd] = jnp.zeros(vprev_scr.shape[1:], BF16)

    w_up_bf_ref[...] = w_up_blk_ref[...].astype(BF16)
    w_down_bf_ref[...] = w_down_blk_ref[...].astype(BF16)

    x = x_ref[...]
    quarters = [slice(i * tb // 4, (i + 1) * tb // 4) for i in range(4)]
    xn_parts = [_rmsnorm(x[p], ln1_ref[...]).astype(BF16) for p in quarters]
    xn = jnp.concatenate(xn_parts, axis=0)

    q = jnp.concatenate([_dot(xn_p, w_q_ref[...]) for xn_p in xn_parts], axis=0)
    k = _dot(xn, w_in_ref[:, OFF_K:OFF_VA])
    va = _dot(xn, w_in_ref[:, OFF_VA:OFF_GA])
    h_v = _dot(xn, w_in_ref[:, OFF_V:OFF_Q])
    h_u0 = _dot(xn, w_in_ref[:, OFF_U:OFF_U + D_MODEL // 2])

    q = (q * (ATTN_SCALE * LOG2E)).astype(BF16)
    for c in range(nblk):
        for g in range(GROUP):
            qs_scr[c, g * CHUNK:(g + 1) * CHUNK, :] = q[c * CHUNK:(c + 1) * CHUNK, g * KV_WIDTH:(g + 1) * KV_WIDTH]

    kt_f32 = k.T
    kwin_ref[...] = kt_f32[:, tb - WINDOW:]
    vwin_ref[...] = va[tb - WINDOW:, :].T
    kt = kt_f32.astype(BF16)
    vab = va.astype(BF16)
    for kvh in range(N_KV_HEADS):
        own = slice(kvh * HEAD_DIM, (kvh + 1) * HEAD_DIM)
        for c in range(nblk):
            kt_scr[kvh, c, own, :] = kt[own, c * CHUNK:(c + 1) * CHUNK]
        vm_scr[kvh, :, own] = vab[:, own]
        kprev_scr[wr, kvh, own, :] = kt[own, tb - WINDOW:]
        vprev_scr[wr, kvh, :, own] = vab[tb - WINDOW:, own]

    no_prev = jnp.where(
        jnp.logical_and(first, lax.broadcasted_iota(jnp.int32, (CHUNK, 2 * CHUNK), 1) < CHUNK), NEG_BIG, 0.0)

    def attn_scores(c):
        qs = qs_scr[c]
        out = []
        for kvh in range(N_KV_HEADS):
            k_prev = kprev_scr[rd, kvh] if c == 0 else kt_scr[kvh, c - 1]
            out.append(_dot(qs, jnp.concatenate([k_prev, kt_scr[kvh, c]], axis=1)))
        return out

    def attn_softmax(c, scores):
        out = []
        for kvh in range(N_KV_HEADS):
            ps = []
            for g in range(GROUP):
                h = g * N_KV_HEADS + kvh
                s = scores[kvh][g * CHUNK:(g + 1) * CHUNK, :] + bias_ref[h]
                if c == 0:
                    s = s + no_prev
                sink = sink_ref[0, kvh * GROUP + g] * LOG2E
                m = jnp.max(s, axis=1, keepdims=True)
                p = jnp.exp2(s - m)
                denom = jnp.sum(p, axis=1, keepdims=True) + jnp.exp2(sink - m)
                ps.append((p * (1.0 / denom)).astype(BF16))
            out.append(jnp.concatenate(ps, axis=0))
        return out

    def attn_values(c, probs):
        rows = slice(c * CHUNK, (c + 1) * CHUNK)
        acc = None
        for kvh in range(N_KV_HEADS):
            if c == 0:
                v_band = jnp.concatenate([vprev_scr[rd, kvh], vm_scr[kvh, 0:CHUNK, :]], axis=0)
            else:
                v_band = vm_scr[kvh, (c - 1) * CHUNK:(c + 1) * CHUNK, :]
            o = _dot(probs[kvh], v_band)
            acc = o if acc is None else acc + o
        for g in range(GROUP):
            b_scr[rows, g * KV_WIDTH:(g + 1) * KV_WIDTH] = acc[g * CHUNK:(g + 1) * CHUNK, :].astype(BF16)

    def sgu_chunk(c):
        rows = slice(c * CHUNK, (c + 1) * CHUNK)
        vn_c = vn_scr[rows, :]
        mixed = jnp.concatenate(
            [_dot(wt_scr[g], vn_c[:, g * SGU_GROUP_DIM:(g + 1) * SGU_GROUP_DIM]) for g in range(SGU_GROUPS)],
            axis=1) + bexp_ref[...]
        a_scr[rows, :] = (u_scr[rows, :] * mixed).astype(BF16)

    half = D_MODEL // 2

    def tail_u(h, lo):
        u_scr[:, lo:lo + half] = _gelu_tanh(h)

    def tail_ga(h, lo):
        gate_scr[0, :, lo:lo + half] = jax.nn.sigmoid(h)

    def tail_gb(h, lo):
        gate_scr[1, :, lo:lo + half] = jax.nn.sigmoid(h)

    fillers = [
        (lambda: _dot(xn, w_in_ref[:, OFF_U + half:OFF_V]), lambda h: tail_u(h, half)),
        (lambda: _dot(xn, w_in_ref[:, OFF_GA:OFF_GA + half]), lambda h: tail_ga(h, 0)),
        (lambda: _dot(xn, w_in_ref[:, OFF_GA + half:OFF_GB]), lambda h: tail_ga(h, half)),
        (lambda: _dot(xn, w_in_ref[:, OFF_GB:IN_WIDTH]),
         lambda h: (tail_gb(h[:, :half], 0), tail_gb(h[:, half:], half))),
    ]
    vn_scr[...] = _rmsnorm(_gelu_tanh(h_v), sgu_g_ref[...]).astype(BF16)
    tail_u(h_u0, 0)
    sgu_after = {nblk - 2: range(0, nblk // 2), nblk - 1: range(nblk // 2, nblk)}
    for c in range(nblk):
        scores = attn_scores(c)
        proj = fillers[c][0]() if c < len(fillers) else None
        for cc in sgu_after.get(c, ()):
            sgu_chunk(cc)
        probs = attn_softmax(c, scores)
        attn_values(c, probs)
        if proj is not None:
            fillers[c][1](proj)
    for matmul, tail in fillers[nblk:]:
        tail(matmul())

    sample = _sample_attn_stages(s_qsel_ref, s_knew_ref, s_vnew_ref, s_ck_ref, s_cv_ref, s_bias_ref, sink_ref,
                                 step % 2 == 1, s_o_ref, s_nk_ref, s_nv_ref)
    next(sample)
    branch_a = _dot(a_scr[...], w_oa_ref[...])
    next(sample)
    branch_b = _dot(b_scr[...], w_ob_ref[...])
    for _ in sample:
        pass
    hm = gate_scr[0] * branch_a + gate_scr[1] * branch_b
    x1_ref[...] = x + _dot(hm.astype(BF16), w_out_ref[...])


def _mix_prompt(x2d, ln1, w_in, w_q, sgu_g, sgu_w, sgu_b, sinks, slopes, w_oa, w_ob, w_out,
                w_up_f32, w_down_f32,
                s_qsel, s_knew, s_vnew, s_cache_k, s_cache_v, s_bias, *, batch, seq):
    n = x2d.shape[0]
    tb = TOKEN_BLOCK
    nblk = tb // CHUNK
    steps = n // tb
    steps_per_seq = seq // tb
    nb = s_cache_k.shape[0]
    per_step = nb // steps
    assert per_step * steps == nb and 2 * per_step == SUBLANES and s_knew.shape == (nb, KV_WIDTH)
    s_head_block = pl.BlockSpec((None, per_step * N_HEADS, KV_WIDTH), lambda i: (i, 0, 0))
    s_new_block = pl.BlockSpec((SUBLANES, KV_WIDTH), lambda i: (i // 2, 0))
    s_cache_block = pl.BlockSpec((per_step, KV_WIDTH, WINDOW), lambda i: (i, 0, 0))
    row_block = pl.BlockSpec((tb, D_MODEL), lambda i: (i, 0))
    win_block = pl.BlockSpec((None, KV_WIDTH, WINDOW), lambda i: (i // steps_per_seq, 0, 0))
    up_block = pl.BlockSpec((D_MODEL // steps, D_FF), lambda i: (i, 0))
    down_block = pl.BlockSpec((D_FF // steps, D_MODEL), lambda i: (i, 0))
    return pl.pallas_call(
        functools.partial(_mix_prompt_kernel, steps_per_seq=steps_per_seq),
        grid=(n // tb,),
        in_specs=[
            row_block,
            _resident((1, D_MODEL)),
            _resident((D_MODEL, IN_WIDTH)),
            _resident((D_MODEL, D_MODEL)),
            _resident((1, D_MODEL)),
            _resident((SGU_GROUPS, CHUNK, CHUNK)),
            _resident((SGU_GROUPS, CHUNK)),
            pl.BlockSpec(memory_space=pltpu.SMEM),
            pl.BlockSpec(memory_space=pltpu.SMEM),
            _resident((D_MODEL, D_MODEL)),
            _resident((D_MODEL, D_MODEL)),
            _resident((D_MODEL, D_MODEL)),
            up_block,
            down_block,
            s_head_block, s_new_block, s_new_block, s_cache_block, s_cache_block,
            _resident((N_HEADS, WINDOW)),
        ],
        out_specs=[row_block, win_block, win_block, up_block, down_block,
                   s_head_block, s_cache_block, s_cache_block],
        out_shape=[
            jax.ShapeDtypeStruct((n, D_MODEL), F32),
            jax.ShapeDtypeStruct((batch, KV_WIDTH, WINDOW), F32),
            jax.ShapeDtypeStruct((batch, KV_WIDTH, WINDOW), F32),
            jax.ShapeDtypeStruct((D_MODEL, D_FF), BF16),
            jax.ShapeDtypeStruct((D_FF, D_MODEL), BF16),
            jax.ShapeDtypeStruct((steps, per_step * N_HEADS, KV_WIDTH), BF16),
            jax.ShapeDtypeStruct((nb, KV_WIDTH, WINDOW), F32),
            jax.ShapeDtypeStruct((nb, KV_WIDTH, WINDOW), F32),
        ],
        scratch_shapes=[
            pltpu.VMEM((nblk, GROUP * CHUNK, KV_WIDTH), BF16),
            pltpu.VMEM((N_KV_HEADS, nblk, KV_WIDTH, CHUNK), BF16),
            pltpu.VMEM((N_KV_HEADS, tb, KV_WIDTH), BF16),
            pltpu.VMEM((2, N_KV_HEADS, KV_WIDTH, CHUNK), BF16),
            pltpu.VMEM((2, N_KV_HEADS, WINDOW, KV_WIDTH), BF16),
            pltpu.VMEM((tb, D_MODEL), BF16),
            pltpu.VMEM((tb, D_MODEL), F32),
            pltpu.VMEM((2, tb, D_MODEL), F32),
            pltpu.VMEM((tb, D_MODEL), BF16),
            pltpu.VMEM((tb, D_MODEL), BF16),
            pltpu.VMEM((SGU_GROUPS, CHUNK, CHUNK), BF16),
            pltpu.VMEM((N_HEADS, CHUNK, 2 * CHUNK), F32),
            pltpu.VMEM((CHUNK, D_MODEL), F32),
        ],
        compiler_params=_params(),
        name="mix_prompt",
    )(x2d, ln1, w_in, w_q, sgu_g, sgu_w, sgu_b, sinks, slopes, w_oa, w_ob, w_out, w_up_f32, w_down_f32,
      s_qsel, s_knew, s_vnew, s_cache_k, s_cache_v, s_bias)


def _store_rows_as_tiles(ref, rows):
    for c in range(rows.shape[1] // LANES):
        ref[:, c, :] = rows[:, c * LANES:(c + 1) * LANES]


def _rows_from_tiles(ref):
    return jnp.concatenate([ref[:, c, :] for c in range(ref.shape[1])], axis=1)


def _ffn_rows(x, store, ln2_ref, w_up_ref, w_down_ref, lnf_ref, *, row_parts):
    n_slabs = D_FF // FF_SLAB
    m = x.shape[0]
    parts = [slice(i * m // row_parts, (i + 1) * m // row_parts) for i in range(row_parts)]

    def up(j, xn):
        return _dot(xn, w_up_ref[:, j * FF_SLAB:(j + 1) * FF_SLAB])

    xns = [_rmsnorm(x[p], ln2_ref[...]).astype(BF16) for p in parts]
    h_next = jnp.concatenate([up(0, xn_p) for xn_p in xns], axis=0) if row_parts > 1 else up(0, xns[0])
    xn = jnp.concatenate(xns, axis=0) if row_parts > 1 else xns[0]

    acc = x
    for j in range(n_slabs):
        h = h_next
        if j + 1 < n_slabs:
            h_next = up(j + 1, xn)
        h = jnp.square(jnp.maximum(h, 0.0)).astype(BF16)
        w_d = w_down_ref[j * FF_SLAB:(j + 1) * FF_SLAB, :]
        if j + 1 < n_slabs:
            acc = acc + _dot(h, w_d)
        else:
            for p in parts:
                store(p, _rmsnorm(acc[p] + _dot(h[p], w_d), lnf_ref[...]))


def _ffn_kernel(x_ref, ln2_ref, w_up_ref, w_down_ref, lnf_ref,
                s_o_hbm, s_selt_hbm, s_a_hbm, s_ga_hbm, s_gb_hbm, s_x_hbm, w_oa_hbm, w_ob_hbm, w_out_hbm,
                y_ref, ys_ref, s_o_ref, s_selt_ref, s_a_ref, s_ga_ref, s_gb_ref, s_x_ref,
                w_oa_ref, w_ob_ref, w_out_ref, late_sem):
    i = pl.program_id(0)
    last = pl.num_programs(0) - 1

    def late_copies():
        pairs = ((s_o_hbm, s_o_ref), (s_selt_hbm, s_selt_ref), (s_a_hbm, s_a_ref), (s_ga_hbm, s_ga_ref),
                 (s_gb_hbm, s_gb_ref), (s_x_hbm, s_x_ref),
                 (w_oa_hbm, w_oa_ref), (w_ob_hbm, w_ob_ref), (w_out_hbm, w_out_ref))
        return [pltpu.make_async_copy(src, dst, late_sem.at[k]) for k, (src, dst) in enumerate(pairs)]

    @pl.when(i == 0)
    def _():
        for copy in late_copies():
            copy.start()

    @pl.when(i < last)
    def _():
        def store(p, rows):
            y_ref[p, :] = rows

        _ffn_rows(x_ref[...], store, ln2_ref, w_up_ref, w_down_ref, lnf_ref, row_parts=4)

    @pl.when(i == last)
    def _():
        for copy in late_copies():
            copy.wait()
        nb = s_x_ref.shape[0]
        bst = _dot(s_selt_ref[...], s_o_ref[...]).astype(BF16)
        ob = _dot(bst[0:nb, :], w_ob_ref[0:KV_WIDTH, :])
        for g in range(1, GROUP):
            ob = ob + _dot(bst[g * nb:(g + 1) * nb, :], w_ob_ref[g * KV_WIDTH:(g + 1) * KV_WIDTH, :])
        hm = s_ga_ref[...] * _dot(s_a_ref[...], w_oa_ref[...]) + s_gb_ref[...] * ob
        xs1 = _rows_from_tiles(s_x_ref) + _dot(hm.astype(BF16), w_out_ref[...])
        _ffn_rows(xs1, lambda p, rows: _store_rows_as_tiles(ys_ref, rows), ln2_ref, w_up_ref, w_down_ref, lnf_ref,
                  row_parts=1)


def _ffn(x2d, ln2, w_up, w_down, lnf, s_o, s_selt, s_a, s_ga, s_gb, xs2d, w_oa, w_ob, w_out):
    n = x2d.shape[0]
    nb = xs2d.shape[0]
    n_prompt_steps = n // FFN_BLOCK
    row_block = pl.BlockSpec((FFN_BLOCK, D_MODEL), lambda i: (jnp.minimum(i, n_prompt_steps - 1), 0))
    assert n_prompt_steps >= 1
    in_hbm = pl.BlockSpec(memory_space=pl.ANY)
    return pl.pallas_call(
        _ffn_kernel,
        grid=(n_prompt_steps + 1,),
        in_specs=[row_block, _resident((1, D_MODEL)), _resident((D_MODEL, D_FF)),
                  _resident((D_FF, D_MODEL)), _resident((1, D_MODEL)),
                  in_hbm, in_hbm, in_hbm, in_hbm, in_hbm, in_hbm,
                  in_hbm, in_hbm, in_hbm],
        out_specs=[row_block, _whole((nb, D_MODEL // LANES, LANES))],
        out_shape=[jax.ShapeDtypeStruct((n, D_MODEL), F32),
                   jax.ShapeDtypeStruct((nb, D_MODEL // LANES, LANES), F32)],
        scratch_shapes=[pltpu.VMEM((nb * N_HEADS, KV_WIDTH), BF16),
                        pltpu.VMEM((GROUP * nb, nb * N_HEADS), BF16),
                        pltpu.VMEM((nb, D_MODEL), BF16), pltpu.VMEM((nb, D_MODEL), F32),
                        pltpu.VMEM((nb, D_MODEL), F32),
                        pltpu.VMEM((nb, D_MODEL // LANES, LANES), F32),
                        pltpu.VMEM((D_MODEL, D_MODEL), BF16), pltpu.VMEM((D_MODEL, D_MODEL), BF16),
                        pltpu.VMEM((D_MODEL, D_MODEL), BF16),
                        pltpu.SemaphoreType.DMA((9,))],
        compiler_params=_params(),
        name="ffn",
    )(x2d, ln2, w_up, w_down, lnf, s_o, s_selt, s_a, s_ga, s_gb, xs2d, w_oa, w_ob, w_out)


def _sample_proj_kernel(x_ref, ln1_ref, w_blk_ref, sgu_g_ref, sgu_w_ref, sgu_b_ref, sel_ref,
                        w_oa_blk_ref, w_ob_blk_ref, w_out_blk_ref,
                        w_bf_ref, w_q_bf_ref, qsel_ref, knew_ref, vnew_ref, vn_ref, a_ref,
                        ga_ref, gb_ref, w_oa_bf_ref, w_ob_bf_ref, w_out_bf_ref, xn_scr, h_scr):
    j = pl.program_id(0)
    nb = x_ref.shape[0]
    n_steps = IN_WIDTH // W_STEP

    @pl.when(j == 0)
    def _():
        xn_scr[...] = _rmsnorm(_rows_from_tiles(x_ref), ln1_ref[...]).astype(BF16)

    w_oa_bf_ref[...] = w_oa_blk_ref[...].astype(BF16)
    w_out_bf_ref[...] = w_out_blk_ref[...].astype(BF16)
    assert w_ob_blk_ref.shape[0] == GROUP * HEAD_DIM and n_steps == N_KV_HEADS
    for g in range(GROUP):
        dst = pl.multiple_of(g * KV_WIDTH + j * HEAD_DIM, HEAD_DIM)
        w_ob_bf_ref[pl.ds(dst, HEAD_DIM), :] = w_ob_blk_ref[g * HEAD_DIM:(g + 1) * HEAD_DIM, :].astype(BF16)

    wb = w_blk_ref[...].astype(BF16)
    w_bf_ref[...] = wb
    h_scr[j] = _dot(xn_scr[...], wb)

    assert W_STEP % HEAD_DIM == 0
    for step in range(n_steps):
        heads = [h for h in range(N_HEADS) if step * W_STEP <= OFF_Q + h * HEAD_DIM < (step + 1) * W_STEP]
        if heads:
            @pl.when(j == step)
            def _(step=step, heads=heads):
                for head in heads:
                    src = OFF_Q + head * HEAD_DIM - step * W_STEP
                    kvh, g = divmod(head, GROUP)
                    dst = g * KV_WIDTH + kvh * HEAD_DIM
                    w_q_bf_ref[:, dst:dst + HEAD_DIM] = wb[:, src:src + HEAD_DIM]

    @pl.when(j == n_steps - 1)
    def _():
        def cols(lo, hi):
            pieces = []
            for step in range(n_steps):
                a, b = max(lo, step * W_STEP), min(hi, (step + 1) * W_STEP)
                if a < b:
                    pieces.append(h_scr[step, :, a - step * W_STEP:b - step * W_STEP])
            return pieces[0] if len(pieces) == 1 else jnp.concatenate(pieces, axis=1)

        u = _gelu_tanh(cols(OFF_U, OFF_V))
        v = _gelu_tanh(cols(OFF_V, OFF_Q))
        vn = _rmsnorm(v, sgu_g_ref[...])
        _store_rows_as_tiles(vn_ref, vn)
        def over_groups(entry):
            return jnp.concatenate(
                [jnp.broadcast_to(entry(g), (1, SGU_GROUP_DIM)) for g in range(SGU_GROUPS)], axis=1)

        w_diag = over_groups(lambda g: sgu_w_ref[g, 0:1, 0:1])
        b_first = over_groups(lambda g: sgu_b_ref[g:g + 1, 0:1])
        a_ref[...] = (u * (vn * w_diag + b_first)).astype(BF16)
        knew_ref[...] = cols(OFF_K, OFF_VA)
        vnew_ref[...] = cols(OFF_VA, OFF_GA)
        ga_ref[...] = jax.nn.sigmoid(cols(OFF_GA, OFF_GB))
        gb_ref[...] = jax.nn.sigmoid(cols(OFF_GB, IN_WIDTH))
        q = _dot(xn_scr[...], w_q_bf_ref[...]) * ATTN_SCALE
        qstack = jnp.concatenate([q[:, g * KV_WIDTH:(g + 1) * KV_WIDTH] for g in range(GROUP)], axis=0).astype(BF16)
        qrep = _dot(sel_ref[...], qstack)
        row_kvh = lax.broadcasted_iota(jnp.int32, (nb * N_HEADS, KV_WIDTH), 0) % N_KV_HEADS
        lane_kvh = lax.broadcasted_iota(jnp.int32, (nb * N_HEADS, KV_WIDTH), 1) // HEAD_DIM
        qsel_ref[...] = jnp.where(row_kvh == lane_kvh, qrep, 0.0).astype(BF16)


def _sample_proj(xs2d, ln1, w_in_f32, sgu_g, sgu_w, sgu_b, sel, w_oa_f32, w_ob_f32, w_out_f32):
    nb = xs2d.shape[0]
    n_blocks = IN_WIDTH // W_STEP
    w_block = pl.BlockSpec((D_MODEL, W_STEP), lambda j: (0, j))
    merge_block = pl.BlockSpec((D_MODEL // n_blocks, D_MODEL), lambda j: (j, 0))
    merge_shape = jax.ShapeDtypeStruct((D_MODEL, D_MODEL), BF16)
    return pl.pallas_call(
        _sample_proj_kernel,
        grid=(n_blocks,),
        in_specs=[_whole((nb, D_MODEL // LANES, LANES)), _whole((1, D_MODEL)), w_block,
                  _whole((1, D_MODEL)),
                  pl.BlockSpec((SGU_GROUPS, SUBLANES, CHUNK), lambda j: (0, 0, 0)),
                  _whole((SGU_GROUPS, CHUNK)),
                  _resident((nb * N_HEADS, GROUP * nb)),
                  merge_block, merge_block, merge_block],
        out_specs=[w_block, _whole((D_MODEL, D_MODEL)),
                   _whole((nb * N_HEADS, KV_WIDTH)), _whole((nb, KV_WIDTH)), _whole((nb, KV_WIDTH)),
                   _whole((nb, D_MODEL // LANES, LANES)), _whole((nb, D_MODEL)),
                   _whole((nb, D_MODEL)), _whole((nb, D_MODEL)),
                   merge_block, _whole((D_MODEL, D_MODEL)), merge_block],
        out_shape=[
            jax.ShapeDtypeStruct((D_MODEL, IN_WIDTH), BF16),
            jax.ShapeDtypeStruct((D_MODEL, D_MODEL), BF16),
            jax.ShapeDtypeStruct((nb * N_HEADS, KV_WIDTH), BF16),
            jax.ShapeDtypeStruct((nb, KV_WIDTH), F32),
            jax.ShapeDtypeStruct((nb, KV_WIDTH), F32),
            jax.ShapeDtypeStruct((nb, D_MODEL // LANES, LANES), F32),
            jax.ShapeDtypeStruct((nb, D_MODEL), BF16),
            jax.ShapeDtypeStruct((nb, D_MODEL), F32),
            jax.ShapeDtypeStruct((nb, D_MODEL), F32),
            merge_shape, merge_shape, merge_shape,
        ],
        scratch_shapes=[pltpu.VMEM((nb, D_MODEL), BF16), pltpu.VMEM((n_blocks, nb, W_STEP), F32)],
        compiler_params=_params(),
        name="sample_proj",
    )(xs2d, ln1, w_in_f32, sgu_g, sgu_w, sgu_b, sel, w_oa_f32, w_ob_f32, w_out_f32)


def _sample_attn_stages(qsel_ref, knew_ref, vnew_ref, ck_ref, cv_ref, bias_ref, sink_ref, upper,
                        o_ref, nk_ref, nv_ref):
    bs = ck_ref.shape[0]
    assert knew_ref.shape[0] == 2 * bs
    row0 = jnp.where(upper, bs, 0)
    row_kvh = lax.broadcasted_iota(jnp.int32, (N_HEADS, KV_WIDTH), 0) % N_KV_HEADS
    lane_kvh = lax.broadcasted_iota(jnp.int32, (N_HEADS, KV_WIDTH), 1) // HEAD_DIM
    own = row_kvh == lane_kvh
    bias = bias_ref[...]
    head_row = lax.broadcasted_iota(jnp.int32, (N_HEADS, 1), 0)
    sink = jnp.zeros((N_HEADS, 1), F32)
    for h in range(N_HEADS):
        g, kvh = divmod(h, N_KV_HEADS)
        sink = jnp.where(head_row == h, sink_ref[0, kvh * GROUP + g], sink)

    qss = [qsel_ref[i * N_HEADS:(i + 1) * N_HEADS, :] for i in range(bs)]
    kns = [knew_ref[pl.ds(row0 + i, 1), :] for i in range(bs)]
    vws = [vnew_ref[pl.ds(row0 + i, 1), :] for i in range(bs)]
    scores = [_dot(qss[i], ck_ref[i].astype(BF16)) + bias for i in range(bs)]
    yield
    probs = []
    for i in range(bs):
        s = scores[i]
        s_new = jnp.sum(qss[i].astype(F32) * kns[i], axis=1, keepdims=True)
        m = jnp.maximum(jnp.maximum(jnp.max(s, axis=1, keepdims=True), s_new), sink)
        p = jnp.exp(s - m)
        p_new = jnp.exp(s_new - m)
        denom = jnp.sum(p, axis=1, keepdims=True) + p_new + jnp.exp(sink - m)
        probs.append((p.astype(BF16), p_new, denom))

    kn_t = knew_ref[...].T
    vw_t = vnew_ref[...].T
    last_lane = lax.broadcasted_iota(jnp.int32, (KV_WIDTH, WINDOW), 1) == WINDOW - 1
    for i in range(bs):
        kn_col = jnp.where(upper, kn_t[:, bs + i:bs + i + 1], kn_t[:, i:i + 1])
        vw_col = jnp.where(upper, vw_t[:, bs + i:bs + i + 1], vw_t[:, i:i + 1])
        nk_ref[i] = jnp.where(last_lane, kn_col, pltpu.roll(ck_ref[i], WINDOW - 1, 1))
        nv_ref[i] = jnp.where(last_lane, vw_col, pltpu.roll(cv_ref[i], WINDOW - 1, 1))
    yield
    for i in range(bs):
        p, p_new, denom = probs[i]
        o = (_dot_nt(p, cv_ref[i].astype(BF16)) + p_new * vws[i]) / denom
        o_ref[i * N_HEADS:(i + 1) * N_HEADS, :] = jnp.where(own, o, 0.0).astype(BF16)


def _head_perm(v):
    return v.reshape(N_KV_HEADS, GROUP).T.reshape(N_HEADS)


def _alibi_slopes():
    h = np.arange(1, N_HEADS + 1, dtype=np.float32)
    return np.exp2(-8.0 * h / N_HEADS).astype(np.float32)


def _selection_matrix(nb):
    r = np.arange(nb * N_HEADS)
    c = np.arange(GROUP * nb)
    same_sample = (r[:, None] // N_HEADS) == (c[None, :] % nb)
    same_member = ((r[:, None] % N_HEADS) // N_KV_HEADS) == (c[None, :] // nb)
    return (same_sample & same_member).astype(np.float32)


def kernel(x_prompt, x_sample, cache_k_win, cache_v_win, ln1_g, w_in, sgu_norm_g, sgu_w, sgu_b, attn_sinks,
           w_oa, w_ob, w_out, ln2_g, w_up, w_down, lnf_g):
    batch, seq, _ = x_prompt.shape
    dec_batch, dec_seq, _ = x_sample.shape
    depth = w_in.shape[0]
    assert depth == 1 and dec_seq == 1
    assert seq % TOKEN_BLOCK == 0 and TOKEN_BLOCK % CHUNK == 0
    assert (batch * seq) % FFN_BLOCK == 0
    assert w_in.shape[-1] == IN_WIDTH

    ln1 = ln1_g[0].reshape(1, D_MODEL)
    ln2 = ln2_g[0].reshape(1, D_MODEL)
    lnf = lnf_g.reshape(1, D_MODEL)
    sgu_g = sgu_norm_g[0].reshape(1, D_MODEL)
    slopes_p = _head_perm(_alibi_slopes())

    xs2d = x_sample.reshape(dec_batch, D_MODEL // LANES, LANES)
    sel_np = _selection_matrix(dec_batch)
    sel = jnp.asarray(sel_np, BF16)
    selt = jnp.asarray(sel_np.T, BF16)
    bias_s = -slopes_p[:, None] * (WINDOW - np.arange(WINDOW, dtype=np.float32))[None, :]

    w_in_b, w_q_b, qsel, knew, vnew, vn, a_s, ga_s, gb_s, w_oa_b, w_ob_b, w_out_b = _sample_proj(
        xs2d, ln1, w_in[0], sgu_g, sgu_w[0], sgu_b[0], sel, w_oa[0], w_ob[0], w_out[0])
    def to_feature_major(c):
        return c[0].transpose(0, 2, 3, 1).reshape(dec_batch, KV_WIDTH, WINDOW)

    def from_feature_major(c):
        return c.reshape(c.shape[0], N_KV_HEADS, HEAD_DIM, WINDOW).transpose(0, 3, 1, 2)[None]

    steps = (batch * seq) // TOKEN_BLOCK
    per_step = dec_batch // steps

    x1, kwin, vwin, w_up_b, w_down_b, o_s, nk, nv = _mix_prompt(
        x_prompt.reshape(batch * seq, D_MODEL), ln1, w_in_b, w_q_b, sgu_g, sgu_w[0], sgu_b[0],
        attn_sinks, slopes_p * LOG2E, w_oa_b, w_ob_b, w_out_b, w_up[0], w_down[0],
        qsel.reshape(steps, per_step * N_HEADS, KV_WIDTH), knew, vnew,
        to_feature_major(cache_k_win), to_feature_major(cache_v_win), bias_s,
        batch=batch, seq=seq)

    y_prompt, y_sample = _ffn(x1, ln2, w_up_b, w_down_b, lnf,
                              o_s.reshape(dec_batch * N_HEADS, KV_WIDTH), selt, a_s, ga_s, gb_s, xs2d,
                              w_oa_b, w_ob_b, w_out_b)

    return (y_prompt.reshape(batch, seq, D_MODEL),
            y_sample.reshape(dec_batch, dec_seq, D_MODEL),
            from_feature_major(kwin), from_feature_major(vwin),
            from_feature_major(nk), from_feature_major(nv),
            vn.reshape(depth, dec_batch, dec_seq, D_MODEL))
```

```python
import functools
import math

import numpy as np
import jax
import jax.numpy as jnp
from jax import lax
from jax.experimental import pallas as pl
from jax.experimental.pallas import tpu as pltpu

D_MODEL = 1024
N_HEADS = 16
HEAD_DIM = 64
N_KV_HEADS = 4
GROUP = N_HEADS // N_KV_HEADS
KV_WIDTH = N_KV_HEADS * HEAD_DIM
WINDOW = 128
CHUNK = 128
SGU_GROUPS = 8
SGU_GROUP_DIM = D_MODEL // SGU_GROUPS
D_FF = 4 * D_MODEL
FF_SLAB = 1024
EPS = 1e-6
NEG_BIG = -1e30
ATTN_SCALE = HEAD_DIM ** -0.5
LOG2E = math.log2(math.e)

OFF_U, OFF_V, OFF_Q, OFF_K, OFF_VA, OFF_GA, OFF_GB, IN_WIDTH = 0, 1024, 2048, 3072, 3328, 3584, 4608, 5632
W_STEP = IN_WIDTH // 4

TOKEN_BLOCK = 512
FFN_BLOCK = 1024
SUBLANES = 8
LANES = 128
VMEM_LIMIT_BYTES = 58 * 1024 * 1024

F32 = jnp.float32
BF16 = jnp.bfloat16


def _rmsnorm(x, g):
    ms = jnp.mean(x * x, axis=-1, keepdims=True)
    return x * lax.rsqrt(ms + EPS) * g


def _gelu_tanh(x):
    c = math.sqrt(2.0 / math.pi)
    return x * (0.5 * (1.0 + jnp.tanh(c * (x + 0.044715 * (x * x * x)))))


def _dot(a, b):
    return jnp.dot(a, b, preferred_element_type=F32)


def _dot_nt(a, b):
    return lax.dot_general(a, b, (((1,), (1,)), ((), ())), preferred_element_type=F32)


def _resident(shape):
    zeros = (0,) * len(shape)
    return pl.BlockSpec(shape, lambda *_: zeros, pipeline_mode=pl.Buffered(1))


def _whole(shape):
    zeros = (0,) * len(shape)
    return pl.BlockSpec(shape, lambda *_: zeros)


def _params():
    return pltpu.CompilerParams(dimension_semantics=("arbitrary",), vmem_limit_bytes=VMEM_LIMIT_BYTES)


def _mix_prompt_kernel(x_ref, ln1_ref, w_in_ref, w_q_ref, sgu_g_ref, sgu_w_ref, sgu_b_ref,
                       sink_ref, slope_ref, w_oa_hbm, w_ob_hbm, w_out_hbm, w_up_blk_ref, w_down_blk_ref,
                       s_qsel_ref, s_knew_ref, s_vnew_ref, s_ck_ref, s_cv_ref, s_bias_ref,
                       x1_ref, kwin_ref, vwin_ref, w_up_bf_ref, w_down_bf_ref, s_o_ref, s_nk_ref, s_nv_ref,
                       qs_scr, kt_scr, vm_scr, kprev_scr, vprev_scr, vn_scr, u_scr, gate_scr, a_scr, b_scr, wt_scr,
                       bias_ref, bexp_ref, w_oa_ref, w_ob_ref, w_out_ref, merge_sem,
                       *, steps_per_seq):
    step = pl.program_id(0)
    tb = x_ref.shape[0]
    nblk = tb // CHUNK
    first = (step % steps_per_seq) == 0

    def merge_copies():
        pairs = ((w_oa_hbm, w_oa_ref), (w_ob_hbm, w_ob_ref), (w_out_hbm, w_out_ref))
        return [pltpu.make_async_copy(src, dst, merge_sem.at[k]) for k, (src, dst) in enumerate(pairs)]

    @pl.when(step == 0)
    def _():
        for copy in merge_copies():
            copy.start()

    rd = step % 2
    wr = 1 - rd

    @pl.when(step == 0)
    def _():
        row = lax.broadcasted_iota(jnp.int32, (CHUNK, CHUNK), 0)
        col = lax.broadcasted_iota(jnp.int32, (CHUNK, CHUNK), 1)
        b_t = sgu_b_ref[...].T
        for g in range(SGU_GROUPS):
            wt_scr[g] = jnp.where(row >= col, sgu_w_ref[g], 0.0).astype(BF16)
            bexp_ref[:, g * SGU_GROUP_DIM:(g + 1) * SGU_GROUP_DIM] = jnp.broadcast_to(
                b_t[:, g:g + 1], (CHUNK, SGU_GROUP_DIM))
        dist = (lax.broadcasted_iota(jnp.int32, (CHUNK, 2 * CHUNK), 0) + CHUNK
                - lax.broadcasted_iota(jnp.int32, (CHUNK, 2 * CHUNK), 1))
        in_band = jnp.logical_and(dist >= 0, dist <= WINDOW)
        dist_f = dist.astype(F32)
        for h in range(N_HEADS):
            bias_ref[h] = jnp.where(in_band, -slope_ref[h] * dist_f, NEG_BIG)
        kt_scr[...] = jnp.zeros(kt_scr.shape, BF16)
        vm_scr[...] = jnp.zeros(vm_scr.shape, BF16)
        kprev_scr[...] = jnp.zeros(kprev_scr.shape, BF16)
        vprev_scr[...] = jnp.zeros(vprev_scr.shape, BF16)

    @pl.when(first)
    def _():
        kprev_scr[rd] = jnp.zeros(kprev_scr.shape[1:], BF16)
        vprev_scr[rd] = jnp.zeros(vprev_scr.shape[1:], BF16)

    w_up_bf_ref[...] = w_up_blk_ref[...].astype(BF16)
    w_down_bf_ref[...] = w_down_blk_ref[...].astype(BF16)

    x = x_ref[...]
    quarters = [slice(i * tb // 4, (i + 1) * tb // 4) for i in range(4)]
    xn_parts = [_rmsnorm(x[p], ln1_ref[...]).astype(BF16) for p in quarters]
    xn = jnp.concatenate(xn_parts, axis=0)

    q = jnp.concatenate([_dot(xn_p, w_q_ref[...]) for xn_p in xn_parts], axis=0)
    k = _dot(xn, w_in_ref[:, OFF_K:OFF_VA])
    va = _dot(xn, w_in_ref[:, OFF_VA:OFF_GA])
    h_v = _dot(xn, w_in_ref[:, OFF_V:OFF_Q])
    h_u0 = _dot(xn, w_in_ref[:, OFF_U:OFF_U + D_MODEL // 2])

    q = (q * (ATTN_SCALE * LOG2E)).astype(BF16)
    for c in range(nblk):
        for g in range(GROUP):
            qs_scr[c, g * CHUNK:(g + 1) * CHUNK, :] = q[c * CHUNK:(c + 1) * CHUNK, g * KV_WIDTH:(g + 1) * KV_WIDTH]

    kt_f32 = k.T
    kwin_ref[...] = kt_f32[:, tb - WINDOW:]
    vwin_ref[...] = va[tb - WINDOW:, :].T
    kt = kt_f32.astype(BF16)
    vab = va.astype(BF16)
    for kvh in range(N_KV_HEADS):
        own = slice(kvh * HEAD_DIM, (kvh + 1) * HEAD_DIM)
        for c in range(nblk):
            kt_scr[kvh, c, own, :] = kt[own, c * CHUNK:(c + 1) * CHUNK]
        vm_scr[kvh, :, own] = vab[:, own]
        kprev_scr[wr, kvh, own, :] = kt[own, tb - WINDOW:]
        vprev_scr[wr, kvh, :, own] = vab[tb - WINDOW:, own]

    no_prev = jnp.where(
        jnp.logical_and(first, lax.broadcasted_iota(jnp.int32, (CHUNK, 2 * CHUNK), 1) < CHUNK), NEG_BIG, 0.0)

    def attn_scores(c):
        qs = qs_scr[c]
        out = []
        for kvh in range(N_KV_HEADS):
            k_prev = kprev_scr[rd, kvh] if c == 0 else kt_scr[kvh, c - 1]
            out.append(_dot(qs, jnp.concatenate([k_prev, kt_scr[kvh, c]], axis=1)))
        return out

    def attn_softmax(c, scores):
        out = []
        for kvh in range(N_KV_HEADS):
            ps = []
            for g in range(GROUP):
                h = g * N_KV_HEADS + kvh
                s = scores[kvh][g * CHUNK:(g + 1) * CHUNK, :] + bias_ref[h]
                if c == 0:
                    s = s + no_prev
                sink = sink_ref[0, kvh * GROUP + g] * LOG2E
                m = jnp.max(s, axis=1, keepdims=True)
                p = jnp.exp2(s - m)
                denom = jnp.sum(p, axis=1, keepdims=True) + jnp.exp2(sink - m)
                ps.append((p * (1.0 / denom)).astype(BF16))
            out.append(jnp.concatenate(ps, axis=0))
        return out

    def attn_values(c, probs):
        rows = slice(c * CHUNK, (c + 1) * CHUNK)
        acc = None
        for kvh in range(N_KV_HEADS):
            if c == 0:
                v_band = jnp.concatenate([vprev_scr[rd, kvh], vm_scr[kvh, 0:CHUNK, :]], axis=0)
            else:
                v_band = vm_scr[kvh, (c - 1) * CHUNK:(c + 1) * CHUNK, :]
            o = _dot(probs[kvh], v_band)
            acc = o if acc is None else acc + o
        for g in range(GROUP):
            b_scr[rows, g * KV_WIDTH:(g + 1) * KV_WIDTH] = acc[g * CHUNK:(g + 1) * CHUNK, :].astype(BF16)

    def sgu_chunk(c):
        rows = slice(c * CHUNK, (c + 1) * CHUNK)
        vn_c = vn_scr[rows, :]
        mixed = jnp.concatenate(
            [_dot(wt_scr[g], vn_c[:, g * SGU_GROUP_DIM:(g + 1) * SGU_GROUP_DIM]) for g in range(SGU_GROUPS)],
            axis=1) + bexp_ref[...]
        a_scr[rows, :] = (u_scr[rows, :] * mixed).astype(BF16)

    half = D_MODEL // 2

    def tail_u(h, lo):
        u_scr[:, lo:lo + half] = _gelu_tanh(h)

    def tail_ga(h, lo):
        gate_scr[0, :, lo:lo + half] = jax.nn.sigmoid(h)

    def tail_gb(h, lo):
        gate_scr[1, :, lo:lo + half] = jax.nn.sigmoid(h)

    fillers = [
        (lambda: _dot(xn, w_in_ref[:, OFF_U + half:OFF_V]), lambda h: tail_u(h, half)),
        (lambda: _dot(xn, w_in_ref[:, OFF_GA:OFF_GA + half]), lambda h: tail_ga(h, 0)),
        (lambda: _dot(xn, w_in_ref[:, OFF_GA + half:OFF_GB]), lambda h: tail_ga(h, half)),
        (lambda: _dot(xn, w_in_ref[:, OFF_GB:IN_WIDTH]),
         lambda h: (tail_gb(h[:, :half], 0), tail_gb(h[:, half:], half))),
    ]
    vn_scr[...] = _rmsnorm(_gelu_tanh(h_v), sgu_g_ref[...]).astype(BF16)
    tail_u(h_u0, 0)
    sgu_after = {nblk - 2: range(0, nblk // 2), nblk - 1: range(nblk // 2, nblk)}
    for c in range(nblk):
        scores = attn_scores(c)
        proj = fillers[c][0]() if c < len(fillers) else None
        for cc in sgu_after.get(c, ()):
            sgu_chunk(cc)
        probs = attn_softmax(c, scores)
        attn_values(c, probs)
        if proj is not None:
            fillers[c][1](proj)
    for matmul, tail in fillers[nblk:]:
        tail(matmul())

    @pl.when(step == 0)
    def _():
        for copy in merge_copies():
            copy.wait()

    sample =_sample_attn_stages(s_qsel_ref, s_knew_ref, s_vnew_ref, s_ck_ref, s_cv_ref, s_bias_ref, sink_ref,
                                 step % 2 == 1, s_o_ref, s_nk_ref, s_nv_ref)
    next(sample)
    branch_a = _dot(a_scr[...], w_oa_ref[...])
    next(sample)
    branch_b = _dot(b_scr[...], w_ob_ref[...])
    for _ in sample:
        pass
    hm = gate_scr[0] * branch_a + gate_scr[1] * branch_b
    x1_ref[...] = x + _dot(hm.astype(BF16), w_out_ref[...])


def _mix_prompt(x2d, ln1, w_in, w_q, sgu_g, sgu_w, sgu_b, sinks, slopes, w_oa, w_ob, w_out,
                w_up_f32, w_down_f32,
                s_qsel, s_knew, s_vnew, s_cache_k, s_cache_v, s_bias, *, batch, seq):
    n = x2d.shape[0]
    tb = TOKEN_BLOCK
    nblk = tb // CHUNK
    steps = n // tb
    steps_per_seq = seq // tb
    nb = s_cache_k.shape[0]
    per_step = nb // steps
    assert per_step * steps == nb and 2 * per_step == SUBLANES and s_knew.shape == (nb, KV_WIDTH)
    s_head_block = pl.BlockSpec((None, per_step * N_HEADS, KV_WIDTH), lambda i: (i, 0, 0))
    s_new_block = pl.BlockSpec((SUBLANES, KV_WIDTH), lambda i: (i // 2, 0))
    s_cache_block = pl.BlockSpec((per_step, KV_WIDTH, WINDOW), lambda i: (i, 0, 0))
    row_block = pl.BlockSpec((tb, D_MODEL), lambda i: (i, 0))
    win_block = pl.BlockSpec((None, KV_WIDTH, WINDOW), lambda i: (i // steps_per_seq, 0, 0))
    up_block = pl.BlockSpec((D_MODEL // steps, D_FF), lambda i: (i, 0))
    down_block = pl.BlockSpec((D_FF // steps, D_MODEL), lambda i: (i, 0))
    return pl.pallas_call(
        functools.partial(_mix_prompt_kernel, steps_per_seq=steps_per_seq),
        grid=(n // tb,),
        in_specs=[
            row_block,
            _resident((1, D_MODEL)),
            _resident((D_MODEL, IN_WIDTH)),
            _resident((D_MODEL, D_MODEL)),
            _resident((1, D_MODEL)),
            _resident((SGU_GROUPS, CHUNK, CHUNK)),
            _resident((SGU_GROUPS, CHUNK)),
            pl.BlockSpec(memory_space=pltpu.SMEM),
            pl.BlockSpec(memory_space=pltpu.SMEM),
            pl.BlockSpec(memory_space=pl.ANY),
            pl.BlockSpec(memory_space=pl.ANY),
            pl.BlockSpec(memory_space=pl.ANY),
            up_block,
            down_block,
            s_head_block, s_new_block, s_new_block, s_cache_block, s_cache_block,
            _resident((N_HEADS, WINDOW)),
        ],
        out_specs=[row_block, win_block, win_block, up_block, down_block,
                   s_head_block, s_cache_block, s_cache_block],
        out_shape=[
            jax.ShapeDtypeStruct((n, D_MODEL), F32),
            jax.ShapeDtypeStruct((batch, KV_WIDTH, WINDOW), F32),
            jax.ShapeDtypeStruct((batch, KV_WIDTH, WINDOW), F32),
            jax.ShapeDtypeStruct((D_MODEL, D_FF), BF16),
            jax.ShapeDtypeStruct((D_FF, D_MODEL), BF16),
            jax.ShapeDtypeStruct((steps, per_step * N_HEADS, KV_WIDTH), BF16),
            jax.ShapeDtypeStruct((nb, KV_WIDTH, WINDOW), F32),
            jax.ShapeDtypeStruct((nb, KV_WIDTH, WINDOW), F32),
        ],
        scratch_shapes=[
            pltpu.VMEM((nblk, GROUP * CHUNK, KV_WIDTH), BF16),
            pltpu.VMEM((N_KV_HEADS, nblk, KV_WIDTH, CHUNK), BF16),
            pltpu.VMEM((N_KV_HEADS, tb, KV_WIDTH), BF16),
            pltpu.VMEM((2, N_KV_HEADS, KV_WIDTH, CHUNK), BF16),
            pltpu.VMEM((2, N_KV_HEADS, WINDOW, KV_WIDTH), BF16),
            pltpu.VMEM((tb, D_MODEL), BF16),
            pltpu.VMEM((tb, D_MODEL), F32),
            pltpu.VMEM((2, tb, D_MODEL), F32),
            pltpu.VMEM((tb, D_MODEL), BF16),
            pltpu.VMEM((tb, D_MODEL), BF16),
            pltpu.VMEM((SGU_GROUPS, CHUNK, CHUNK), BF16),
            pltpu.VMEM((N_HEADS, CHUNK, 2 * CHUNK), F32),
            pltpu.VMEM((CHUNK, D_MODEL), F32),
            pltpu.VMEM((D_MODEL, D_MODEL), BF16),
            pltpu.VMEM((D_MODEL, D_MODEL), BF16),
            pltpu.VMEM((D_MODEL, D_MODEL), BF16),
            pltpu.SemaphoreType.DMA((3,)),
        ],
        compiler_params=_params(),
        name="mix_prompt",
    )(x2d, ln1, w_in, w_q, sgu_g, sgu_w, sgu_b, sinks, slopes, w_oa, w_ob, w_out, w_up_f32, w_down_f32,
      s_qsel, s_knew, s_vnew, s_cache_k, s_cache_v, s_bias)


def _store_rows_as_tiles(ref, rows):
    for c in range(rows.shape[1] // LANES):
        ref[:, c, :] = rows[:, c * LANES:(c + 1) * LANES]


def _rows_from_tiles(ref):
    return jnp.concatenate([ref[:, c, :] for c in range(ref.shape[1])], axis=1)


def _ffn_rows(x, store, ln2_ref, w_up_ref, w_down_ref, lnf_ref, *, row_parts):
    n_slabs = D_FF // FF_SLAB
    m = x.shape[0]
    parts = [slice(i * m // row_parts, (i + 1) * m // row_parts) for i in range(row_parts)]

    def up(j, xn):
        return _dot(xn, w_up_ref[:, j * FF_SLAB:(j + 1) * FF_SLAB])

    xns = [_rmsnorm(x[p], ln2_ref[...]).astype(BF16) for p in parts]
    h_next = jnp.concatenate([up(0, xn_p) for xn_p in xns], axis=0) if row_parts > 1 else up(0, xns[0])
    xn = jnp.concatenate(xns, axis=0) if row_parts > 1 else xns[0]

    acc = x
    for j in range(n_slabs):
        h = h_next
        if j + 1 < n_slabs:
            h_next = up(j + 1, xn)
        h = jnp.square(jnp.maximum(h, 0.0)).astype(BF16)
        w_d = w_down_ref[j * FF_SLAB:(j + 1) * FF_SLAB, :]
        if j + 1 < n_slabs:
            acc = acc + _dot(h, w_d)
        else:
            for p in parts:
                store(p, _rmsnorm(acc[p] + _dot(h[p], w_d), lnf_ref[...]))


def _ffn_kernel(x_ref, ln2_ref, w_up_ref, w_down_ref, lnf_ref,
                s_o_hbm, s_selt_hbm, s_a_hbm, s_ga_hbm, s_gb_hbm, s_x_hbm, w_oa_hbm, w_ob_hbm, w_out_hbm,
                y_ref, ys_ref, s_o_ref, s_selt_ref, s_a_ref, s_ga_ref, s_gb_ref, s_x_ref,
                w_oa_ref, w_ob_ref, w_out_ref, late_sem):
    i = pl.program_id(0)
    last = pl.num_programs(0) - 1

    def late_copies():
        pairs = ((s_o_hbm, s_o_ref), (s_selt_hbm, s_selt_ref), (s_a_hbm, s_a_ref), (s_ga_hbm, s_ga_ref),
                 (s_gb_hbm, s_gb_ref), (s_x_hbm, s_x_ref),
                 (w_oa_hbm, w_oa_ref), (w_ob_hbm, w_ob_ref), (w_out_hbm, w_out_ref))
        return [pltpu.make_async_copy(src, dst, late_sem.at[k]) for k, (src, dst) in enumerate(pairs)]

    @pl.when(i == 0)
    def _():
        for copy in late_copies():
            copy.start()

    @pl.when(i < last)
    def _():
        def store(p, rows):
            y_ref[p, :] = rows

        _ffn_rows(x_ref[...], store, ln2_ref, w_up_ref, w_down_ref, lnf_ref, row_parts=4)

    @pl.when(i == last)
    def _():
        for copy in late_copies():
            copy.wait()
        nb = s_x_ref.shape[0]
        bst = _dot(s_selt_ref[...], s_o_ref[...]).astype(BF16)
        ob = _dot(bst[0:nb, :], w_ob_ref[0:KV_WIDTH, :])
        for g in range(1, GROUP):
            ob = ob + _dot(bst[g * nb:(g + 1) * nb, :], w_ob_ref[g * KV_WIDTH:(g + 1) * KV_WIDTH, :])
        hm = s_ga_ref[...] * _dot(s_a_ref[...], w_oa_ref[...]) + s_gb_ref[...] * ob
        xs1 = _rows_from_tiles(s_x_ref) + _dot(hm.astype(BF16), w_out_ref[...])
        _ffn_rows(xs1, lambda p, rows: _store_rows_as_tiles(ys_ref, rows), ln2_ref, w_up_ref, w_down_ref, lnf_ref,
                  row_parts=1)


def _ffn(x2d, ln2, w_up, w_down, lnf, s_o, s_selt, s_a, s_ga, s_gb, xs2d, w_oa, w_ob, w_out):
    n = x2d.shape[0]
    nb = xs2d.shape[0]
    n_prompt_steps = n // FFN_BLOCK
    row_block = pl.BlockSpec((FFN_BLOCK, D_MODEL), lambda i: (jnp.minimum(i, n_prompt_steps - 1), 0))
    assert n_prompt_steps >= 1
    in_hbm = pl.BlockSpec(memory_space=pl.ANY)
    return pl.pallas_call(
        _ffn_kernel,
        grid=(n_prompt_steps + 1,),
        in_specs=[row_block, _resident((1, D_MODEL)), _resident((D_MODEL, D_FF)),
                  _resident((D_FF, D_MODEL)), _resident((1, D_MODEL)),
                  in_hbm, in_hbm, in_hbm, in_hbm, in_hbm, in_hbm,
                  in_hbm, in_hbm, in_hbm],
        out_specs=[row_block, _whole((nb, D_MODEL // LANES, LANES))],
        out_shape=[jax.ShapeDtypeStruct((n, D_MODEL), F32),
                   jax.ShapeDtypeStruct((nb, D_MODEL // LANES, LANES), F32)],
        scratch_shapes=[pltpu.VMEM((nb * N_HEADS, KV_WIDTH), BF16),
                        pltpu.VMEM((GROUP * nb, nb * N_HEADS), BF16),
                        pltpu.VMEM((nb, D_MODEL), BF16), pltpu.VMEM((nb, D_MODEL), F32),
                        pltpu.VMEM((nb, D_MODEL), F32),
                        pltpu.VMEM((nb, D_MODEL // LANES, LANES), F32),
                        pltpu.VMEM((D_MODEL, D_MODEL), BF16), pltpu.VMEM((D_MODEL, D_MODEL), BF16),
                        pltpu.VMEM((D_MODEL, D_MODEL), BF16),
                        pltpu.SemaphoreType.DMA((9,))],
        compiler_params=_params(),
        name="ffn",
    )(x2d, ln2, w_up, w_down, lnf, s_o, s_selt, s_a, s_ga, s_gb, xs2d, w_oa, w_ob, w_out)


def _sample_proj_kernel(x_ref, ln1_ref, w_blk_ref, sgu_g_ref, sgu_w_ref, sgu_b_ref, sel_ref,
                        w_oa_blk_ref, w_ob_blk_ref, w_out_blk_ref,
                        w_bf_ref, w_q_bf_ref, qsel_ref, knew_ref, vnew_ref, vn_ref, a_ref,
                        ga_ref, gb_ref, w_oa_bf_ref, w_ob_bf_ref, w_out_bf_ref, xn_scr, h_scr):
    j = pl.program_id(0)
    nb = x_ref.shape[0]
    n_steps = IN_WIDTH // W_STEP

    @pl.when(j == 0)
    def _():
        xn_scr[...] = _rmsnorm(_rows_from_tiles(x_ref), ln1_ref[...]).astype(BF16)

    w_oa_bf_ref[...] = w_oa_blk_ref[...].astype(BF16)
    w_out_bf_ref[...] = w_out_blk_ref[...].astype(BF16)
    assert w_ob_blk_ref.shape[0] == GROUP * HEAD_DIM and n_steps == N_KV_HEADS
    for g in range(GROUP):
        dst = pl.multiple_of(g * KV_WIDTH + j * HEAD_DIM, HEAD_DIM)
        w_ob_bf_ref[pl.ds(dst, HEAD_DIM), :] = w_ob_blk_ref[g * HEAD_DIM:(g + 1) * HEAD_DIM, :].astype(BF16)

    wb = w_blk_ref[...].astype(BF16)
    w_bf_ref[...] = wb
    h_scr[j] = _dot(xn_scr[...], wb)

    assert W_STEP % HEAD_DIM == 0
    for step in range(n_steps):
        heads = [h for h in range(N_HEADS) if step * W_STEP <= OFF_Q + h * HEAD_DIM < (step + 1) * W_STEP]
        if heads:
            @pl.when(j == step)
            def _(step=step, heads=heads):
                for head in heads:
                    src = OFF_Q + head * HEAD_DIM - step * W_STEP
                    kvh, g = divmod(head, GROUP)
                    dst = g * KV_WIDTH + kvh * HEAD_DIM
                    w_q_bf_ref[:, dst:dst + HEAD_DIM] = wb[:, src:src + HEAD_DIM]

    @pl.when(j == n_steps - 1)
    def _():
        def cols(lo, hi):
            pieces = []
            for step in range(n_steps):
                a, b = max(lo, step * W_STEP), min(hi, (step + 1) * W_STEP)
                if a < b:
                    pieces.append(h_scr[step, :, a - step * W_STEP:b - step * W_STEP])
            return pieces[0] if len(pieces) == 1 else jnp.concatenate(pieces, axis=1)

        u = _gelu_tanh(cols(OFF_U, OFF_V))
        v = _gelu_tanh(cols(OFF_V, OFF_Q))
        vn = _rmsnorm(v, sgu_g_ref[...])
        _store_rows_as_tiles(vn_ref, vn)
        def over_groups(entry):
            return jnp.concatenate(
                [jnp.broadcast_to(entry(g), (1, SGU_GROUP_DIM)) for g in range(SGU_GROUPS)], axis=1)

        w_diag = over_groups(lambda g: sgu_w_ref[g, 0:1, 0:1])
        b_first = over_groups(lambda g: sgu_b_ref[g:g + 1, 0:1])
        a_ref[...] = (u * (vn * w_diag + b_first)).astype(BF16)
        knew_ref[...] = cols(OFF_K, OFF_VA)
        vnew_ref[...] = cols(OFF_VA, OFF_GA)
        ga_ref[...] = jax.nn.sigmoid(cols(OFF_GA, OFF_GB))
        gb_ref[...] = jax.nn.sigmoid(cols(OFF_GB, IN_WIDTH))
        q = _dot(xn_scr[...], w_q_bf_ref[...]) * ATTN_SCALE
        qstack = jnp.concatenate([q[:, g * KV_WIDTH:(g + 1) * KV_WIDTH] for g in range(GROUP)], axis=0).astype(BF16)
        qrep = _dot(sel_ref[...], qstack)
        row_kvh = lax.broadcasted_iota(jnp.int32, (nb * N_HEADS, KV_WIDTH), 0) % N_KV_HEADS
        lane_kvh = lax.broadcasted_iota(jnp.int32, (nb * N_HEADS, KV_WIDTH), 1) // HEAD_DIM
        qsel_ref[...] = jnp.where(row_kvh == lane_kvh, qrep, 0.0).astype(BF16)


def _sample_proj(xs2d, ln1, w_in_f32, sgu_g, sgu_w, sgu_b, sel, w_oa_f32, w_ob_f32, w_out_f32):
    nb = xs2d.shape[0]
    n_blocks = IN_WIDTH // W_STEP
    w_block = pl.BlockSpec((D_MODEL, W_STEP), lambda j: (0, j))
    merge_block = pl.BlockSpec((D_MODEL // n_blocks, D_MODEL), lambda j: (j, 0))
    merge_shape = jax.ShapeDtypeStruct((D_MODEL, D_MODEL), BF16)
    return pl.pallas_call(
        _sample_proj_kernel,
        grid=(n_blocks,),
        in_specs=[_whole((nb, D_MODEL // LANES, LANES)), _whole((1, D_MODEL)), w_block,
                  _whole((1, D_MODEL)),
                  pl.BlockSpec((SGU_GROUPS, SUBLANES, CHUNK), lambda j: (0, 0, 0)),
                  _whole((SGU_GROUPS, CHUNK)),
                  _resident((nb * N_HEADS, GROUP * nb)),
                  merge_block, merge_block, merge_block],
        out_specs=[w_block, _whole((D_MODEL, D_MODEL)),
                   _whole((nb * N_HEADS, KV_WIDTH)), _whole((nb, KV_WIDTH)), _whole((nb, KV_WIDTH)),
                   _whole((nb, D_MODEL // LANES, LANES)), _whole((nb, D_MODEL)),
                   _whole((nb, D_MODEL)), _whole((nb, D_MODEL)),
                   merge_block, _whole((D_MODEL, D_MODEL)), merge_block],
        out_shape=[
            jax.ShapeDtypeStruct((D_MODEL, IN_WIDTH), BF16),
            jax.ShapeDtypeStruct((D_MODEL, D_MODEL), BF16),
            jax.ShapeDtypeStruct((nb * N_HEADS, KV_WIDTH), BF16),
            jax.ShapeDtypeStruct((nb, KV_WIDTH), F32),
            jax.ShapeDtypeStruct((nb, KV_WIDTH), F32),
            jax.ShapeDtypeStruct((nb, D_MODEL // LANES, LANES), F32),
            jax.ShapeDtypeStruct((nb, D_MODEL), BF16),
            jax.ShapeDtypeStruct((nb, D_MODEL), F32),
            jax.ShapeDtypeStruct((nb, D_MODEL), F32),
            merge_shape, merge_shape, merge_shape,
        ],
        scratch_shapes=[pltpu.VMEM((nb, D_MODEL), BF16), pltpu.VMEM((n_blocks, nb, W_STEP), F32)],
        compiler_params=_params(),
        name="sample_proj",
    )(xs2d, ln1, w_in_f32, sgu_g, sgu_w, sgu_b, sel, w_oa_f32, w_ob_f32, w_out_f32)


def _sample_attn_stages(qsel_ref, knew_ref, vnew_ref, ck_ref, cv_ref, bias_ref, sink_ref, upper,
                        o_ref, nk_ref, nv_ref):
    bs = ck_ref.shape[0]
    assert knew_ref.shape[0] == 2 * bs
    row0 = jnp.where(upper, bs, 0)
    row_kvh = lax.broadcasted_iota(jnp.int32, (N_HEADS, KV_WIDTH), 0) % N_KV_HEADS
    lane_kvh = lax.broadcasted_iota(jnp.int32, (N_HEADS, KV_WIDTH), 1) // HEAD_DIM
    own = row_kvh == lane_kvh
    bias = bias_ref[...]
    head_row = lax.broadcasted_iota(jnp.int32, (N_HEADS, 1), 0)
    sink = jnp.zeros((N_HEADS, 1), F32)
    for h in range(N_HEADS):
        g, kvh = divmod(h, N_KV_HEADS)
        sink = jnp.where(head_row == h, sink_ref[0, kvh * GROUP + g], sink)

    qss = [qsel_ref[i * N_HEADS:(i + 1) * N_HEADS, :] for i in range(bs)]
    kns = [knew_ref[pl.ds(row0 + i, 1), :] for i in range(bs)]
    vws = [vnew_ref[pl.ds(row0 + i, 1), :] for i in range(bs)]
    scores = [_dot(qss[i], ck_ref[i].astype(BF16)) + bias for i in range(bs)]
    yield
    probs = []
    for i in range(bs):
        s = scores[i]
        s_new = jnp.sum(qss[i].astype(F32) * kns[i], axis=1, keepdims=True)
        m = jnp.maximum(jnp.maximum(jnp.max(s, axis=1, keepdims=True), s_new), sink)
        p = jnp.exp(s - m)
        p_new = jnp.exp(s_new - m)
        denom = jnp.sum(p, axis=1, keepdims=True) + p_new + jnp.exp(sink - m)
        probs.append((p.astype(BF16), p_new, denom))

    kn_t = knew_ref[...].T
    vw_t = vnew_ref[...].T
    last_lane = lax.broadcasted_iota(jnp.int32, (KV_WIDTH, WINDOW), 1) == WINDOW - 1
    for i in range(bs):
        kn_col = jnp.where(upper, kn_t[:, bs + i:bs + i + 1], kn_t[:, i:i + 1])
        vw_col = jnp.where(upper, vw_t[:, bs + i:bs + i + 1], vw_t[:, i:i + 1])
        nk_ref[i] = jnp.where(last_lane, kn_col, pltpu.roll(ck_ref[i], WINDOW - 1, 1))
        nv_ref[i] = jnp.where(last_lane, vw_col, pltpu.roll(cv_ref[i], WINDOW - 1, 1))
    yield
    for i in range(bs):
        p, p_new, denom = probs[i]
        o = (_dot_nt(p, cv_ref[i].astype(BF16)) + p_new * vws[i]) / denom
        o_ref[i * N_HEADS:(i + 1) * N_HEADS, :] = jnp.where(own, o, 0.0).astype(BF16)


def _head_perm(v):
    return v.reshape(N_KV_HEADS, GROUP).T.reshape(N_HEADS)


def _alibi_slopes():
    h = np.arange(1, N_HEADS + 1, dtype=np.float32)
    return np.exp2(-8.0 * h / N_HEADS).astype(np.float32)


def _selection_matrix(nb):
    r = np.arange(nb * N_HEADS)
    c = np.arange(GROUP * nb)
    same_sample = (r[:, None] // N_HEADS) == (c[None, :] % nb)
    same_member = ((r[:, None] % N_HEADS) // N_KV_HEADS) == (c[None, :] // nb)
    return (same_sample & same_member).astype(np.float32)


def kernel(x_prompt, x_sample, cache_k_win, cache_v_win, ln1_g, w_in, sgu_norm_g, sgu_w, sgu_b, attn_sinks,
           w_oa, w_ob, w_out, ln2_g, w_up, w_down, lnf_g):
    batch, seq, _ = x_prompt.shape
    dec_batch, dec_seq, _ = x_sample.shape
    depth = w_in.shape[0]
    assert depth == 1 and dec_seq == 1
    assert seq % TOKEN_BLOCK == 0 and TOKEN_BLOCK % CHUNK == 0
    assert (batch * seq) % FFN_BLOCK == 0
    assert w_in.shape[-1] == IN_WIDTH

    ln1 = ln1_g[0].reshape(1, D_MODEL)
    ln2 = ln2_g[0].reshape(1, D_MODEL)
    lnf = lnf_g.reshape(1, D_MODEL)
    sgu_g = sgu_norm_g[0].reshape(1, D_MODEL)
    slopes_p = _head_perm(_alibi_slopes())

    xs2d = x_sample.reshape(dec_batch, D_MODEL // LANES, LANES)
    sel_np = _selection_matrix(dec_batch)
    sel = jnp.asarray(sel_np, BF16)
    selt = jnp.asarray(sel_np.T, BF16)
    bias_s = -slopes_p[:, None] * (WINDOW - np.arange(WINDOW, dtype=np.float32))[None, :]

    w_in_b, w_q_b, qsel, knew, vnew, vn, a_s, ga_s, gb_s, w_oa_b, w_ob_b, w_out_b = _sample_proj(
        xs2d, ln1, w_in[0], sgu_g, sgu_w[0], sgu_b[0], sel, w_oa[0], w_ob[0], w_out[0])
    def to_feature_major(c):
        return c[0].transpose(0, 2, 3, 1).reshape(dec_batch, KV_WIDTH, WINDOW)

    def from_feature_major(c):
        return c.reshape(c.shape[0], N_KV_HEADS, HEAD_DIM, WINDOW).transpose(0, 3, 1, 2)[None]

    steps = (batch * seq) // TOKEN_BLOCK
    per_step = dec_batch // steps

    x1, kwin, vwin, w_up_b, w_down_b, o_s, nk, nv = _mix_prompt(
        x_prompt.reshape(batch * seq, D_MODEL), ln1, w_in_b, w_q_b, sgu_g, sgu_w[0], sgu_b[0],
        attn_sinks, slopes_p * LOG2E, w_oa_b, w_ob_b, w_out_b, w_up[0], w_down[0],
        qsel.reshape(steps, per_step * N_HEADS, KV_WIDTH), knew, vnew,
        to_feature_major(cache_k_win), to_feature_major(cache_v_win), bias_s,
        batch=batch, seq=seq)

    y_prompt, y_sample = _ffn(x1, ln2, w_up_b, w_down_b, lnf,
                              o_s.reshape(dec_batch * N_HEADS, KV_WIDTH), selt, a_s, ga_s, gb_s, xs2d,
                              w_oa_b, w_ob_b, w_out_b)

    return (y_prompt.reshape(batch, seq, D_MODEL),
            y_sample.reshape(dec_batch, dec_seq, D_MODEL),
            from_feature_major(kwin), from_feature_major(vwin),
            from_feature_major(nk), from_feature_major(nv),
            vn.reshape(depth, dec_batch, dec_seq, D_MODEL))
```
